```python
import math
import jax, jax.numpy as jnp
from jax import lax
import numpy as np

D_MODEL = 2048
BATCH = 8
SEQ = 2048
DEPTH = 1

DN_HEADS = 8
DN_HEAD_DIM = 128
DN_WIDTH = DN_HEADS * DN_HEAD_DIM
DN_CONV = 4
DN_CHUNK = 64
MLA_HEADS = 8
MLA_Q_RANK = 512
MLA_KV_RANK = 512
MLA_NOPE = 128
MLA_ROPE = 64
MLA_V = 128
MLA_WIDTH = MLA_HEADS * MLA_V
ROPE_THETA = 10000.0
ATTN_BLOCK = 128
MIX_WIDTH = DN_WIDTH + MLA_WIDTH
IN_SPLITS = (3 * DN_WIDTH, 4 * DN_WIDTH, 4 * DN_WIDTH + DN_HEADS, 4 * DN_WIDTH + 2 * DN_HEADS, 4 * DN_WIDTH + 2 * DN_HEADS + MLA_Q_RANK, 4 * DN_WIDTH + 2 * DN_HEADS + MLA_Q_RANK + MLA_KV_RANK)
IN_WIDTH = IN_SPLITS[-1] + MLA_ROPE
N_EXPERTS = 64
TOP_K = 6
N_GROUPS = 8
TOPK_GROUPS = 4
EXPERT_FF = 512
SHARED_FF = 512
ROUTED_SCALE = 2.5
EXPERT_BLOCK = 256
RMS_EPS = 1e-6
POS_OFFSET_MAX = 4096

kernel_name = "hybrid_deltanet_mla_moe_adaln"


def rms_norm(x, gain):
    xf = x.astype(jnp.float32)
    y = xf * lax.rsqrt(jnp.mean(xf * xf, axis=-1, keepdims=True) + RMS_EPS)
    return (y * gain.astype(jnp.float32)).astype(x.dtype)


def modulate(xn, shift, scale):
    return xn * (1.0 + scale[:, None, :]) + shift[:, None, :]


def l2_normalize(x):
    return x * lax.rsqrt(jnp.sum(x * x, axis=-1, keepdims=True) + 1e-6)


def causal_depthwise_conv(x, w):
    k = w.shape[-1]
    return lax.conv_general_dilated(
        x, w.T[:, None, :].astype(x.dtype), window_strides=(1,), padding=((k - 1, 0),),
        dimension_numbers=('NWC', 'WIO', 'NWC'), feature_group_count=x.shape[-1])


def rotary(x, positions):
    half = x.shape[-1] // 2
    inv_freq = ROPE_THETA ** (-jnp.arange(half, dtype=jnp.float32) / half)
    ang = positions.astype(jnp.float32)[..., None] * inv_freq
    cos = jnp.cos(ang)[:, :, None, :]
    sin = jnp.sin(ang)[:, :, None, :]
    xf = x.astype(jnp.float32)
    x1, x2 = xf[..., :half], xf[..., half:]
    return jnp.concatenate([x1 * cos - x2 * sin, x2 * cos + x1 * sin], axis=-1).astype(x.dtype)


def gated_delta_rule_chunked(q, k, v, g, beta):
    f32 = jnp.float32
    bsz, seq, nh, dk = q.shape
    dv = v.shape[-1]
    n_chunks = seq // DN_CHUNK
    q = l2_normalize(q.astype(f32)) * (dk ** -0.5)
    k = l2_normalize(k.astype(f32))

    def to_chunks(t):
        t = t.reshape((bsz, n_chunks, DN_CHUNK, nh) + t.shape[3:])
        return jnp.moveaxis(t, (1, 3), (0, 2))

    qc, kc, vc = to_chunks(q), to_chunks(k), to_chunks(v.astype(f32))
    gc = jnp.cumsum(to_chunks(g.astype(f32)), axis=-1)
    bc = to_chunks(beta.astype(f32))
    idx = jnp.arange(DN_CHUNK)
    causal = idx[:, None] >= idx[None, :]
    strict = idx[:, None] > idx[None, :]
    decay = jnp.exp(jnp.where(causal, gc[..., :, None] - gc[..., None, :], -jnp.inf))
    kb = kc * bc[..., None]
    m = jnp.where(strict, jnp.einsum('nbhid,nbhjd->nbhij', kb, kc) * decay, 0.0)
    rhs = jnp.concatenate([vc * bc[..., None], kb * jnp.exp(gc)[..., None]], axis=-1)
    sol = lax.linalg.triangular_solve(m + jnp.eye(DN_CHUNK, dtype=f32), rhs, left_side=True, lower=True, unit_diagonal=True)
    u, w = sol[..., :dv], sol[..., dv:]
    a_intra = jnp.where(causal, jnp.einsum('nbhid,nbhjd->nbhij', qc, kc) * decay, 0.0)

    def chunk_step(state, inp):
        q_i, k_i, u_i, w_i, g_i, a_i = inp
        v_new = u_i - jnp.einsum('bhck,bhkv->bhcv', w_i, state)
        o_i = (jnp.einsum('bhck,bhkv->bhcv', q_i * jnp.exp(g_i)[..., None], state)
               + jnp.einsum('bhij,bhjv->bhiv', a_i, v_new))
        g_last = g_i[..., -1:]
        state = (state * jnp.exp(g_last)[..., None]
                 + jnp.einsum('bhck,bhcv->bhkv', k_i * jnp.exp(g_last - g_i)[..., None], v_new))
        return state, o_i

    state0 = jnp.zeros((bsz, nh, dk, dv), f32)
    _, o = lax.scan(chunk_step, state0, (qc, kc, u, w, gc, a_intra))
    return jnp.moveaxis(o, (0, 2), (1, 3)).reshape(bsz, seq, nh, dv)


def causal_block_attention(q, k, v):
    seq = q.shape[1]
    scale = q.shape[-1] ** -0.5
    outs = []
    for blk in range(seq // ATTN_BLOCK):
        q0 = blk * ATTN_BLOCK
        kend = q0 + ATTN_BLOCK
        s = jnp.einsum('bqhd,bkhd->bhqk', q[:, q0:kend], k[:, :kend], preferred_element_type=jnp.float32) * scale
        qpos = q0 + jnp.arange(ATTN_BLOCK)
        kpos = jnp.arange(kend)
        s = jnp.where(kpos[None, :] <= qpos[:, None], s, -jnp.inf)
        p = jax.nn.softmax(s, axis=-1)
        outs.append(jnp.einsum('bhqk,bkhd->bqhd', p.astype(v.dtype), v[:, :kend]))
    return jnp.concatenate(outs, axis=1)


def hybrid_mixer(h, positions, w_in, dn_conv_w, dn_a_log, dn_dt_bias, dn_norm_gain,
                 mla_q_norm_gain, w_q_up, mla_kv_norm_gain, w_kv_up, w_out):
    f32 = jnp.float32
    bsz, seq, _ = h.shape
    dn_qkv, dn_z, dn_b, dn_a, mla_cq, mla_ckv, mla_kpe = jnp.split(h @ w_in, IN_SPLITS, axis=-1)
    hs = (bsz, seq, DN_HEADS, DN_HEAD_DIM)
    qkv = jax.nn.silu(causal_depthwise_conv(dn_qkv, dn_conv_w)).astype(f32)
    q, k, v = [t.reshape(hs) for t in jnp.split(qkv, 3, axis=-1)]
    beta = jax.nn.sigmoid(dn_b.astype(f32))
    g = -jnp.exp(dn_a_log.astype(f32)) * jax.nn.softplus(dn_a.astype(f32) + dn_dt_bias.astype(f32))
    o = gated_delta_rule_chunked(q, k, v, g, beta)
    o = rms_norm(o, dn_norm_gain) * jax.nn.silu(dn_z.astype(f32).reshape(hs))
    dn_out = o.reshape(bsz, seq, DN_WIDTH).astype(h.dtype)
    q_lat = (rms_norm(mla_cq, mla_q_norm_gain) @ w_q_up).reshape(bsz, seq, MLA_HEADS, MLA_NOPE + MLA_ROPE)
    kv = (rms_norm(mla_ckv, mla_kv_norm_gain) @ w_kv_up).reshape(bsz, seq, MLA_HEADS, MLA_NOPE + MLA_V)
    k_nope, v_mla = kv[..., :MLA_NOPE], kv[..., MLA_NOPE:]
    k_pe = jnp.broadcast_to(rotary(mla_kpe[:, :, None, :], positions), (bsz, seq, MLA_HEADS, MLA_ROPE))
    q_mla = jnp.concatenate([q_lat[..., :MLA_NOPE], rotary(q_lat[..., MLA_NOPE:], positions)], axis=-1)
    k_mla = jnp.concatenate([k_nope, k_pe.astype(k_nope.dtype)], axis=-1)
    mla_out = causal_block_attention(q_mla, k_mla, v_mla).reshape(bsz, seq, MLA_WIDTH)
    return jnp.concatenate([dn_out, mla_out.astype(h.dtype)], axis=-1) @ w_out


def moe_ffn(h, w_router, router_bias, w_exp_gate_up, w_exp_down, w_sh_gate_up, w_sh_down):
    f32 = jnp.float32
    bsz, seq, d = h.shape
    n_tok = bsz * seq
    hf = h.reshape(n_tok, d)
    scores = jax.nn.sigmoid(jnp.dot(hf, w_router, preferred_element_type=f32))
    biased = scores + router_bias.astype(f32)
    group_score = lax.top_k(biased.reshape(n_tok, N_GROUPS, N_EXPERTS // N_GROUPS), 2)[0].sum(-1)
    _, top_groups = lax.top_k(group_score, TOPK_GROUPS)
    group_mask = jnp.any(top_groups[:, :, None] == jnp.arange(N_GROUPS)[None, None, :], axis=1)
    expert_mask = jnp.repeat(group_mask, N_EXPERTS // N_GROUPS, axis=-1)
    _, top_idx = lax.top_k(jnp.where(expert_mask, biased, -jnp.inf), TOP_K)
    top_w = jnp.take_along_axis(scores, top_idx, axis=-1)
    top_w = top_w / jnp.sum(top_w, axis=-1, keepdims=True) * ROUTED_SCALE
    n_assign = n_tok * TOP_K
    flat_e = top_idx.reshape(n_assign)
    order = jnp.argsort(flat_e)
    sorted_e = flat_e[order]
    counts = jnp.bincount(flat_e, length=N_EXPERTS)
    padded = (counts + EXPERT_BLOCK - 1) // EXPERT_BLOCK * EXPERT_BLOCK
    pad_end = jnp.cumsum(padded)
    start = jnp.cumsum(counts) - counts
    dest = (pad_end - padded)[sorted_e] + jnp.arange(n_assign) - start[sorted_e]
    n_rows = -(-(n_assign + N_EXPERTS * (EXPERT_BLOCK - 1)) // EXPERT_BLOCK) * EXPERT_BLOCK
    n_blocks = n_rows // EXPERT_BLOCK
    row_tok = jnp.full((n_rows,), n_tok, jnp.int32).at[dest].set((order // TOP_K).astype(jnp.int32))
    row_w = jnp.zeros((n_rows,), f32).at[dest].set(top_w.reshape(n_assign)[order])
    block_e = jnp.minimum(jnp.searchsorted(pad_end, jnp.arange(n_blocks) * EXPERT_BLOCK, side='right'), N_EXPERTS - 1)
    h_pad = jnp.concatenate([hf, jnp.zeros((1, d), hf.dtype)], axis=0)

    def expert_block(acc, blk):
        tok, wts, e = blk
        gate, up = jnp.split(h_pad[tok] @ w_exp_gate_up[e], 2, axis=-1)
        y = (jax.nn.silu(gate) * up) @ w_exp_down[e]
        return acc.at[tok].add(y.astype(f32) * wts[:, None]), None

    acc, _ = lax.scan(expert_block, jnp.zeros((n_tok + 1, d), f32),
                      (row_tok.reshape(n_blocks, EXPERT_BLOCK), row_w.reshape(n_blocks, EXPERT_BLOCK), block_e))
    sg, su = jnp.split(hf @ w_sh_gate_up, 2, axis=-1)
    shared = (jax.nn.silu(sg) * su) @ w_sh_down
    return (acc[:n_tok] + shared.astype(f32)).astype(h.dtype).reshape(bsz, seq, d)


def setup_inputs(seed: int = 0) -> dict:
    key = jax.random.key(seed)
    ks = jax.random.split(key, 24)
    f32 = jnp.float32
    L = DEPTH

    def dense(k, shape, fan_in, gain=1.0):
        return jax.random.normal(k, shape, f32) * (gain * fan_in ** -0.5)

    def norm_gain(k, shape):
        return 1.0 + 0.02 * jax.random.normal(k, shape, f32)

    x = jax.random.normal(ks[0], (BATCH, SEQ, D_MODEL), f32)
    c = jax.random.normal(ks[1], (BATCH, D_MODEL), f32)
    positions = (jax.random.randint(ks[2], (BATCH, 1), 0, POS_OFFSET_MAX, dtype=jnp.int32)
                 + jnp.arange(SEQ, dtype=jnp.int32)[None, :]).astype(jnp.int32)
    w_ada = dense(ks[3], (L, D_MODEL, 6 * D_MODEL), D_MODEL, 0.5)
    b_ada = 0.02 * jax.random.normal(ks[4], (L, 6 * D_MODEL), f32)
    norm1_gain = norm_gain(ks[5], (L, D_MODEL))
    w_in = dense(ks[6], (L, D_MODEL, IN_WIDTH), D_MODEL)
    dn_conv_w = dense(ks[7], (L, 3 * DN_WIDTH, DN_CONV), DN_CONV)
    dn_a_log = jnp.log(jax.random.uniform(ks[8], (L, DN_HEADS), f32, 1.0, 16.0))
    dt = jnp.exp(jax.random.uniform(ks[9], (L, DN_HEADS), f32, math.log(1e-3), math.log(1e-1)))
    dn_dt_bias = dt + jnp.log(-jnp.expm1(-dt))
    dn_norm_gain = norm_gain(ks[10], (L, DN_HEAD_DIM))
    mla_q_norm_gain = norm_gain(ks[11], (L, MLA_Q_RANK))
    w_q_up = dense(ks[12], (L, MLA_Q_RANK, MLA_HEADS * (MLA_NOPE + MLA_ROPE)), MLA_Q_RANK)
    mla_kv_norm_gain = norm_gain(ks[13], (L, MLA_KV_RANK))
    w_kv_up = dense(ks[14], (L, MLA_KV_RANK, MLA_HEADS * (MLA_NOPE + MLA_V)), MLA_KV_RANK)
    w_out = dense(ks[15], (L, MIX_WIDTH, D_MODEL), MIX_WIDTH)
    norm2_gain = norm_gain(ks[16], (L, D_MODEL))
    w_router = dense(ks[17], (L, D_MODEL, N_EXPERTS), D_MODEL)
    router_bias = 0.01 * jax.random.normal(ks[18], (L, N_EXPERTS), f32)
    w_exp_gate_up = dense(ks[19], (L, N_EXPERTS, D_MODEL, 2 * EXPERT_FF), D_MODEL)
    w_exp_down = dense(ks[20], (L, N_EXPERTS, EXPERT_FF, D_MODEL), EXPERT_FF)
    w_sh_gate_up = dense(ks[21], (L, D_MODEL, 2 * SHARED_FF), D_MODEL)
    w_sh_down = dense(ks[22], (L, SHARED_FF, D_MODEL), SHARED_FF)
    final_norm_gain = norm_gain(ks[23], (D_MODEL,))
    return {"x": x, "c": c, "positions": positions, "w_ada": w_ada, "b_ada": b_ada,
            "norm1_gain": norm1_gain, "w_in": w_in, "dn_conv_w": dn_conv_w, "dn_a_log": dn_a_log,
            "dn_dt_bias": dn_dt_bias, "dn_norm_gain": dn_norm_gain, "mla_q_norm_gain": mla_q_norm_gain,
            "w_q_up": w_q_up, "mla_kv_norm_gain": mla_kv_norm_gain, "w_kv_up": w_kv_up, "w_out": w_out,
            "norm2_gain": norm2_gain, "w_router": w_router, "router_bias": router_bias,
            "w_exp_gate_up": w_exp_gate_up, "w_exp_down": w_exp_down, "w_sh_gate_up": w_sh_gate_up,
            "w_sh_down": w_sh_down, "final_norm_gain": final_norm_gain}


def reference(x, c, positions, w_ada, b_ada, norm1_gain, w_in, dn_conv_w, dn_a_log, dn_dt_bias,
              dn_norm_gain, mla_q_norm_gain, w_q_up, mla_kv_norm_gain, w_kv_up, w_out, norm2_gain,
              w_router, router_bias, w_exp_gate_up, w_exp_down, w_sh_gate_up, w_sh_down, final_norm_gain):
    cond = jax.nn.silu(c)
    for layer in range(DEPTH):
        ada = cond @ w_ada[layer] + b_ada[layer]
        shift1, scale1, gate1, shift2, scale2, gate2 = jnp.split(ada, 6, axis=-1)
        h = modulate(rms_norm(x, norm1_gain[layer]), shift1, scale1)
        mix = hybrid_mixer(h, positions, w_in[layer], dn_conv_w[layer], dn_a_log[layer], dn_dt_bias[layer],
                           dn_norm_gain[layer], mla_q_norm_gain[layer], w_q_up[layer], mla_kv_norm_gain[layer],
                           w_kv_up[layer], w_out[layer])
        x = x + gate1[:, None, :] * mix
        h = modulate(rms_norm(x, norm2_gain[layer]), shift2, scale2)
        ffn = moe_ffn(h, w_router[layer], router_bias[layer], w_exp_gate_up[layer], w_exp_down[layer],
                      w_sh_gate_up[layer], w_sh_down[layer])
        x = x + gate2[:, None, :] * ffn
    return rms_norm(x, final_norm_gain)
```

```python
import functools

import jax
import jax.numpy as jnp
from jax import lax
from jax.experimental import pallas as pl
from jax.experimental.pallas import tpu as pltpu

F32 = jnp.float32
BF16 = jnp.bfloat16

DN_HEADS = 8
DN_DIM = 128
DN_WIDTH = DN_HEADS * DN_DIM
DN_CONV = 4
DN_CHUNK = 64
MLA_HEADS = 8
MLA_Q_RANK = 512
MLA_KV_RANK = 512
MLA_NOPE = 128
MLA_ROPE = 64
MLA_V = 128
MLA_QK_PAD = 256
ROPE_THETA = 10000.0
N_EXPERTS = 64
TOP_K = 6
N_GROUPS = 8
TOPK_GROUPS = 4
EXPERT_FF = 512
ROUTED_SCALE = 2.5
EXPERT_BLOCK = 256
RMS_EPS = 1e-6
L2_EPS = 1e-6

LANES = 128
MAIN_WIDTH = 4 * DN_WIDTH + MLA_Q_RANK + MLA_KV_RANK
AUX_BETA = MLA_ROPE
AUX_G = MLA_ROPE + DN_HEADS
VMEM_LIMIT = 56 * 1024 * 1024


def _params(n_parallel, n_arbitrary=0):
    sem = ("parallel",) * n_parallel + ("arbitrary",) * n_arbitrary
    return pltpu.CompilerParams(dimension_semantics=sem, vmem_limit_bytes=VMEM_LIMIT)


def _silu(x):
    return x * jax.nn.sigmoid(x)


def _bdot(a, b):
    return jnp.dot(a.astype(BF16), b.astype(BF16), preferred_element_type=F32)


def _bdot_nt(a, b):
    return lax.dot_general(a.astype(BF16), b.astype(BF16), (((1,), (1,)), ((), ())),
                           preferred_element_type=F32)


def _bdot_tn(a, b):
    return lax.dot_general(a.astype(BF16), b.astype(BF16), (((0,), (0,)), ((), ())),
                           preferred_element_type=F32)


def _rms(x, gain):
    return x * lax.rsqrt(jnp.mean(x * x, axis=-1, keepdims=True) + RMS_EPS) * gain


def _ada_kernel(c_ref, w_ref, b_ref, o_ref):
    o_ref[...] = _bdot(_silu(c_ref[...]), w_ref[...]) + b_ref[...]


def _ada(c, w_ada, b_ada):
    bsz, d = c.shape
    n = w_ada.shape[1]
    tn = 1024
    return pl.pallas_call(
        _ada_kernel,
        grid=(n // tn,),
        in_specs=[pl.BlockSpec((bsz, d), lambda j: (0, 0)),
                  pl.BlockSpec((d, tn), lambda j: (0, j)),
                  pl.BlockSpec((1, tn), lambda j: (0, j))],
        out_specs=pl.BlockSpec((bsz, tn), lambda j: (0, j)),
        out_shape=jax.ShapeDtypeStruct((bsz, n), F32),
        compiler_params=_params(1),
        name="ada",
    )(c, w_ada, b_ada.reshape(1, n))


def _in_proj_kernel(x_ref, ada_ref, gain_ref, wm_ref, wa_ref, auxp_ref,
                    main_ref, auxc_ref, auxr_ref, h_ref):
    j = pl.program_id(1)

    @pl.when(j == 0)
    def _():
        ada = ada_ref[0]
        h = _rms(x_ref[...], gain_ref[...]) * (1.0 + ada[1:2, :]) + ada[0:1, :]
        hb = h.astype(BF16)
        h_ref[...] = hb
        aux = jnp.dot(hb, wa_ref[...], preferred_element_type=F32)
        tm = aux.shape[0]
        lane = lax.broadcasted_iota(jnp.int32, aux.shape, 1)
        is_beta = (lane >= AUX_BETA) & (lane < AUX_G)
        is_g = (lane >= AUX_G) & (lane < AUX_G + DN_HEADS)
        a_log = auxp_ref[0:1, :]
        dt_bias = auxp_ref[1:2, :]
        sp_in = aux + dt_bias
        softplus = jnp.maximum(sp_in, 0.0) + jnp.log(1.0 + jnp.exp(-jnp.abs(sp_in)))
        g = -jnp.exp(a_log) * softplus
        aux = jnp.where(is_beta, jax.nn.sigmoid(aux), jnp.where(is_g, g, aux))
        r = lax.broadcasted_iota(jnp.int32, (LANES, LANES), 0)
        cidx = lax.broadcasted_iota(jnp.int32, (LANES, LANES), 1)
        tri = ((cidx <= r) & (cidx // DN_CHUNK == r // DN_CHUNK)).astype(F32)
        g_cols = (cidx >= AUX_G) & (cidx < AUX_G + DN_HEADS)
        parts = []
        for t in range(tm // LANES):
            blk = aux[t * LANES:(t + 1) * LANES, :]
            cs = jnp.dot(tri, blk, precision=lax.Precision.HIGHEST, preferred_element_type=F32)
            parts.append(jnp.where(g_cols, cs, blk))
        aux = jnp.concatenate(parts, axis=0)
        auxc_ref[...] = aux
        auxr_ref[...] = aux.T

    main_ref[...] = jnp.dot(h_ref[...], wm_ref[...], preferred_element_type=F32).astype(main_ref.dtype)


def _in_proj(x2, ada3, gain, w_main, w_aux, auxp, seq):
    t, d = x2.shape
    n = w_main.shape[1]
    tm = min(512, seq)
    tn = 1024
    per_b = seq // tm
    return pl.pallas_call(
        _in_proj_kernel,
        grid=(t // tm, n // tn),
        in_specs=[pl.BlockSpec((tm, d), lambda i, j: (i, 0)),
                  pl.BlockSpec((1, 6, d), lambda i, j: (i // per_b, 0, 0)),
                  pl.BlockSpec((1, d), lambda i, j: (0, 0)),
                  pl.BlockSpec((d, tn), lambda i, j: (0, j)),
                  pl.BlockSpec((d, LANES), lambda i, j: (0, 0)),
                  pl.BlockSpec((8, LANES), lambda i, j: (0, 0))],
        out_specs=[pl.BlockSpec((tm, tn), lambda i, j: (i, j)),
                   pl.BlockSpec((tm, LANES), lambda i, j: (i, 0)),
                   pl.BlockSpec((LANES, tm), lambda i, j: (0, i))],
        out_shape=[jax.ShapeDtypeStruct((t, n), BF16),
                   jax.ShapeDtypeStruct((t, LANES), F32),
                   jax.ShapeDtypeStruct((LANES, t), F32)],
        scratch_shapes=[pltpu.VMEM((tm, d), BF16)],
        compiler_params=_params(1, 1),
        name="in_proj",
    )(x2, ada3, gain, w_main, w_aux, auxp)


def _unit_lower_inverse(m):
    c = m.shape[0]
    r = lax.broadcasted_iota(jnp.int32, (c, c), 0)
    cc = lax.broadcasted_iota(jnp.int32, (c, c), 1)
    eye = (r == cc).astype(F32)
    same = (r // 16) == (cc // 16)
    md = jnp.where(same, m, 0.0)
    mo = jnp.where(same, 0.0, m)
    p1 = _bdot(md, md)
    p2 = _bdot(p1, p1)
    p3 = _bdot(p2, p2)
    td = eye - md
    td = td + _bdot(td, p1)
    td = td + _bdot(td, p2)
    td = td + _bdot(td, p3)
    n1 = _bdot(td, mo)
    n2 = _bdot(n1, n1)
    left = (eye - n1) + _bdot(eye - n1, n2)
    return _bdot(left, td)


def _delta_kernel(q_ref, k_ref, v_ref, z_ref, auxc_ref, auxr_ref, wq_ref, wk_ref, wv_ref, gain_ref,
                  o_ref, qs_ref, ks_ref, vs_ref, us_ref, ws_ref, as_ref, st_ref):
    head = pl.program_id(1)
    seq = q_ref.shape[0]
    n_pairs = seq // (2 * DN_CHUNK)
    c = DN_CHUNK

    def conv_silu(x_ref, w_ref):
        x = x_ref[...].astype(F32)
        w = w_ref[...]
        row = lax.broadcasted_iota(jnp.int32, x.shape, 0)
        acc = x * w[DN_CONV - 1:DN_CONV, :]
        for s in range(1, DN_CONV):
            xs = jnp.where(row >= s, pltpu.roll(x, s, 0), 0.0)
            acc = acc + xs * w[DN_CONV - 1 - s:DN_CONV - s, :]
        return _silu(acc)

    q = conv_silu(q_ref, wq_ref)
    k = conv_silu(k_ref, wk_ref)
    qs_ref[...] = q * lax.rsqrt(jnp.sum(q * q, axis=-1, keepdims=True) + L2_EPS) * (DN_DIM ** -0.5)
    ks_ref[...] = k * lax.rsqrt(jnp.sum(k * k, axis=-1, keepdims=True) + L2_EPS)
    vs_ref[...] = conv_silu(v_ref, wv_ref)

    lane = lax.broadcasted_iota(jnp.int32, (c, LANES), 1)
    head_row = lax.broadcasted_iota(jnp.int32, (DN_HEADS, 2 * c), 0)
    ri =lax.broadcasted_iota(jnp.int32, (c, c), 0)
    ci = lax.broadcasted_iota(jnp.int32, (c, c), 1)
    causal = ri >= ci
    strict = ri > ci

    def prep_pair(p, carry):
        base = pl.multiple_of(p * 2 * c, 2 * c)
        grows = auxr_ref[pl.ds(AUX_G, DN_HEADS), pl.ds(base, 2 * c)]
        grow2 = jnp.sum(jnp.where(head_row == head, grows, 0.0), axis=0, keepdims=True)
        for half in range(2):
            rows = pl.ds(base + half * c, c)
            auxc = auxc_ref[rows, :]
            beta = jnp.sum(jnp.where(lane == AUX_BETA + head, auxc, 0.0), axis=1, keepdims=True)
            gcol = jnp.sum(jnp.where(lane == AUX_G + head, auxc, 0.0), axis=1, keepdims=True)
            grow = grow2[:, half * c:(half + 1) * c]
            qc = qs_ref[rows, :]
            kc = ks_ref[rows, :]
            vc = vs_ref[rows, :]
            decay = jnp.exp(jnp.where(causal, gcol - grow, -jnp.inf))
            kb = kc * beta
            both = _bdot_nt(jnp.concatenate([kb, qc], axis=0), kc)
            m = jnp.where(strict, both[:c] * decay, 0.0)
            a_intra = jnp.where(causal, both[c:] * decay, 0.0)
            tinv = _unit_lower_inverse(m)
            rhs = jnp.concatenate([vc * beta, kb * jnp.exp(gcol)], axis=1)
            sol = _bdot(tinv, rhs)
            us_ref[rows, :] = sol[:, :DN_DIM]
            ws_ref[rows, :] = sol[:, DN_DIM:]
            as_ref[rows, :] = a_intra
        return carry

    lax.fori_loop(0, n_pairs, prep_pair, 0)

    st_ref[...] = jnp.zeros_like(st_ref)
    gain = gain_ref[...]

    def scan_chunk(i, carry):
        rows = pl.ds(pl.multiple_of(i * c, c), c)
        auxc = auxc_ref[rows, :]
        gcol = jnp.sum(jnp.where(lane == AUX_G + head, auxc, 0.0), axis=1, keepdims=True)
        glast = gcol[c - 1:c, :]
        state = st_ref[...]
        qc = qs_ref[rows, :]
        kc = ks_ref[rows, :]
        v_new = us_ref[rows, :] - _bdot(ws_ref[rows, :], state)
        o = _bdot(qc * jnp.exp(gcol), state) + _bdot(as_ref[rows, :], v_new)
        st_ref[...] = state * jnp.exp(glast) + _bdot_tn(kc * jnp.exp(glast - gcol), v_new)
        z = z_ref[rows, :].astype(F32)
        o_ref[rows, :] = (_rms(o, gain) * _silu(z)).astype(o_ref.dtype)
        return carry

    lax.fori_loop(0, seq // c, scan_chunk, 0)


def _delta(main3, auxc3, auxr, conv_w_t, gain):
    bsz, seq, _ = main3.shape
    hb = DN_WIDTH // DN_DIM

    def col(offset):
        return pl.BlockSpec((None, seq, DN_DIM), lambda b, h: (b, 0, offset * hb + h))

    def wcol(offset):
        return pl.BlockSpec((DN_CONV, DN_DIM), lambda b, h: (0, offset * hb + h))

    per_b = seq // LANES
    del per_b
    return pl.pallas_call(
        _delta_kernel,
        grid=(bsz, DN_HEADS),
        in_specs=[col(0), col(1), col(2), col(3),
                  pl.BlockSpec((None, seq, LANES), lambda b, h: (b, 0, 0)),
                  pl.BlockSpec((LANES, seq), lambda b, h: (0, b)),
                  wcol(0), wcol(1), wcol(2),
                  pl.BlockSpec((1, DN_DIM), lambda b, h: (0, 0))],
        out_specs=pl.BlockSpec((None, seq, DN_DIM), lambda b, h: (b, 0, h)),
        out_shape=jax.ShapeDtypeStruct((bsz, seq, DN_WIDTH), BF16),
        scratch_shapes=[pltpu.VMEM((seq, DN_DIM), F32)] * 5
        + [pltpu.VMEM((seq, DN_CHUNK), F32), pltpu.VMEM((DN_DIM, DN_DIM), F32)],
        compiler_params=_params(2),
        name="delta",
    )(main3, main3, main3, main3, auxc3, auxr, conv_w_t, conv_w_t, conv_w_t, gain)


def _mla_proj_kernel(cq_ref, ckv_ref, auxc_ref, pos_ref, qg_ref, kvg_ref, wq_ref, wkv_ref, rope_ref,
                     q_ref, k_ref, v_ref):
    scale = (MLA_NOPE + MLA_ROPE) ** -0.5
    ang = pos_ref[...].astype(F32) * rope_ref[0:1, :]
    cos_t = jnp.cos(ang) * rope_ref[1:2, :]
    sin_t = jnp.sin(ang) * rope_ref[2:3, :]
    lane = lax.broadcasted_iota(jnp.int32, ang.shape, 1)
    first = lane < MLA_ROPE // 2

    def rope(a):
        swapped = jnp.where(first, pltpu.roll(a, LANES - MLA_ROPE // 2, 1), pltpu.roll(a, MLA_ROPE // 2, 1))
        return a * cos_t + swapped * sin_t

    ql = _bdot(_rms(cq_ref[...].astype(F32), qg_ref[...]), wq_ref[...]) * scale
    kv = _bdot(_rms(ckv_ref[...].astype(F32), kvg_ref[...]), wkv_ref[...])
    kpe = rope(auxc_ref[...]).astype(k_ref.dtype)
    for h in range(MLA_HEADS):
        o = h * MLA_QK_PAD
        q_ref[:, o:o + MLA_NOPE] = ql[:, o:o + MLA_NOPE].astype(q_ref.dtype)
        q_ref[:, o + MLA_NOPE:o + MLA_QK_PAD] = rope(ql[:, o + MLA_NOPE:o + MLA_QK_PAD]).astype(q_ref.dtype)
        k_ref[:, o:o + MLA_NOPE] = kv[:, h * MLA_NOPE:(h + 1) * MLA_NOPE].astype(k_ref.dtype)
        k_ref[:, o + MLA_NOPE:o + MLA_QK_PAD] = kpe
    v_ref[...] = kv[:, MLA_HEADS * MLA_NOPE:].astype(v_ref.dtype)


def _mla_proj(main, auxc, pos, q_gain, kv_gain, wq, wkv, rope_tab, seq):
    t = main.shape[0]
    tm = min(512, seq)
    cq_blk = (4 * DN_WIDTH) // MLA_Q_RANK
    hq = MLA_HEADS * MLA_QK_PAD
    hv = MLA_HEADS * MLA_V
    return pl.pallas_call(
        _mla_proj_kernel,
        grid=(t // tm,),
        in_specs=[pl.BlockSpec((tm, MLA_Q_RANK), lambda i: (i, cq_blk)),
                  pl.BlockSpec((tm, MLA_KV_RANK), lambda i: (i, cq_blk + 1)),
                  pl.BlockSpec((tm, LANES), lambda i: (i, 0)),
                  pl.BlockSpec((tm, 1), lambda i: (i, 0)),
                  pl.BlockSpec((1, MLA_Q_RANK), lambda i: (0, 0)),
                  pl.BlockSpec((1, MLA_KV_RANK), lambda i: (0, 0)),
                  pl.BlockSpec((MLA_Q_RANK, hq), lambda i: (0, 0)),
                  pl.BlockSpec((MLA_KV_RANK, 2 * hv), lambda i: (0, 0)),
                  pl.BlockSpec((8, LANES), lambda i: (0, 0))],
        out_specs=[pl.BlockSpec((tm, hq), lambda i: (i, 0)),
                   pl.BlockSpec((tm, hq), lambda i: (i, 0)),
                   pl.BlockSpec((tm, hv), lambda i: (i, 0))],
        out_shape=[jax.ShapeDtypeStruct((t, hq), BF16),
                   jax.ShapeDtypeStruct((t, hq), BF16),
                   jax.ShapeDtypeStruct((t, hv), BF16)],
        compiler_params=_params(1),
        name="mla_proj",
    )(main, main, auxc, pos, q_gain, kv_gain, wq, wkv, rope_tab)


def _attn_kernel(q_ref, k_ref, v_ref, o_ref, m_ref, l_ref, acc_ref):
    qi = pl.program_id(2)
    tq = q_ref.shape[0]
    q = q_ref[...]
    m_ref[...] = jnp.full_like(m_ref, -jnp.inf)
    l_ref[...] = jnp.zeros_like(l_ref)
    acc_ref[...] = jnp.zeros_like(acc_ref)

    def step(j, masked):
        rows = pl.ds(pl.multiple_of(j * tq, tq), tq)
        s = lax.dot_general(q, k_ref[rows, :], (((1,), (1,)), ((), ())), preferred_element_type=F32)
        if masked:
            r = lax.broadcasted_iota(jnp.int32, s.shape, 0)
            cc = lax.broadcasted_iota(jnp.int32, s.shape, 1)
            s = jnp.where(cc <= r, s, -jnp.inf)
        m_old = m_ref[...]
        m_new = jnp.maximum(m_old, jnp.max(s, axis=-1, keepdims=True))
        alpha = jnp.exp(m_old - m_new)
        p = jnp.exp(s - m_new)
        l_ref[...] = alpha * l_ref[...] + jnp.sum(p, axis=-1, keepdims=True)
        acc_ref[...] = alpha * acc_ref[...] + jnp.dot(p.astype(BF16), v_ref[rows, :],
                                                      preferred_element_type=F32)
        m_ref[...] = m_new

    def body(j, carry):
        step(j, False)
        return carry

    lax.fori_loop(0, qi, body, 0)
    step(qi, True)
    o_ref[...] = (acc_ref[...] / l_ref[...]).astype(o_ref.dtype)


def _attention(q3, k3, v3):
    bsz, seq, _ = q3.shape
    tq = min(512, seq)
    return pl.pallas_call(
        _attn_kernel,
        grid=(bsz, MLA_HEADS, seq // tq),
        in_specs=[pl.BlockSpec((None, tq, MLA_QK_PAD), lambda b, h, i: (b, i, h)),
                  pl.BlockSpec((None, seq, MLA_QK_PAD), lambda b, h, i: (b, 0, h)),
                  pl.BlockSpec((None, seq, MLA_V), lambda b, h, i: (b, 0, h))],
        out_specs=pl.BlockSpec((None, tq, MLA_V), lambda b, h, i: (b, i, h)),
        out_shape=jax.ShapeDtypeStruct((bsz, seq, MLA_HEADS * MLA_V), BF16),
        scratch_shapes=[pltpu.VMEM((tq, 1), F32), pltpu.VMEM((tq, 1), F32), pltpu.VMEM((tq, MLA_V), F32)],
        compiler_params=_params(3),
        name="attention",
    )(q3, k3, v3)


def _out_proj_kernel(x_ref, dn_ref, mla_ref, ada_ref, gain_ref, w_ref, wr_ref, x1_ref, h2_ref, lg_ref):
    ada = ada_ref[0]
    half = dn_ref.shape[1]
    mix = (jnp.dot(dn_ref[...], w_ref[:half, :], preferred_element_type=F32)
           + jnp.dot(mla_ref[...], w_ref[half:, :], preferred_element_type=F32))
    x1 = x_ref[...] + ada[2:3, :] * mix
    x1_ref[...] = x1
    h2 = _rms(x1, gain_ref[...]) * (1.0 + ada[4:5, :]) + ada[3:4, :]
    h2_ref[...] = h2.astype(h2_ref.dtype)
    lg_ref[...] = jnp.dot(h2, wr_ref[...], precision=lax.Precision.HIGHEST, preferred_element_type=F32)


def _out_proj(x2, dn, mla, ada3, gain, w_out, w_router, seq):
    t, d = x2.shape
    tm = min(256, seq)
    per_b = seq // tm
    half = dn.shape[1]
    return pl.pallas_call(
        _out_proj_kernel,
        grid=(t // tm,),
        in_specs=[pl.BlockSpec((tm, d), lambda i: (i, 0)),
                  pl.BlockSpec((tm, half), lambda i: (i, 0)),
                  pl.BlockSpec((tm, half), lambda i: (i, 0)),
                  pl.BlockSpec((1, 6, d), lambda i: (i // per_b, 0, 0)),
                  pl.BlockSpec((1, d), lambda i: (0, 0)),
                  pl.BlockSpec((2 * half, d), lambda i: (0, 0)),
                  pl.BlockSpec((d, LANES), lambda i: (0, 0))],
        out_specs=[pl.BlockSpec((tm, d), lambda i: (i, 0)),
                   pl.BlockSpec((tm, d), lambda i: (i, 0)),
                   pl.BlockSpec((tm, LANES), lambda i: (i, 0))],
        out_shape=[jax.ShapeDtypeStruct((t, d), F32),
                   jax.ShapeDtypeStruct((t, d), BF16),
                   jax.ShapeDtypeStruct((t, LANES), F32)],
        compiler_params=_params(1),
        name="out_proj",
    )(x2, dn, mla, ada3, gain, w_out, w_router)


def _expert_kernel(be_ref, x_ref, w_ref, wgu_ref, wd_ref, y_ref, gu_s, d_s):
    i = pl.program_id(0)
    changed = (i == 0) | (be_ref[i] != be_ref[jnp.maximum(i - 1, 0)])

    @pl.when(changed)
    def _():
        gu_s[...] = wgu_ref[...].astype(BF16)
        d_s[...] = wd_ref[...].astype(BF16)

    gu = jnp.dot(x_ref[...], gu_s[...], preferred_element_type=F32)
    ff = gu.shape[1] // 2
    act = _silu(gu[:, :ff]) * gu[:, ff:]
    y = jnp.dot(act.astype(BF16), d_s[...], preferred_element_type=F32)
    y_ref[...] = (y * w_ref[...]).astype(y_ref.dtype)


def _experts(block_e, x_sorted, row_w, w_gu, w_d):
    n_rows, d = x_sorted.shape
    ff2 = w_gu.shape[2]
    n_blocks = n_rows // EXPERT_BLOCK
    grid_spec = pltpu.PrefetchScalarGridSpec(
        num_scalar_prefetch=1,
        grid=(n_blocks,),
        in_specs=[pl.BlockSpec((EXPERT_BLOCK, d), lambda i, be: (i, 0)),
                  pl.BlockSpec((EXPERT_BLOCK, 1), lambda i, be: (i, 0)),
                  pl.BlockSpec((None, d, ff2), lambda i, be: (be[i], 0, 0)),
                  pl.BlockSpec((None, ff2 // 2, d), lambda i, be: (be[i], 0, 0))],
        out_specs=pl.BlockSpec((EXPERT_BLOCK, d), lambda i, be: (i, 0)),
        scratch_shapes=[pltpu.VMEM((d, ff2), BF16), pltpu.VMEM((ff2 // 2, d), BF16)],
    )
    return pl.pallas_call(
        _expert_kernel,
        grid_spec=grid_spec,
        out_shape=jax.ShapeDtypeStruct((n_rows, d), BF16),
        compiler_params=_params(0, 1),
        name="experts",
    )(block_e, x_sorted, row_w, w_gu, w_d)


def _final_kernel(x1_ref, h2_ref, routed_ref, ada_ref, wgu_ref, wd_ref, gain_ref, o_ref):
    ada = ada_ref[0]
    gu = jnp.dot(h2_ref[...], wgu_ref[...], preferred_element_type=F32)
    ff = gu.shape[1] // 2
    act = _silu(gu[:, :ff]) * gu[:, ff:]
    shared = jnp.dot(act.astype(BF16), wd_ref[...], preferred_element_type=F32)
    x2 = x1_ref[...] + ada[5:6, :] * (routed_ref[...] + shared)
    o_ref[...] = _rms(x2, gain_ref[...])


def _final(x1, h2, routed, ada3, w_gu, w_d, gain, seq):
    t, d = x1.shape
    tm = min(256, seq)
    per_b = seq // tm
    ff2 = w_gu.shape[1]
    return pl.pallas_call(
        _final_kernel,
        grid=(t // tm,),
        in_specs=[pl.BlockSpec((tm, d), lambda i: (i, 0)),
                  pl.BlockSpec((tm, d), lambda i: (i, 0)),
                  pl.BlockSpec((tm, d), lambda i: (i, 0)),
                  pl.BlockSpec((1, 6, d), lambda i: (i // per_b, 0, 0)),
                  pl.BlockSpec((d, ff2), lambda i: (0, 0)),
                  pl.BlockSpec((ff2 // 2, d), lambda i: (0, 0)),
                  pl.BlockSpec((1, d), lambda i: (0, 0))],
        out_specs=pl.BlockSpec((tm, d), lambda i: (i, 0)),
        out_shape=jax.ShapeDtypeStruct((t, d), F32),
        compiler_params=_params(1),
        name="final",
    )(x1, h2, routed, ada3, w_gu, w_d, gain)


def _route(logits, router_bias, n_tok):
    scores = jax.nn.sigmoid(logits)
    biased = scores + router_bias.astype(F32)
    per_group = N_EXPERTS // N_GROUPS
    group_score = lax.top_k(biased.reshape(n_tok, N_GROUPS, per_group), 2)[0].sum(-1)
    _, top_groups = lax.top_k(group_score, TOPK_GROUPS)
    group_mask = jnp.any(top_groups[:, :, None] == jnp.arange(N_GROUPS)[None, None, :], axis=1)
    expert_mask = jnp.repeat(group_mask, per_group, axis=-1)
    _, top_idx = lax.top_k(jnp.where(expert_mask, biased, -jnp.inf), TOP_K)
    top_w = jnp.take_along_axis(scores, top_idx, axis=-1)
    top_w = top_w / jnp.sum(top_w, axis=-1, keepdims=True) * ROUTED_SCALE
    n_assign = n_tok * TOP_K
    flat_e = top_idx.reshape(n_assign)
    order = jnp.argsort(flat_e)
    sorted_e = flat_e[order]
    counts = jnp.bincount(flat_e, length=N_EXPERTS)
    padded = (counts + EXPERT_BLOCK - 1) // EXPERT_BLOCK * EXPERT_BLOCK
    pad_end = jnp.cumsum(padded)
    start = jnp.cumsum(counts) - counts
    dest = ((pad_end - padded)[sorted_e] + jnp.arange(n_assign) - start[sorted_e]).astype(jnp.int32)
    n_rows = -(-(n_assign + N_EXPERTS * (EXPERT_BLOCK - 1)) // EXPERT_BLOCK) * EXPERT_BLOCK
    n_blocks = n_rows // EXPERT_BLOCK
    row_tok = jnp.zeros((n_rows,), jnp.int32).at[dest].set((order // TOP_K).astype(jnp.int32))
    row_w = jnp.zeros((n_rows,), F32).at[dest].set(top_w.reshape(n_assign)[order])
    block_e = jnp.minimum(jnp.searchsorted(pad_end, jnp.arange(n_blocks) * EXPERT_BLOCK, side='right'),
                          N_EXPERTS - 1).astype(jnp.int32)
    slot = jnp.zeros((n_assign,), jnp.int32).at[order].set(dest).reshape(n_tok, TOP_K)
    return row_tok, row_w, block_e, slot


def _layer(x, cond_ada, positions, w_in, dn_conv_w, dn_a_log, dn_dt_bias, dn_norm_gain, mla_q_norm_gain,
           w_q_up, mla_kv_norm_gain, w_kv_up, w_out, norm1_gain, norm2_gain, w_router, router_bias,
           w_exp_gate_up, w_exp_down, w_sh_gate_up, w_sh_down, out_gain):
    bsz, seq, d = x.shape
    t = bsz * seq
    x2 = x.reshape(t, d)
    ada3 = cond_ada.reshape(bsz, 6, d)

    s_z = 4 * DN_WIDTH
    s_a = s_z + 2 * DN_HEADS
    s_kpe = s_a + MLA_Q_RANK + MLA_KV_RANK
    w_main = jnp.concatenate([w_in[:, :s_z], w_in[:, s_a:s_kpe]], axis=1).astype(BF16)
    w_aux = jnp.concatenate([w_in[:, s_kpe:], w_in[:, s_z:s_a],
                             jnp.zeros((d, LANES - MLA_ROPE - 2 * DN_HEADS), F32)], axis=1).astype(BF16)
    auxp = jnp.zeros((8, LANES), F32)
    auxp = auxp.at[0, AUX_G:AUX_G + DN_HEADS].set(dn_a_log).at[1, AUX_G:AUX_G + DN_HEADS].set(dn_dt_bias)
    main, auxc, auxr = _in_proj(x2, ada3, norm1_gain.reshape(1, d), w_main, w_aux, auxp, seq)

    dn = _delta(main.reshape(bsz, seq, MAIN_WIDTH), auxc.reshape(bsz, seq, LANES), auxr,
                dn_conv_w.T, dn_norm_gain.reshape(1, DN_DIM))

    qk = MLA_NOPE + MLA_ROPE
    wq3 = w_q_up.reshape(MLA_Q_RANK, MLA_HEADS, qk)
    wq = jnp.concatenate([wq3, jnp.zeros((MLA_Q_RANK, MLA_HEADS, MLA_QK_PAD - qk), F32)], axis=2)
    wq = wq.reshape(MLA_Q_RANK, MLA_HEADS * MLA_QK_PAD).astype(BF16)
    wkv3 = w_kv_up.reshape(MLA_KV_RANK, MLA_HEADS, MLA_NOPE + MLA_V)
    wkv = jnp.concatenate([wkv3[:, :, :MLA_NOPE].reshape(MLA_KV_RANK, -1),
                           wkv3[:, :, MLA_NOPE:].reshape(MLA_KV_RANK, -1)], axis=1).astype(BF16)
    half = MLA_ROPE // 2
    inv_freq = ROPE_THETA ** (-jnp.arange(half, dtype=F32) / half)
    zeros = jnp.zeros((LANES - MLA_ROPE,), F32)
    rope_tab = jnp.zeros((8, LANES), F32)
    rope_tab = rope_tab.at[0].set(jnp.concatenate([inv_freq, inv_freq, zeros]))
    rope_tab = rope_tab.at[1].set(jnp.concatenate([jnp.ones((MLA_ROPE,), F32), zeros]))
    rope_tab = rope_tab.at[2].set(jnp.concatenate([-jnp.ones((half,), F32), jnp.ones((half,), F32), zeros]))
    q, k, v = _mla_proj(main, auxc, positions.reshape(t, 1), mla_q_norm_gain.reshape(1, -1),
                        mla_kv_norm_gain.reshape(1, -1), wq, wkv, rope_tab, seq)
    mla = _attention(q.reshape(bsz, seq, -1), k.reshape(bsz, seq, -1), v.reshape(bsz, seq, -1))

    w_r = jnp.concatenate([w_router, jnp.zeros((d, LANES - N_EXPERTS), F32)], axis=1)
    x1, h2, logits = _out_proj(x2, dn.reshape(t, DN_WIDTH), mla.reshape(t, -1), ada3,
                               norm2_gain.reshape(1, d), w_out.astype(BF16), w_r, seq)

    row_tok, row_w, block_e, slot = _route(logits[:, :N_EXPERTS], router_bias, t)
    x_sorted = jnp.take(h2, row_tok, axis=0)
    y_sorted = _experts(block_e, x_sorted, row_w.reshape(-1, 1), w_exp_gate_up, w_exp_down)
    routed = jnp.take(y_sorted, slot, axis=0).astype(F32).sum(axis=1)

    out = _final(x1, h2, routed, ada3, w_sh_gate_up.astype(BF16), w_sh_down.astype(BF16),
                 out_gain.reshape(1, d), seq)
    return out.reshape(bsz, seq, d)


def kernel(x, c, positions, w_ada, b_ada, norm1_gain, w_in, dn_conv_w, dn_a_log, dn_dt_bias, dn_norm_gain,
           mla_q_norm_gain, w_q_up, mla_kv_norm_gain, w_kv_up, w_out, norm2_gain, w_router, router_bias,
           w_exp_gate_up, w_exp_down, w_sh_gate_up, w_sh_down, final_norm_gain):
    depth = w_ada.shape[0]
    assert depth == 1, "the final RMSNorm is fused into the single layer's last kernel"
    ada = _ada(c, w_ada[0], b_ada[0])
    return _layer(x, ada, positions, w_in[0], dn_conv_w[0], dn_a_log[0], dn_dt_bias[0], dn_norm_gain[0],
                  mla_q_norm_gain[0], w_q_up[0], mla_kv_norm_gain[0], w_kv_up[0], w_out[0], norm1_gain[0],
                  norm2_gain[0], w_router[0], router_bias[0], w_exp_gate_up[0], w_exp_down[0],
                  w_sh_gate_up[0], w_sh_down[0], final_norm_gain)
```

```python
import functools

import jax
import jax.numpy as jnp
from jax import lax
from jax.experimental import pallas as pl
from jax.experimental.pallas import tpu as pltpu

F32 = jnp.float32
BF16 = jnp.bfloat16

DN_HEADS = 8
DN_DIM = 128
DN_WIDTH = DN_HEADS * DN_DIM
DN_CONV = 4
DN_CHUNK = 64
MLA_HEADS = 8
MLA_Q_RANK = 512
MLA_KV_RANK = 512
MLA_NOPE = 128
MLA_ROPE = 64
MLA_V = 128
MLA_QK_PAD = 256
ROPE_THETA = 10000.0
N_EXPERTS = 64
TOP_K = 6
N_GROUPS = 8
TOPK_GROUPS = 4
EXPERT_FF = 512
ROUTED_SCALE = 2.5
EXPERT_BLOCK = 256
RMS_EPS = 1e-6
L2_EPS = 1e-6

LANES = 128
MAIN_WIDTH = 4 * DN_WIDTH + MLA_Q_RANK + MLA_KV_RANK
AUX_BETA = MLA_ROPE
AUX_G = MLA_ROPE + DN_HEADS
VMEM_LIMIT = 56 * 1024 * 1024


def _params(n_parallel, n_arbitrary=0):
    sem = ("parallel",) * n_parallel + ("arbitrary",) * n_arbitrary
    return pltpu.CompilerParams(dimension_semantics=sem, vmem_limit_bytes=VMEM_LIMIT)


def _silu(x):
    return x * jax.nn.sigmoid(x)


def _bdot(a, b):
    return jnp.dot(a.astype(BF16), b.astype(BF16), preferred_element_type=F32)


def _bdot_nt(a, b):
    return lax.dot_general(a.astype(BF16), b.astype(BF16), (((1,), (1,)), ((), ())),
                           preferred_element_type=F32)


def _bdot_tn(a, b):
    return lax.dot_general(a.astype(BF16), b.astype(BF16), (((0,), (0,)), ((), ())),
                           preferred_element_type=F32)


def _rms(x, gain):
    return x * lax.rsqrt(jnp.mean(x * x, axis=-1, keepdims=True) + RMS_EPS) * gain


def _ada_kernel(c_ref, w_ref, b_ref, o_ref):
    o_ref[...] = _bdot(_silu(c_ref[...]), w_ref[...]) + b_ref[...]


def _ada(c, w_ada, b_ada):
    bsz, d = c.shape
    n = w_ada.shape[1]
    tn = 1024
    return pl.pallas_call(
        _ada_kernel,
        grid=(n // tn,),
        in_specs=[pl.BlockSpec((bsz, d), lambda j: (0, 0)),
                  pl.BlockSpec((d, tn), lambda j: (0, j)),
                  pl.BlockSpec((1, tn), lambda j: (0, j))],
        out_specs=pl.BlockSpec((bsz, tn), lambda j: (0, j)),
        out_shape=jax.ShapeDtypeStruct((bsz, n), F32),
        compiler_params=_params(1),
        name="ada",
    )(c, w_ada, b_ada.reshape(1, n))


def _in_proj_kernel(x_ref, ada_ref, gain_ref, wm_ref, wa_ref, auxp_ref,
                    main_ref, auxc_ref, auxr_ref, h_ref):
    j = pl.program_id(1)

    @pl.when(j == 0)
    def _():
        ada = ada_ref[0]
        h = _rms(x_ref[...], gain_ref[...]) * (1.0 + ada[1:2, :]) + ada[0:1, :]
        hb = h.astype(BF16)
        h_ref[...] = hb
        aux = jnp.dot(hb, wa_ref[...], preferred_element_type=F32)
        tm = aux.shape[0]
        lane = lax.broadcasted_iota(jnp.int32, aux.shape, 1)
        is_beta = (lane >= AUX_BETA) & (lane < AUX_G)
        is_g = (lane >= AUX_G) & (lane < AUX_G + DN_HEADS)
        a_log = auxp_ref[0:1, :]
        dt_bias = auxp_ref[1:2, :]
        sp_in = aux + dt_bias
        softplus = jnp.maximum(sp_in, 0.0) + jnp.log(1.0 + jnp.exp(-jnp.abs(sp_in)))
        g = -jnp.exp(a_log) * softplus
        aux = jnp.where(is_beta, jax.nn.sigmoid(aux), jnp.where(is_g, g, aux))
        r = lax.broadcasted_iota(jnp.int32, (LANES, LANES), 0)
        cidx = lax.broadcasted_iota(jnp.int32, (LANES, LANES), 1)
        tri = ((cidx <= r) & (cidx // DN_CHUNK == r // DN_CHUNK)).astype(F32)
        g_cols = (cidx >= AUX_G) & (cidx < AUX_G + DN_HEADS)
        parts = []
        for t in range(tm // LANES):
            blk = aux[t * LANES:(t + 1) * LANES, :]
            cs = jnp.dot(tri, blk, precision=lax.Precision.HIGHEST, preferred_element_type=F32)
            parts.append(jnp.where(g_cols, cs, blk))
        aux = jnp.concatenate(parts, axis=0)
        auxc_ref[...] = aux
        auxr_ref[...] = aux.T

    main_ref[...] = jnp.dot(h_ref[...], wm_ref[...], preferred_element_type=F32).astype(main_ref.dtype)


def _in_proj(x2, ada3, gain, w_main, w_aux, auxp, seq):
    t, d = x2.shape
    n = w_main.shape[1]
    tm = min(512, seq)
    tn = 1024
    per_b = seq // tm
    return pl.pallas_call(
        _in_proj_kernel,
        grid=(t // tm, n // tn),
        in_specs=[pl.BlockSpec((tm, d), lambda i, j: (i, 0)),
                  pl.BlockSpec((1, 6, d), lambda i, j: (i // per_b, 0, 0)),
                  pl.BlockSpec((1, d), lambda i, j: (0, 0)),
                  pl.BlockSpec((d, tn), lambda i, j: (0, j)),
                  pl.BlockSpec((d, LANES), lambda i, j: (0, 0)),
                  pl.BlockSpec((8, LANES), lambda i, j: (0, 0))],
        out_specs=[pl.BlockSpec((tm, tn), lambda i, j: (i, j)),
                   pl.BlockSpec((tm, LANES), lambda i, j: (i, 0)),
                   pl.BlockSpec((LANES, tm), lambda i, j: (0, i))],
        out_shape=[jax.ShapeDtypeStruct((t, n), BF16),
                   jax.ShapeDtypeStruct((t, LANES), F32),
                   jax.ShapeDtypeStruct((LANES, t), F32)],
        scratch_shapes=[pltpu.VMEM((tm, d), BF16)],
        compiler_params=_params(1, 1),
        name="in_proj",
    )(x2, ada3, gain, w_main, w_aux, auxp)


def _unit_lower_inverse(m):
    c = m.shape[0]
    r = lax.broadcasted_iota(jnp.int32, (c, c), 0)
    cc = lax.broadcasted_iota(jnp.int32, (c, c), 1)
    eye = (r == cc).astype(F32)
    same = (r // 16) == (cc // 16)
    md = jnp.where(same, m, 0.0)
    mo = jnp.where(same, 0.0, m)
    p1 = _bdot(md, md)
    p2 = _bdot(p1, p1)
    p3 = _bdot(p2, p2)
    td = eye - md
    td = td + _bdot(td, p1)
    td = td + _bdot(td, p2)
    td = td + _bdot(td, p3)
    n1 = _bdot(td, mo)
    n2 = _bdot(n1, n1)
    left = (eye - n1) + _bdot(eye - n1, n2)
    return _bdot(left, td)


def _delta_kernel(q_ref, k_ref, v_ref, z_ref, auxc_ref, auxr_ref, wq_ref, wk_ref, wv_ref, gain_ref,
                  o_ref, qs_ref, ks_ref, vs_ref, us_ref, ws_ref, as_ref, st_ref):
    head = pl.program_id(1)
    seq = q_ref.shape[0]
    n_pairs = seq // (2 * DN_CHUNK)
    c = DN_CHUNK

    def conv_silu(x_ref, w_ref):
        x = x_ref[...].astype(F32)
        w = w_ref[...]
        row = lax.broadcasted_iota(jnp.int32, x.shape, 0)
        acc = x * w[DN_CONV - 1:DN_CONV, :]
        for s in range(1, DN_CONV):
            xs = jnp.where(row >= s, pltpu.roll(x, s, 0), 0.0)
            acc = acc + xs * w[DN_CONV - 1 - s:DN_CONV - s, :]
        return _silu(acc)

    q = conv_silu(q_ref, wq_ref)
    k = conv_silu(k_ref, wk_ref)
    qs_ref[...] = q * lax.rsqrt(jnp.sum(q * q, axis=-1, keepdims=True) + L2_EPS) * (DN_DIM ** -0.5)
    ks_ref[...] = k * lax.rsqrt(jnp.sum(k * k, axis=-1, keepdims=True) + L2_EPS)
    vs_ref[...] = conv_silu(v_ref, wv_ref)

    lane = lax.broadcasted_iota(jnp.int32, (c, LANES), 1)
    head_row = lax.broadcasted_iota(jnp.int32, (DN_HEADS, 2 * c), 0)
    ri =lax.broadcasted_iota(jnp.int32, (c, c), 0)
    ci = lax.broadcasted_iota(jnp.int32, (c, c), 1)
    causal = ri >= ci
    strict = ri > ci

    def prep_pair(p, carry):
        base = pl.multiple_of(p * 2 * c, 2 * c)
        grows = auxr_ref[pl.ds(AUX_G, DN_HEADS), pl.ds(base, 2 * c)]
        grow2 = jnp.sum(jnp.where(head_row == head, grows, 0.0), axis=0, keepdims=True)
        for half in range(2):
            rows = pl.ds(base + half * c, c)
            auxc = auxc_ref[rows, :]
            beta = jnp.sum(jnp.where(lane == AUX_BETA + head, auxc, 0.0), axis=1, keepdims=True)
            gcol = jnp.sum(jnp.where(lane == AUX_G + head, auxc, 0.0), axis=1, keepdims=True)
            grow = grow2[:, half * c:(half + 1) * c]
            qc = qs_ref[rows, :]
            kc = ks_ref[rows, :]
            vc = vs_ref[rows, :]
            decay = jnp.exp(jnp.where(causal, gcol - grow, -jnp.inf))
            kb = kc * beta
            both = _bdot_nt(jnp.concatenate([kb, qc], axis=0), kc)
            m = jnp.where(strict, both[:c] * decay, 0.0)
            a_intra = jnp.where(causal, both[c:] * decay, 0.0)
            tinv = _unit_lower_inverse(m)
            rhs = jnp.concatenate([vc * beta, kb * jnp.exp(gcol)], axis=1)
            sol = _bdot(tinv, rhs)
            us_ref[rows, :] = sol[:, :DN_DIM]
            ws_ref[rows, :] = sol[:, DN_DIM:]
            as_ref[rows, :] = a_intra
        return carry

    lax.fori_loop(0, n_pairs, prep_pair, 0)

    st_ref[...] = jnp.zeros_like(st_ref)
    gain = gain_ref[...]

    def scan_chunk(i, carry):
        rows = pl.ds(pl.multiple_of(i * c, c), c)
        auxc = auxc_ref[rows, :]
        gcol = jnp.sum(jnp.where(lane == AUX_G + head, auxc, 0.0), axis=1, keepdims=True)
        glast = gcol[c - 1:c, :]
        state = st_ref[...]
        qc = qs_ref[rows, :]
        kc = ks_ref[rows, :]
        v_new = us_ref[rows, :] - _bdot(ws_ref[rows, :], state)
        o = _bdot(qc * jnp.exp(gcol), state) + _bdot(as_ref[rows, :], v_new)
        st_ref[...] = state * jnp.exp(glast) + _bdot_tn(kc * jnp.exp(glast - gcol), v_new)
        z = z_ref[rows, :].astype(F32)
        o_ref[rows, :] = (_rms(o, gain) * _silu(z)).astype(o_ref.dtype)
        return carry

    lax.fori_loop(0, seq // c, scan_chunk, 0)


def _delta(main3, auxc3, auxr, conv_w_t, gain):
    bsz, seq, _ = main3.shape
    hb = DN_WIDTH // DN_DIM

    def col(offset):
        return pl.BlockSpec((None, seq, DN_DIM), lambda b, h: (b, 0, offset * hb + h))

    def wcol(offset):
        return pl.BlockSpec((DN_CONV, DN_DIM), lambda b, h: (0, offset * hb + h))

    per_b = seq // LANES
    del per_b
    return pl.pallas_call(
        _delta_kernel,
        grid=(bsz, DN_HEADS),
        in_specs=[col(0), col(1), col(2), col(3),
                  pl.BlockSpec((None, seq, LANES), lambda b, h: (b, 0, 0)),
                  pl.BlockSpec((LANES, seq), lambda b, h: (0, b)),
                  wcol(0), wcol(1), wcol(2),
                  pl.BlockSpec((1, DN_DIM), lambda b, h: (0, 0))],
        out_specs=pl.BlockSpec((None, seq, DN_DIM), lambda b, h: (b, 0, h)),
        out_shape=jax.ShapeDtypeStruct((bsz, seq, DN_WIDTH), BF16),
        scratch_shapes=[pltpu.VMEM((seq, DN_DIM), F32)] * 5
        + [pltpu.VMEM((seq, DN_CHUNK), F32), pltpu.VMEM((DN_DIM, DN_DIM), F32)],
        compiler_params=_params(2),
        name="delta",
    )(main3, main3, main3, main3, auxc3, auxr, conv_w_t, conv_w_t, conv_w_t, gain)


def _mla_proj_kernel(cq_ref, ckv_ref, auxc_ref, pos_ref, qg_ref, kvg_ref, wq_ref, wkv_ref, rope_ref,
                     q_ref, k_ref, v_ref):
    scale = (MLA_NOPE + MLA_ROPE) ** -0.5
    ang = pos_ref[...].astype(F32) * rope_ref[0:1, :]
    cos_t = jnp.cos(ang) * rope_ref[1:2, :]
    sin_t = jnp.sin(ang) * rope_ref[2:3, :]
    lane = lax.broadcasted_iota(jnp.int32, ang.shape, 1)
    first = lane < MLA_ROPE // 2

    def rope(a):
        swapped = jnp.where(first, pltpu.roll(a, LANES - MLA_ROPE // 2, 1), pltpu.roll(a, MLA_ROPE // 2, 1))
        return a * cos_t + swapped * sin_t

    ql = _bdot(_rms(cq_ref[...].astype(F32), qg_ref[...]), wq_ref[...]) * scale
    kv = _bdot(_rms(ckv_ref[...].astype(F32), kvg_ref[...]), wkv_ref[...])
    kpe = rope(auxc_ref[...]).astype(k_ref.dtype)
    for h in range(MLA_HEADS):
        o = h * MLA_QK_PAD
        q_ref[:, o:o + MLA_NOPE] = ql[:, o:o + MLA_NOPE].astype(q_ref.dtype)
        q_ref[:, o + MLA_NOPE:o + MLA_QK_PAD] = rope(ql[:, o + MLA_NOPE:o + MLA_QK_PAD]).astype(q_ref.dtype)
        k_ref[:, o:o + MLA_NOPE] = kv[:, h * MLA_NOPE:(h + 1) * MLA_NOPE].astype(k_ref.dtype)
        k_ref[:, o + MLA_NOPE:o + MLA_QK_PAD] = kpe
    v_ref[...] = kv[:, MLA_HEADS * MLA_NOPE:].astype(v_ref.dtype)


def _mla_proj(main, auxc, pos, q_gain, kv_gain, wq, wkv, rope_tab, seq):
    t = main.shape[0]
    tm = min(512, seq)
    cq_blk = (4 * DN_WIDTH) // MLA_Q_RANK
    hq = MLA_HEADS * MLA_QK_PAD
    hv = MLA_HEADS * MLA_V
    return pl.pallas_call(
        _mla_proj_kernel,
        grid=(t // tm,),
        in_specs=[pl.BlockSpec((tm, MLA_Q_RANK), lambda i: (i, cq_blk)),
                  pl.BlockSpec((tm, MLA_KV_RANK), lambda i: (i, cq_blk + 1)),
                  pl.BlockSpec((tm, LANES), lambda i: (i, 0)),
                  pl.BlockSpec((tm, 1), lambda i: (i, 0)),
                  pl.BlockSpec((1, MLA_Q_RANK), lambda i: (0, 0)),
                  pl.BlockSpec((1, MLA_KV_RANK), lambda i: (0, 0)),
                  pl.BlockSpec((MLA_Q_RANK, hq), lambda i: (0, 0)),
                  pl.BlockSpec((MLA_KV_RANK, 2 * hv), lambda i: (0, 0)),
                  pl.BlockSpec((8, LANES), lambda i: (0, 0))],
        out_specs=[pl.BlockSpec((tm, hq), lambda i: (i, 0)),
                   pl.BlockSpec((tm, hq), lambda i: (i, 0)),
                   pl.BlockSpec((tm, hv), lambda i: (i, 0))],
        out_shape=[jax.ShapeDtypeStruct((t, hq), BF16),
                   jax.ShapeDtypeStruct((t, hq), BF16),
                   jax.ShapeDtypeStruct((t, hv), BF16)],
        compiler_params=_params(1),
        name="mla_proj",
    )(main, main, auxc, pos, q_gain, kv_gain, wq, wkv, rope_tab)


def _attn_kernel(q_ref, k_ref, v_ref, o_ref, m_ref, l_ref, acc_ref):
    qi = pl.program_id(2)
    tq = q_ref.shape[0]
    q = q_ref[...]
    m_ref[...] = jnp.full_like(m_ref, -jnp.inf)
    l_ref[...] = jnp.zeros_like(l_ref)
    acc_ref[...] = jnp.zeros_like(acc_ref)

    def step(j, masked):
        rows = pl.ds(pl.multiple_of(j * tq, tq), tq)
        s = lax.dot_general(q, k_ref[rows, :], (((1,), (1,)), ((), ())), preferred_element_type=F32)
        if masked:
            r = lax.broadcasted_iota(jnp.int32, s.shape, 0)
            cc = lax.broadcasted_iota(jnp.int32, s.shape, 1)
            s = jnp.where(cc <= r, s, -jnp.inf)
        m_old = m_ref[...]
        m_new = jnp.maximum(m_old, jnp.max(s, axis=-1, keepdims=True))
        alpha = jnp.exp(m_old - m_new)
        p = jnp.exp(s - m_new)
        l_ref[...] = alpha * l_ref[...] + jnp.sum(p, axis=-1, keepdims=True)
        acc_ref[...] = alpha * acc_ref[...] + jnp.dot(p.astype(BF16), v_ref[rows, :],
                                                      preferred_element_type=F32)
        m_ref[...] = m_new

    def body(j, carry):
        step(j, False)
        return carry

    lax.fori_loop(0, qi, body, 0)
    step(qi, True)
    o_ref[...] = (acc_ref[...] / l_ref[...]).astype(o_ref.dtype)


def _attention(q3, k3, v3):
    bsz, seq, _ = q3.shape
    tq = min(512, seq)
    return pl.pallas_call(
        _attn_kernel,
        grid=(bsz, MLA_HEADS, seq // tq),
        in_specs=[pl.BlockSpec((None, tq, MLA_QK_PAD), lambda b, h, i: (b, i, h)),
                  pl.BlockSpec((None, seq, MLA_QK_PAD), lambda b, h, i: (b, 0, h)),
                  pl.BlockSpec((None, seq, MLA_V), lambda b, h, i: (b, 0, h))],
        out_specs=pl.BlockSpec((None, tq, MLA_V), lambda b, h, i: (b, i, h)),
        out_shape=jax.ShapeDtypeStruct((bsz, seq, MLA_HEADS * MLA_V), BF16),
        scratch_shapes=[pltpu.VMEM((tq, 1), F32), pltpu.VMEM((tq, 1), F32), pltpu.VMEM((tq, MLA_V), F32)],
        compiler_params=_params(3),
        name="attention",
    )(q3, k3, v3)


def _out_proj_kernel(x_ref, dn_ref, mla_ref, ada_ref, gain_ref, w_ref, wr_ref, x1_ref, h2_ref, lg_ref):
    ada = ada_ref[0]
    half = dn_ref.shape[1]
    mix = (jnp.dot(dn_ref[...], w_ref[:half, :], preferred_element_type=F32)
           + jnp.dot(mla_ref[...], w_ref[half:, :], preferred_element_type=F32))
    x1 = x_ref[...] + ada[2:3, :] * mix
    x1_ref[...] = x1
    h2 = _rms(x1, gain_ref[...]) * (1.0 + ada[4:5, :]) + ada[3:4, :]
    h2_ref[...] = h2.astype(h2_ref.dtype)
    lg_ref[...] = jnp.dot(h2, wr_ref[...], precision=lax.Precision.HIGHEST, preferred_element_type=F32).T


def _out_proj(x2, dn, mla, ada3, gain, w_out, w_router, seq):
    t, d = x2.shape
    tm = min(256, seq)
    per_b = seq // tm
    half = dn.shape[1]
    return pl.pallas_call(
        _out_proj_kernel,
        grid=(t // tm,),
        in_specs=[pl.BlockSpec((tm, d), lambda i: (i, 0)),
                  pl.BlockSpec((tm, half), lambda i: (i, 0)),
                  pl.BlockSpec((tm, half), lambda i: (i, 0)),
                  pl.BlockSpec((1, 6, d), lambda i: (i // per_b, 0, 0)),
                  pl.BlockSpec((1, d), lambda i: (0, 0)),
                  pl.BlockSpec((2 * half, d), lambda i: (0, 0)),
                  pl.BlockSpec((d, LANES), lambda i: (0, 0))],
        out_specs=[pl.BlockSpec((tm, d), lambda i: (i, 0)),
                   pl.BlockSpec((tm, d), lambda i: (i, 0)),
                   pl.BlockSpec((LANES, tm), lambda i: (0, i))],
        out_shape=[jax.ShapeDtypeStruct((t, d), F32),
                   jax.ShapeDtypeStruct((t, d), BF16),
                   jax.ShapeDtypeStruct((LANES, t), F32)],
        compiler_params=_params(1),
        name="out_proj",
    )(x2, dn, mla, ada3, gain, w_out, w_router)


def _expert_kernel(be_ref, nv_ref, x_ref, wgu_ref, wd_ref, y_ref, gu_s, d_s):
    i = pl.program_id(0)
    changed = (i == 0) | (be_ref[i] != be_ref[jnp.maximum(i - 1, 0)])

    @pl.when(changed)
    def _():
        gu_s[...] = wgu_ref[...].astype(BF16)
        d_s[...] = wd_ref[...].astype(BF16)

    gu = jnp.dot(x_ref[...], gu_s[...], preferred_element_type=F32)
    ff = gu.shape[1] // 2
    act = _silu(gu[:, :ff]) * gu[:, ff:]
    y = jnp.dot(act.astype(BF16), d_s[...], preferred_element_type=F32)
    row = lax.broadcasted_iota(jnp.int32, y.shape, 0)
    y_ref[...] = jnp.where(row < nv_ref[i], y, 0.0).astype(y_ref.dtype)


def _experts(block_e, n_valid, x_sorted, w_gu, w_d):
    n_rows, d = x_sorted.shape
    ff2 = w_gu.shape[2]
    n_blocks = n_rows // EXPERT_BLOCK
    grid_spec = pltpu.PrefetchScalarGridSpec(
        num_scalar_prefetch=2,
        grid=(n_blocks,),
        in_specs=[pl.BlockSpec((EXPERT_BLOCK, d), lambda i, be, nv: (i, 0)),
                  pl.BlockSpec((None, d, ff2), lambda i, be, nv: (be[i], 0, 0)),
                  pl.BlockSpec((None, ff2 // 2, d), lambda i, be, nv: (be[i], 0, 0))],
        out_specs=pl.BlockSpec((EXPERT_BLOCK, d), lambda i, be, nv: (i, 0)),
        scratch_shapes=[pltpu.VMEM((d, ff2), BF16), pltpu.VMEM((ff2 // 2, d), BF16)],
    )
    return pl.pallas_call(
        _expert_kernel,
        grid_spec=grid_spec,
        out_shape=jax.ShapeDtypeStruct((n_rows, d), BF16),
        compiler_params=_params(0, 1),
        name="experts",
    )(block_e, n_valid, x_sorted, w_gu, w_d)


def _final_kernel(x1_ref, h2_ref, y_ref, wc_ref, ada_ref, wgu_ref, wd_ref, gain_ref, o_ref):
    ada = ada_ref[0]
    gu = jnp.dot(h2_ref[...], wgu_ref[...], preferred_element_type=F32)
    ff = gu.shape[1] // 2
    act = _silu(gu[:, :ff]) * gu[:, ff:]
    ffn = jnp.dot(act.astype(BF16), wd_ref[...], preferred_element_type=F32)
    wc = wc_ref[...]
    for kk in range(TOP_K):
        ffn = ffn + wc[:, kk:kk + 1] * y_ref[kk].astype(F32)
    x2 = x1_ref[...] + ada[5:6, :] * ffn
    o_ref[...] = _rms(x2, gain_ref[...])


def _final(x1, h2, y_tok, wc, ada3, w_gu, w_d, gain, seq):
    t, d = x1.shape
    tm = min(256, seq)
    per_b = seq // tm
    ff2 = w_gu.shape[1]
    return pl.pallas_call(
        _final_kernel,
        grid=(t // tm,),
        in_specs=[pl.BlockSpec((tm, d), lambda i: (i, 0)),
                  pl.BlockSpec((tm, d), lambda i: (i, 0)),
                  pl.BlockSpec((TOP_K, tm, d), lambda i: (0, i, 0)),
                  pl.BlockSpec((tm, LANES), lambda i: (i, 0)),
                  pl.BlockSpec((1, 6, d), lambda i: (i // per_b, 0, 0)),
                  pl.BlockSpec((d, ff2), lambda i: (0, 0)),
                  pl.BlockSpec((ff2 // 2, d), lambda i: (0, 0)),
                  pl.BlockSpec((1, d), lambda i: (0, 0))],
        out_specs=pl.BlockSpec((tm, d), lambda i: (i, 0)),
        out_shape=jax.ShapeDtypeStruct((t, d), F32),
        compiler_params=_params(1),
        name="final",
    )(x1, h2, y_tok, wc, ada3, w_gu, w_d, gain)


ROUTE_TILE = 512
MAX_BLOCK_LANES = 512


def _first_argmax(x, idx_f, n):
    m = jnp.max(x, axis=0, keepdims=True)
    first = jnp.min(jnp.where(x == m, idx_f, float(n)), axis=0, keepdims=True)
    return m, first


def _route_kernel(lg_ref, bias_ref, ek_ref, rk_ref, wk_ref, wc_ref, ps_ref, be_ref, carry_ref, upper_ref):
    i = pl.program_id(0)
    tm = lg_ref.shape[1]
    per_group = N_EXPERTS // N_GROUPS

    @pl.when(i == 0)
    def _():
        carry_ref[...] = jnp.zeros_like(carry_ref)
        r = lax.broadcasted_iota(jnp.int32, (tm, tm), 0)
        cc = lax.broadcasted_iota(jnp.int32, (tm, tm), 1)
        upper_ref[...] = (r < cc).astype(BF16)

    scores = jax.nn.sigmoid(lg_ref[0:N_EXPERTS, :])
    biased = scores + bias_ref[:, 0:1]
    sub8 = lax.broadcasted_iota(jnp.int32, (per_group, tm), 0).astype(F32)
    group_rows = []
    for g in range(N_GROUPS):
        xg = biased[g * per_group:(g + 1) * per_group, :]
        m1, i1 = _first_argmax(xg, sub8, per_group)
        m2 = jnp.max(jnp.where(sub8 == i1, -jnp.inf, xg), axis=0, keepdims=True)
        group_rows.append(m1 + m2)
    cur = jnp.concatenate(group_rows, axis=0)
    gself = jnp.zeros(cur.shape, F32)
    for _ in range(TOPK_GROUPS):
        _, gi = _first_argmax(cur, sub8, N_GROUPS)
        hit = sub8 == gi
        gself = jnp.where(hit, 1.0, gself)
        cur = jnp.where(hit, -jnp.inf, cur)
    masked = jnp.concatenate(
        [jnp.where(gself[g:g + 1, :] > 0.5, biased[g * per_group:(g + 1) * per_group, :], -jnp.inf)
         for g in range(N_GROUPS)], axis=0)
    sub64 = lax.broadcasted_iota(jnp.int32, (N_EXPERTS, tm), 0).astype(F32)
    e_rows, s_rows = [], []
    sel = jnp.zeros(masked.shape, F32)
    for _ in range(TOP_K):
        _, ei = _first_argmax(masked, sub64, N_EXPERTS)
        hit = sub64 == ei
        e_rows.append(ei)
        s_rows.append(jnp.sum(jnp.where(hit, scores, 0.0), axis=0, keepdims=True))
        sel = jnp.where(hit, 1.0, sel)
        masked = jnp.where(hit, -jnp.inf, masked)
    total = s_rows[0]
    for s in s_rows[1:]:
        total = total + s
    w_rows = [s / total * ROUTED_SCALE for s in s_rows]
    before = jnp.dot(sel.astype(BF16), upper_ref[...], preferred_element_type=F32) + carry_ref[:, 0:1]
    r_rows = [jnp.sum(jnp.where(sub64 == ei, before, 0.0), axis=0, keepdims=True) for ei in e_rows]
    zrow = jnp.zeros((8 - TOP_K, tm), F32)
    ek_ref[...] = jnp.concatenate(e_rows + [zrow], axis=0).astype(jnp.int32)
    rk_ref[...] = jnp.concatenate(r_rows + [zrow], axis=0).astype(jnp.int32)
    wk = jnp.concatenate(w_rows + [zrow], axis=0)
    wk_ref[...] = wk
    wc_ref[...] = jnp.concatenate([wk, jnp.zeros((LANES - 8, tm), F32)], axis=0).T
    carry_ref[...] = carry_ref[...] + jnp.sum(sel, axis=1, keepdims=True)

    @pl.when(i == pl.num_programs(0) - 1)
    def _():
        counts = carry_ref[...]
        padded = jnp.floor((counts + (EXPERT_BLOCK - 1.0)) * (1.0 / EXPERT_BLOCK)) * EXPERT_BLOCK
        r = lax.broadcasted_iota(jnp.int32, (N_EXPERTS, N_EXPERTS), 0)
        cc = lax.broadcasted_iota(jnp.int32, (N_EXPERTS, N_EXPERTS), 1)
        pad_end = jnp.dot((cc <= r).astype(F32), padded, precision=lax.Precision.HIGHEST,
                          preferred_element_type=F32)
        pad_start = pad_end - padded
        ps_ref[...] = pad_start
        blk0 = lax.broadcasted_iota(jnp.int32, (N_EXPERTS, MAX_BLOCK_LANES), 1).astype(F32) * EXPERT_BLOCK
        e_of_blk = jnp.minimum(jnp.sum((pad_end[:, 0:1] <= blk0).astype(F32), axis=0, keepdims=True),
                               N_EXPERTS - 1.0)
        sub = lax.broadcasted_iota(jnp.int32, (N_EXPERTS, MAX_BLOCK_LANES), 0).astype(F32)
        seg_end = jnp.sum(jnp.where(sub == e_of_blk, (pad_start + counts)[:, 0:1], 0.0), axis=0, keepdims=True)
        n_valid = jnp.clip(seg_end - blk0[0:1, :], 0.0, float(EXPERT_BLOCK))
        be_ref[...] = jnp.concatenate([e_of_blk, n_valid, jnp.zeros((6, MAX_BLOCK_LANES), F32)],
                                      axis=0).astype(jnp.int32)


def _route_pallas(logits_t, bias_col):
    t = logits_t.shape[1]
    tm = min(ROUTE_TILE, t)
    row8 = pl.BlockSpec((8, tm), lambda i: (0, i))
    return pl.pallas_call(
        _route_kernel,
        grid=(t // tm,),
        in_specs=[pl.BlockSpec((LANES, tm), lambda i: (0, i)),
                  pl.BlockSpec((N_EXPERTS, 1), lambda i: (0, 0))],
        out_specs=[row8, row8, row8,
                   pl.BlockSpec((tm, LANES), lambda i: (i, 0)),
                   pl.BlockSpec((N_EXPERTS, LANES), lambda i: (0, 0)),
                   pl.BlockSpec((8, MAX_BLOCK_LANES), lambda i: (0, 0))],
        out_shape=[jax.ShapeDtypeStruct((8, t), jnp.int32),
                   jax.ShapeDtypeStruct((8, t), jnp.int32),
                   jax.ShapeDtypeStruct((8, t), F32),
                   jax.ShapeDtypeStruct((t, LANES), F32),
                   jax.ShapeDtypeStruct((N_EXPERTS, LANES), F32),
                   jax.ShapeDtypeStruct((8, MAX_BLOCK_LANES), jnp.int32)],
        scratch_shapes=[pltpu.VMEM((N_EXPERTS, LANES), F32), pltpu.VMEM((tm, tm), BF16)],
        compiler_params=_params(0, 1),
        name="route",
    )(logits_t, bias_col)


def _slot_kernel(ek_ref, rk_ref, ps_ref, slot_ref):
    tm = ek_ref.shape[1]
    sub64 = lax.broadcasted_iota(jnp.int32, (N_EXPERTS, tm), 0)
    start = ps_ref[:, 0:1]
    rows = []
    for kk in range(TOP_K):
        seg = jnp.sum(jnp.where(sub64 == ek_ref[kk:kk + 1, :], start, 0.0), axis=0, keepdims=True)
        rows.append(seg.astype(jnp.int32) + rk_ref[kk:kk + 1, :])
    slot_ref[...] = jnp.concatenate(rows + [jnp.zeros((8 - TOP_K, tm), jnp.int32)], axis=0)


def _slots(ek, rk, ps):
    t = ek.shape[1]
    tm = min(ROUTE_TILE, t)
    row8 = pl.BlockSpec((8, tm), lambda i: (0, i))
    return pl.pallas_call(
        _slot_kernel,
        grid=(t // tm,),
        in_specs=[row8, row8, pl.BlockSpec((N_EXPERTS, LANES), lambda i: (0, 0))],
        out_specs=row8,
        out_shape=jax.ShapeDtypeStruct((8, t), jnp.int32),
        compiler_params=_params(1),
        name="slots",
    )(ek, rk, ps)


def _layer(x, cond_ada, positions, w_in, dn_conv_w, dn_a_log, dn_dt_bias, dn_norm_gain, mla_q_norm_gain,
           w_q_up, mla_kv_norm_gain, w_kv_up, w_out, norm1_gain, norm2_gain, w_router, router_bias,
           w_exp_gate_up, w_exp_down, w_sh_gate_up, w_sh_down, out_gain):
    bsz, seq, d = x.shape
    t = bsz * seq
    x2 = x.reshape(t, d)
    ada3 = cond_ada.reshape(bsz, 6, d)

    s_z = 4 * DN_WIDTH
    s_a = s_z + 2 * DN_HEADS
    s_kpe = s_a + MLA_Q_RANK + MLA_KV_RANK
    w_main = jnp.concatenate([w_in[:, :s_z], w_in[:, s_a:s_kpe]], axis=1).astype(BF16)
    w_aux = jnp.concatenate([w_in[:, s_kpe:], w_in[:, s_z:s_a],
                             jnp.zeros((d, LANES - MLA_ROPE - 2 * DN_HEADS), F32)], axis=1).astype(BF16)
    auxp = jnp.zeros((8, LANES), F32)
    auxp = auxp.at[0, AUX_G:AUX_G + DN_HEADS].set(dn_a_log).at[1, AUX_G:AUX_G + DN_HEADS].set(dn_dt_bias)
    main, auxc, auxr = _in_proj(x2, ada3, norm1_gain.reshape(1, d), w_main, w_aux, auxp, seq)

    dn = _delta(main.reshape(bsz, seq, MAIN_WIDTH), auxc.reshape(bsz, seq, LANES), auxr,
                dn_conv_w.T, dn_norm_gain.reshape(1, DN_DIM))

    qk = MLA_NOPE + MLA_ROPE
    wq3 = w_q_up.reshape(MLA_Q_RANK, MLA_HEADS, qk)
    wq = jnp.concatenate([wq3, jnp.zeros((MLA_Q_RANK, MLA_HEADS, MLA_QK_PAD - qk), F32)], axis=2)
    wq = wq.reshape(MLA_Q_RANK, MLA_HEADS * MLA_QK_PAD).astype(BF16)
    wkv3 = w_kv_up.reshape(MLA_KV_RANK, MLA_HEADS, MLA_NOPE + MLA_V)
    wkv = jnp.concatenate([wkv3[:, :, :MLA_NOPE].reshape(MLA_KV_RANK, -1),
                           wkv3[:, :, MLA_NOPE:].reshape(MLA_KV_RANK, -1)], axis=1).astype(BF16)
    half = MLA_ROPE // 2
    inv_freq = ROPE_THETA ** (-jnp.arange(half, dtype=F32) / half)
    zeros = jnp.zeros((LANES - MLA_ROPE,), F32)
    rope_tab = jnp.zeros((8, LANES), F32)
    rope_tab = rope_tab.at[0].set(jnp.concatenate([inv_freq, inv_freq, zeros]))
    rope_tab = rope_tab.at[1].set(jnp.concatenate([jnp.ones((MLA_ROPE,), F32), zeros]))
    rope_tab = rope_tab.at[2].set(jnp.concatenate([-jnp.ones((half,), F32), jnp.ones((half,), F32), zeros]))
    q, k, v = _mla_proj(main, auxc, positions.reshape(t, 1), mla_q_norm_gain.reshape(1, -1),
                        mla_kv_norm_gain.reshape(1, -1), wq, wkv, rope_tab, seq)
    mla = _attention(q.reshape(bsz, seq, -1), k.reshape(bsz, seq, -1), v.reshape(bsz, seq, -1))

    w_r = jnp.concatenate([w_router, jnp.zeros((d, LANES - N_EXPERTS), F32)], axis=1)
    x1, h2, logits_t = _out_proj(x2, dn.reshape(t, DN_WIDTH), mla.reshape(t, -1), ada3,
                                 norm2_gain.reshape(1, d), w_out.astype(BF16), w_r, seq)

    n_rows = -(-(t * TOP_K + N_EXPERTS * (EXPERT_BLOCK - 1)) // EXPERT_BLOCK) * EXPERT_BLOCK
    n_blocks = n_rows // EXPERT_BLOCK
    assert n_blocks <= MAX_BLOCK_LANES
    ek, rk, _, wc, seg_start, blocks = _route_pallas(logits_t, router_bias.reshape(N_EXPERTS, 1))
    slot = _slots(ek, rk, seg_start)[:TOP_K]
    tok = jnp.broadcast_to(jnp.arange(t, dtype=jnp.int32)[None, :], (TOP_K, t))
    row_tok = jnp.zeros((n_rows,), jnp.int32).at[slot.reshape(-1)].set(tok.reshape(-1))
    x_sorted = jnp.take(h2, row_tok, axis=0)
    y_sorted = _experts(blocks[0, :n_blocks], blocks[1, :n_blocks], x_sorted, w_exp_gate_up, w_exp_down)
    y_tok = jnp.take(y_sorted, slot, axis=0)

    out = _final(x1, h2, y_tok, wc, ada3, w_sh_gate_up.astype(BF16), w_sh_down.astype(BF16),
                 out_gain.reshape(1, d), seq)
    return out.reshape(bsz, seq, d)


def kernel(x, c, positions, w_ada, b_ada, norm1_gain, w_in, dn_conv_w, dn_a_log, dn_dt_bias, dn_norm_gain,
           mla_q_norm_gain, w_q_up, mla_kv_norm_gain, w_kv_up, w_out, norm2_gain, w_router, router_bias,
           w_exp_gate_up, w_exp_down, w_sh_gate_up, w_sh_down, final_norm_gain):
    depth = w_ada.shape[0]
    assert depth == 1, "the final RMSNorm is fused into the single layer's last kernel"
    ada = _ada(c, w_ada[0], b_ada[0])
    return _layer(x, ada, positions, w_in[0], dn_conv_w[0], dn_a_log[0], dn_dt_bias[0], dn_norm_gain[0],
                  mla_q_norm_gain[0], w_q_up[0], mla_kv_norm_gain[0], w_kv_up[0], w_out[0], norm1_gain[0],
                  norm2_gain[0], w_router[0], router_bias[0], w_exp_gate_up[0], w_exp_down[0],
                  w_sh_gate_up[0], w_sh_down[0], final_norm_gain)
```

```python
import functools

import jax
import jax.numpy as jnp
from jax import lax
from jax.experimental import pallas as pl
from jax.experimental.pallas import tpu as pltpu

F32 = jnp.float32
BF16 = jnp.bfloat16

DN_HEADS = 8
DN_DIM = 128
DN_WIDTH = DN_HEADS * DN_DIM
DN_CONV = 4
DN_CHUNK = 64
MLA_HEADS = 8
MLA_Q_RANK = 512
MLA_KV_RANK = 512
MLA_NOPE = 128
MLA_ROPE = 64
MLA_V = 128
MLA_QK_PAD = 256
ROPE_THETA = 10000.0
N_EXPERTS = 64
TOP_K = 6
N_GROUPS = 8
TOPK_GROUPS = 4
EXPERT_FF = 512
ROUTED_SCALE = 2.5
EXPERT_BLOCK = 256
RMS_EPS = 1e-6
L2_EPS = 1e-6

LANES = 128
MAIN_WIDTH = 4 * DN_WIDTH + MLA_Q_RANK + MLA_KV_RANK
AUX_BETA = MLA_ROPE
AUX_G = MLA_ROPE + DN_HEADS
VMEM_LIMIT = 56 * 1024 * 1024


def _params(n_parallel, n_arbitrary=0):
    sem = ("parallel",) * n_parallel + ("arbitrary",) * n_arbitrary
    return pltpu.CompilerParams(dimension_semantics=sem, vmem_limit_bytes=VMEM_LIMIT)


def _silu(x):
    return x * jax.nn.sigmoid(x)


def _bdot(a, b):
    return jnp.dot(a.astype(BF16), b.astype(BF16), preferred_element_type=F32)


def _bdot_nt(a, b):
    return lax.dot_general(a.astype(BF16), b.astype(BF16), (((1,), (1,)), ((), ())),
                           preferred_element_type=F32)


def _bdot_tn(a, b):
    return lax.dot_general(a.astype(BF16), b.astype(BF16), (((0,), (0,)), ((), ())),
                           preferred_element_type=F32)


def _rms(x, gain):
    return x * lax.rsqrt(jnp.mean(x * x, axis=-1, keepdims=True) + RMS_EPS) * gain


def _ada_kernel(c_ref, w_ref, b_ref, o_ref):
    o_ref[...] = _bdot(_silu(c_ref[...]), w_ref[...]) + b_ref[...]


def _ada(c, w_ada, b_ada):
    bsz, d = c.shape
    n = w_ada.shape[1]
    tn = 1024
    return pl.pallas_call(
        _ada_kernel,
        grid=(n // tn,),
        in_specs=[pl.BlockSpec((bsz, d), lambda j: (0, 0)),
                  pl.BlockSpec((d, tn), lambda j: (0, j)),
                  pl.BlockSpec((1, tn), lambda j: (0, j))],
        out_specs=pl.BlockSpec((bsz, tn), lambda j: (0, j)),
        out_shape=jax.ShapeDtypeStruct((bsz, n), F32),
        compiler_params=_params(1),
        name="ada",
    )(c, w_ada, b_ada.reshape(1, n))


def _in_proj_kernel(x_ref, ada_ref, gain_ref, wm_ref, wa_ref, auxp_ref,
                    main_ref, auxc_ref, auxr_ref, h_ref):
    j = pl.program_id(1)

    @pl.when(j == 0)
    def _():
        ada = ada_ref[0]
        h = _rms(x_ref[...], gain_ref[...]) * (1.0 + ada[1:2, :]) + ada[0:1, :]
        hb = h.astype(BF16)
        h_ref[...] = hb
        aux = jnp.dot(hb, wa_ref[...], preferred_element_type=F32)
        tm = aux.shape[0]
        lane = lax.broadcasted_iota(jnp.int32, aux.shape, 1)
        is_beta = (lane >= AUX_BETA) & (lane < AUX_G)
        is_g = (lane >= AUX_G) & (lane < AUX_G + DN_HEADS)
        a_log = auxp_ref[0:1, :]
        dt_bias = auxp_ref[1:2, :]
        sp_in = aux + dt_bias
        softplus = jnp.maximum(sp_in, 0.0) + jnp.log(1.0 + jnp.exp(-jnp.abs(sp_in)))
        g = -jnp.exp(a_log) * softplus
        aux = jnp.where(is_beta, jax.nn.sigmoid(aux), jnp.where(is_g, g, aux))
        r = lax.broadcasted_iota(jnp.int32, (LANES, LANES), 0)
        cidx = lax.broadcasted_iota(jnp.int32, (LANES, LANES), 1)
        tri = ((cidx <= r) & (cidx // DN_CHUNK == r // DN_CHUNK)).astype(F32)
        g_cols = (cidx >= AUX_G) & (cidx < AUX_G + DN_HEADS)
        parts = []
        for t in range(tm // LANES):
            blk = aux[t * LANES:(t + 1) * LANES, :]
            cs = jnp.dot(tri, blk, precision=lax.Precision.HIGHEST, preferred_element_type=F32)
            parts.append(jnp.where(g_cols, cs, blk))
        aux = jnp.concatenate(parts, axis=0)
        auxc_ref[...] = aux
        auxr_ref[...] = aux.T

    main_ref[...] = jnp.dot(h_ref[...], wm_ref[...], preferred_element_type=F32).astype(main_ref.dtype)


def _in_proj(x2, ada3, gain, w_main, w_aux, auxp, seq):
    t, d = x2.shape
    n = w_main.shape[1]
    tm = min(512, seq)
    tn = 1024
    per_b = seq // tm
    return pl.pallas_call(
        _in_proj_kernel,
        grid=(t // tm, n // tn),
        in_specs=[pl.BlockSpec((tm, d), lambda i, j: (i, 0)),
                  pl.BlockSpec((1, 6, d), lambda i, j: (i // per_b, 0, 0)),
                  pl.BlockSpec((1, d), lambda i, j: (0, 0)),
                  pl.BlockSpec((d, tn), lambda i, j: (0, j)),
                  pl.BlockSpec((d, LANES), lambda i, j: (0, 0)),
                  pl.BlockSpec((8, LANES), lambda i, j: (0, 0))],
        out_specs=[pl.BlockSpec((tm, tn), lambda i, j: (i, j)),
                   pl.BlockSpec((tm, LANES), lambda i, j: (i, 0)),
                   pl.BlockSpec((LANES, tm), lambda i, j: (0, i))],
        out_shape=[jax.ShapeDtypeStruct((t, n), BF16),
                   jax.ShapeDtypeStruct((t, LANES), F32),
                   jax.ShapeDtypeStruct((LANES, t), F32)],
        scratch_shapes=[pltpu.VMEM((tm, d), BF16)],
        compiler_params=_params(1, 1),
        name="in_proj",
    )(x2, ada3, gain, w_main, w_aux, auxp)


def _unit_lower_inverse(m):
    c = m.shape[0]
    r = lax.broadcasted_iota(jnp.int32, (c, c), 0)
    cc = lax.broadcasted_iota(jnp.int32, (c, c), 1)
    eye = (r == cc).astype(F32)
    same = (r // 16) == (cc // 16)
    md = jnp.where(same, m, 0.0)
    mo = jnp.where(same, 0.0, m)
    p1 = _bdot(md, md)
    p2 = _bdot(p1, p1)
    p3 = _bdot(p2, p2)
    td = eye - md
    td = td + _bdot(td, p1)
    td = td + _bdot(td, p2)
    td = td + _bdot(td, p3)
    n1 = _bdot(td, mo)
    n2 = _bdot(n1, n1)
    left = (eye - n1) + _bdot(eye - n1, n2)
    return _bdot(left, td)


DELTA_TILE = 512
DELTA_SCAN_HEADS = 4


def _delta_prep_kernel(q_ref, k_ref, v_ref, qh_ref, kh_ref, vh_ref, auxc_ref, auxr_ref, wq_ref, wk_ref, wv_ref,
                       u_ref, w_ref, qe_ref, kd_ref, a_ref):
    head = pl.program_id(1)
    first_tile = pl.program_id(2) == 0
    ts = q_ref.shape[0]
    c = DN_CHUNK
    halo = qh_ref.shape[0]

    def conv_silu(x_ref, h_ref, w_ref):
        prev = jnp.where(first_tile, 0.0, h_ref[...].astype(F32))
        x = jnp.concatenate([prev, x_ref[...].astype(F32)], axis=0)
        w = w_ref[...]
        acc = x * w[DN_CONV - 1:DN_CONV, :]
        for s in range(1, DN_CONV):
            acc = acc + pltpu.roll(x, s, 0) * w[DN_CONV - 1 - s:DN_CONV - s, :]
        return _silu(acc[halo:, :])

    q = conv_silu(q_ref, qh_ref, wq_ref)
    k = conv_silu(k_ref, kh_ref, wk_ref)
    v = conv_silu(v_ref, vh_ref, wv_ref)
    q = q * lax.rsqrt(jnp.sum(q * q, axis=-1, keepdims=True) + L2_EPS) * (DN_DIM ** -0.5)
    k = k * lax.rsqrt(jnp.sum(k * k, axis=-1, keepdims=True) + L2_EPS)

    auxc = auxc_ref[...]
    lane = lax.broadcasted_iota(jnp.int32, auxc.shape, 1)
    beta_all = jnp.sum(jnp.where(lane == AUX_BETA + head, auxc, 0.0), axis=1, keepdims=True)
    gcol_all = jnp.sum(jnp.where(lane == AUX_G + head, auxc, 0.0), axis=1, keepdims=True)
    grows = auxr_ref[AUX_G:AUX_G + DN_HEADS, :]
    head_row = lax.broadcasted_iota(jnp.int32, grows.shape, 0)
    grow_all = jnp.sum(jnp.where(head_row == head, grows, 0.0), axis=0, keepdims=True)

    ri = lax.broadcasted_iota(jnp.int32, (c, c), 0)
    ci = lax.broadcasted_iota(jnp.int32, (c, c), 1)
    causal = ri >= ci
    strict = ri > ci
    for j in range(ts // c):
        rows = slice(j * c, (j + 1) * c)
        qc, kc, vc = q[rows], k[rows], v[rows]
        beta, gcol = beta_all[rows], gcol_all[rows]
        grow = grow_all[:, rows]
        glast = gcol[c - 1:c, :]
        decay = jnp.exp(jnp.where(causal, gcol - grow, -jnp.inf))
        kb = kc * beta
        both = _bdot_nt(jnp.concatenate([kb, qc], axis=0), kc)
        m = jnp.where(strict, both[:c] * decay, 0.0)
        tinv = _unit_lower_inverse(m)
        eg = jnp.exp(gcol)
        sol = _bdot(tinv, jnp.concatenate([vc * beta, kb * eg], axis=1))
        u_ref[rows, :] = sol[:, :DN_DIM].astype(u_ref.dtype)
        w_ref[rows, :] = sol[:, DN_DIM:].astype(w_ref.dtype)
        qe_ref[rows, :] = (qc * eg).astype(qe_ref.dtype)
        kd_ref[rows, :] = (kc * jnp.exp(glast - gcol)).astype(kd_ref.dtype)
        a_ref[rows, :] = jnp.where(causal, both[c:] * decay, 0.0).astype(a_ref.dtype)


def _delta_scan_kernel(u_ref, w_ref, qe_ref, kd_ref, a_ref, z_ref, auxc_ref, gain_ref, o_ref, st_ref):
    hb = pl.program_id(1)
    seq = u_ref.shape[0]
    n_heads = a_ref.shape[0]
    c = DN_CHUNK
    st_ref[...] = jnp.zeros_like(st_ref)
    gain = gain_ref[...]
    lane = lax.broadcasted_iota(jnp.int32, (1, LANES), 1)

    def scan_chunk(i, carry):
        start = pl.multiple_of(i * c, c)
        rows = pl.ds(start, c)
        last = auxc_ref[pl.ds(start + c - 1, 1), :]
        for hh in range(n_heads):
            cols = slice(hh * DN_DIM, (hh + 1) * DN_DIM)
            glast = jnp.sum(jnp.where(lane == AUX_G + hb * n_heads + hh, last, 0.0), axis=1, keepdims=True)
            state = st_ref[hh]
            sb = state.astype(BF16)
            v_new = u_ref[rows, cols].astype(F32) - jnp.dot(w_ref[rows, cols], sb, preferred_element_type=F32)
            vb = v_new.astype(BF16)
            o = (jnp.dot(qe_ref[rows, cols], sb, preferred_element_type=F32)
                 + jnp.dot(a_ref[hh, rows, :], vb, preferred_element_type=F32))
            st_ref[hh] = state * jnp.exp(glast) + lax.dot_general(
                kd_ref[rows, cols], vb, (((0,), (0,)), ((), ())), preferred_element_type=F32)
            z = z_ref[rows, cols].astype(F32)
            o_ref[rows, cols] = (_rms(o, gain) * _silu(z)).astype(o_ref.dtype)
        return carry

    lax.fori_loop(0, seq // c, scan_chunk, 0)


def _delta(main3, auxc3, auxr, conv_w_t, gain):
    bsz, seq, _ = main3.shape
    hb = DN_WIDTH // DN_DIM
    ts = min(DELTA_TILE, seq)
    halo = 16
    tiles = seq // ts

    def col(offset):
        return pl.BlockSpec((None, ts, DN_DIM), lambda b, h, i: (b, i, offset * hb + h))

    def halo_col(offset):
        return pl.BlockSpec((None, halo, DN_DIM),
                            lambda b, h, i: (b, jnp.maximum(i * (ts // halo) - 1, 0), offset * hb + h))

    def wcol(offset):
        return pl.BlockSpec((DN_CONV, DN_DIM), lambda b, h, i: (0, offset * hb + h))

    tok = pl.BlockSpec((None, ts, DN_DIM), lambda b, h, i: (b, i, h))
    tok_shape = jax.ShapeDtypeStruct((bsz, seq, DN_WIDTH), BF16)
    u, w, qe, kd, a = pl.pallas_call(
        _delta_prep_kernel,
        grid=(bsz, DN_HEADS, tiles),
        in_specs=[col(0), col(1), col(2), halo_col(0), halo_col(1), halo_col(2),
                  pl.BlockSpec((None, ts, LANES), lambda b, h, i: (b, i, 0)),
                  pl.BlockSpec((LANES, ts), lambda b, h, i: (0, b * tiles + i)),
                  wcol(0), wcol(1), wcol(2)],
        out_specs=[tok, tok, tok, tok,
                   pl.BlockSpec((None, None, ts, DN_CHUNK), lambda b, h, i: (b, h, i, 0))],
        out_shape=[tok_shape, tok_shape, tok_shape, tok_shape,
                   jax.ShapeDtypeStruct((bsz, DN_HEADS, seq, DN_CHUNK), BF16)],
        compiler_params=_params(3),
        name="delta_prep",
    )(main3, main3, main3, main3, main3, main3, auxc3, auxr, conv_w_t, conv_w_t, conv_w_t)

    hp = DELTA_SCAN_HEADS
    wide = pl.BlockSpec((None, seq, hp * DN_DIM), lambda b, g: (b, 0, g))
    return pl.pallas_call(
        _delta_scan_kernel,
        grid=(bsz, DN_HEADS // hp),
        in_specs=[wide, wide, wide, wide,
                  pl.BlockSpec((None, hp, seq, DN_CHUNK), lambda b, g: (b, g, 0, 0)),
                  pl.BlockSpec((None, seq, hp * DN_DIM), lambda b, g: (b, 0, 3 * (hb // hp) + g)),
                  pl.BlockSpec((None, seq, LANES), lambda b, g: (b, 0, 0)),
                  pl.BlockSpec((1, DN_DIM), lambda b, g: (0, 0))],
        out_specs=wide,
        out_shape=tok_shape,
        scratch_shapes=[pltpu.VMEM((hp, DN_DIM, DN_DIM), F32)],
        compiler_params=_params(2),
        name="delta_scan",
    )(u, w, qe, kd, a, main3, auxc3, gain)


def _mla_proj_kernel(cq_ref, ckv_ref, auxc_ref, pos_ref, qg_ref, kvg_ref, wq_ref, wkv_ref, rope_ref,
                     q_ref, k_ref, v_ref):
    scale = (MLA_NOPE + MLA_ROPE) ** -0.5
    ang = pos_ref[...].astype(F32) * rope_ref[0:1, :]
    cos_t = jnp.cos(ang) * rope_ref[1:2, :]
    sin_t = jnp.sin(ang) * rope_ref[2:3, :]
    lane = lax.broadcasted_iota(jnp.int32, ang.shape, 1)
    first = lane < MLA_ROPE // 2

    def rope(a):
        swapped = jnp.where(first, pltpu.roll(a, LANES - MLA_ROPE // 2, 1), pltpu.roll(a, MLA_ROPE // 2, 1))
        return a * cos_t + swapped * sin_t

    ql = _bdot(_rms(cq_ref[...].astype(F32), qg_ref[...]), wq_ref[...]) * scale
    kv = _bdot(_rms(ckv_ref[...].astype(F32), kvg_ref[...]), wkv_ref[...])
    kpe = rope(auxc_ref[...]).astype(k_ref.dtype)
    for h in range(MLA_HEADS):
        o = h * MLA_QK_PAD
        q_ref[:, o:o + MLA_NOPE] = ql[:, o:o + MLA_NOPE].astype(q_ref.dtype)
        q_ref[:, o + MLA_NOPE:o + MLA_QK_PAD] = rope(ql[:, o + MLA_NOPE:o + MLA_QK_PAD]).astype(q_ref.dtype)
        k_ref[:, o:o + MLA_NOPE] = kv[:, h * MLA_NOPE:(h + 1) * MLA_NOPE].astype(k_ref.dtype)
        k_ref[:, o + MLA_NOPE:o + MLA_QK_PAD] = kpe
    v_ref[...] = kv[:, MLA_HEADS * MLA_NOPE:].astype(v_ref.dtype)


def _mla_proj(main, auxc, pos, q_gain, kv_gain, wq, wkv, rope_tab, seq):
    t = main.shape[0]
    tm = min(512, seq)
    cq_blk = (4 * DN_WIDTH) // MLA_Q_RANK
    hq = MLA_HEADS * MLA_QK_PAD
    hv = MLA_HEADS * MLA_V
    return pl.pallas_call(
        _mla_proj_kernel,
        grid=(t // tm,),
        in_specs=[pl.BlockSpec((tm, MLA_Q_RANK), lambda i: (i, cq_blk)),
                  pl.BlockSpec((tm, MLA_KV_RANK), lambda i: (i, cq_blk + 1)),
                  pl.BlockSpec((tm, LANES), lambda i: (i, 0)),
                  pl.BlockSpec((tm, 1), lambda i: (i, 0)),
                  pl.BlockSpec((1, MLA_Q_RANK), lambda i: (0, 0)),
                  pl.BlockSpec((1, MLA_KV_RANK), lambda i: (0, 0)),
                  pl.BlockSpec((MLA_Q_RANK, hq), lambda i: (0, 0)),
                  pl.BlockSpec((MLA_KV_RANK, 2 * hv), lambda i: (0, 0)),
                  pl.BlockSpec((8, LANES), lambda i: (0, 0))],
        out_specs=[pl.BlockSpec((tm, hq), lambda i: (i, 0)),
                   pl.BlockSpec((tm, hq), lambda i: (i, 0)),
                   pl.BlockSpec((tm, hv), lambda i: (i, 0))],
        out_shape=[jax.ShapeDtypeStruct((t, hq), BF16),
                   jax.ShapeDtypeStruct((t, hq), BF16),
                   jax.ShapeDtypeStruct((t, hv), BF16)],
        compiler_params=_params(1),
        name="mla_proj",
    )(main, main, auxc, pos, q_gain, kv_gain, wq, wkv, rope_tab)


def _attn_kernel(q_ref, k_ref, v_ref, o_ref, m_ref, l_ref, acc_ref):
    qi = pl.program_id(2)
    tq = q_ref.shape[0]
    q = q_ref[...]
    m_ref[...] = jnp.full_like(m_ref, -jnp.inf)
    l_ref[...] = jnp.zeros_like(l_ref)
    acc_ref[...] = jnp.zeros_like(acc_ref)

    def step(j, masked):
        rows = pl.ds(pl.multiple_of(j * tq, tq), tq)
        s = lax.dot_general(q, k_ref[rows, :], (((1,), (1,)), ((), ())), preferred_element_type=F32)
        if masked:
            r = lax.broadcasted_iota(jnp.int32, s.shape, 0)
            cc = lax.broadcasted_iota(jnp.int32, s.shape, 1)
            s = jnp.where(cc <= r, s, -jnp.inf)
        m_old = m_ref[...]
        m_new = jnp.maximum(m_old, jnp.max(s, axis=-1, keepdims=True))
        alpha = jnp.exp(m_old - m_new)
        p = jnp.exp(s - m_new)
        l_ref[...] = alpha * l_ref[...] + jnp.sum(p, axis=-1, keepdims=True)
        acc_ref[...] = alpha * acc_ref[...] + jnp.dot(p.astype(BF16), v_ref[rows, :],
                                                      preferred_element_type=F32)
        m_ref[...] = m_new

    def body(j, carry):
        step(j, False)
        return carry

    lax.fori_loop(0, qi, body, 0)
    step(qi, True)
    o_ref[...] = (acc_ref[...] / l_ref[...]).astype(o_ref.dtype)


def _attention(q3, k3, v3):
    bsz, seq, _ = q3.shape
    tq = min(512, seq)
    return pl.pallas_call(
        _attn_kernel,
        grid=(bsz, MLA_HEADS, seq // tq),
        in_specs=[pl.BlockSpec((None, tq, MLA_QK_PAD), lambda b, h, i: (b, i, h)),
                  pl.BlockSpec((None, seq, MLA_QK_PAD), lambda b, h, i: (b, 0, h)),
                  pl.BlockSpec((None, seq, MLA_V), lambda b, h, i: (b, 0, h))],
        out_specs=pl.BlockSpec((None, tq, MLA_V), lambda b, h, i: (b, i, h)),
        out_shape=jax.ShapeDtypeStruct((bsz, seq, MLA_HEADS * MLA_V), BF16),
        scratch_shapes=[pltpu.VMEM((tq, 1), F32), pltpu.VMEM((tq, 1), F32), pltpu.VMEM((tq, MLA_V), F32)],
        compiler_params=_params(3),
        name="attention",
    )(q3, k3, v3)


def _out_proj_kernel(x_ref, dn_ref, mla_ref, ada_ref, gain_ref, w_ref, wr_ref, x1_ref, h2_ref, lg_ref):
    ada = ada_ref[0]
    half = dn_ref.shape[1]
    mix = (jnp.dot(dn_ref[...], w_ref[:half, :], preferred_element_type=F32)
           + jnp.dot(mla_ref[...], w_ref[half:, :], preferred_element_type=F32))
    x1 = x_ref[...] + ada[2:3, :] * mix
    x1_ref[...] = x1
    h2 = _rms(x1, gain_ref[...]) * (1.0 + ada[4:5, :]) + ada[3:4, :]
    h2_ref[...] = h2.astype(h2_ref.dtype)
    lg_ref[...] = jnp.dot(h2, wr_ref[...], precision=lax.Precision.HIGHEST, preferred_element_type=F32).T


def _out_proj(x2, dn, mla, ada3, gain, w_out, w_router, seq):
    t, d = x2.shape
    tm = min(256, seq)
    per_b = seq // tm
    half = dn.shape[1]
    return pl.pallas_call(
        _out_proj_kernel,
        grid=(t // tm,),
        in_specs=[pl.BlockSpec((tm, d), lambda i: (i, 0)),
                  pl.BlockSpec((tm, half), lambda i: (i, 0)),
                  pl.BlockSpec((tm, half), lambda i: (i, 0)),
                  pl.BlockSpec((1, 6, d), lambda i: (i // per_b, 0, 0)),
                  pl.BlockSpec((1, d), lambda i: (0, 0)),
                  pl.BlockSpec((2 * half, d), lambda i: (0, 0)),
                  pl.BlockSpec((d, LANES), lambda i: (0, 0))],
        out_specs=[pl.BlockSpec((tm, d), lambda i: (i, 0)),
                   pl.BlockSpec((tm, d), lambda i: (i, 0)),
                   pl.BlockSpec((LANES, tm), lambda i: (0, i))],
        out_shape=[jax.ShapeDtypeStruct((t, d), F32),
                   jax.ShapeDtypeStruct((t, d), BF16),
                   jax.ShapeDtypeStruct((LANES, t), F32)],
        compiler_params=_params(1),
        name="out_proj",
    )(x2, dn, mla, ada3, gain, w_out, w_router)


def _expert_kernel(be_ref, nv_ref, x_ref, wgu_ref, wd_ref, y_ref, gu_s, d_s):
    i = pl.program_id(0)
    changed = (i == 0) | (be_ref[i] != be_ref[jnp.maximum(i - 1, 0)])

    @pl.when(changed)
    def _():
        gu_s[...] = wgu_ref[...].astype(BF16)
        d_s[...] = wd_ref[...].astype(BF16)

    gu = jnp.dot(x_ref[...], gu_s[...], preferred_element_type=F32)
    ff = gu.shape[1] // 2
    act = _silu(gu[:, :ff]) * gu[:, ff:]
    y = jnp.dot(act.astype(BF16), d_s[...], preferred_element_type=F32)
    row = lax.broadcasted_iota(jnp.int32, y.shape, 0)
    y_ref[...] = jnp.where(row < nv_ref[i], y, 0.0).astype(y_ref.dtype)


def _experts(block_e, n_valid, x_sorted, w_gu, w_d):
    n_rows, d = x_sorted.shape
    ff2 = w_gu.shape[2]
    n_blocks = n_rows // EXPERT_BLOCK
    grid_spec = pltpu.PrefetchScalarGridSpec(
        num_scalar_prefetch=2,
        grid=(n_blocks,),
        in_specs=[pl.BlockSpec((EXPERT_BLOCK, d), lambda i, be, nv: (i, 0)),
                  pl.BlockSpec((None, d, ff2), lambda i, be, nv: (be[i], 0, 0)),
                  pl.BlockSpec((None, ff2 // 2, d), lambda i, be, nv: (be[i], 0, 0))],
        out_specs=pl.BlockSpec((EXPERT_BLOCK, d), lambda i, be, nv: (i, 0)),
        scratch_shapes=[pltpu.VMEM((d, ff2), BF16), pltpu.VMEM((ff2 // 2, d), BF16)],
    )
    return pl.pallas_call(
        _expert_kernel,
        grid_spec=grid_spec,
        out_shape=jax.ShapeDtypeStruct((n_rows, d), BF16),
        compiler_params=_params(0, 1),
        name="experts",
    )(block_e, n_valid, x_sorted, w_gu, w_d)


def _final_kernel(x1_ref, h2_ref, y_ref, wc_ref, ada_ref, wgu_ref, wd_ref, gain_ref, o_ref):
    ada = ada_ref[0]
    gu = jnp.dot(h2_ref[...], wgu_ref[...], preferred_element_type=F32)
    ff = gu.shape[1] // 2
    act = _silu(gu[:, :ff]) * gu[:, ff:]
    ffn = jnp.dot(act.astype(BF16), wd_ref[...], preferred_element_type=F32)
    wc = wc_ref[...]
    for kk in range(TOP_K):
        ffn = ffn + wc[:, kk:kk + 1] * y_ref[kk].astype(F32)
    x2 = x1_ref[...] + ada[5:6, :] * ffn
    o_ref[...] = _rms(x2, gain_ref[...])


def _final(x1, h2, y_tok, wc, ada3, w_gu, w_d, gain, seq):
    t, d = x1.shape
    tm = min(256, seq)
    per_b = seq // tm
    ff2 = w_gu.shape[1]
    return pl.pallas_call(
        _final_kernel,
        grid=(t // tm,),
        in_specs=[pl.BlockSpec((tm, d), lambda i: (i, 0)),
                  pl.BlockSpec((tm, d), lambda i: (i, 0)),
                  pl.BlockSpec((TOP_K, tm, d), lambda i: (0, i, 0)),
                  pl.BlockSpec((tm, LANES), lambda i: (i, 0)),
                  pl.BlockSpec((1, 6, d), lambda i: (i // per_b, 0, 0)),
                  pl.BlockSpec((d, ff2), lambda i: (0, 0)),
                  pl.BlockSpec((ff2 // 2, d), lambda i: (0, 0)),
                  pl.BlockSpec((1, d), lambda i: (0, 0))],
        out_specs=pl.BlockSpec((tm, d), lambda i: (i, 0)),
        out_shape=jax.ShapeDtypeStruct((t, d), F32),
        compiler_params=_params(1),
        name="final",
    )(x1, h2, y_tok, wc, ada3, w_gu, w_d, gain)


ROUTE_TILE = 512
MAX_BLOCK_LANES = 512


def _first_argmax(x, idx_f, n):
    m = jnp.max(x, axis=0, keepdims=True)
    first = jnp.min(jnp.where(x == m, idx_f, float(n)), axis=0, keepdims=True)
    return m, first


def _route_kernel(lg_ref, bias_ref, ek_ref, rk_ref, wk_ref, wc_ref, ps_ref, be_ref, carry_ref, upper_ref):
    i = pl.program_id(0)
    tm = lg_ref.shape[1]
    per_group = N_EXPERTS // N_GROUPS

    @pl.when(i == 0)
    def _():
        carry_ref[...] = jnp.zeros_like(carry_ref)
        r = lax.broadcasted_iota(jnp.int32, (tm, tm), 0)
        cc = lax.broadcasted_iota(jnp.int32, (tm, tm), 1)
        upper_ref[...] = (r < cc).astype(BF16)

    scores = jax.nn.sigmoid(lg_ref[0:N_EXPERTS, :])
    biased = scores + bias_ref[:, 0:1]
    sub8 = lax.broadcasted_iota(jnp.int32, (per_group, tm), 0).astype(F32)
    group_rows = []
    for g in range(N_GROUPS):
        xg = biased[g * per_group:(g + 1) * per_group, :]
        m1, i1 = _first_argmax(xg, sub8, per_group)
        m2 = jnp.max(jnp.where(sub8 == i1, -jnp.inf, xg), axis=0, keepdims=True)
        group_rows.append(m1 + m2)
    cur = jnp.concatenate(group_rows, axis=0)
    gself = jnp.zeros(cur.shape, F32)
    for _ in range(TOPK_GROUPS):
        _, gi = _first_argmax(cur, sub8, N_GROUPS)
        hit = sub8 == gi
        gself = jnp.where(hit, 1.0, gself)
        cur = jnp.where(hit, -jnp.inf, cur)
    masked = jnp.concatenate(
        [jnp.where(gself[g:g + 1, :] > 0.5, biased[g * per_group:(g + 1) * per_group, :], -jnp.inf)
         for g in range(N_GROUPS)], axis=0)
    sub64 = lax.broadcasted_iota(jnp.int32, (N_EXPERTS, tm), 0).astype(F32)
    e_rows, s_rows = [], []
    sel = jnp.zeros(masked.shape, F32)
    for _ in range(TOP_K):
        _, ei = _first_argmax(masked, sub64, N_EXPERTS)
        hit = sub64 == ei
        e_rows.append(ei)
        s_rows.append(jnp.sum(jnp.where(hit, scores, 0.0), axis=0, keepdims=True))
        sel = jnp.where(hit, 1.0, sel)
        masked = jnp.where(hit, -jnp.inf, masked)
    total = s_rows[0]
    for s in s_rows[1:]:
        total = total + s
    w_rows = [s / total * ROUTED_SCALE for s in s_rows]
    before = jnp.dot(sel.astype(BF16), upper_ref[...], preferred_element_type=F32) + carry_ref[:, 0:1]
    r_rows = [jnp.sum(jnp.where(sub64 == ei, before, 0.0), axis=0, keepdims=True) for ei in e_rows]
    zrow = jnp.zeros((8 - TOP_K, tm), F32)
    ek_ref[...] = jnp.concatenate(e_rows + [zrow], axis=0).astype(jnp.int32)
    rk_ref[...] = jnp.concatenate(r_rows + [zrow], axis=0).astype(jnp.int32)
    wk = jnp.concatenate(w_rows + [zrow], axis=0)
    wk_ref[...] = wk
    wc_ref[...] = jnp.concatenate([wk, jnp.zeros((LANES - 8, tm), F32)], axis=0).T
    carry_ref[...] = carry_ref[...] + jnp.sum(sel, axis=1, keepdims=True)

    @pl.when(i == pl.num_programs(0) - 1)
    def _():
        counts = carry_ref[...]
        padded = jnp.floor((counts + (EXPERT_BLOCK - 1.0)) * (1.0 / EXPERT_BLOCK)) * EXPERT_BLOCK
        r = lax.broadcasted_iota(jnp.int32, (N_EXPERTS, N_EXPERTS), 0)
        cc = lax.broadcasted_iota(jnp.int32, (N_EXPERTS, N_EXPERTS), 1)
        pad_end = jnp.dot((cc <= r).astype(F32), padded, precision=lax.Precision.HIGHEST,
                          preferred_element_type=F32)
        pad_start = pad_end - padded
        ps_ref[...] = pad_start
        blk0 = lax.broadcasted_iota(jnp.int32, (N_EXPERTS, MAX_BLOCK_LANES), 1).astype(F32) * EXPERT_BLOCK
        e_of_blk = jnp.minimum(jnp.sum((pad_end[:, 0:1] <= blk0).astype(F32), axis=0, keepdims=True),
                               N_EXPERTS - 1.0)
        sub = lax.broadcasted_iota(jnp.int32, (N_EXPERTS, MAX_BLOCK_LANES), 0).astype(F32)
        seg_end = jnp.sum(jnp.where(sub == e_of_blk, (pad_start + counts)[:, 0:1], 0.0), axis=0, keepdims=True)
        n_valid = jnp.clip(seg_end - blk0[0:1, :], 0.0, float(EXPERT_BLOCK))
        be_ref[...] = jnp.concatenate([e_of_blk, n_valid, jnp.zeros((6, MAX_BLOCK_LANES), F32)],
                                      axis=0).astype(jnp.int32)


def _route_pallas(logits_t, bias_col):
    t = logits_t.shape[1]
    tm = min(ROUTE_TILE, t)
    row8 = pl.BlockSpec((8, tm), lambda i: (0, i))
    return pl.pallas_call(
        _route_kernel,
        grid=(t // tm,),
        in_specs=[pl.BlockSpec((LANES, tm), lambda i: (0, i)),
                  pl.BlockSpec((N_EXPERTS, 1), lambda i: (0, 0))],
        out_specs=[row8, row8, row8,
                   pl.BlockSpec((tm, LANES), lambda i: (i, 0)),
                   pl.BlockSpec((N_EXPERTS, LANES), lambda i: (0, 0)),
                   pl.BlockSpec((8, MAX_BLOCK_LANES), lambda i: (0, 0))],
        out_shape=[jax.ShapeDtypeStruct((8, t), jnp.int32),
                   jax.ShapeDtypeStruct((8, t), jnp.int32),
                   jax.ShapeDtypeStruct((8, t), F32),
                   jax.ShapeDtypeStruct((t, LANES), F32),
                   jax.ShapeDtypeStruct((N_EXPERTS, LANES), F32),
                   jax.ShapeDtypeStruct((8, MAX_BLOCK_LANES), jnp.int32)],
        scratch_shapes=[pltpu.VMEM((N_EXPERTS, LANES), F32), pltpu.VMEM((tm, tm), BF16)],
        compiler_params=_params(0, 1),
        name="route",
    )(logits_t, bias_col)


def _slot_kernel(ek_ref, rk_ref, ps_ref, slot_ref):
    tm = ek_ref.shape[1]
    sub64 = lax.broadcasted_iota(jnp.int32, (N_EXPERTS, tm), 0)
    start = ps_ref[:, 0:1]
    rows = []
    for kk in range(TOP_K):
        seg = jnp.sum(jnp.where(sub64 == ek_ref[kk:kk + 1, :], start, 0.0), axis=0, keepdims=True)
        rows.append(seg.astype(jnp.int32) + rk_ref[kk:kk + 1, :])
    slot_ref[...] = jnp.concatenate(rows + [jnp.zeros((8 - TOP_K, tm), jnp.int32)], axis=0)


def _slots(ek, rk, ps):
    t = ek.shape[1]
    tm = min(ROUTE_TILE, t)
    row8 = pl.BlockSpec((8, tm), lambda i: (0, i))
    return pl.pallas_call(
        _slot_kernel,
        grid=(t // tm,),
        in_specs=[row8, row8, pl.BlockSpec((N_EXPERTS, LANES), lambda i: (0, 0))],
        out_specs=row8,
        out_shape=jax.ShapeDtypeStruct((8, t), jnp.int32),
        compiler_params=_params(1),
        name="slots",
    )(ek, rk, ps)


def _layer(x, cond_ada, positions, w_in, dn_conv_w, dn_a_log, dn_dt_bias, dn_norm_gain, mla_q_norm_gain,
           w_q_up, mla_kv_norm_gain, w_kv_up, w_out, norm1_gain, norm2_gain, w_router, router_bias,
           w_exp_gate_up, w_exp_down, w_sh_gate_up, w_sh_down, out_gain):
    bsz, seq, d = x.shape
    t = bsz * seq
    x2 = x.reshape(t, d)
    ada3 = cond_ada.reshape(bsz, 6, d)

    s_z = 4 * DN_WIDTH
    s_a = s_z + 2 * DN_HEADS
    s_kpe = s_a + MLA_Q_RANK + MLA_KV_RANK
    w_main = jnp.concatenate([w_in[:, :s_z], w_in[:, s_a:s_kpe]], axis=1).astype(BF16)
    w_aux = jnp.concatenate([w_in[:, s_kpe:], w_in[:, s_z:s_a],
                             jnp.zeros((d, LANES - MLA_ROPE - 2 * DN_HEADS), F32)], axis=1).astype(BF16)
    auxp = jnp.zeros((8, LANES), F32)
    auxp = auxp.at[0, AUX_G:AUX_G + DN_HEADS].set(dn_a_log).at[1, AUX_G:AUX_G + DN_HEADS].set(dn_dt_bias)
    main, auxc, auxr = _in_proj(x2, ada3, norm1_gain.reshape(1, d), w_main, w_aux, auxp, seq)

    dn = _delta(main.reshape(bsz, seq, MAIN_WIDTH), auxc.reshape(bsz, seq, LANES), auxr,
                dn_conv_w.T, dn_norm_gain.reshape(1, DN_DIM))

    qk = MLA_NOPE + MLA_ROPE
    wq3 = w_q_up.reshape(MLA_Q_RANK, MLA_HEADS, qk)
    wq = jnp.concatenate([wq3, jnp.zeros((MLA_Q_RANK, MLA_HEADS, MLA_QK_PAD - qk), F32)], axis=2)
    wq = wq.reshape(MLA_Q_RANK, MLA_HEADS * MLA_QK_PAD).astype(BF16)
    wkv3 = w_kv_up.reshape(MLA_KV_RANK, MLA_HEADS, MLA_NOPE + MLA_V)
    wkv = jnp.concatenate([wkv3[:, :, :MLA_NOPE].reshape(MLA_KV_RANK, -1),
                           wkv3[:, :, MLA_NOPE:].reshape(MLA_KV_RANK, -1)], axis=1).astype(BF16)
    half = MLA_ROPE // 2
    inv_freq = ROPE_THETA ** (-jnp.arange(half, dtype=F32) / half)
    zeros = jnp.zeros((LANES - MLA_ROPE,), F32)
    rope_tab = jnp.zeros((8, LANES), F32)
    rope_tab = rope_tab.at[0].set(jnp.concatenate([inv_freq, inv_freq, zeros]))
    rope_tab = rope_tab.at[1].set(jnp.concatenate([jnp.ones((MLA_ROPE,), F32), zeros]))
    rope_tab = rope_tab.at[2].set(jnp.concatenate([-jnp.ones((half,), F32), jnp.ones((half,), F32), zeros]))
    q, k, v = _mla_proj(main, auxc, positions.reshape(t, 1), mla_q_norm_gain.reshape(1, -1),
                        mla_kv_norm_gain.reshape(1, -1), wq, wkv, rope_tab, seq)
    mla = _attention(q.reshape(bsz, seq, -1), k.reshape(bsz, seq, -1), v.reshape(bsz, seq, -1))

    w_r = jnp.concatenate([w_router, jnp.zeros((d, LANES - N_EXPERTS), F32)], axis=1)
    x1, h2, logits_t = _out_proj(x2, dn.reshape(t, DN_WIDTH), mla.reshape(t, -1), ada3,
                                 norm2_gain.reshape(1, d), w_out.astype(BF16), w_r, seq)

    n_rows = -(-(t * TOP_K + N_EXPERTS * (EXPERT_BLOCK - 1)) // EXPERT_BLOCK) * EXPERT_BLOCK
    n_blocks = n_rows // EXPERT_BLOCK
    assert n_blocks <= MAX_BLOCK_LANES
    ek, rk, _, wc, seg_start, blocks = _route_pallas(logits_t, router_bias.reshape(N_EXPERTS, 1))
    slot = _slots(ek, rk, seg_start)[:TOP_K]
    tok = jnp.broadcast_to(jnp.arange(t, dtype=jnp.int32)[None, :], (TOP_K, t))
    row_tok = jnp.zeros((n_rows,), jnp.int32).at[slot.reshape(-1)].set(tok.reshape(-1))
    x_sorted = jnp.take(h2, row_tok, axis=0)
    y_sorted = _experts(blocks[0, :n_blocks], blocks[1, :n_blocks], x_sorted, w_exp_gate_up, w_exp_down)
    y_tok = jnp.take(y_sorted, slot, axis=0)

    out = _final(x1, h2, y_tok, wc, ada3, w_sh_gate_up.astype(BF16), w_sh_down.astype(BF16),
                 out_gain.reshape(1, d), seq)
    return out.reshape(bsz, seq, d)


def kernel(x, c, positions, w_ada, b_ada, norm1_gain, w_in, dn_conv_w, dn_a_log, dn_dt_bias, dn_norm_gain,
           mla_q_norm_gain, w_q_up, mla_kv_norm_gain, w_kv_up, w_out, norm2_gain, w_router, router_bias,
           w_exp_gate_up, w_exp_down, w_sh_gate_up, w_sh_down, final_norm_gain):
    depth = w_ada.shape[0]
    assert depth == 1, "the final RMSNorm is fused into the single layer's last kernel"
    ada = _ada(c, w_ada[0], b_ada[0])
    return _layer(x, ada, positions, w_in[0], dn_conv_w[0], dn_a_log[0], dn_dt_bias[0], dn_norm_gain[0],
                  mla_q_norm_gain[0], w_q_up[0], mla_kv_norm_gain[0], w_kv_up[0], w_out[0], norm1_gain[0],
                  norm2_gain[0], w_router[0], router_bias[0], w_exp_gate_up[0], w_exp_down[0],
                  w_sh_gate_up[0], w_sh_down[0], final_norm_gain)
```

```python
import functools

import jax
import jax.numpy as jnp
from jax import lax
from jax.experimental import pallas as pl
from jax.experimental.pallas import tpu as pltpu

F32 = jnp.float32
BF16 = jnp.bfloat16

DN_HEADS = 8
DN_DIM = 128
DN_WIDTH = DN_HEADS * DN_DIM
DN_CONV = 4
DN_CHUNK = 64
MLA_HEADS = 8
MLA_Q_RANK = 512
MLA_KV_RANK = 512
MLA_NOPE = 128
MLA_ROPE = 64
MLA_V = 128
MLA_QK_PAD = 256
ROPE_THETA = 10000.0
N_EXPERTS = 64
TOP_K = 6
N_GROUPS = 8
TOPK_GROUPS = 4
EXPERT_FF = 512
ROUTED_SCALE = 2.5
EXPERT_BLOCK = 256
RMS_EPS = 1e-6
L2_EPS = 1e-6

LANES = 128
MAIN_WIDTH = 4 * DN_WIDTH + MLA_Q_RANK + MLA_KV_RANK
AUX_BETA = MLA_ROPE
AUX_G = MLA_ROPE + DN_HEADS
VMEM_LIMIT = 56 * 1024 * 1024


def _params(n_parallel, n_arbitrary=0):
    sem = ("parallel",) * n_parallel + ("arbitrary",) * n_arbitrary
    return pltpu.CompilerParams(dimension_semantics=sem, vmem_limit_bytes=VMEM_LIMIT)


def _silu(x):
    return x * jax.nn.sigmoid(x)


def _bdot(a, b):
    return jnp.dot(a.astype(BF16), b.astype(BF16), preferred_element_type=F32)


def _bdot_nt(a, b):
    return lax.dot_general(a.astype(BF16), b.astype(BF16), (((1,), (1,)), ((), ())),
                           preferred_element_type=F32)


def _bdot_tn(a, b):
    return lax.dot_general(a.astype(BF16), b.astype(BF16), (((0,), (0,)), ((), ())),
                           preferred_element_type=F32)


def _rms(x, gain):
    return x * lax.rsqrt(jnp.mean(x * x, axis=-1, keepdims=True) + RMS_EPS) * gain


def _ada_kernel(c_ref, w_ref, b_ref, o_ref):
    o_ref[...] = _bdot(_silu(c_ref[...]), w_ref[...]) + b_ref[...]


def _ada(c, w_ada, b_ada):
    bsz, d = c.shape
    n = w_ada.shape[1]
    tn = 1024
    return pl.pallas_call(
        _ada_kernel,
        grid=(n // tn,),
        in_specs=[pl.BlockSpec((bsz, d), lambda j: (0, 0)),
                  pl.BlockSpec((d, tn), lambda j: (0, j)),
                  pl.BlockSpec((1, tn), lambda j: (0, j))],
        out_specs=pl.BlockSpec((bsz, tn), lambda j: (0, j)),
        out_shape=jax.ShapeDtypeStruct((bsz, n), F32),
        compiler_params=_params(1),
        name="ada",
    )(c, w_ada, b_ada.reshape(1, n))


def _in_proj_kernel(x_ref, ada_ref, gain_ref, wm_ref, wa_ref, auxp_ref,
                    main_ref, auxc_ref, auxr_ref, h_ref):
    j = pl.program_id(1)

    @pl.when(j == 0)
    def _():
        ada = ada_ref[0]
        h = _rms(x_ref[...], gain_ref[...]) * (1.0 + ada[1:2, :]) + ada[0:1, :]
        hb = h.astype(BF16)
        h_ref[...] = hb
        aux = jnp.dot(hb, wa_ref[...], preferred_element_type=F32)
        tm = aux.shape[0]
        lane = lax.broadcasted_iota(jnp.int32, aux.shape, 1)
        is_beta = (lane >= AUX_BETA) & (lane < AUX_G)
        is_g = (lane >= AUX_G) & (lane < AUX_G + DN_HEADS)
        a_log = auxp_ref[0:1, :]
        dt_bias = auxp_ref[1:2, :]
        sp_in = aux + dt_bias
        softplus = jnp.maximum(sp_in, 0.0) + jnp.log(1.0 + jnp.exp(-jnp.abs(sp_in)))
        g = -jnp.exp(a_log) * softplus
        aux = jnp.where(is_beta, jax.nn.sigmoid(aux), jnp.where(is_g, g, aux))
        r = lax.broadcasted_iota(jnp.int32, (LANES, LANES), 0)
        cidx = lax.broadcasted_iota(jnp.int32, (LANES, LANES), 1)
        tri = ((cidx <= r) & (cidx // DN_CHUNK == r // DN_CHUNK)).astype(F32)
        g_cols = (cidx >= AUX_G) & (cidx < AUX_G + DN_HEADS)
        parts = []
        for t in range(tm // LANES):
            blk = aux[t * LANES:(t + 1) * LANES, :]
            cs = jnp.dot(tri, blk, precision=lax.Precision.HIGHEST, preferred_element_type=F32)
            parts.append(jnp.where(g_cols, cs, blk))
        aux = jnp.concatenate(parts, axis=0)
        auxc_ref[...] = aux
        auxr_ref[...] = aux.T

    main_ref[...] = jnp.dot(h_ref[...], wm_ref[...], preferred_element_type=F32).astype(main_ref.dtype)


def _in_proj(x2, ada3, gain, w_main, w_aux, auxp, seq):
    t, d = x2.shape
    n = w_main.shape[1]
    tm = min(512, seq)
    tn = 1024
    per_b = seq // tm
    return pl.pallas_call(
        _in_proj_kernel,
        grid=(t // tm, n // tn),
        in_specs=[pl.BlockSpec((tm, d), lambda i, j: (i, 0)),
                  pl.BlockSpec((1, 6, d), lambda i, j: (i // per_b, 0, 0)),
                  pl.BlockSpec((1, d), lambda i, j: (0, 0)),
                  pl.BlockSpec((d, tn), lambda i, j: (0, j)),
                  pl.BlockSpec((d, LANES), lambda i, j: (0, 0)),
                  pl.BlockSpec((8, LANES), lambda i, j: (0, 0))],
        out_specs=[pl.BlockSpec((tm, tn), lambda i, j: (i, j)),
                   pl.BlockSpec((tm, LANES), lambda i, j: (i, 0)),
                   pl.BlockSpec((LANES, tm), lambda i, j: (0, i))],
        out_shape=[jax.ShapeDtypeStruct((t, n), BF16),
                   jax.ShapeDtypeStruct((t, LANES), F32),
                   jax.ShapeDtypeStruct((LANES, t), F32)],
        scratch_shapes=[pltpu.VMEM((tm, d), BF16)],
        compiler_params=_params(1, 1),
        name="in_proj",
    )(x2, ada3, gain, w_main, w_aux, auxp)


def _unit_lower_inverses(ms):
    c = ms[0].shape[0]
    r = lax.broadcasted_iota(jnp.int32, (c, c), 0)
    cc = lax.broadcasted_iota(jnp.int32, (c, c), 1)
    eye = (r == cc).astype(F32)
    same = (r // 16) == (cc // 16)
    md = [jnp.where(same, m, 0.0) for m in ms]
    mo = [jnp.where(same, 0.0, m) for m in ms]
    p1 = [_bdot(a, a) for a in md]
    p2 = [_bdot(a, a) for a in p1]
    p3 = [_bdot(a, a) for a in p2]
    td = [eye - a for a in md]
    td = [t + _bdot(t, p) for t, p in zip(td, p1)]
    td = [t + _bdot(t, p) for t, p in zip(td, p2)]
    td = [t + _bdot(t, p) for t, p in zip(td, p3)]
    n1 = [_bdot(t, o) for t, o in zip(td, mo)]
    n2 = [_bdot(n, n) for n in n1]
    left = [(eye - a) + _bdot(eye - a, b) for a, b in zip(n1, n2)]
    return [_bdot(l, t) for l, t in zip(left, td)]


DELTA_TILE = 512
DELTA_SCAN_HEADS = 4


def _delta_prep_kernel(q_ref, k_ref, v_ref, qh_ref, kh_ref, vh_ref, auxc_ref, auxr_ref, wq_ref, wk_ref, wv_ref,
                       u_ref, w_ref, qe_ref, kd_ref, a_ref):
    head = pl.program_id(1)
    first_tile = pl.program_id(2) == 0
    ts = q_ref.shape[0]
    c = DN_CHUNK
    halo = qh_ref.shape[0]

    def conv_silu(x_ref, h_ref, w_ref):
        prev = jnp.where(first_tile, 0.0, h_ref[...].astype(F32))
        x = jnp.concatenate([prev, x_ref[...].astype(F32)], axis=0)
        w = w_ref[...]
        acc = x * w[DN_CONV - 1:DN_CONV, :]
        for s in range(1, DN_CONV):
            acc = acc + pltpu.roll(x, s, 0) * w[DN_CONV - 1 - s:DN_CONV - s, :]
        return _silu(acc[halo:, :])

    q = conv_silu(q_ref, qh_ref, wq_ref)
    k = conv_silu(k_ref, kh_ref, wk_ref)
    v = conv_silu(v_ref, vh_ref, wv_ref)
    q = q * lax.rsqrt(jnp.sum(q * q, axis=-1, keepdims=True) + L2_EPS) * (DN_DIM ** -0.5)
    k = k * lax.rsqrt(jnp.sum(k * k, axis=-1, keepdims=True) + L2_EPS)

    auxc = auxc_ref[...]
    lane = lax.broadcasted_iota(jnp.int32, auxc.shape, 1)
    beta_all = jnp.sum(jnp.where(lane == AUX_BETA + head, auxc, 0.0), axis=1, keepdims=True)
    gcol_all = jnp.sum(jnp.where(lane == AUX_G + head, auxc, 0.0), axis=1, keepdims=True)
    grows = auxr_ref[AUX_G:AUX_G + DN_HEADS, :]
    head_row = lax.broadcasted_iota(jnp.int32, grows.shape, 0)
    grow_all = jnp.sum(jnp.where(head_row == head, grows, 0.0), axis=0, keepdims=True)

    ri = lax.broadcasted_iota(jnp.int32, (c, c), 0)
    ci = lax.broadcasted_iota(jnp.int32, (c, c), 1)
    causal = ri >= ci
    strict = ri > ci
    chunks = [slice(j * c, (j + 1) * c) for j in range(ts // c)]
    decay = [jnp.exp(jnp.where(causal, gcol_all[rows] - grow_all[:, rows], -jnp.inf)) for rows in chunks]
    kb = [k[rows] * beta_all[rows] for rows in chunks]
    both = [_bdot_nt(jnp.concatenate([kb_j, q[rows]], axis=0), k[rows])
            for kb_j, rows in zip(kb, chunks)]
    tinv = _unit_lower_inverses([jnp.where(strict, b[:c] * d, 0.0) for b, d in zip(both, decay)])
    eg = [jnp.exp(gcol_all[rows]) for rows in chunks]
    sol = [_bdot(t, jnp.concatenate([v[rows] * beta_all[rows], kb_j * e], axis=1))
           for t, rows, kb_j, e in zip(tinv, chunks, kb, eg)]
    for j, rows in enumerate(chunks):
        gcol = gcol_all[rows]
        u_ref[rows, :] = sol[j][:, :DN_DIM].astype(u_ref.dtype)
        w_ref[rows, :] = sol[j][:, DN_DIM:].astype(w_ref.dtype)
        qe_ref[rows, :] = (q[rows] * eg[j]).astype(qe_ref.dtype)
        kd_ref[rows, :] = (k[rows] * jnp.exp(gcol[c - 1:c, :] - gcol)).astype(kd_ref.dtype)
        a_ref[rows, :] = jnp.where(causal, both[j][c:] * decay[j], 0.0).astype(a_ref.dtype)


def _delta_scan_kernel(u_ref, w_ref, qe_ref, kd_ref, a_ref, z_ref, auxc_ref, gain_ref, o_ref, st_ref):
    hb = pl.program_id(1)
    seq = u_ref.shape[0]
    n_heads = a_ref.shape[0]
    c = DN_CHUNK
    st_ref[...] = jnp.zeros_like(st_ref)
    gain = gain_ref[...]
    lane = lax.broadcasted_iota(jnp.int32, (1, LANES), 1)

    def scan_chunk(i, carry):
        start = pl.multiple_of(i * c, c)
        rows = pl.ds(start, c)
        last = auxc_ref[pl.ds(start + c - 1, 1), :]
        heads = range(n_heads)
        cols = [slice(hh * DN_DIM, (hh + 1) * DN_DIM) for hh in heads]
        state = [st_ref[hh] for hh in heads]
        sb = [s.astype(BF16) for s in state]
        ws = [jnp.dot(w_ref[rows, cl], s, preferred_element_type=F32) for cl, s in zip(cols, sb)]
        qs = [jnp.dot(qe_ref[rows, cl], s, preferred_element_type=F32) for cl, s in zip(cols, sb)]
        vb = [(u_ref[rows, cl].astype(F32) - x).astype(BF16) for cl, x in zip(cols, ws)]
        kv = [lax.dot_general(kd_ref[rows, cl], x, (((0,), (0,)), ((), ())), preferred_element_type=F32)
              for cl, x in zip(cols, vb)]
        av = [jnp.dot(a_ref[hh, rows, :], x, preferred_element_type=F32) for hh, x in zip(heads, vb)]
        for hh in heads:
            glast = jnp.sum(jnp.where(lane == AUX_G + hb * n_heads + hh, last, 0.0), axis=1, keepdims=True)
            st_ref[hh] = state[hh] * jnp.exp(glast) + kv[hh]
            z = z_ref[rows, cols[hh]].astype(F32)
            o_ref[rows, cols[hh]] = (_rms(qs[hh] + av[hh], gain) * _silu(z)).astype(o_ref.dtype)
        return carry

    lax.fori_loop(0, seq // c, scan_chunk, 0)


def _delta(main3, auxc3, auxr, conv_w_t, gain):
    bsz, seq, _ = main3.shape
    hb = DN_WIDTH // DN_DIM
    ts = min(DELTA_TILE, seq)
    halo = 16
    tiles = seq // ts

    def col(offset):
        return pl.BlockSpec((None, ts, DN_DIM), lambda b, h, i: (b, i, offset * hb + h))

    def halo_col(offset):
        return pl.BlockSpec((None, halo, DN_DIM),
                            lambda b, h, i: (b, jnp.maximum(i * (ts // halo) - 1, 0), offset * hb + h))

    def wcol(offset):
        return pl.BlockSpec((DN_CONV, DN_DIM), lambda b, h, i: (0, offset * hb + h))

    tok = pl.BlockSpec((None, ts, DN_DIM), lambda b, h, i: (b, i, h))
    tok_shape = jax.ShapeDtypeStruct((bsz, seq, DN_WIDTH), BF16)
    u, w, qe, kd, a = pl.pallas_call(
        _delta_prep_kernel,
        grid=(bsz, DN_HEADS, tiles),
        in_specs=[col(0), col(1), col(2), halo_col(0), halo_col(1), halo_col(2),
                  pl.BlockSpec((None, ts, LANES), lambda b, h, i: (b, i, 0)),
                  pl.BlockSpec((LANES, ts), lambda b, h, i: (0, b * tiles + i)),
                  wcol(0), wcol(1), wcol(2)],
        out_specs=[tok, tok, tok, tok,
                   pl.BlockSpec((None, None, ts, DN_CHUNK), lambda b, h, i: (b, h, i, 0))],
        out_shape=[tok_shape, tok_shape, tok_shape, tok_shape,
                   jax.ShapeDtypeStruct((bsz, DN_HEADS, seq, DN_CHUNK), BF16)],
        compiler_params=_params(3),
        name="delta_prep",
    )(main3, main3, main3, main3, main3, main3, auxc3, auxr, conv_w_t, conv_w_t, conv_w_t)

    hp = DELTA_SCAN_HEADS
    wide = pl.BlockSpec((None, seq, hp * DN_DIM), lambda b, g: (b, 0, g))
    return pl.pallas_call(
        _delta_scan_kernel,
        grid=(bsz, DN_HEADS // hp),
        in_specs=[wide, wide, wide, wide,
                  pl.BlockSpec((None, hp, seq, DN_CHUNK), lambda b, g: (b, g, 0, 0)),
                  pl.BlockSpec((None, seq, hp * DN_DIM), lambda b, g: (b, 0, 3 * (hb // hp) + g)),
                  pl.BlockSpec((None, seq, LANES), lambda b, g: (b, 0, 0)),
                  pl.BlockSpec((1, DN_DIM), lambda b, g: (0, 0))],
        out_specs=wide,
        out_shape=tok_shape,
        scratch_shapes=[pltpu.VMEM((hp, DN_DIM, DN_DIM), F32)],
        compiler_params=_params(2),
        name="delta_scan",
    )(u, w, qe, kd, a, main3, auxc3, gain)


def _mla_proj_kernel(cq_ref, ckv_ref, auxc_ref, pos_ref, qg_ref, kvg_ref, wq_ref, wkv_ref, rope_ref,
                     q_ref, k_ref, v_ref):
    scale = (MLA_NOPE + MLA_ROPE) ** -0.5
    ang = pos_ref[...].astype(F32) * rope_ref[0:1, :]
    cos_t = jnp.cos(ang) * rope_ref[1:2, :]
    sin_t = jnp.sin(ang) * rope_ref[2:3, :]
    lane = lax.broadcasted_iota(jnp.int32, ang.shape, 1)
    first = lane < MLA_ROPE // 2

    def rope(a):
        swapped = jnp.where(first, pltpu.roll(a, LANES - MLA_ROPE // 2, 1), pltpu.roll(a, MLA_ROPE // 2, 1))
        return a * cos_t + swapped * sin_t

    ql = _bdot(_rms(cq_ref[...].astype(F32), qg_ref[...]), wq_ref[...]) * scale
    kv = _bdot(_rms(ckv_ref[...].astype(F32), kvg_ref[...]), wkv_ref[...])
    kpe = rope(auxc_ref[...]).astype(k_ref.dtype)
    for h in range(MLA_HEADS):
        o = h * MLA_QK_PAD
        q_ref[:, o:o + MLA_NOPE] = ql[:, o:o + MLA_NOPE].astype(q_ref.dtype)
        q_ref[:, o + MLA_NOPE:o + MLA_QK_PAD] = rope(ql[:, o + MLA_NOPE:o + MLA_QK_PAD]).astype(q_ref.dtype)
        k_ref[:, o:o + MLA_NOPE] = kv[:, h * MLA_NOPE:(h + 1) * MLA_NOPE].astype(k_ref.dtype)
        k_ref[:, o + MLA_NOPE:o + MLA_QK_PAD] = kpe
    v_ref[...] = kv[:, MLA_HEADS * MLA_NOPE:].astype(v_ref.dtype)


def _mla_proj(main, auxc, pos, q_gain, kv_gain, wq, wkv, rope_tab, seq):
    t = main.shape[0]
    tm = min(512, seq)
    cq_blk = (4 * DN_WIDTH) // MLA_Q_RANK
    hq = MLA_HEADS * MLA_QK_PAD
    hv = MLA_HEADS * MLA_V
    return pl.pallas_call(
        _mla_proj_kernel,
        grid=(t // tm,),
        in_specs=[pl.BlockSpec((tm, MLA_Q_RANK), lambda i: (i, cq_blk)),
                  pl.BlockSpec((tm, MLA_KV_RANK), lambda i: (i, cq_blk + 1)),
                  pl.BlockSpec((tm, LANES), lambda i: (i, 0)),
                  pl.BlockSpec((tm, 1), lambda i: (i, 0)),
                  pl.BlockSpec((1, MLA_Q_RANK), lambda i: (0, 0)),
                  pl.BlockSpec((1, MLA_KV_RANK), lambda i: (0, 0)),
                  pl.BlockSpec((MLA_Q_RANK, hq), lambda i: (0, 0)),
                  pl.BlockSpec((MLA_KV_RANK, 2 * hv), lambda i: (0, 0)),
                  pl.BlockSpec((8, LANES), lambda i: (0, 0))],
        out_specs=[pl.BlockSpec((tm, hq), lambda i: (i, 0)),
                   pl.BlockSpec((tm, hq), lambda i: (i, 0)),
                   pl.BlockSpec((tm, hv), lambda i: (i, 0))],
        out_shape=[jax.ShapeDtypeStruct((t, hq), BF16),
                   jax.ShapeDtypeStruct((t, hq), BF16),
                   jax.ShapeDtypeStruct((t, hv), BF16)],
        compiler_params=_params(1),
        name="mla_proj",
    )(main, main, auxc, pos, q_gain, kv_gain, wq, wkv, rope_tab)


ATTN_PARTS = 2


def _attn_kernel(q_ref, k_ref, v_ref, o_ref, m_ref, l_ref, acc_ref):
    qi = pl.program_id(2)
    n_parts = m_ref.shape[0]
    tk = m_ref.shape[1]
    q = [q_ref[p * tk:(p + 1) * tk, :] for p in range(n_parts)]
    m_ref[...] = jnp.full_like(m_ref, -jnp.inf)
    l_ref[...] = jnp.zeros_like(l_ref)
    acc_ref[...] = jnp.zeros_like(acc_ref)
    r = lax.broadcasted_iota(jnp.int32, (tk, tk), 0)
    cc = lax.broadcasted_iota(jnp.int32, (tk, tk), 1)
    on_or_below = cc <= r

    def step(j, parts, masked_part):
        rows = pl.ds(pl.multiple_of(j * tk, tk), tk)
        kb = k_ref[rows, :]
        vb = v_ref[rows, :]
        s = [lax.dot_general(q[p], kb, (((1,), (1,)), ((), ())), preferred_element_type=F32) for p in parts]
        s = [jnp.where(on_or_below, x, -jnp.inf) if p == masked_part else x for p, x in zip(parts, s)]
        m_old = [m_ref[p] for p in parts]
        m_new = [jnp.maximum(mo, jnp.max(x, axis=-1, keepdims=True)) for mo, x in zip(m_old, s)]
        e = [jnp.exp(x - mn) for x, mn in zip(s, m_new)]
        pv = [jnp.dot(x.astype(BF16), vb, preferred_element_type=F32) for x in e]
        for i, p in enumerate(parts):
            alpha = jnp.exp(m_old[i] - m_new[i])
            l_ref[p] = alpha * l_ref[p] + jnp.sum(e[i], axis=-1, keepdims=True)
            acc_ref[p] = alpha * acc_ref[p] + pv[i]
            m_ref[p] = m_new[i]

    every = tuple(range(n_parts))

    def body(j, carry):
        step(j, every, None)
        return carry

    lax.fori_loop(0, qi * n_parts, body, 0)
    for d in range(n_parts):
        step(qi * n_parts + d, every[d:], d)
    for p in every:
        o_ref[p * tk:(p + 1) * tk, :] = (acc_ref[p] / l_ref[p]).astype(o_ref.dtype)


def _attention(q3, k3, v3):
    bsz, seq, _ = q3.shape
    tk = min(512, seq)
    n_parts = min(ATTN_PARTS, seq // tk)
    tq = tk * n_parts
    return pl.pallas_call(
        _attn_kernel,
        grid=(bsz, MLA_HEADS, seq // tq),
        in_specs=[pl.BlockSpec((None, tq, MLA_QK_PAD), lambda b, h, i: (b, i, h)),
                  pl.BlockSpec((None, seq, MLA_QK_PAD), lambda b, h, i: (b, 0, h)),
                  pl.BlockSpec((None, seq, MLA_V), lambda b, h, i: (b, 0, h))],
        out_specs=pl.BlockSpec((None, tq, MLA_V), lambda b, h, i: (b, i, h)),
        out_shape=jax.ShapeDtypeStruct((bsz, seq, MLA_HEADS * MLA_V), BF16),
        scratch_shapes=[pltpu.VMEM((n_parts, tk, 1), F32), pltpu.VMEM((n_parts, tk, 1), F32),
                        pltpu.VMEM((n_parts, tk, MLA_V), F32)],
        compiler_params=_params(3),
        name="attention",
    )(q3, k3, v3)


def _out_proj_kernel(x_ref, dn_ref, mla_ref, ada_ref, gain_ref, w_ref, wr_ref, x1_ref, h2_ref, lg_ref):
    ada = ada_ref[0]
    half = dn_ref.shape[1]
    mix = (jnp.dot(dn_ref[...], w_ref[:half, :], preferred_element_type=F32)
           + jnp.dot(mla_ref[...], w_ref[half:, :], preferred_element_type=F32))
    x1 = x_ref[...] + ada[2:3, :] * mix
    x1_ref[...] = x1
    h2 = _rms(x1, gain_ref[...]) * (1.0 + ada[4:5, :]) + ada[3:4, :]
    h2_ref[...] = h2.astype(h2_ref.dtype)
    lg_ref[...] = jnp.dot(h2, wr_ref[...], precision=lax.Precision.HIGHEST, preferred_element_type=F32).T


def _out_proj(x2, dn, mla, ada3, gain, w_out, w_router, seq):
    t, d = x2.shape
    tm = min(256, seq)
    per_b = seq // tm
    half = dn.shape[1]
    return pl.pallas_call(
        _out_proj_kernel,
        grid=(t // tm,),
        in_specs=[pl.BlockSpec((tm, d), lambda i: (i, 0)),
                  pl.BlockSpec((tm, half), lambda i: (i, 0)),
                  pl.BlockSpec((tm, half), lambda i: (i, 0)),
                  pl.BlockSpec((1, 6, d), lambda i: (i // per_b, 0, 0)),
                  pl.BlockSpec((1, d), lambda i: (0, 0)),
                  pl.BlockSpec((2 * half, d), lambda i: (0, 0)),
                  pl.BlockSpec((d, LANES), lambda i: (0, 0))],
        out_specs=[pl.BlockSpec((tm, d), lambda i: (i, 0)),
                   pl.BlockSpec((tm, d), lambda i: (i, 0)),
                   pl.BlockSpec((LANES, tm), lambda i: (0, i))],
        out_shape=[jax.ShapeDtypeStruct((t, d), F32),
                   jax.ShapeDtypeStruct((t, d), BF16),
                   jax.ShapeDtypeStruct((LANES, t), F32)],
        compiler_params=_params(1),
        name="out_proj",
    )(x2, dn, mla, ada3, gain, w_out, w_router)


def _expert_kernel(be_ref, nv_ref, x_ref, wgu_ref, wd_ref, y_ref, gu_s, d_s):
    i = pl.program_id(0)
    changed = (i == 0) | (be_ref[i] != be_ref[jnp.maximum(i - 1, 0)])

    @pl.when(changed)
    def _():
        gu_s[...] = wgu_ref[...].astype(BF16)
        d_s[...] = wd_ref[...].astype(BF16)

    gu = jnp.dot(x_ref[...], gu_s[...], preferred_element_type=F32)
    ff = gu.shape[1] // 2
    act = _silu(gu[:, :ff]) * gu[:, ff:]
    y = jnp.dot(act.astype(BF16), d_s[...], preferred_element_type=F32)
    row = lax.broadcasted_iota(jnp.int32, y.shape, 0)
    y_ref[...] = jnp.where(row < nv_ref[i], y, 0.0).astype(y_ref.dtype)


def _experts(block_e, n_valid, x_sorted, w_gu, w_d):
    n_rows, d = x_sorted.shape
    ff2 = w_gu.shape[2]
    n_blocks = n_rows // EXPERT_BLOCK
    grid_spec = pltpu.PrefetchScalarGridSpec(
        num_scalar_prefetch=2,
        grid=(n_blocks,),
        in_specs=[pl.BlockSpec((EXPERT_BLOCK, d), lambda i, be, nv: (i, 0)),
                  pl.BlockSpec((None, d, ff2), lambda i, be, nv: (be[i], 0, 0)),
                  pl.BlockSpec((None, ff2 // 2, d), lambda i, be, nv: (be[i], 0, 0))],
        out_specs=pl.BlockSpec((EXPERT_BLOCK, d), lambda i, be, nv: (i, 0)),
        scratch_shapes=[pltpu.VMEM((d, ff2), BF16), pltpu.VMEM((ff2 // 2, d), BF16)],
    )
    return pl.pallas_call(
        _expert_kernel,
        grid_spec=grid_spec,
        out_shape=jax.ShapeDtypeStruct((n_rows, d), BF16),
        compiler_params=_params(0, 1),
        name="experts",
    )(block_e, n_valid, x_sorted, w_gu, w_d)


def _final_kernel(x1_ref, h2_ref, y_ref, wc_ref, ada_ref, wgu_ref, wd_ref, gain_ref, o_ref):
    ada = ada_ref[0]
    gu = jnp.dot(h2_ref[...], wgu_ref[...], preferred_element_type=F32)
    ff = gu.shape[1] // 2
    act = _silu(gu[:, :ff]) * gu[:, ff:]
    ffn = jnp.dot(act.astype(BF16), wd_ref[...], preferred_element_type=F32)
    wc = wc_ref[...]
    for kk in range(TOP_K):
        ffn = ffn + wc[:, kk:kk + 1] * y_ref[kk].astype(F32)
    x2 = x1_ref[...] + ada[5:6, :] * ffn
    o_ref[...] = _rms(x2, gain_ref[...])


def _final(x1, h2, y_tok, wc, ada3, w_gu, w_d, gain, seq):
    t, d = x1.shape
    tm = min(256, seq)
    per_b = seq // tm
    ff2 = w_gu.shape[1]
    return pl.pallas_call(
        _final_kernel,
        grid=(t // tm,),
        in_specs=[pl.BlockSpec((tm, d), lambda i: (i, 0)),
                  pl.BlockSpec((tm, d), lambda i: (i, 0)),
                  pl.BlockSpec((TOP_K, tm, d), lambda i: (0, i, 0)),
                  pl.BlockSpec((tm, LANES), lambda i: (i, 0)),
                  pl.BlockSpec((1, 6, d), lambda i: (i // per_b, 0, 0)),
                  pl.BlockSpec((d, ff2), lambda i: (0, 0)),
                  pl.BlockSpec((ff2 // 2, d), lambda i: (0, 0)),
                  pl.BlockSpec((1, d), lambda i: (0, 0))],
        out_specs=pl.BlockSpec((tm, d), lambda i: (i, 0)),
        out_shape=jax.ShapeDtypeStruct((t, d), F32),
        compiler_params=_params(1),
        name="final",
    )(x1, h2, y_tok, wc, ada3, w_gu, w_d, gain)


ROUTE_TILE = 512
MAX_BLOCK_LANES = 512


def _first_argmax(x, idx_f, n):
    m = jnp.max(x, axis=0, keepdims=True)
    first = jnp.min(jnp.where(x == m, idx_f, float(n)), axis=0, keepdims=True)
    return m, first


def _route_kernel(lg_ref, bias_ref, ek_ref, rk_ref, wk_ref, wc_ref, ps_ref, be_ref, carry_ref, upper_ref):
    i = pl.program_id(0)
    tm = lg_ref.shape[1]
    per_group = N_EXPERTS // N_GROUPS

    @pl.when(i == 0)
    def _():
        carry_ref[...] = jnp.zeros_like(carry_ref)
        r = lax.broadcasted_iota(jnp.int32, (tm, tm), 0)
        cc = lax.broadcasted_iota(jnp.int32, (tm, tm), 1)
        upper_ref[...] = (r < cc).astype(BF16)

    scores = jax.nn.sigmoid(lg_ref[0:N_EXPERTS, :])
    biased = scores + bias_ref[:, 0:1]
    sub8 = lax.broadcasted_iota(jnp.int32, (per_group, tm), 0).astype(F32)
    group_rows = []
    for g in range(N_GROUPS):
        xg = biased[g * per_group:(g + 1) * per_group, :]
        m1, i1 = _first_argmax(xg, sub8, per_group)
        m2 = jnp.max(jnp.where(sub8 == i1, -jnp.inf, xg), axis=0, keepdims=True)
        group_rows.append(m1 + m2)
    cur = jnp.concatenate(group_rows, axis=0)
    gself = jnp.zeros(cur.shape, F32)
    for _ in range(TOPK_GROUPS):
        _, gi = _first_argmax(cur, sub8, N_GROUPS)
        hit = sub8 == gi
        gself = jnp.where(hit, 1.0, gself)
        cur = jnp.where(hit, -jnp.inf, cur)
    masked = jnp.concatenate(
        [jnp.where(gself[g:g + 1, :] > 0.5, biased[g * per_group:(g + 1) * per_group, :], -jnp.inf)
         for g in range(N_GROUPS)], axis=0)
    sub64 = lax.broadcasted_iota(jnp.int32, (N_EXPERTS, tm), 0).astype(F32)
    e_rows, s_rows = [], []
    sel = jnp.zeros(masked.shape, F32)
    for _ in range(TOP_K):
        _, ei = _first_argmax(masked, sub64, N_EXPERTS)
        hit = sub64 == ei
        e_rows.append(ei)
        s_rows.append(jnp.sum(jnp.where(hit, scores, 0.0), axis=0, keepdims=True))
        sel = jnp.where(hit, 1.0, sel)
        masked = jnp.where(hit, -jnp.inf, masked)
    total = s_rows[0]
    for s in s_rows[1:]:
        total = total + s
    w_rows = [s / total * ROUTED_SCALE for s in s_rows]
    before = jnp.dot(sel.astype(BF16), upper_ref[...], preferred_element_type=F32) + carry_ref[:, 0:1]
    r_rows = [jnp.sum(jnp.where(sub64 == ei, before, 0.0), axis=0, keepdims=True) for ei in e_rows]
    zrow = jnp.zeros((8 - TOP_K, tm), F32)
    ek_ref[...] = jnp.concatenate(e_rows + [zrow], axis=0).astype(jnp.int32)
    rk_ref[...] = jnp.concatenate(r_rows + [zrow], axis=0).astype(jnp.int32)
    wk = jnp.concatenate(w_rows + [zrow], axis=0)
    wk_ref[...] = wk
    wc_ref[...] = jnp.concatenate([wk, jnp.zeros((LANES - 8, tm), F32)], axis=0).T
    carry_ref[...] = carry_ref[...] + jnp.sum(sel, axis=1, keepdims=True)

    @pl.when(i == pl.num_programs(0) - 1)
    def _():
        counts = carry_ref[...]
        padded = jnp.floor((counts + (EXPERT_BLOCK - 1.0)) * (1.0 / EXPERT_BLOCK)) * EXPERT_BLOCK
        r = lax.broadcasted_iota(jnp.int32, (N_EXPERTS, N_EXPERTS), 0)
        cc = lax.broadcasted_iota(jnp.int32, (N_EXPERTS, N_EXPERTS), 1)
        pad_end = jnp.dot((cc <= r).astype(F32), padded, precision=lax.Precision.HIGHEST,
                          preferred_element_type=F32)
        pad_start = pad_end - padded
        ps_ref[...] = pad_start
        blk0 = lax.broadcasted_iota(jnp.int32, (N_EXPERTS, MAX_BLOCK_LANES), 1).astype(F32) * EXPERT_BLOCK
        e_of_blk = jnp.minimum(jnp.sum((pad_end[:, 0:1] <= blk0).astype(F32), axis=0, keepdims=True),
                               N_EXPERTS - 1.0)
        sub = lax.broadcasted_iota(jnp.int32, (N_EXPERTS, MAX_BLOCK_LANES), 0).astype(F32)
        seg_end = jnp.sum(jnp.where(sub == e_of_blk, (pad_start + counts)[:, 0:1], 0.0), axis=0, keepdims=True)
        n_valid = jnp.clip(seg_end - blk0[0:1, :], 0.0, float(EXPERT_BLOCK))
        be_ref[...] = jnp.concatenate([e_of_blk, n_valid, jnp.zeros((6, MAX_BLOCK_LANES), F32)],
                                      axis=0).astype(jnp.int32)


def _route_pallas(logits_t, bias_col):
    t = logits_t.shape[1]
    tm = min(ROUTE_TILE, t)
    row8 = pl.BlockSpec((8, tm), lambda i: (0, i))
    return pl.pallas_call(
        _route_kernel,
        grid=(t // tm,),
        in_specs=[pl.BlockSpec((LANES, tm), lambda i: (0, i)),
                  pl.BlockSpec((N_EXPERTS, 1), lambda i: (0, 0))],
        out_specs=[row8, row8, row8,
                   pl.BlockSpec((tm, LANES), lambda i: (i, 0)),
                   pl.BlockSpec((N_EXPERTS, LANES), lambda i: (0, 0)),
                   pl.BlockSpec((8, MAX_BLOCK_LANES), lambda i: (0, 0))],
        out_shape=[jax.ShapeDtypeStruct((8, t), jnp.int32),
                   jax.ShapeDtypeStruct((8, t), jnp.int32),
                   jax.ShapeDtypeStruct((8, t), F32),
                   jax.ShapeDtypeStruct((t, LANES), F32),
                   jax.ShapeDtypeStruct((N_EXPERTS, LANES), F32),
                   jax.ShapeDtypeStruct((8, MAX_BLOCK_LANES), jnp.int32)],
        scratch_shapes=[pltpu.VMEM((N_EXPERTS, LANES), F32), pltpu.VMEM((tm, tm), BF16)],
        compiler_params=_params(0, 1),
        name="route",
    )(logits_t, bias_col)


def _slot_kernel(ek_ref, rk_ref, ps_ref, slot_ref):
    tm = ek_ref.shape[1]
    sub64 = lax.broadcasted_iota(jnp.int32, (N_EXPERTS, tm), 0)
    start = ps_ref[:, 0:1]
    rows = []
    for kk in range(TOP_K):
        seg = jnp.sum(jnp.where(sub64 == ek_ref[kk:kk + 1, :], start, 0.0), axis=0, keepdims=True)
        rows.append(seg.astype(jnp.int32) + rk_ref[kk:kk + 1, :])
    slot_ref[...] = jnp.concatenate(rows + [jnp.zeros((8 - TOP_K, tm), jnp.int32)], axis=0)


def _slots(ek, rk, ps):
    t = ek.shape[1]
    tm = min(ROUTE_TILE, t)
    row8 = pl.BlockSpec((8, tm), lambda i: (0, i))
    return pl.pallas_call(
        _slot_kernel,
        grid=(t // tm,),
        in_specs=[row8, row8, pl.BlockSpec((N_EXPERTS, LANES), lambda i: (0, 0))],
        out_specs=row8,
        out_shape=jax.ShapeDtypeStruct((8, t), jnp.int32),
        compiler_params=_params(1),
        name="slots",
    )(ek, rk, ps)


def _layer(x, cond_ada, positions, w_in, dn_conv_w, dn_a_log, dn_dt_bias, dn_norm_gain, mla_q_norm_gain,
           w_q_up, mla_kv_norm_gain, w_kv_up, w_out, norm1_gain, norm2_gain, w_router, router_bias,
           w_exp_gate_up, w_exp_down, w_sh_gate_up, w_sh_down, out_gain):
    bsz, seq, d = x.shape
    t = bsz * seq
    x2 = x.reshape(t, d)
    ada3 = cond_ada.reshape(bsz, 6, d)

    s_z = 4 * DN_WIDTH
    s_a = s_z + 2 * DN_HEADS
    s_kpe = s_a + MLA_Q_RANK + MLA_KV_RANK
    w_main = jnp.concatenate([w_in[:, :s_z], w_in[:, s_a:s_kpe]], axis=1).astype(BF16)
    w_aux = jnp.concatenate([w_in[:, s_kpe:], w_in[:, s_z:s_a],
                             jnp.zeros((d, LANES - MLA_ROPE - 2 * DN_HEADS), F32)], axis=1).astype(BF16)
    auxp = jnp.zeros((8, LANES), F32)
    auxp = auxp.at[0, AUX_G:AUX_G + DN_HEADS].set(dn_a_log).at[1, AUX_G:AUX_G + DN_HEADS].set(dn_dt_bias)
    main, auxc, auxr = _in_proj(x2, ada3, norm1_gain.reshape(1, d), w_main, w_aux, auxp, seq)

    dn = _delta(main.reshape(bsz, seq, MAIN_WIDTH), auxc.reshape(bsz, seq, LANES), auxr,
                dn_conv_w.T, dn_norm_gain.reshape(1, DN_DIM))

    qk = MLA_NOPE + MLA_ROPE
    wq3 = w_q_up.reshape(MLA_Q_RANK, MLA_HEADS, qk)
    wq = jnp.concatenate([wq3, jnp.zeros((MLA_Q_RANK, MLA_HEADS, MLA_QK_PAD - qk), F32)], axis=2)
    wq = wq.reshape(MLA_Q_RANK, MLA_HEADS * MLA_QK_PAD).astype(BF16)
    wkv3 = w_kv_up.reshape(MLA_KV_RANK, MLA_HEADS, MLA_NOPE + MLA_V)
    wkv = jnp.concatenate([wkv3[:, :, :MLA_NOPE].reshape(MLA_KV_RANK, -1),
                           wkv3[:, :, MLA_NOPE:].reshape(MLA_KV_RANK, -1)], axis=1).astype(BF16)
    half = MLA_ROPE // 2
    inv_freq = ROPE_THETA ** (-jnp.arange(half, dtype=F32) / half)
    zeros = jnp.zeros((LANES - MLA_ROPE,), F32)
    rope_tab = jnp.zeros((8, LANES), F32)
    rope_tab = rope_tab.at[0].set(jnp.concatenate([inv_freq, inv_freq, zeros]))
    rope_tab = rope_tab.at[1].set(jnp.concatenate([jnp.ones((MLA_ROPE,), F32), zeros]))
    rope_tab = rope_tab.at[2].set(jnp.concatenate([-jnp.ones((half,), F32), jnp.ones((half,), F32), zeros]))
    q, k, v = _mla_proj(main, auxc, positions.reshape(t, 1), mla_q_norm_gain.reshape(1, -1),
                        mla_kv_norm_gain.reshape(1, -1), wq, wkv, rope_tab, seq)
    mla = _attention(q.reshape(bsz, seq, -1), k.reshape(bsz, seq, -1), v.reshape(bsz, seq, -1))

    w_r = jnp.concatenate([w_router, jnp.zeros((d, LANES - N_EXPERTS), F32)], axis=1)
    x1, h2, logits_t = _out_proj(x2, dn.reshape(t, DN_WIDTH), mla.reshape(t, -1), ada3,
                                 norm2_gain.reshape(1, d), w_out.astype(BF16), w_r, seq)

    n_rows = -(-(t * TOP_K + N_EXPERTS * (EXPERT_BLOCK - 1)) // EXPERT_BLOCK) * EXPERT_BLOCK
    n_blocks = n_rows // EXPERT_BLOCK
    assert n_blocks <= MAX_BLOCK_LANES
    ek, rk, _, wc, seg_start, blocks = _route_pallas(logits_t, router_bias.reshape(N_EXPERTS, 1))
    slot = _slots(ek, rk, seg_start)[:TOP_K]
    tok = jnp.broadcast_to(jnp.arange(t, dtype=jnp.int32)[None, :], (TOP_K, t))
    row_tok = jnp.zeros((n_rows,), jnp.int32).at[slot.reshape(-1)].set(tok.reshape(-1))
    x_sorted = jnp.take(h2, row_tok, axis=0)
    y_sorted = _experts(blocks[0, :n_blocks], blocks[1, :n_blocks], x_sorted, w_exp_gate_up, w_exp_down)
    y_tok = jnp.take(y_sorted, slot, axis=0)

    out = _final(x1, h2, y_tok, wc, ada3, w_sh_gate_up.astype(BF16), w_sh_down.astype(BF16),
                 out_gain.reshape(1, d), seq)
    return out.reshape(bsz, seq, d)


def kernel(x, c, positions, w_ada, b_ada, norm1_gain, w_in, dn_conv_w, dn_a_log, dn_dt_bias, dn_norm_gain,
           mla_q_norm_gain, w_q_up, mla_kv_norm_gain, w_kv_up, w_out, norm2_gain, w_router, router_bias,
           w_exp_gate_up, w_exp_down, w_sh_gate_up, w_sh_down, final_norm_gain):
    depth = w_ada.shape[0]
    assert depth == 1, "the final RMSNorm is fused into the single layer's last kernel"
    ada = _ada(c, w_ada[0], b_ada[0])
    return _layer(x, ada, positions, w_in[0], dn_conv_w[0], dn_a_log[0], dn_dt_bias[0], dn_norm_gain[0],
                  mla_q_norm_gain[0], w_q_up[0], mla_kv_norm_gain[0], w_kv_up[0], w_out[0], norm1_gain[0],
                  norm2_gain[0], w_router[0], router_bias[0], w_exp_gate_up[0], w_exp_down[0],
                  w_sh_gate_up[0], w_sh_down[0], final_norm_gain)
```

```python
import functools

import jax
import jax.numpy as jnp
from jax import lax
from jax.experimental import pallas as pl
from jax.experimental.pallas import tpu as pltpu
from jax.experimental.pallas import tpu_sc as plsc

F32 = jnp.float32
BF16 = jnp.bfloat16

DN_HEADS = 8
DN_DIM = 128
DN_WIDTH = DN_HEADS * DN_DIM
DN_CONV = 4
DN_CHUNK = 64
MLA_HEADS = 8
MLA_Q_RANK = 512
MLA_KV_RANK = 512
MLA_NOPE = 128
MLA_ROPE = 64
MLA_V = 128
MLA_QK_PAD = 256
ROPE_THETA = 10000.0
N_EXPERTS = 64
TOP_K = 6
N_GROUPS = 8
TOPK_GROUPS = 4
EXPERT_FF = 512
ROUTED_SCALE = 2.5
EXPERT_BLOCK = 256
RMS_EPS = 1e-6
L2_EPS = 1e-6

LANES = 128
MAIN_WIDTH = 4 * DN_WIDTH + MLA_Q_RANK + MLA_KV_RANK
AUX_BETA = MLA_ROPE
AUX_G = MLA_ROPE + DN_HEADS
VMEM_LIMIT = 56 * 1024 * 1024


def _params(n_parallel, n_arbitrary=0):
    sem = ("parallel",) * n_parallel + ("arbitrary",) * n_arbitrary
    return pltpu.CompilerParams(dimension_semantics=sem, vmem_limit_bytes=VMEM_LIMIT)


def _silu(x):
    return x * jax.nn.sigmoid(x)


def _bdot(a, b):
    return jnp.dot(a.astype(BF16), b.astype(BF16), preferred_element_type=F32)


def _bdot_nt(a, b):
    return lax.dot_general(a.astype(BF16), b.astype(BF16), (((1,), (1,)), ((), ())),
                           preferred_element_type=F32)


def _bdot_tn(a, b):
    return lax.dot_general(a.astype(BF16), b.astype(BF16), (((0,), (0,)), ((), ())),
                           preferred_element_type=F32)


def _rms(x, gain):
    return x * lax.rsqrt(jnp.mean(x * x, axis=-1, keepdims=True) + RMS_EPS) * gain


HIGH_HALF = 0xFFFF0000


def _pack_bf16_pairs(x):
    n = x.shape[1] // 2
    bits = lax.bitcast_convert_type(x.astype(BF16).astype(F32), jnp.uint32)
    return (bits[:, :n] >> 16) | (bits[:, n:] & jnp.uint32(HIGH_HALF))


def _unpack_bf16_pairs(p):
    lo = lax.bitcast_convert_type(p << 16, F32)
    hi = lax.bitcast_convert_type(p & jnp.uint32(HIGH_HALF), F32)
    return jnp.concatenate([lo, hi], axis=1)


def _ada_kernel(c_ref, w_ref, b_ref, o_ref):
    o_ref[...] = _bdot(_silu(c_ref[...]), w_ref[...]) + b_ref[...]


def _ada(c, w_ada, b_ada):
    bsz, d = c.shape
    n = w_ada.shape[1]
    tn = 1024
    return pl.pallas_call(
        _ada_kernel,
        grid=(n // tn,),
        in_specs=[pl.BlockSpec((bsz, d), lambda j: (0, 0)),
                  pl.BlockSpec((d, tn), lambda j: (0, j)),
                  pl.BlockSpec((1, tn), lambda j: (0, j))],
        out_specs=pl.BlockSpec((bsz, tn), lambda j: (0, j)),
        out_shape=jax.ShapeDtypeStruct((bsz, n), F32),
        compiler_params=_params(1),
        name="ada",
    )(c, w_ada, b_ada.reshape(1, n))


def _in_proj_kernel(x_ref, ada_ref, gain_ref, wm_ref, wa_ref, auxp_ref,
                    main_ref, auxc_ref, auxr_ref, h_ref):
    j = pl.program_id(1)

    @pl.when(j == 0)
    def _():
        ada = ada_ref[0]
        h = _rms(x_ref[...], gain_ref[...]) * (1.0 + ada[1:2, :]) + ada[0:1, :]
        hb = h.astype(BF16)
        h_ref[...] = hb
        aux = jnp.dot(hb, wa_ref[...], preferred_element_type=F32)
        tm = aux.shape[0]
        lane = lax.broadcasted_iota(jnp.int32, aux.shape, 1)
        is_beta = (lane >= AUX_BETA) & (lane < AUX_G)
        is_g = (lane >= AUX_G) & (lane < AUX_G + DN_HEADS)
        a_log = auxp_ref[0:1, :]
        dt_bias = auxp_ref[1:2, :]
        sp_in = aux + dt_bias
        softplus = jnp.maximum(sp_in, 0.0) + jnp.log(1.0 + jnp.exp(-jnp.abs(sp_in)))
        g = -jnp.exp(a_log) * softplus
        aux = jnp.where(is_beta, jax.nn.sigmoid(aux), jnp.where(is_g, g, aux))
        r = lax.broadcasted_iota(jnp.int32, (LANES, LANES), 0)
        cidx = lax.broadcasted_iota(jnp.int32, (LANES, LANES), 1)
        tri = ((cidx <= r) & (cidx // DN_CHUNK == r // DN_CHUNK)).astype(F32)
        g_cols = (cidx >= AUX_G) & (cidx < AUX_G + DN_HEADS)
        parts = []
        for t in range(tm // LANES):
            blk = aux[t * LANES:(t + 1) * LANES, :]
            cs = jnp.dot(tri, blk, precision=lax.Precision.HIGHEST, preferred_element_type=F32)
            parts.append(jnp.where(g_cols, cs, blk))
        aux = jnp.concatenate(parts, axis=0)
        auxc_ref[...] = aux
        auxr_ref[...] = aux.T

    main_ref[...] = jnp.dot(h_ref[...], wm_ref[...], preferred_element_type=F32).astype(main_ref.dtype)


def _in_proj(x2, ada3, gain, w_main, w_aux, auxp, seq):
    t, d = x2.shape
    n = w_main.shape[1]
    tm = min(512, seq)
    tn = 1024
    per_b = seq // tm
    return pl.pallas_call(
        _in_proj_kernel,
        grid=(t // tm, n // tn),
        in_specs=[pl.BlockSpec((tm, d), lambda i, j: (i, 0)),
                  pl.BlockSpec((1, 6, d), lambda i, j: (i // per_b, 0, 0)),
                  pl.BlockSpec((1, d), lambda i, j: (0, 0)),
                  pl.BlockSpec((d, tn), lambda i, j: (0, j)),
                  pl.BlockSpec((d, LANES), lambda i, j: (0, 0)),
                  pl.BlockSpec((8, LANES), lambda i, j: (0, 0))],
        out_specs=[pl.BlockSpec((tm, tn), lambda i, j: (i, j)),
                   pl.BlockSpec((tm, LANES), lambda i, j: (i, 0)),
                   pl.BlockSpec((LANES, tm), lambda i, j: (0, i))],
        out_shape=[jax.ShapeDtypeStruct((t, n), BF16),
                   jax.ShapeDtypeStruct((t, LANES), F32),
                   jax.ShapeDtypeStruct((LANES, t), F32)],
        scratch_shapes=[pltpu.VMEM((tm, d), BF16)],
        compiler_params=_params(1, 1),
        name="in_proj",
    )(x2, ada3, gain, w_main, w_aux, auxp)


def _unit_lower_inverses(ms):
    c = ms[0].shape[0]
    r = lax.broadcasted_iota(jnp.int32, (c, c), 0)
    cc = lax.broadcasted_iota(jnp.int32, (c, c), 1)
    eye = (r == cc).astype(F32)
    same = (r // 16) == (cc // 16)
    md = [jnp.where(same, m, 0.0) for m in ms]
    mo = [jnp.where(same, 0.0, m) for m in ms]
    p1 = [_bdot(a, a) for a in md]
    p2 = [_bdot(a, a) for a in p1]
    p3 = [_bdot(a, a) for a in p2]
    td = [eye - a for a in md]
    td = [t + _bdot(t, p) for t, p in zip(td, p1)]
    td = [t + _bdot(t, p) for t, p in zip(td, p2)]
    td = [t + _bdot(t, p) for t, p in zip(td, p3)]
    n1 = [_bdot(t, o) for t, o in zip(td, mo)]
    n2 = [_bdot(n, n) for n in n1]
    left = [(eye - a) + _bdot(eye - a, b) for a, b in zip(n1, n2)]
    return [_bdot(l, t) for l, t in zip(left, td)]


DELTA_TILE = 512
DELTA_SCAN_HEADS = 4


def _delta_prep_kernel(q_ref, k_ref, v_ref, qh_ref, kh_ref, vh_ref, auxc_ref, auxr_ref, wq_ref, wk_ref, wv_ref,
                       u_ref, w_ref, qe_ref, kd_ref, a_ref):
    head = pl.program_id(1)
    first_tile = pl.program_id(2) == 0
    ts = q_ref.shape[0]
    c = DN_CHUNK
    halo = qh_ref.shape[0]

    def conv_silu(x_ref, h_ref, w_ref):
        prev = jnp.where(first_tile, 0.0, h_ref[...].astype(F32))
        x = jnp.concatenate([prev, x_ref[...].astype(F32)], axis=0)
        w = w_ref[...]
        acc = x * w[DN_CONV - 1:DN_CONV, :]
        for s in range(1, DN_CONV):
            acc = acc + pltpu.roll(x, s, 0) * w[DN_CONV - 1 - s:DN_CONV - s, :]
        return _silu(acc[halo:, :])

    q = conv_silu(q_ref, qh_ref, wq_ref)
    k = conv_silu(k_ref, kh_ref, wk_ref)
    v = conv_silu(v_ref, vh_ref, wv_ref)
    q = q * lax.rsqrt(jnp.sum(q * q, axis=-1, keepdims=True) + L2_EPS) * (DN_DIM ** -0.5)
    k = k * lax.rsqrt(jnp.sum(k * k, axis=-1, keepdims=True) + L2_EPS)

    auxc = auxc_ref[...]
    lane = lax.broadcasted_iota(jnp.int32, auxc.shape, 1)
    beta_all = jnp.sum(jnp.where(lane == AUX_BETA + head, auxc, 0.0), axis=1, keepdims=True)
    gcol_all = jnp.sum(jnp.where(lane == AUX_G + head, auxc, 0.0), axis=1, keepdims=True)
    grows = auxr_ref[AUX_G:AUX_G + DN_HEADS, :]
    head_row = lax.broadcasted_iota(jnp.int32, grows.shape, 0)
    grow_all = jnp.sum(jnp.where(head_row == head, grows, 0.0), axis=0, keepdims=True)

    ri = lax.broadcasted_iota(jnp.int32, (c, c), 0)
    ci = lax.broadcasted_iota(jnp.int32, (c, c), 1)
    causal = ri >= ci
    strict = ri > ci
    chunks = [slice(j * c, (j + 1) * c) for j in range(ts // c)]
    decay = [jnp.exp(jnp.where(causal, gcol_all[rows] - grow_all[:, rows], -jnp.inf)) for rows in chunks]
    kb = [k[rows] * beta_all[rows] for rows in chunks]
    both = [_bdot_nt(jnp.concatenate([kb_j, q[rows]], axis=0), k[rows])
            for kb_j, rows in zip(kb, chunks)]
    tinv = _unit_lower_inverses([jnp.where(strict, b[:c] * d, 0.0) for b, d in zip(both, decay)])
    eg = [jnp.exp(gcol_all[rows]) for rows in chunks]
    sol = [_bdot(t, jnp.concatenate([v[rows] * beta_all[rows], kb_j * e], axis=1))
           for t, rows, kb_j, e in zip(tinv, chunks, kb, eg)]
    for j, rows in enumerate(chunks):
        gcol = gcol_all[rows]
        u_ref[rows, :] = sol[j][:, :DN_DIM].astype(u_ref.dtype)
        w_ref[rows, :] = sol[j][:, DN_DIM:].astype(w_ref.dtype)
        qe_ref[rows, :] = (q[rows] * eg[j]).astype(qe_ref.dtype)
        kd_ref[rows, :] = (k[rows] * jnp.exp(gcol[c - 1:c, :] - gcol)).astype(kd_ref.dtype)
        a_ref[rows, :] = jnp.where(causal, both[j][c:] * decay[j], 0.0).astype(a_ref.dtype)


def _delta_scan_kernel(u_ref, w_ref, qe_ref, kd_ref, a_ref, z_ref, auxc_ref, gain_ref, o_ref, st_ref):
    hb = pl.program_id(1)
    seq = u_ref.shape[0]
    n_heads = a_ref.shape[0]
    c = DN_CHUNK
    st_ref[...] = jnp.zeros_like(st_ref)
    gain = gain_ref[...]
    lane = lax.broadcasted_iota(jnp.int32, (1, LANES), 1)

    def scan_chunk(i, carry):
        start = pl.multiple_of(i * c, c)
        rows = pl.ds(start, c)
        last = auxc_ref[pl.ds(start + c - 1, 1), :]
        heads = range(n_heads)
        cols = [slice(hh * DN_DIM, (hh + 1) * DN_DIM) for hh in heads]
        state = [st_ref[hh] for hh in heads]
        sb = [s.astype(BF16) for s in state]
        ws = [jnp.dot(w_ref[rows, cl], s, preferred_element_type=F32) for cl, s in zip(cols, sb)]
        qs = [jnp.dot(qe_ref[rows, cl], s, preferred_element_type=F32) for cl, s in zip(cols, sb)]
        vb = [(u_ref[rows, cl].astype(F32) - x).astype(BF16) for cl, x in zip(cols, ws)]
        kv = [lax.dot_general(kd_ref[rows, cl], x, (((0,), (0,)), ((), ())), preferred_element_type=F32)
              for cl, x in zip(cols, vb)]
        av = [jnp.dot(a_ref[hh, rows, :], x, preferred_element_type=F32) for hh, x in zip(heads, vb)]
        for hh in heads:
            glast = jnp.sum(jnp.where(lane == AUX_G + hb * n_heads + hh, last, 0.0), axis=1, keepdims=True)
            st_ref[hh] = state[hh] * jnp.exp(glast) + kv[hh]
            z = z_ref[rows, cols[hh]].astype(F32)
            o_ref[rows, cols[hh]] = (_rms(qs[hh] + av[hh], gain) * _silu(z)).astype(o_ref.dtype)
        return carry

    lax.fori_loop(0, seq // c, scan_chunk, 0)


def _delta(main3, auxc3, auxr, conv_w_t, gain):
    bsz, seq, _ = main3.shape
    hb = DN_WIDTH // DN_DIM
    ts = min(DELTA_TILE, seq)
    halo = 16
    tiles = seq // ts

    def col(offset):
        return pl.BlockSpec((None, ts, DN_DIM), lambda b, h, i: (b, i, offset * hb + h))

    def halo_col(offset):
        return pl.BlockSpec((None, halo, DN_DIM),
                            lambda b, h, i: (b, jnp.maximum(i * (ts // halo) - 1, 0), offset * hb + h))

    def wcol(offset):
        return pl.BlockSpec((DN_CONV, DN_DIM), lambda b, h, i: (0, offset * hb + h))

    tok = pl.BlockSpec((None, ts, DN_DIM), lambda b, h, i: (b, i, h))
    tok_shape = jax.ShapeDtypeStruct((bsz, seq, DN_WIDTH), BF16)
    u, w, qe, kd, a = pl.pallas_call(
        _delta_prep_kernel,
        grid=(bsz, DN_HEADS, tiles),
        in_specs=[col(0), col(1), col(2), halo_col(0), halo_col(1), halo_col(2),
                  pl.BlockSpec((None, ts, LANES), lambda b, h, i: (b, i, 0)),
                  pl.BlockSpec((LANES, ts), lambda b, h, i: (0, b * tiles + i)),
                  wcol(0), wcol(1), wcol(2)],
        out_specs=[tok, tok, tok, tok,
                   pl.BlockSpec((None, None, ts, DN_CHUNK), lambda b, h, i: (b, h, i, 0))],
        out_shape=[tok_shape, tok_shape, tok_shape, tok_shape,
                   jax.ShapeDtypeStruct((bsz, DN_HEADS, seq, DN_CHUNK), BF16)],
        compiler_params=_params(3),
        name="delta_prep",
    )(main3, main3, main3, main3, main3, main3, auxc3, auxr, conv_w_t, conv_w_t, conv_w_t)

    hp = DELTA_SCAN_HEADS
    wide = pl.BlockSpec((None, seq, hp * DN_DIM), lambda b, g: (b, 0, g))
    return pl.pallas_call(
        _delta_scan_kernel,
        grid=(bsz, DN_HEADS // hp),
        in_specs=[wide, wide, wide, wide,
                  pl.BlockSpec((None, hp, seq, DN_CHUNK), lambda b, g: (b, g, 0, 0)),
                  pl.BlockSpec((None, seq, hp * DN_DIM), lambda b, g: (b, 0, 3 * (hb // hp) + g)),
                  pl.BlockSpec((None, seq, LANES), lambda b, g: (b, 0, 0)),
                  pl.BlockSpec((1, DN_DIM), lambda b, g: (0, 0))],
        out_specs=wide,
        out_shape=tok_shape,
        scratch_shapes=[pltpu.VMEM((hp, DN_DIM, DN_DIM), F32)],
        compiler_params=_params(2),
        name="delta_scan",
    )(u, w, qe, kd, a, main3, auxc3, gain)


def _mla_proj_kernel(cq_ref, ckv_ref, auxc_ref, pos_ref, qg_ref, kvg_ref, wq_ref, wkv_ref, rope_ref,
                     q_ref, k_ref, v_ref):
    scale = (MLA_NOPE + MLA_ROPE) ** -0.5
    ang = pos_ref[...].astype(F32) * rope_ref[0:1, :]
    cos_t = jnp.cos(ang) * rope_ref[1:2, :]
    sin_t = jnp.sin(ang) * rope_ref[2:3, :]
    lane = lax.broadcasted_iota(jnp.int32, ang.shape, 1)
    first = lane < MLA_ROPE // 2

    def rope(a):
        swapped = jnp.where(first, pltpu.roll(a, LANES - MLA_ROPE // 2, 1), pltpu.roll(a, MLA_ROPE // 2, 1))
        return a * cos_t + swapped * sin_t

    ql = _bdot(_rms(cq_ref[...].astype(F32), qg_ref[...]), wq_ref[...]) * scale
    kv = _bdot(_rms(ckv_ref[...].astype(F32), kvg_ref[...]), wkv_ref[...])
    kpe = rope(auxc_ref[...]).astype(k_ref.dtype)
    for h in range(MLA_HEADS):
        o = h * MLA_QK_PAD
        q_ref[:, o:o + MLA_NOPE] = ql[:, o:o + MLA_NOPE].astype(q_ref.dtype)
        q_ref[:, o + MLA_NOPE:o + MLA_QK_PAD] = rope(ql[:, o + MLA_NOPE:o + MLA_QK_PAD]).astype(q_ref.dtype)
        k_ref[:, o:o + MLA_NOPE] = kv[:, h * MLA_NOPE:(h + 1) * MLA_NOPE].astype(k_ref.dtype)
        k_ref[:, o + MLA_NOPE:o + MLA_QK_PAD] = kpe
    v_ref[...] = kv[:, MLA_HEADS * MLA_NOPE:].astype(v_ref.dtype)


def _mla_proj(main, auxc, pos, q_gain, kv_gain, wq, wkv, rope_tab, seq):
    t = main.shape[0]
    tm = min(512, seq)
    cq_blk = (4 * DN_WIDTH) // MLA_Q_RANK
    hq = MLA_HEADS * MLA_QK_PAD
    hv = MLA_HEADS * MLA_V
    return pl.pallas_call(
        _mla_proj_kernel,
        grid=(t // tm,),
        in_specs=[pl.BlockSpec((tm, MLA_Q_RANK), lambda i: (i, cq_blk)),
                  pl.BlockSpec((tm, MLA_KV_RANK), lambda i: (i, cq_blk + 1)),
                  pl.BlockSpec((tm, LANES), lambda i: (i, 0)),
                  pl.BlockSpec((tm, 1), lambda i: (i, 0)),
                  pl.BlockSpec((1, MLA_Q_RANK), lambda i: (0, 0)),
                  pl.BlockSpec((1, MLA_KV_RANK), lambda i: (0, 0)),
                  pl.BlockSpec((MLA_Q_RANK, hq), lambda i: (0, 0)),
                  pl.BlockSpec((MLA_KV_RANK, 2 * hv), lambda i: (0, 0)),
                  pl.BlockSpec((8, LANES), lambda i: (0, 0))],
        out_specs=[pl.BlockSpec((tm, hq), lambda i: (i, 0)),
                   pl.BlockSpec((tm, hq), lambda i: (i, 0)),
                   pl.BlockSpec((tm, hv), lambda i: (i, 0))],
        out_shape=[jax.ShapeDtypeStruct((t, hq), BF16),
                   jax.ShapeDtypeStruct((t, hq), BF16),
                   jax.ShapeDtypeStruct((t, hv), BF16)],
        compiler_params=_params(1),
        name="mla_proj",
    )(main, main, auxc, pos, q_gain, kv_gain, wq, wkv, rope_tab)


ATTN_PARTS = 2


def _attn_kernel(q_ref, k_ref, v_ref, o_ref, m_ref, l_ref, acc_ref):
    qi = pl.program_id(2)
    n_parts = m_ref.shape[0]
    tk = m_ref.shape[1]
    q = [q_ref[p * tk:(p + 1) * tk, :] for p in range(n_parts)]
    m_ref[...] = jnp.full_like(m_ref, -jnp.inf)
    l_ref[...] = jnp.zeros_like(l_ref)
    acc_ref[...] = jnp.zeros_like(acc_ref)
    r = lax.broadcasted_iota(jnp.int32, (tk, tk), 0)
    cc = lax.broadcasted_iota(jnp.int32, (tk, tk), 1)
    on_or_below = cc <= r

    def step(j, parts, masked_part):
        rows = pl.ds(pl.multiple_of(j * tk, tk), tk)
        kb = k_ref[rows, :]
        vb = v_ref[rows, :]
        s = [lax.dot_general(q[p], kb, (((1,), (1,)), ((), ())), preferred_element_type=F32) for p in parts]
        s = [jnp.where(on_or_below, x, -jnp.inf) if p == masked_part else x for p, x in zip(parts, s)]
        m_old = [m_ref[p] for p in parts]
        m_new = [jnp.maximum(mo, jnp.max(x, axis=-1, keepdims=True)) for mo, x in zip(m_old, s)]
        e = [jnp.exp(x - mn) for x, mn in zip(s, m_new)]
        pv = [jnp.dot(x.astype(BF16), vb, preferred_element_type=F32) for x in e]
        for i, p in enumerate(parts):
            alpha = jnp.exp(m_old[i] - m_new[i])
            l_ref[p] = alpha * l_ref[p] + jnp.sum(e[i], axis=-1, keepdims=True)
            acc_ref[p] = alpha * acc_ref[p] + pv[i]
            m_ref[p] = m_new[i]

    every = tuple(range(n_parts))

    def body(j, carry):
        step(j, every, None)
        return carry

    lax.fori_loop(0, qi * n_parts, body, 0)
    for d in range(n_parts):
        step(qi * n_parts + d, every[d:], d)
    for p in every:
        o_ref[p * tk:(p + 1) * tk, :] = (acc_ref[p] / l_ref[p]).astype(o_ref.dtype)


def _attention(q3, k3, v3):
    bsz, seq, _ = q3.shape
    tk = min(512, seq)
    n_parts = min(ATTN_PARTS, seq // tk)
    tq = tk * n_parts
    return pl.pallas_call(
        _attn_kernel,
        grid=(bsz, MLA_HEADS, seq // tq),
        in_specs=[pl.BlockSpec((None, tq, MLA_QK_PAD), lambda b, h, i: (b, i, h)),
                  pl.BlockSpec((None, seq, MLA_QK_PAD), lambda b, h, i: (b, 0, h)),
                  pl.BlockSpec((None, seq, MLA_V), lambda b, h, i: (b, 0, h))],
        out_specs=pl.BlockSpec((None, tq, MLA_V), lambda b, h, i: (b, i, h)),
        out_shape=jax.ShapeDtypeStruct((bsz, seq, MLA_HEADS * MLA_V), BF16),
        scratch_shapes=[pltpu.VMEM((n_parts, tk, 1), F32), pltpu.VMEM((n_parts, tk, 1), F32),
                        pltpu.VMEM((n_parts, tk, MLA_V), F32)],
        compiler_params=_params(3),
        name="attention",
    )(q3, k3, v3)


def _out_proj_kernel(x_ref, dn_ref, mla_ref, ada_ref, gain_ref, w_ref, wr_ref, x1_ref, h2_ref, lg_ref):
    ada = ada_ref[0]
    half = dn_ref.shape[1]
    mix = (jnp.dot(dn_ref[...], w_ref[:half, :], preferred_element_type=F32)
           + jnp.dot(mla_ref[...], w_ref[half:, :], preferred_element_type=F32))
    x1 = x_ref[...] + ada[2:3, :] * mix
    x1_ref[...] = x1
    h2 = _rms(x1, gain_ref[...]) * (1.0 + ada[4:5, :]) + ada[3:4, :]
    h2_ref[...] = _pack_bf16_pairs(h2)
    lg_ref[...] = jnp.dot(h2, wr_ref[...], precision=lax.Precision.HIGHEST, preferred_element_type=F32).T


def _out_proj(x2, dn, mla, ada3, gain, w_out, w_router, seq):
    t, d = x2.shape
    tm = min(256, seq)
    per_b = seq // tm
    half = dn.shape[1]
    return pl.pallas_call(
        _out_proj_kernel,
        grid=(t // tm,),
        in_specs=[pl.BlockSpec((tm, d), lambda i: (i, 0)),
                  pl.BlockSpec((tm, half), lambda i: (i, 0)),
                  pl.BlockSpec((tm, half), lambda i: (i, 0)),
                  pl.BlockSpec((1, 6, d), lambda i: (i // per_b, 0, 0)),
                  pl.BlockSpec((1, d), lambda i: (0, 0)),
                  pl.BlockSpec((2 * half, d), lambda i: (0, 0)),
                  pl.BlockSpec((d, LANES), lambda i: (0, 0))],
        out_specs=[pl.BlockSpec((tm, d), lambda i: (i, 0)),
                   pl.BlockSpec((tm, d // 2), lambda i: (i, 0)),
                   pl.BlockSpec((LANES, tm), lambda i: (0, i))],
        out_shape=[jax.ShapeDtypeStruct((t, d), F32),
                   jax.ShapeDtypeStruct((t, d // 2), jnp.uint32),
                   jax.ShapeDtypeStruct((LANES, t), F32)],
        compiler_params=_params(1),
        name="out_proj",
    )(x2, dn, mla, ada3, gain, w_out, w_router)


def _expert_kernel(be_ref, nv_ref, x_ref, wgu_ref, wd_ref, y_ref, gu_s, d_s):
    i = pl.program_id(0)
    changed = (i == 0) | (be_ref[i] != be_ref[jnp.maximum(i - 1, 0)])

    @pl.when(changed)
    def _():
        gu_s[...] = wgu_ref[...].astype(BF16)
        d_s[...] = wd_ref[...].astype(BF16)

    x = _unpack_bf16_pairs(x_ref[...]).astype(BF16)
    gu = jnp.dot(x, gu_s[...], preferred_element_type=F32)
    ff = gu.shape[1] // 2
    act = _silu(gu[:, :ff]) * gu[:, ff:]
    y = jnp.dot(act.astype(BF16), d_s[...], preferred_element_type=F32)
    row = lax.broadcasted_iota(jnp.int32, y.shape, 0)
    y_ref[...] = _pack_bf16_pairs(jnp.where(row < nv_ref[i], y, 0.0))


def _experts(block_e, n_valid, x_sorted, w_gu, w_d):
    n_rows, dp = x_sorted.shape
    ff2 = w_gu.shape[2]
    n_blocks = n_rows // EXPERT_BLOCK
    grid_spec = pltpu.PrefetchScalarGridSpec(
        num_scalar_prefetch=2,
        grid=(n_blocks,),
        in_specs=[pl.BlockSpec((EXPERT_BLOCK, dp), lambda i, be, nv: (i, 0)),
                  pl.BlockSpec((None, 2 * dp, ff2), lambda i, be, nv: (be[i], 0, 0)),
                  pl.BlockSpec((None, ff2 // 2, 2 * dp), lambda i, be, nv: (be[i], 0, 0))],
        out_specs=pl.BlockSpec((EXPERT_BLOCK, dp), lambda i, be, nv: (i, 0)),
        scratch_shapes=[pltpu.VMEM((2 * dp, ff2), BF16), pltpu.VMEM((ff2 // 2, 2 * dp), BF16)],
    )
    return pl.pallas_call(
        _expert_kernel,
        grid_spec=grid_spec,
        out_shape=jax.ShapeDtypeStruct((n_rows, dp), jnp.uint32),
        compiler_params=_params(0, 1),
        name="experts",
    )(block_e, n_valid, x_sorted, w_gu, w_d)


def _final_kernel(x1_ref, h2_ref, y_ref, wc_ref, ada_ref, wgu_ref, wd_ref, gain_ref, o_ref):
    ada = ada_ref[0]
    h2 = _unpack_bf16_pairs(h2_ref[...]).astype(BF16)
    gu = jnp.dot(h2, wgu_ref[...], preferred_element_type=F32)
    ff = gu.shape[1] // 2
    act = _silu(gu[:, :ff]) * gu[:, ff:]
    ffn = jnp.dot(act.astype(BF16), wd_ref[...], preferred_element_type=F32)
    wc = wc_ref[...]
    for kk in range(TOP_K):
        ffn = ffn + wc[:, kk:kk + 1] * _unpack_bf16_pairs(y_ref[kk])
    x2 = x1_ref[...] + ada[5:6, :] * ffn
    o_ref[...] = _rms(x2, gain_ref[...])


def _final(x1, h2, y_tok, wc, ada3, w_gu, w_d, gain, seq):
    t, d = x1.shape
    tm = min(256, seq)
    per_b = seq // tm
    ff2 = w_gu.shape[1]
    return pl.pallas_call(
        _final_kernel,
        grid=(t // tm,),
        in_specs=[pl.BlockSpec((tm, d), lambda i: (i, 0)),
                  pl.BlockSpec((tm, d // 2), lambda i: (i, 0)),
                  pl.BlockSpec((TOP_K, tm, d // 2), lambda i: (0, i, 0)),
                  pl.BlockSpec((tm, LANES), lambda i: (i, 0)),
                  pl.BlockSpec((1, 6, d), lambda i: (i // per_b, 0, 0)),
                  pl.BlockSpec((d, ff2), lambda i: (0, 0)),
                  pl.BlockSpec((ff2 // 2, d), lambda i: (0, 0)),
                  pl.BlockSpec((1, d), lambda i: (0, 0))],
        out_specs=pl.BlockSpec((tm, d), lambda i: (i, 0)),
        out_shape=jax.ShapeDtypeStruct((t, d), F32),
        compiler_params=_params(1),
        name="final",
    )(x1, h2, y_tok, wc, ada3, w_gu, w_d, gain)


ROUTE_TILE = 512
MAX_BLOCK_LANES = 512


def _first_argmax(x, idx_f, n):
    m = jnp.max(x, axis=0, keepdims=True)
    first = jnp.min(jnp.where(x == m, idx_f, float(n)), axis=0, keepdims=True)
    return m, first


def _route_kernel(lg_ref, bias_ref, ek_ref, rk_ref, wk_ref, wc_ref, ps_ref, be_ref, carry_ref, upper_ref):
    i = pl.program_id(0)
    tm = lg_ref.shape[1]
    per_group = N_EXPERTS // N_GROUPS

    @pl.when(i == 0)
    def _():
        carry_ref[...] = jnp.zeros_like(carry_ref)
        r = lax.broadcasted_iota(jnp.int32, (tm, tm), 0)
        cc = lax.broadcasted_iota(jnp.int32, (tm, tm), 1)
        upper_ref[...] = (r < cc).astype(BF16)

    scores = jax.nn.sigmoid(lg_ref[0:N_EXPERTS, :])
    biased = scores + bias_ref[:, 0:1]
    sub8 = lax.broadcasted_iota(jnp.int32, (per_group, tm), 0).astype(F32)
    group_rows = []
    for g in range(N_GROUPS):
        xg = biased[g * per_group:(g + 1) * per_group, :]
        m1, i1 = _first_argmax(xg, sub8, per_group)
        m2 = jnp.max(jnp.where(sub8 == i1, -jnp.inf, xg), axis=0, keepdims=True)
        group_rows.append(m1 + m2)
    cur = jnp.concatenate(group_rows, axis=0)
    gself = jnp.zeros(cur.shape, F32)
    for _ in range(TOPK_GROUPS):
        _, gi = _first_argmax(cur, sub8, N_GROUPS)
        hit = sub8 == gi
        gself = jnp.where(hit, 1.0, gself)
        cur = jnp.where(hit, -jnp.inf, cur)
    masked = jnp.concatenate(
        [jnp.where(gself[g:g + 1, :] > 0.5, biased[g * per_group:(g + 1) * per_group, :], -jnp.inf)
         for g in range(N_GROUPS)], axis=0)
    sub64 = lax.broadcasted_iota(jnp.int32, (N_EXPERTS, tm), 0).astype(F32)
    e_rows, s_rows = [], []
    sel = jnp.zeros(masked.shape, F32)
    for _ in range(TOP_K):
        _, ei = _first_argmax(masked, sub64, N_EXPERTS)
        hit = sub64 == ei
        e_rows.append(ei)
        s_rows.append(jnp.sum(jnp.where(hit, scores, 0.0), axis=0, keepdims=True))
        sel = jnp.where(hit, 1.0, sel)
        masked = jnp.where(hit, -jnp.inf, masked)
    total = s_rows[0]
    for s in s_rows[1:]:
        total = total + s
    w_rows = [s / total * ROUTED_SCALE for s in s_rows]
    before = jnp.dot(sel.astype(BF16), upper_ref[...], preferred_element_type=F32) + carry_ref[:, 0:1]
    r_rows = [jnp.sum(jnp.where(sub64 == ei, before, 0.0), axis=0, keepdims=True) for ei in e_rows]
    zrow = jnp.zeros((8 - TOP_K, tm), F32)
    ek_ref[...] = jnp.concatenate(e_rows + [zrow], axis=0).astype(jnp.int32)
    rk_ref[...] = jnp.concatenate(r_rows + [zrow], axis=0).astype(jnp.int32)
    wk = jnp.concatenate(w_rows + [zrow], axis=0)
    wk_ref[...] = wk
    wc_ref[...] = jnp.concatenate([wk, jnp.zeros((LANES - 8, tm), F32)], axis=0).T
    carry_ref[...] = carry_ref[...] + jnp.sum(sel, axis=1, keepdims=True)

    @pl.when(i == pl.num_programs(0) - 1)
    def _():
        counts = carry_ref[...]
        padded = jnp.floor((counts + (EXPERT_BLOCK - 1.0)) * (1.0 / EXPERT_BLOCK)) * EXPERT_BLOCK
        r = lax.broadcasted_iota(jnp.int32, (N_EXPERTS, N_EXPERTS), 0)
        cc = lax.broadcasted_iota(jnp.int32, (N_EXPERTS, N_EXPERTS), 1)
        pad_end = jnp.dot((cc <= r).astype(F32), padded, precision=lax.Precision.HIGHEST,
                          preferred_element_type=F32)
        pad_start = pad_end - padded
        ps_ref[...] = pad_start
        blk0 = lax.broadcasted_iota(jnp.int32, (N_EXPERTS, MAX_BLOCK_LANES), 1).astype(F32) * EXPERT_BLOCK
        e_of_blk = jnp.minimum(jnp.sum((pad_end[:, 0:1] <= blk0).astype(F32), axis=0, keepdims=True),
                               N_EXPERTS - 1.0)
        sub = lax.broadcasted_iota(jnp.int32, (N_EXPERTS, MAX_BLOCK_LANES), 0).astype(F32)
        seg_end = jnp.sum(jnp.where(sub == e_of_blk, (pad_start + counts)[:, 0:1], 0.0), axis=0, keepdims=True)
        n_valid = jnp.clip(seg_end - blk0[0:1, :], 0.0, float(EXPERT_BLOCK))
        be_ref[...] = jnp.concatenate([e_of_blk, n_valid, jnp.zeros((6, MAX_BLOCK_LANES), F32)],
                                      axis=0).astype(jnp.int32)


def _route_pallas(logits_t, bias_col):
    t = logits_t.shape[1]
    tm = min(ROUTE_TILE, t)
    row8 = pl.BlockSpec((8, tm), lambda i: (0, i))
    return pl.pallas_call(
        _route_kernel,
        grid=(t // tm,),
        in_specs=[pl.BlockSpec((LANES, tm), lambda i: (0, i)),
                  pl.BlockSpec((N_EXPERTS, 1), lambda i: (0, 0))],
        out_specs=[row8, row8, row8,
                   pl.BlockSpec((tm, LANES), lambda i: (i, 0)),
                   pl.BlockSpec((N_EXPERTS, LANES), lambda i: (0, 0)),
                   pl.BlockSpec((8, MAX_BLOCK_LANES), lambda i: (0, 0))],
        out_shape=[jax.ShapeDtypeStruct((8, t), jnp.int32),
                   jax.ShapeDtypeStruct((8, t), jnp.int32),
                   jax.ShapeDtypeStruct((8, t), F32),
                   jax.ShapeDtypeStruct((t, LANES), F32),
                   jax.ShapeDtypeStruct((N_EXPERTS, LANES), F32),
                   jax.ShapeDtypeStruct((8, MAX_BLOCK_LANES), jnp.int32)],
        scratch_shapes=[pltpu.VMEM((N_EXPERTS, LANES), F32), pltpu.VMEM((tm, tm), BF16)],
        compiler_params=_params(0, 1),
        name="route",
    )(logits_t, bias_col)


def _slot_kernel(ek_ref, rk_ref, ps_ref, slot_ref):
    tm = ek_ref.shape[1]
    sub64 = lax.broadcasted_iota(jnp.int32, (N_EXPERTS, tm), 0)
    start = ps_ref[:, 0:1]
    rows = []
    for kk in range(TOP_K):
        seg = jnp.sum(jnp.where(sub64 == ek_ref[kk:kk + 1, :], start, 0.0), axis=0, keepdims=True)
        rows.append(seg.astype(jnp.int32) + rk_ref[kk:kk + 1, :])
    slot_ref[...] = jnp.concatenate(rows + [jnp.zeros((8 - TOP_K, tm), jnp.int32)], axis=0)


def _slots(ek, rk, ps):
    t = ek.shape[1]
    tm = min(ROUTE_TILE, t)
    row8 = pl.BlockSpec((8, tm), lambda i: (0, i))
    return pl.pallas_call(
        _slot_kernel,
        grid=(t // tm,),
        in_specs=[row8, row8, pl.BlockSpec((N_EXPERTS, LANES), lambda i: (0, 0))],
        out_specs=row8,
        out_shape=jax.ShapeDtypeStruct((8, t), jnp.int32),
        compiler_params=_params(1),
        name="slots",
    )(ek, rk, ps)


SC_ROWS = 32


def _sc_mesh():
    info = plsc.get_sparse_core_info()
    mesh = plsc.VectorSubcoreMesh(core_axis_name="c", subcore_axis_name="s")
    return mesh, info.num_cores, info.num_cores * info.num_subcores


def _sc_dispatch(x, slot_flat, n_rows):
    t, dp = x.shape
    n_slots = slot_flat.shape[0] // t
    mesh, n_cores, n_workers = _sc_mesh()
    per_w = t // n_workers
    n_chunks = per_w // SC_ROWS
    assert per_w * n_workers == t and n_chunks * SC_ROWS == per_w and n_chunks % 2 == 0

    @functools.partial(
        pl.kernel, out_type=jax.ShapeDtypeStruct((n_rows, dp), x.dtype), mesh=mesh,
        scratch_types=[pltpu.VMEM((2, n_slots, SC_ROWS), jnp.int32), pltpu.VMEM((2, SC_ROWS, dp), x.dtype),
                       pltpu.SemaphoreType.DMA((2,)), pltpu.SemaphoreType.DMA((2,))])
    def dispatch(x_hbm, i_hbm, o_hbm, idx_v, rows_v, read_sem, scatter_sem):
        base = (lax.axis_index("s") * n_cores + lax.axis_index("c")) * per_w

        def read(j, b):
            return pltpu.make_async_copy(x_hbm.at[pl.ds(pl.multiple_of(base + j * SC_ROWS, 8), SC_ROWS)],
                                         rows_v.at[b], read_sem.at[b])

        def scatter(k, b):
            return pltpu.make_async_copy(rows_v.at[b], o_hbm.at[idx_v.at[b, k]], scatter_sem.at[b])

        read(0, 0).start()

        @pl.loop(0, n_chunks, step=2)
        def _(j):
            for b in range(2):
                jj = j + b

                @pl.when(jj >= 1)
                def _():
                    for k in range(n_slots):
                        scatter(k, 1 - b).wait()

                @pl.when(jj + 1 < n_chunks)
                def _():
                    read(jj + 1, 1 - b).start()

                for k in range(n_slots):
                    pltpu.sync_copy(i_hbm.at[pl.ds(pl.multiple_of(k * t + base + jj * SC_ROWS, 8), SC_ROWS)],
                                    idx_v.at[b, k])
                read(jj, b).wait()
                for k in range(n_slots):
                    scatter(k, b).start()

        for k in range(n_slots):
            scatter(k, (n_chunks - 1) % 2).wait()

    return dispatch(x, slot_flat)


def _sc_combine(y, slot_flat):
    n = slot_flat.shape[0]
    dp = y.shape[1]
    mesh, n_cores, n_workers = _sc_mesh()
    per_w = n // n_workers
    n_chunks = per_w // SC_ROWS
    assert per_w * n_workers == n and n_chunks * SC_ROWS == per_w and n_chunks % 2 == 0

    @functools.partial(
        pl.kernel, out_type=jax.ShapeDtypeStruct((n, dp), y.dtype), mesh=mesh,
        scratch_types=[pltpu.VMEM((per_w,), jnp.int32), pltpu.VMEM((2, SC_ROWS, dp), y.dtype),
                       pltpu.SemaphoreType.DMA((2,)), pltpu.SemaphoreType.DMA((2,))])
    def combine(y_hbm, i_hbm, o_hbm, idx_v, rows_v, gather_sem, write_sem):
        base = (lax.axis_index("s") * n_cores + lax.axis_index("c")) * per_w
        pltpu.sync_copy(i_hbm.at[pl.ds(pl.multiple_of(base, 8), per_w)], idx_v)

        def gather(j, b):
            return pltpu.make_async_copy(y_hbm.at[idx_v.at[pl.ds(pl.multiple_of(j * SC_ROWS, 8), SC_ROWS)]],
                                         rows_v.at[b], gather_sem.at[b])

        def write(j, b):
            return pltpu.make_async_copy(rows_v.at[b],
                                         o_hbm.at[pl.ds(pl.multiple_of(base + j * SC_ROWS, 8), SC_ROWS)],
                                         write_sem.at[b])

        gather(0, 0).start()

        @pl.loop(0, n_chunks, step=2)
        def _(j):
            for b in range(2):
                jj = j + b

                @pl.when(jj >= 1)
                def _():
                    write(jj - 1, 1 - b).wait()

                @pl.when(jj + 1 < n_chunks)
                def _():
                    gather(jj + 1, 1 - b).start()

                gather(jj, b).wait()
                write(jj, b).start()

        write(n_chunks - 1, (n_chunks - 1) % 2).wait()

    return combine(y, slot_flat)


def _layer(x, cond_ada, positions, w_in, dn_conv_w, dn_a_log, dn_dt_bias, dn_norm_gain, mla_q_norm_gain,
           w_q_up, mla_kv_norm_gain, w_kv_up, w_out, norm1_gain, norm2_gain, w_router, router_bias,
           w_exp_gate_up, w_exp_down, w_sh_gate_up, w_sh_down, out_gain):
    bsz, seq, d = x.shape
    t = bsz * seq
    x2 = x.reshape(t, d)
    ada3 = cond_ada.reshape(bsz, 6, d)

    s_z = 4 * DN_WIDTH
    s_a = s_z + 2 * DN_HEADS
    s_kpe = s_a + MLA_Q_RANK + MLA_KV_RANK
    w_main = jnp.concatenate([w_in[:, :s_z], w_in[:, s_a:s_kpe]], axis=1).astype(BF16)
    w_aux = jnp.concatenate([w_in[:, s_kpe:], w_in[:, s_z:s_a],
                             jnp.zeros((d, LANES - MLA_ROPE - 2 * DN_HEADS), F32)], axis=1).astype(BF16)
    auxp = jnp.zeros((8, LANES), F32)
    auxp = auxp.at[0, AUX_G:AUX_G + DN_HEADS].set(dn_a_log).at[1, AUX_G:AUX_G + DN_HEADS].set(dn_dt_bias)
    main, auxc, auxr = _in_proj(x2, ada3, norm1_gain.reshape(1, d), w_main, w_aux, auxp, seq)

    dn = _delta(main.reshape(bsz, seq, MAIN_WIDTH), auxc.reshape(bsz, seq, LANES), auxr,
                dn_conv_w.T, dn_norm_gain.reshape(1, DN_DIM))

    qk = MLA_NOPE + MLA_ROPE
    wq3 = w_q_up.reshape(MLA_Q_RANK, MLA_HEADS, qk)
    wq = jnp.concatenate([wq3, jnp.zeros((MLA_Q_RANK, MLA_HEADS, MLA_QK_PAD - qk), F32)], axis=2)
    wq = wq.reshape(MLA_Q_RANK, MLA_HEADS * MLA_QK_PAD).astype(BF16)
    wkv3 = w_kv_up.reshape(MLA_KV_RANK, MLA_HEADS, MLA_NOPE + MLA_V)
    wkv = jnp.concatenate([wkv3[:, :, :MLA_NOPE].reshape(MLA_KV_RANK, -1),
                           wkv3[:, :, MLA_NOPE:].reshape(MLA_KV_RANK, -1)], axis=1).astype(BF16)
    half = MLA_ROPE // 2
    inv_freq = ROPE_THETA ** (-jnp.arange(half, dtype=F32) / half)
    zeros = jnp.zeros((LANES - MLA_ROPE,), F32)
    rope_tab = jnp.zeros((8, LANES), F32)
    rope_tab = rope_tab.at[0].set(jnp.concatenate([inv_freq, inv_freq, zeros]))
    rope_tab = rope_tab.at[1].set(jnp.concatenate([jnp.ones((MLA_ROPE,), F32), zeros]))
    rope_tab = rope_tab.at[2].set(jnp.concatenate([-jnp.ones((half,), F32), jnp.ones((half,), F32), zeros]))
    q, k, v = _mla_proj(main, auxc, positions.reshape(t, 1), mla_q_norm_gain.reshape(1, -1),
                        mla_kv_norm_gain.reshape(1, -1), wq, wkv, rope_tab, seq)
    mla = _attention(q.reshape(bsz, seq, -1), k.reshape(bsz, seq, -1), v.reshape(bsz, seq, -1))

    w_r = jnp.concatenate([w_router, jnp.zeros((d, LANES - N_EXPERTS), F32)], axis=1)
    x1, h2, logits_t = _out_proj(x2, dn.reshape(t, DN_WIDTH), mla.reshape(t, -1), ada3,
                                 norm2_gain.reshape(1, d), w_out.astype(BF16), w_r, seq)

    n_rows = -(-(t * TOP_K + N_EXPERTS * (EXPERT_BLOCK - 1)) // EXPERT_BLOCK) * EXPERT_BLOCK
    n_blocks = n_rows // EXPERT_BLOCK
    assert n_blocks <= MAX_BLOCK_LANES
    ek, rk, _, wc, seg_start, blocks = _route_pallas(logits_t, router_bias.reshape(N_EXPERTS, 1))
    slot_flat = _slots(ek, rk, seg_start)[:TOP_K].reshape(-1)
    x_sorted = _sc_dispatch(h2, slot_flat, n_rows)
    y_sorted = _experts(blocks[0, :n_blocks], blocks[1, :n_blocks], x_sorted, w_exp_gate_up, w_exp_down)
    y_tok = _sc_combine(y_sorted, slot_flat).reshape(TOP_K, t, d // 2)

    out = _final(x1, h2, y_tok, wc, ada3, w_sh_gate_up.astype(BF16), w_sh_down.astype(BF16),
                 out_gain.reshape(1, d), seq)
    return out.reshape(bsz, seq, d)


def kernel(x, c, positions, w_ada, b_ada, norm1_gain, w_in, dn_conv_w, dn_a_log, dn_dt_bias, dn_norm_gain,
           mla_q_norm_gain, w_q_up, mla_kv_norm_gain, w_kv_up, w_out, norm2_gain, w_router, router_bias,
           w_exp_gate_up, w_exp_down, w_sh_gate_up, w_sh_down, final_norm_gain):
    depth = w_ada.shape[0]
    assert depth == 1, "the final RMSNorm is fused into the single layer's last kernel"
    ada = _ada(c, w_ada[0], b_ada[0])
    return _layer(x, ada, positions, w_in[0], dn_conv_w[0], dn_a_log[0], dn_dt_bias[0], dn_norm_gain[0],
                  mla_q_norm_gain[0], w_q_up[0], mla_kv_norm_gain[0], w_kv_up[0], w_out[0], norm1_gain[0],
                  norm2_gain[0], w_router[0], router_bias[0], w_exp_gate_up[0], w_exp_down[0],
                  w_sh_gate_up[0], w_sh_down[0], final_norm_gain)
```

```python
import functools

import jax
import jax.numpy as jnp
from jax import lax
from jax.experimental import pallas as pl
from jax.experimental.pallas import tpu as pltpu
from jax.experimental.pallas import tpu_sc as plsc

F32 = jnp.float32
BF16 = jnp.bfloat16

DN_HEADS = 8
DN_DIM = 128
DN_WIDTH = DN_HEADS * DN_DIM
DN_CONV = 4
DN_CHUNK = 64
MLA_HEADS = 8
MLA_Q_RANK = 512
MLA_KV_RANK = 512
MLA_NOPE = 128
MLA_ROPE = 64
MLA_V = 128
MLA_QK_PAD = 256
ROPE_THETA = 10000.0
N_EXPERTS = 64
TOP_K = 6
N_GROUPS = 8
TOPK_GROUPS = 4
EXPERT_FF = 512
ROUTED_SCALE = 2.5
EXPERT_BLOCK = 256
RMS_EPS = 1e-6
L2_EPS = 1e-6

LANES = 128
MXU_WIDTH = 256
MAIN_WIDTH = 4 * DN_WIDTH + MLA_Q_RANK + MLA_KV_RANK
AUX_BETA = MLA_ROPE
AUX_G = MLA_ROPE + DN_HEADS
VMEM_LIMIT = 56 * 1024 * 1024


def _params(n_parallel, n_arbitrary=0):
    sem = ("parallel",) * n_parallel + ("arbitrary",) * n_arbitrary
    return pltpu.CompilerParams(dimension_semantics=sem, vmem_limit_bytes=VMEM_LIMIT)


def _silu(x):
    return x * jax.nn.sigmoid(x)


def _bdot(a, b):
    return jnp.dot(a.astype(BF16), b.astype(BF16), preferred_element_type=F32)


def _bdot_nt(a, b):
    return lax.dot_general(a.astype(BF16), b.astype(BF16), (((1,), (1,)), ((), ())),
                           preferred_element_type=F32)


def _bdot_tn(a, b):
    return lax.dot_general(a.astype(BF16), b.astype(BF16), (((0,), (0,)), ((), ())),
                           preferred_element_type=F32)


def _rms(x, gain):
    return x * lax.rsqrt(jnp.mean(x * x, axis=-1, keepdims=True) + RMS_EPS) * gain


HIGH_HALF = 0xFFFF0000


def _pack_bf16_pairs(x):
    n = x.shape[1] // 2
    bits = lax.bitcast_convert_type(x.astype(BF16).astype(F32), jnp.uint32)
    return (bits[:, :n] >> 16) | (bits[:, n:] & jnp.uint32(HIGH_HALF))


def _unpack_bf16_pairs(p):
    lo = lax.bitcast_convert_type(p << 16, F32)
    hi = lax.bitcast_convert_type(p & jnp.uint32(HIGH_HALF), F32)
    return jnp.concatenate([lo, hi], axis=1)


def _ada_kernel(c_ref, w_ref, b_ref, o_ref):
    o_ref[...] = _bdot(_silu(c_ref[...]), w_ref[...]) + b_ref[...]


def _ada(c, w_ada, b_ada):
    bsz, d = c.shape
    n = w_ada.shape[1]
    tn = 1024
    return pl.pallas_call(
        _ada_kernel,
        grid=(n // tn,),
        in_specs=[pl.BlockSpec((bsz, d), lambda j: (0, 0)),
                  pl.BlockSpec((d, tn), lambda j: (0, j)),
                  pl.BlockSpec((1, tn), lambda j: (0, j))],
        out_specs=pl.BlockSpec((bsz, tn), lambda j: (0, j)),
        out_shape=jax.ShapeDtypeStruct((bsz, n), F32),
        compiler_params=_params(1),
        name="ada",
    )(c, w_ada, b_ada.reshape(1, n))


def _in_proj_kernel(x_ref, ada_ref, gain_ref, wm_ref, wa_ref, auxp_ref,
                    main_ref, auxc_ref, auxr_ref, h_ref):
    j = pl.program_id(1)

    @pl.when(j == 0)
    def _():
        ada = ada_ref[0]
        h = _rms(x_ref[...], gain_ref[...]) * (1.0 + ada[1:2, :]) + ada[0:1, :]
        hb = h.astype(BF16)
        h_ref[...] = hb
        aux = jnp.dot(hb, wa_ref[...], preferred_element_type=F32)
        tm = aux.shape[0]
        lane = lax.broadcasted_iota(jnp.int32, aux.shape, 1)
        is_beta = (lane >= AUX_BETA) & (lane < AUX_G)
        is_g = (lane >= AUX_G) & (lane < AUX_G + DN_HEADS)
        a_log = auxp_ref[0:1, :]
        dt_bias = auxp_ref[1:2, :]
        sp_in = aux + dt_bias
        softplus = jnp.maximum(sp_in, 0.0) + jnp.log(1.0 + jnp.exp(-jnp.abs(sp_in)))
        g = -jnp.exp(a_log) * softplus
        aux = jnp.where(is_beta, jax.nn.sigmoid(aux), jnp.where(is_g, g, aux))
        r = lax.broadcasted_iota(jnp.int32, (LANES, LANES), 0)
        cidx = lax.broadcasted_iota(jnp.int32, (LANES, LANES), 1)
        tri = ((cidx <= r) & (cidx // DN_CHUNK == r // DN_CHUNK)).astype(F32)
        g_cols = (cidx >= AUX_G) & (cidx < AUX_G + DN_HEADS)
        parts = []
        for t in range(tm // LANES):
            blk = aux[t * LANES:(t + 1) * LANES, :]
            cs = jnp.dot(tri, blk, precision=lax.Precision.HIGHEST, preferred_element_type=F32)
            parts.append(jnp.where(g_cols, cs, blk))
        aux = jnp.concatenate(parts, axis=0)
        auxc_ref[...] = aux
        auxr_ref[...] = aux.T

    main_ref[...] = jnp.dot(h_ref[...], wm_ref[...], preferred_element_type=F32).astype(main_ref.dtype)


def _in_proj(x2, ada3, gain, w_main, w_aux, auxp, seq):
    t, d = x2.shape
    n = w_main.shape[1]
    tm = min(512, seq)
    tn = 1024
    per_b = seq // tm
    return pl.pallas_call(
        _in_proj_kernel,
        grid=(t // tm, n // tn),
        in_specs=[pl.BlockSpec((tm, d), lambda i, j: (i, 0)),
                  pl.BlockSpec((1, 6, d), lambda i, j: (i // per_b, 0, 0)),
                  pl.BlockSpec((1, d), lambda i, j: (0, 0)),
                  pl.BlockSpec((d, tn), lambda i, j: (0, j)),
                  pl.BlockSpec((d, LANES), lambda i, j: (0, 0)),
                  pl.BlockSpec((8, LANES), lambda i, j: (0, 0))],
        out_specs=[pl.BlockSpec((tm, tn), lambda i, j: (i, j)),
                   pl.BlockSpec((tm, LANES), lambda i, j: (i, 0)),
                   pl.BlockSpec((LANES, tm), lambda i, j: (0, i))],
        out_shape=[jax.ShapeDtypeStruct((t, n), BF16),
                   jax.ShapeDtypeStruct((t, LANES), F32),
                   jax.ShapeDtypeStruct((LANES, t), F32)],
        scratch_shapes=[pltpu.VMEM((tm, d), BF16)],
        compiler_params=_params(1, 1),
        name="in_proj",
    )(x2, ada3, gain, w_main, w_aux, auxp)


def _unit_lower_inverses(ms):
    c = ms[0].shape[0]
    r = lax.broadcasted_iota(jnp.int32, (c, c), 0)
    cc = lax.broadcasted_iota(jnp.int32, (c, c), 1)
    eye = (r == cc).astype(F32)
    same = (r // 16) == (cc // 16)
    md = [jnp.where(same, m, 0.0) for m in ms]
    mo = [jnp.where(same, 0.0, m) for m in ms]
    p1 = [_bdot(a, a) for a in md]
    p2 = [_bdot(a, a) for a in p1]
    p3 = [_bdot(a, a) for a in p2]
    td = [eye - a for a in md]
    td = [t + _bdot(t, p) for t, p in zip(td, p1)]
    td = [t + _bdot(t, p) for t, p in zip(td, p2)]
    td = [t + _bdot(t, p) for t, p in zip(td, p3)]
    n1 = [_bdot(t, o) for t, o in zip(td, mo)]
    n2 = [_bdot(n, n) for n in n1]
    left = [(eye - a) + _bdot(eye - a, b) for a, b in zip(n1, n2)]
    return [_bdot(l, t) for l, t in zip(left, td)]


DELTA_TILE = 512
DELTA_SCAN_HEADS = 4


def _delta_prep_kernel(q_ref, k_ref, v_ref, qh_ref, kh_ref, vh_ref, auxc_ref, auxr_ref, wq_ref, wk_ref, wv_ref,
                       u_ref, w_ref, qe_ref, kd_ref, a_ref):
    head = pl.program_id(1)
    first_tile = pl.program_id(2) == 0
    ts = q_ref.shape[0]
    c = DN_CHUNK
    halo = qh_ref.shape[0]

    def conv_silu(x_ref, h_ref, w_ref):
        prev = jnp.where(first_tile, 0.0, h_ref[...].astype(F32))
        x = jnp.concatenate([prev, x_ref[...].astype(F32)], axis=0)
        w = w_ref[...]
        acc = x * w[DN_CONV - 1:DN_CONV, :]
        for s in range(1, DN_CONV):
            acc = acc + pltpu.roll(x, s, 0) * w[DN_CONV - 1 - s:DN_CONV - s, :]
        return _silu(acc[halo:, :])

    q = conv_silu(q_ref, qh_ref, wq_ref)
    k = conv_silu(k_ref, kh_ref, wk_ref)
    v = conv_silu(v_ref, vh_ref, wv_ref)
    q = q * lax.rsqrt(jnp.sum(q * q, axis=-1, keepdims=True) + L2_EPS) * (DN_DIM ** -0.5)
    k = k * lax.rsqrt(jnp.sum(k * k, axis=-1, keepdims=True) + L2_EPS)

    auxc = auxc_ref[...]
    lane = lax.broadcasted_iota(jnp.int32, auxc.shape, 1)
    beta_all = jnp.sum(jnp.where(lane == AUX_BETA + head, auxc, 0.0), axis=1, keepdims=True)
    gcol_all = jnp.sum(jnp.where(lane == AUX_G + head, auxc, 0.0), axis=1, keepdims=True)
    grows = auxr_ref[AUX_G:AUX_G + DN_HEADS, :]
    head_row = lax.broadcasted_iota(jnp.int32, grows.shape, 0)
    grow_all = jnp.sum(jnp.where(head_row == head, grows, 0.0), axis=0, keepdims=True)

    ri = lax.broadcasted_iota(jnp.int32, (c, c), 0)
    ci = lax.broadcasted_iota(jnp.int32, (c, c), 1)
    causal = ri >= ci
    strict = ri > ci
    chunks = [slice(j * c, (j + 1) * c) for j in range(ts // c)]
    decay = [jnp.exp(jnp.where(causal, gcol_all[rows] - grow_all[:, rows], -jnp.inf)) for rows in chunks]
    kb = [k[rows] * beta_all[rows] for rows in chunks]
    both = [_bdot_nt(jnp.concatenate([kb_j, q[rows]], axis=0), k[rows])
            for kb_j, rows in zip(kb, chunks)]
    tinv = _unit_lower_inverses([jnp.where(strict, b[:c] * d, 0.0) for b, d in zip(both, decay)])
    eg = [jnp.exp(gcol_all[rows]) for rows in chunks]
    sol = [_bdot(t, jnp.concatenate([v[rows] * beta_all[rows], kb_j * e], axis=1))
           for t, rows, kb_j, e in zip(tinv, chunks, kb, eg)]
    for j, rows in enumerate(chunks):
        gcol = gcol_all[rows]
        u_ref[rows, :] = sol[j][:, :DN_DIM].astype(u_ref.dtype)
        w_ref[rows, :] = sol[j][:, DN_DIM:].astype(w_ref.dtype)
        qe_ref[rows, :] = (q[rows] * eg[j]).astype(qe_ref.dtype)
        kd_ref[rows, :] = (k[rows] * jnp.exp(gcol[c - 1:c, :] - gcol)).astype(kd_ref.dtype)
        a_ref[rows, :] = jnp.where(causal, both[j][c:] * decay[j], 0.0).astype(a_ref.dtype)


def _delta_scan_kernel(u_ref, w_ref, qe_ref, kd_ref, a_ref, z_ref, auxc_ref, gain_ref, o_ref, st_ref):
    hb = pl.program_id(1)
    seq = u_ref.shape[0]
    n_heads = a_ref.shape[0]
    c = DN_CHUNK
    st_ref[...] = jnp.zeros_like(st_ref)
    gain = gain_ref[...]
    lane = lax.broadcasted_iota(jnp.int32, (1, LANES), 1)

    def scan_chunk(i, carry):
        start = pl.multiple_of(i * c, c)
        rows = pl.ds(start, c)
        last = auxc_ref[pl.ds(start + c - 1, 1), :]
        heads = range(n_heads)
        cols = [slice(hh * DN_DIM, (hh + 1) * DN_DIM) for hh in heads]
        state = [st_ref[hh] for hh in heads]
        sb = [s.astype(BF16) for s in state]
        ws = [jnp.dot(w_ref[rows, cl], s, preferred_element_type=F32) for cl, s in zip(cols, sb)]
        qs = [jnp.dot(qe_ref[rows, cl], s, preferred_element_type=F32) for cl, s in zip(cols, sb)]
        vb = [(u_ref[rows, cl].astype(F32) - x).astype(BF16) for cl, x in zip(cols, ws)]
        kv = [lax.dot_general(kd_ref[rows, cl], x, (((0,), (0,)), ((), ())), preferred_element_type=F32)
              for cl, x in zip(cols, vb)]
        av = [jnp.dot(a_ref[hh, rows, :], x, preferred_element_type=F32) for hh, x in zip(heads, vb)]
        for hh in heads:
            glast = jnp.sum(jnp.where(lane == AUX_G + hb * n_heads + hh, last, 0.0), axis=1, keepdims=True)
            st_ref[hh] = state[hh] * jnp.exp(glast) + kv[hh]
            z = z_ref[rows, cols[hh]].astype(F32)
            o_ref[rows, cols[hh]] = (_rms(qs[hh] + av[hh], gain) * _silu(z)).astype(o_ref.dtype)
        return carry

    lax.fori_loop(0, seq // c, scan_chunk, 0)


def _delta(main3, auxc3, auxr, conv_w_t, gain):
    bsz, seq, _ = main3.shape
    hb = DN_WIDTH // DN_DIM
    ts = min(DELTA_TILE, seq)
    halo = 16
    tiles = seq // ts

    def col(offset):
        return pl.BlockSpec((None, ts, DN_DIM), lambda b, h, i: (b, i, offset * hb + h))

    def halo_col(offset):
        return pl.BlockSpec((None, halo, DN_DIM),
                            lambda b, h, i: (b, jnp.maximum(i * (ts // halo) - 1, 0), offset * hb + h))

    def wcol(offset):
        return pl.BlockSpec((DN_CONV, DN_DIM), lambda b, h, i: (0, offset * hb + h))

    tok = pl.BlockSpec((None, ts, DN_DIM), lambda b, h, i: (b, i, h))
    tok_shape = jax.ShapeDtypeStruct((bsz, seq, DN_WIDTH), BF16)
    u, w, qe, kd, a = pl.pallas_call(
        _delta_prep_kernel,
        grid=(bsz, DN_HEADS, tiles),
        in_specs=[col(0), col(1), col(2), halo_col(0), halo_col(1), halo_col(2),
                  pl.BlockSpec((None, ts, LANES), lambda b, h, i: (b, i, 0)),
                  pl.BlockSpec((LANES, ts), lambda b, h, i: (0, b * tiles + i)),
                  wcol(0), wcol(1), wcol(2)],
        out_specs=[tok, tok, tok, tok,
                   pl.BlockSpec((None, None, ts, DN_CHUNK), lambda b, h, i: (b, h, i, 0))],
        out_shape=[tok_shape, tok_shape, tok_shape, tok_shape,
                   jax.ShapeDtypeStruct((bsz, DN_HEADS, seq, DN_CHUNK), BF16)],
        compiler_params=_params(3),
        name="delta_prep",
    )(main3, main3, main3, main3, main3, main3, auxc3, auxr, conv_w_t, conv_w_t, conv_w_t)

    hp = DELTA_SCAN_HEADS
    wide = pl.BlockSpec((None, seq, hp * DN_DIM), lambda b, g: (b, 0, g))
    return pl.pallas_call(
        _delta_scan_kernel,
        grid=(bsz, DN_HEADS // hp),
        in_specs=[wide, wide, wide, wide,
                  pl.BlockSpec((None, hp, seq, DN_CHUNK), lambda b, g: (b, g, 0, 0)),
                  pl.BlockSpec((None, seq, hp * DN_DIM), lambda b, g: (b, 0, 3 * (hb // hp) + g)),
                  pl.BlockSpec((None, seq, LANES), lambda b, g: (b, 0, 0)),
                  pl.BlockSpec((1, DN_DIM), lambda b, g: (0, 0))],
        out_specs=wide,
        out_shape=tok_shape,
        scratch_shapes=[pltpu.VMEM((hp, DN_DIM, DN_DIM), F32)],
        compiler_params=_params(2),
        name="delta_scan",
    )(u, w, qe, kd, a, main3, auxc3, gain)


def _mla_proj_kernel(cq_ref, ckv_ref, auxc_ref, pos_ref, qg_ref, kvg_ref, wq_ref, wkv_ref, rope_ref,
                     q_ref, k_ref, v_ref):
    scale = (MLA_NOPE + MLA_ROPE) ** -0.5
    ang = pos_ref[...].astype(F32) * rope_ref[0:1, :]
    cos_t = jnp.cos(ang) * rope_ref[1:2, :]
    sin_t = jnp.sin(ang) * rope_ref[2:3, :]
    lane = lax.broadcasted_iota(jnp.int32, ang.shape, 1)
    first = lane < MLA_ROPE // 2

    def rope(a):
        swapped = jnp.where(first, pltpu.roll(a, LANES - MLA_ROPE // 2, 1), pltpu.roll(a, MLA_ROPE // 2, 1))
        return a * cos_t + swapped * sin_t

    ql = _bdot(_rms(cq_ref[...].astype(F32), qg_ref[...]), wq_ref[...]) * scale
    kv = _bdot(_rms(ckv_ref[...].astype(F32), kvg_ref[...]), wkv_ref[...])
    kpe = rope(auxc_ref[...]).astype(k_ref.dtype)
    for h in range(MLA_HEADS):
        o = h * MLA_QK_PAD
        q_ref[:, o:o + MLA_NOPE] = ql[:, o:o + MLA_NOPE].astype(q_ref.dtype)
        q_ref[:, o + MLA_NOPE:o + MLA_QK_PAD] = rope(ql[:, o + MLA_NOPE:o + MLA_QK_PAD]).astype(q_ref.dtype)
        k_ref[:, o:o + MLA_NOPE] = kv[:, h * MLA_NOPE:(h + 1) * MLA_NOPE].astype(k_ref.dtype)
        k_ref[:, o + MLA_NOPE:o + MLA_QK_PAD] = kpe
    v_ref[...] = kv[:, MLA_HEADS * MLA_NOPE:].T.astype(v_ref.dtype)


def _mla_proj(main, auxc, pos, q_gain, kv_gain, wq, wkv, rope_tab, seq):
    t = main.shape[0]
    tm = min(512, seq)
    per_b = seq // tm
    cq_blk = (4 * DN_WIDTH) // MLA_Q_RANK
    hq = MLA_HEADS * MLA_QK_PAD
    hv = MLA_HEADS * MLA_V
    return pl.pallas_call(
        _mla_proj_kernel,
        grid=(t // tm,),
        in_specs=[pl.BlockSpec((tm, MLA_Q_RANK), lambda i: (i, cq_blk)),
                  pl.BlockSpec((tm, MLA_KV_RANK), lambda i: (i, cq_blk + 1)),
                  pl.BlockSpec((tm, LANES), lambda i: (i, 0)),
                  pl.BlockSpec((tm, 1), lambda i: (i, 0)),
                  pl.BlockSpec((1, MLA_Q_RANK), lambda i: (0, 0)),
                  pl.BlockSpec((1, MLA_KV_RANK), lambda i: (0, 0)),
                  pl.BlockSpec((MLA_Q_RANK, hq), lambda i: (0, 0)),
                  pl.BlockSpec((MLA_KV_RANK, 2 * hv), lambda i: (0, 0)),
                  pl.BlockSpec((8, LANES), lambda i: (0, 0))],
        out_specs=[pl.BlockSpec((tm, hq), lambda i: (i, 0)),
                   pl.BlockSpec((tm, hq), lambda i: (i, 0)),
                   pl.BlockSpec((None, hv, tm), lambda i: (i // per_b, 0, i % per_b))],
        out_shape=[jax.ShapeDtypeStruct((t, hq), BF16),
                   jax.ShapeDtypeStruct((t, hq), BF16),
                   jax.ShapeDtypeStruct((t // seq, hv, seq), BF16)],
        compiler_params=_params(1),
        name="mla_proj",
    )(main, main, auxc, pos, q_gain, kv_gain, wq, wkv, rope_tab)


ATTN_PARTS = 2


def _attn_kernel(q_ref, k_ref, vt_ref, o_ref, m_ref, l_ref, acc_ref):
    qi = pl.program_id(2)
    n_parts = m_ref.shape[0]
    tk = m_ref.shape[2]
    q = [q_ref[p * tk:(p + 1) * tk, :] for p in range(n_parts)]
    m_ref[...] = jnp.full_like(m_ref, -jnp.inf)
    l_ref[...] = jnp.zeros_like(l_ref)
    acc_ref[...] = jnp.zeros_like(acc_ref)
    key = lax.broadcasted_iota(jnp.int32, (tk, tk), 0)
    query = lax.broadcasted_iota(jnp.int32, (tk, tk), 1)
    visible = key <= query

    def step(j, parts, masked_part):
        start = pl.multiple_of(j * tk, tk)
        kb = k_ref[pl.ds(start, tk), :]
        vtb = vt_ref[:, pl.ds(start, tk)]
        s = [lax.dot_general(kb, q[p], (((1,), (1,)), ((), ())), preferred_element_type=F32) for p in parts]
        s = [jnp.where(visible, x, -jnp.inf) if p == masked_part else x for p, x in zip(parts, s)]
        m_old = [m_ref[p] for p in parts]
        m_new = [jnp.maximum(mo, jnp.max(x, axis=0, keepdims=True)) for mo, x in zip(m_old, s)]
        e = [jnp.exp(x - mn) for x, mn in zip(s, m_new)]
        pv = [jnp.dot(vtb, x.astype(BF16), preferred_element_type=F32) for x in e]
        for i, p in enumerate(parts):
            alpha = jnp.exp(m_old[i] - m_new[i])
            l_ref[p] = alpha * l_ref[p] + jnp.sum(e[i], axis=0, keepdims=True)
            acc_ref[p] = alpha * acc_ref[p] + pv[i]
            m_ref[p] = m_new[i]

    every = tuple(range(n_parts))

    def body(j, carry):
        step(j, every, None)
        return carry

    lax.fori_loop(0, qi * n_parts, body, 0)
    for d in range(n_parts):
        step(qi * n_parts + d, every[d:], d)
    for p in every:
        o_ref[p * tk:(p + 1) * tk, :] = (acc_ref[p] / l_ref[p]).T.astype(o_ref.dtype)


def _attention(q3, k3, vt3):
    bsz, seq, _ = q3.shape
    tk = min(512, seq)
    n_parts = min(ATTN_PARTS, seq // tk)
    tq = tk * n_parts
    return pl.pallas_call(
        _attn_kernel,
        grid=(bsz, MLA_HEADS, seq // tq),
        in_specs=[pl.BlockSpec((None, tq, MLA_QK_PAD), lambda b, h, i: (b, i, h)),
                  pl.BlockSpec((None, seq, MLA_QK_PAD), lambda b, h, i: (b, 0, h)),
                  pl.BlockSpec((None, MLA_V, seq), lambda b, h, i: (b, h, 0))],
        out_specs=pl.BlockSpec((None, tq, MLA_V), lambda b, h, i: (b, i, h)),
        out_shape=jax.ShapeDtypeStruct((bsz, seq, MLA_HEADS * MLA_V), BF16),
        scratch_shapes=[pltpu.VMEM((n_parts, 1, tk), F32), pltpu.VMEM((n_parts, 1, tk), F32),
                        pltpu.VMEM((n_parts, MLA_V, tk), F32)],
        compiler_params=_params(3),
        name="attention",
    )(q3, k3, vt3)


def _out_proj_kernel(x_ref, dn_ref, mla_ref, ada_ref, gain_ref, w_ref, wr_ref, x1_ref, h2_ref, lg_ref):
    ada = ada_ref[0]
    half = dn_ref.shape[1]
    mix = (jnp.dot(dn_ref[...], w_ref[:half, :], preferred_element_type=F32)
           + jnp.dot(mla_ref[...], w_ref[half:, :], preferred_element_type=F32))
    x1 = x_ref[...] + ada[2:3, :] * mix
    x1_ref[...] = x1
    h2 = _rms(x1, gain_ref[...]) * (1.0 + ada[4:5, :]) + ada[3:4, :]
    h2_ref[...] = _pack_bf16_pairs(h2)
    h_hi = h2.astype(BF16)
    h_lo = (h2 - h_hi.astype(F32)).astype(BF16)
    both = jnp.dot(h_hi, wr_ref[...], preferred_element_type=F32)
    logits = both[:, :LANES] + both[:, LANES:] + jnp.dot(h_lo, wr_ref[:, :LANES], preferred_element_type=F32)
    lg_ref[...] = logits.T


def _out_proj(x2, dn, mla, ada3, gain, w_out, w_router, seq):
    t, d = x2.shape
    tm = min(256, seq)
    per_b = seq // tm
    half = dn.shape[1]
    return pl.pallas_call(
        _out_proj_kernel,
        grid=(t // tm,),
        in_specs=[pl.BlockSpec((tm, d), lambda i: (i, 0)),
                  pl.BlockSpec((tm, half), lambda i: (i, 0)),
                  pl.BlockSpec((tm, half), lambda i: (i, 0)),
                  pl.BlockSpec((1, 6, d), lambda i: (i // per_b, 0, 0)),
                  pl.BlockSpec((1, d), lambda i: (0, 0)),
                  pl.BlockSpec((2 * half, d), lambda i: (0, 0)),
                  pl.BlockSpec((d, 2 * LANES), lambda i: (0, 0))],
        out_specs=[pl.BlockSpec((tm, d), lambda i: (i, 0)),
                   pl.BlockSpec((tm, d // 2), lambda i: (i, 0)),
                   pl.BlockSpec((LANES, tm), lambda i: (0, i))],
        out_shape=[jax.ShapeDtypeStruct((t, d), F32),
                   jax.ShapeDtypeStruct((t, d // 2), jnp.uint32),
                   jax.ShapeDtypeStruct((LANES, t), F32)],
        compiler_params=_params(1),
        name="out_proj",
    )(x2, dn, mla, ada3, gain, w_out, w_router)


def _expert_kernel(be_ref, nv_ref, x_ref, wgu_ref, wd_ref, y_ref, gu_s, d_s):
    i = pl.program_id(0)
    changed = (i == 0) | (be_ref[i] != be_ref[jnp.maximum(i - 1, 0)])

    @pl.when(changed)
    def _():
        gu_s[...] = wgu_ref[...].astype(BF16)
        d_s[...] = wd_ref[...].astype(BF16)

    x = _unpack_bf16_pairs(x_ref[...]).astype(BF16)
    ff = d_s.shape[0]
    chunks = [slice(c0, c0 + MXU_WIDTH) for c0 in range(0, ff, MXU_WIDTH)]
    gate_up = [(jnp.dot(x, gu_s[:, cs], preferred_element_type=F32),
                jnp.dot(x, gu_s[:, slice(ff + cs.start, ff + cs.stop)], preferred_element_type=F32))
               for cs in chunks]
    act = [(_silu(g) * u).astype(BF16) for g, u in gate_up]
    y = jnp.dot(act[0], d_s[chunks[0], :], preferred_element_type=F32)
    for a, cs in zip(act[1:], chunks[1:]):
        y = y + jnp.dot(a, d_s[cs, :], preferred_element_type=F32)
    row = lax.broadcasted_iota(jnp.int32, y.shape, 0)
    y_ref[...] = _pack_bf16_pairs(jnp.where(row < nv_ref[i], y, 0.0))


def _experts(block_e, n_valid, x_sorted, w_gu, w_d):
    n_rows, dp = x_sorted.shape
    ff2 = w_gu.shape[2]
    n_blocks = n_rows // EXPERT_BLOCK
    grid_spec = pltpu.PrefetchScalarGridSpec(
        num_scalar_prefetch=2,
        grid=(n_blocks,),
        in_specs=[pl.BlockSpec((EXPERT_BLOCK, dp), lambda i, be, nv: (i, 0)),
                  pl.BlockSpec((None, 2 * dp, ff2), lambda i, be, nv: (be[i], 0, 0)),
                  pl.BlockSpec((None, ff2 // 2, 2 * dp), lambda i, be, nv: (be[i], 0, 0))],
        out_specs=pl.BlockSpec((EXPERT_BLOCK, dp), lambda i, be, nv: (i, 0)),
        scratch_shapes=[pltpu.VMEM((2 * dp, ff2), BF16), pltpu.VMEM((ff2 // 2, 2 * dp), BF16)],
    )
    return pl.pallas_call(
        _expert_kernel,
        grid_spec=grid_spec,
        out_shape=jax.ShapeDtypeStruct((n_rows, dp), jnp.uint32),
        compiler_params=_params(0, 1),
        name="experts",
    )(block_e, n_valid, x_sorted, w_gu, w_d)


def _final_kernel(x1_ref, h2_ref, y_ref, wc_ref, ada_ref, wgu_ref, wd_ref, gain_ref, o_ref):
    ada = ada_ref[0]
    h2 = _unpack_bf16_pairs(h2_ref[...]).astype(BF16)
    gu = jnp.dot(h2, wgu_ref[...], preferred_element_type=F32)
    ff = gu.shape[1] // 2
    act = _silu(gu[:, :ff]) * gu[:, ff:]
    ffn = jnp.dot(act.astype(BF16), wd_ref[...], preferred_element_type=F32)
    wc = wc_ref[...]
    for kk in range(TOP_K):
        ffn = ffn + wc[:, kk:kk + 1] * _unpack_bf16_pairs(y_ref[kk])
    x2 = x1_ref[...] + ada[5:6, :] * ffn
    o_ref[...] = _rms(x2, gain_ref[...])


def _final(x1, h2, y_tok, wc, ada3, w_gu, w_d, gain, seq):
    t, d = x1.shape
    tm = min(256, seq)
    per_b = seq // tm
    ff2 = w_gu.shape[1]
    return pl.pallas_call(
        _final_kernel,
        grid=(t // tm,),
        in_specs=[pl.BlockSpec((tm, d), lambda i: (i, 0)),
                  pl.BlockSpec((tm, d // 2), lambda i: (i, 0)),
                  pl.BlockSpec((TOP_K, tm, d // 2), lambda i: (0, i, 0)),
                  pl.BlockSpec((tm, LANES), lambda i: (i, 0)),
                  pl.BlockSpec((1, 6, d), lambda i: (i // per_b, 0, 0)),
                  pl.BlockSpec((d, ff2), lambda i: (0, 0)),
                  pl.BlockSpec((ff2 // 2, d), lambda i: (0, 0)),
                  pl.BlockSpec((1, d), lambda i: (0, 0))],
        out_specs=pl.BlockSpec((tm, d), lambda i: (i, 0)),
        out_shape=jax.ShapeDtypeStruct((t, d), F32),
        compiler_params=_params(1),
        name="final",
    )(x1, h2, y_tok, wc, ada3, w_gu, w_d, gain)


ROUTE_TILE = 512
MAX_BLOCK_LANES = 512


def _first_argmax(x, idx_f, n):
    m = jnp.max(x, axis=0, keepdims=True)
    first = jnp.min(jnp.where(x == m, idx_f, float(n)), axis=0, keepdims=True)
    return m, first


def _route_kernel(lg_ref, bias_ref, ek_ref, rk_ref, wk_ref, wc_ref, ps_ref, be_ref, carry_ref, upper_ref):
    i = pl.program_id(0)
    tm = lg_ref.shape[1]
    per_group = N_EXPERTS // N_GROUPS

    @pl.when(i == 0)
    def _():
        carry_ref[...] = jnp.zeros_like(carry_ref)
        r = lax.broadcasted_iota(jnp.int32, (tm, tm), 0)
        cc = lax.broadcasted_iota(jnp.int32, (tm, tm), 1)
        upper_ref[...] = (r < cc).astype(BF16)

    scores = jax.nn.sigmoid(lg_ref[0:N_EXPERTS, :])
    biased = scores + bias_ref[:, 0:1]
    sub8 = lax.broadcasted_iota(jnp.int32, (per_group, tm), 0).astype(F32)
    group_rows = []
    for g in range(N_GROUPS):
        xg = biased[g * per_group:(g + 1) * per_group, :]
        m1, i1 = _first_argmax(xg, sub8, per_group)
        m2 = jnp.max(jnp.where(sub8 == i1, -jnp.inf, xg), axis=0, keepdims=True)
        group_rows.append(m1 + m2)
    cur = jnp.concatenate(group_rows, axis=0)
    gself = jnp.zeros(cur.shape, F32)
    for _ in range(TOPK_GROUPS):
        _, gi = _first_argmax(cur, sub8, N_GROUPS)
        hit = sub8 == gi
        gself = jnp.where(hit, 1.0, gself)
        cur = jnp.where(hit, -jnp.inf, cur)
    masked = jnp.concatenate(
        [jnp.where(gself[g:g + 1, :] > 0.5, biased[g * per_group:(g + 1) * per_group, :], -jnp.inf)
         for g in range(N_GROUPS)], axis=0)
    sub64 = lax.broadcasted_iota(jnp.int32, (N_EXPERTS, tm), 0).astype(F32)
    e_rows, s_rows = [], []
    sel = jnp.zeros(masked.shape, F32)
    for _ in range(TOP_K):
        _, ei = _first_argmax(masked, sub64, N_EXPERTS)
        hit = sub64 == ei
        e_rows.append(ei)
        s_rows.append(jnp.sum(jnp.where(hit, scores, 0.0), axis=0, keepdims=True))
        sel = jnp.where(hit, 1.0, sel)
        masked = jnp.where(hit, -jnp.inf, masked)
    total = s_rows[0]
    for s in s_rows[1:]:
        total = total + s
    w_rows = [s / total * ROUTED_SCALE for s in s_rows]
    before = jnp.dot(sel.astype(BF16), upper_ref[...], preferred_element_type=F32) + carry_ref[:, 0:1]
    r_rows = [jnp.sum(jnp.where(sub64 == ei, before, 0.0), axis=0, keepdims=True) for ei in e_rows]
    zrow = jnp.zeros((8 - TOP_K, tm), F32)
    ek_ref[...] = jnp.concatenate(e_rows + [zrow], axis=0).astype(jnp.int32)
    rk_ref[...] = jnp.concatenate(r_rows + [zrow], axis=0).astype(jnp.int32)
    wk = jnp.concatenate(w_rows + [zrow], axis=0)
    wk_ref[...] = wk
    wc_ref[...] = jnp.concatenate([wk, jnp.zeros((LANES - 8, tm), F32)], axis=0).T
    carry_ref[...] = carry_ref[...] + jnp.sum(sel, axis=1, keepdims=True)

    @pl.when(i == pl.num_programs(0) - 1)
    def _():
        counts = carry_ref[...]
        padded = jnp.floor((counts + (EXPERT_BLOCK - 1.0)) * (1.0 / EXPERT_BLOCK)) * EXPERT_BLOCK
        r = lax.broadcasted_iota(jnp.int32, (N_EXPERTS, N_EXPERTS), 0)
        cc = lax.broadcasted_iota(jnp.int32, (N_EXPERTS, N_EXPERTS), 1)
        pad_end = jnp.dot((cc <= r).astype(F32), padded, precision=lax.Precision.HIGHEST,
                          preferred_element_type=F32)
        pad_start = pad_end - padded
        ps_ref[...] = pad_start
        blk0 = lax.broadcasted_iota(jnp.int32, (N_EXPERTS, MAX_BLOCK_LANES), 1).astype(F32) * EXPERT_BLOCK
        e_of_blk = jnp.minimum(jnp.sum((pad_end[:, 0:1] <= blk0).astype(F32), axis=0, keepdims=True),
                               N_EXPERTS - 1.0)
        sub = lax.broadcasted_iota(jnp.int32, (N_EXPERTS, MAX_BLOCK_LANES), 0).astype(F32)
        seg_end = jnp.sum(jnp.where(sub == e_of_blk, (pad_start + counts)[:, 0:1], 0.0), axis=0, keepdims=True)
        n_valid = jnp.clip(seg_end - blk0[0:1, :], 0.0, float(EXPERT_BLOCK))
        be_ref[...] = jnp.concatenate([e_of_blk, n_valid, jnp.zeros((6, MAX_BLOCK_LANES), F32)],
                                      axis=0).astype(jnp.int32)


def _route_pallas(logits_t, bias_col):
    t = logits_t.shape[1]
    tm = min(ROUTE_TILE, t)
    row8 = pl.BlockSpec((8, tm), lambda i: (0, i))
    return pl.pallas_call(
        _route_kernel,
        grid=(t // tm,),
        in_specs=[pl.BlockSpec((LANES, tm), lambda i: (0, i)),
                  pl.BlockSpec((N_EXPERTS, 1), lambda i: (0, 0))],
        out_specs=[row8, row8, row8,
                   pl.BlockSpec((tm, LANES), lambda i: (i, 0)),
                   pl.BlockSpec((N_EXPERTS, LANES), lambda i: (0, 0)),
                   pl.BlockSpec((8, MAX_BLOCK_LANES), lambda i: (0, 0))],
        out_shape=[jax.ShapeDtypeStruct((8, t), jnp.int32),
                   jax.ShapeDtypeStruct((8, t), jnp.int32),
                   jax.ShapeDtypeStruct((8, t), F32),
                   jax.ShapeDtypeStruct((t, LANES), F32),
                   jax.ShapeDtypeStruct((N_EXPERTS, LANES), F32),
                   jax.ShapeDtypeStruct((8, MAX_BLOCK_LANES), jnp.int32)],
        scratch_shapes=[pltpu.VMEM((N_EXPERTS, LANES), F32), pltpu.VMEM((tm, tm), BF16)],
        compiler_params=_params(0, 1),
        name="route",
    )(logits_t, bias_col)


def _slot_kernel(ek_ref, rk_ref, ps_ref, slot_ref):
    tm = ek_ref.shape[1]
    sub64 = lax.broadcasted_iota(jnp.int32, (N_EXPERTS, tm), 0)
    start = ps_ref[:, 0:1]
    rows = []
    for kk in range(TOP_K):
        seg = jnp.sum(jnp.where(sub64 == ek_ref[kk:kk + 1, :], start, 0.0), axis=0, keepdims=True)
        rows.append(seg.astype(jnp.int32) + rk_ref[kk:kk + 1, :])
    slot_ref[...] = jnp.concatenate(rows + [jnp.zeros((8 - TOP_K, tm), jnp.int32)], axis=0)


def _slots(ek, rk, ps):
    t = ek.shape[1]
    tm = min(ROUTE_TILE, t)
    row8 = pl.BlockSpec((8, tm), lambda i: (0, i))
    return pl.pallas_call(
        _slot_kernel,
        grid=(t // tm,),
        in_specs=[row8, row8, pl.BlockSpec((N_EXPERTS, LANES), lambda i: (0, 0))],
        out_specs=row8,
        out_shape=jax.ShapeDtypeStruct((8, t), jnp.int32),
        compiler_params=_params(1),
        name="slots",
    )(ek, rk, ps)


SC_ROWS = 32


def _sc_mesh():
    info = plsc.get_sparse_core_info()
    mesh = plsc.VectorSubcoreMesh(core_axis_name="c", subcore_axis_name="s")
    return mesh, info.num_cores, info.num_cores * info.num_subcores


def _sc_dispatch(x, slot_flat, n_rows):
    t, dp = x.shape
    n_slots = slot_flat.shape[0] // t
    mesh, n_cores, n_workers = _sc_mesh()
    per_w = t // n_workers
    n_chunks = per_w // SC_ROWS
    assert per_w * n_workers == t and n_chunks * SC_ROWS == per_w and n_chunks % 2 == 0

    @functools.partial(
        pl.kernel, out_type=jax.ShapeDtypeStruct((n_rows, dp), x.dtype), mesh=mesh,
        scratch_types=[pltpu.VMEM((2, n_slots, SC_ROWS), jnp.int32), pltpu.VMEM((2, SC_ROWS, dp), x.dtype),
                       pltpu.SemaphoreType.DMA((2,)), pltpu.SemaphoreType.DMA((2,))])
    def dispatch(x_hbm, i_hbm, o_hbm, idx_v, rows_v, read_sem, scatter_sem):
        base = (lax.axis_index("s") * n_cores + lax.axis_index("c")) * per_w

        def read(j, b):
            return pltpu.make_async_copy(x_hbm.at[pl.ds(pl.multiple_of(base + j * SC_ROWS, 8), SC_ROWS)],
                                         rows_v.at[b], read_sem.at[b])

        def scatter(k, b):
            return pltpu.make_async_copy(rows_v.at[b], o_hbm.at[idx_v.at[b, k]], scatter_sem.at[b])

        read(0, 0).start()

        @pl.loop(0, n_chunks, step=2)
        def _(j):
            for b in range(2):
                jj = j + b

                @pl.when(jj >= 1)
                def _():
                    for k in range(n_slots):
                        scatter(k, 1 - b).wait()

                @pl.when(jj + 1 < n_chunks)
                def _():
                    read(jj + 1, 1 - b).start()

                for k in range(n_slots):
                    pltpu.sync_copy(i_hbm.at[pl.ds(pl.multiple_of(k * t + base + jj * SC_ROWS, 8), SC_ROWS)],
                                    idx_v.at[b, k])
                read(jj, b).wait()
                for k in range(n_slots):
                    scatter(k, b).start()

        for k in range(n_slots):
            scatter(k, (n_chunks - 1) % 2).wait()

    return dispatch(x, slot_flat)


def _sc_combine(y, slot_flat):
    n = slot_flat.shape[0]
    dp = y.shape[1]
    mesh, n_cores, n_workers = _sc_mesh()
    per_w = n // n_workers
    n_chunks = per_w // SC_ROWS
    assert per_w * n_workers == n and n_chunks * SC_ROWS == per_w and n_chunks % 2 == 0

    @functools.partial(
        pl.kernel, out_type=jax.ShapeDtypeStruct((n, dp), y.dtype), mesh=mesh,
        scratch_types=[pltpu.VMEM((per_w,), jnp.int32), pltpu.VMEM((2, SC_ROWS, dp), y.dtype),
                       pltpu.SemaphoreType.DMA((2,)), pltpu.SemaphoreType.DMA((2,))])
    def combine(y_hbm, i_hbm, o_hbm, idx_v, rows_v, gather_sem, write_sem):
        base = (lax.axis_index("s") * n_cores + lax.axis_index("c")) * per_w
        pltpu.sync_copy(i_hbm.at[pl.ds(pl.multiple_of(base, 8), per_w)], idx_v)

        def gather(j, b):
            return pltpu.make_async_copy(y_hbm.at[idx_v.at[pl.ds(pl.multiple_of(j * SC_ROWS, 8), SC_ROWS)]],
                                         rows_v.at[b], gather_sem.at[b])

        def write(j, b):
            return pltpu.make_async_copy(rows_v.at[b],
                                         o_hbm.at[pl.ds(pl.multiple_of(base + j * SC_ROWS, 8), SC_ROWS)],
                                         write_sem.at[b])

        gather(0, 0).start()

        @pl.loop(0, n_chunks, step=2)
        def _(j):
            for b in range(2):
                jj = j + b

                @pl.when(jj >= 1)
                def _():
                    write(jj - 1, 1 - b).wait()

                @pl.when(jj + 1 < n_chunks)
                def _():
                    gather(jj + 1, 1 - b).start()

                gather(jj, b).wait()
                write(jj, b).start()

        write(n_chunks - 1, (n_chunks - 1) % 2).wait()

    return combine(y, slot_flat)


def _layer(x, cond_ada, positions, w_in, dn_conv_w, dn_a_log, dn_dt_bias, dn_norm_gain, mla_q_norm_gain,
           w_q_up, mla_kv_norm_gain, w_kv_up, w_out, norm1_gain, norm2_gain, w_router, router_bias,
           w_exp_gate_up, w_exp_down, w_sh_gate_up, w_sh_down, out_gain):
    bsz, seq, d = x.shape
    t = bsz * seq
    x2 = x.reshape(t, d)
    ada3 = cond_ada.reshape(bsz, 6, d)

    s_z = 4 * DN_WIDTH
    s_a = s_z + 2 * DN_HEADS
    s_kpe = s_a + MLA_Q_RANK + MLA_KV_RANK
    w_main = jnp.concatenate([w_in[:, :s_z], w_in[:, s_a:s_kpe]], axis=1).astype(BF16)
    w_aux = jnp.concatenate([w_in[:, s_kpe:], w_in[:, s_z:s_a],
                             jnp.zeros((d, LANES - MLA_ROPE - 2 * DN_HEADS), F32)], axis=1).astype(BF16)
    auxp = jnp.zeros((8, LANES), F32)
    auxp = auxp.at[0, AUX_G:AUX_G + DN_HEADS].set(dn_a_log).at[1, AUX_G:AUX_G + DN_HEADS].set(dn_dt_bias)
    main, auxc, auxr = _in_proj(x2, ada3, norm1_gain.reshape(1, d), w_main, w_aux, auxp, seq)

    dn = _delta(main.reshape(bsz, seq, MAIN_WIDTH), auxc.reshape(bsz, seq, LANES), auxr,
                dn_conv_w.T, dn_norm_gain.reshape(1, DN_DIM))

    qk = MLA_NOPE + MLA_ROPE
    wq3 = w_q_up.reshape(MLA_Q_RANK, MLA_HEADS, qk)
    wq = jnp.concatenate([wq3, jnp.zeros((MLA_Q_RANK, MLA_HEADS, MLA_QK_PAD - qk), F32)], axis=2)
    wq = wq.reshape(MLA_Q_RANK, MLA_HEADS * MLA_QK_PAD).astype(BF16)
    wkv3 = w_kv_up.reshape(MLA_KV_RANK, MLA_HEADS, MLA_NOPE + MLA_V)
    wkv = jnp.concatenate([wkv3[:, :, :MLA_NOPE].reshape(MLA_KV_RANK, -1),
                           wkv3[:, :, MLA_NOPE:].reshape(MLA_KV_RANK, -1)], axis=1).astype(BF16)
    half = MLA_ROPE // 2
    inv_freq = ROPE_THETA ** (-jnp.arange(half, dtype=F32) / half)
    zeros = jnp.zeros((LANES - MLA_ROPE,), F32)
    rope_tab = jnp.zeros((8, LANES), F32)
    rope_tab = rope_tab.at[0].set(jnp.concatenate([inv_freq, inv_freq, zeros]))
    rope_tab = rope_tab.at[1].set(jnp.concatenate([jnp.ones((MLA_ROPE,), F32), zeros]))
    rope_tab = rope_tab.at[2].set(jnp.concatenate([-jnp.ones((half,), F32), jnp.ones((half,), F32), zeros]))
    q, k, v = _mla_proj(main, auxc, positions.reshape(t, 1), mla_q_norm_gain.reshape(1, -1),
                        mla_kv_norm_gain.reshape(1, -1), wq, wkv, rope_tab, seq)
    mla = _attention(q.reshape(bsz, seq, -1), k.reshape(bsz, seq, -1), v)

    w_r = jnp.concatenate([w_router, jnp.zeros((d, LANES - N_EXPERTS), F32)], axis=1)
    w_r_hi = w_r.astype(BF16)
    w_r = jnp.concatenate([w_r_hi, (w_r - w_r_hi.astype(F32)).astype(BF16)], axis=1)
    x1, h2, logits_t = _out_proj(x2, dn.reshape(t, DN_WIDTH), mla.reshape(t, -1), ada3,
                                 norm2_gain.reshape(1, d), w_out.astype(BF16), w_r, seq)

    n_rows = -(-(t * TOP_K + N_EXPERTS * (EXPERT_BLOCK - 1)) // EXPERT_BLOCK) * EXPERT_BLOCK
    n_blocks = n_rows // EXPERT_BLOCK
    assert n_blocks <= MAX_BLOCK_LANES
    ek, rk, _, wc, seg_start, blocks = _route_pallas(logits_t, router_bias.reshape(N_EXPERTS, 1))
    slot_flat = _slots(ek, rk, seg_start)[:TOP_K].reshape(-1)
    x_sorted = _sc_dispatch(h2, slot_flat, n_rows)
    y_sorted = _experts(blocks[0, :n_blocks], blocks[1, :n_blocks], x_sorted, w_exp_gate_up, w_exp_down)
    y_tok = _sc_combine(y_sorted, slot_flat).reshape(TOP_K, t, d // 2)

    out = _final(x1, h2, y_tok, wc, ada3, w_sh_gate_up.astype(BF16), w_sh_down.astype(BF16),
                 out_gain.reshape(1, d), seq)
    return out.reshape(bsz, seq, d)


def kernel(x, c, positions, w_ada, b_ada, norm1_gain, w_in, dn_conv_w, dn_a_log, dn_dt_bias, dn_norm_gain,
           mla_q_norm_gain, w_q_up, mla_kv_norm_gain, w_kv_up, w_out, norm2_gain, w_router, router_bias,
           w_exp_gate_up, w_exp_down, w_sh_gate_up, w_sh_down, final_norm_gain):
    depth = w_ada.shape[0]
    assert depth == 1, "the final RMSNorm is fused into the single layer's last kernel"
    ada = _ada(c, w_ada[0], b_ada[0])
    return _layer(x, ada, positions, w_in[0], dn_conv_w[0], dn_a_log[0], dn_dt_bias[0], dn_norm_gain[0],
                  mla_q_norm_gain[0], w_q_up[0], mla_kv_norm_gain[0], w_kv_up[0], w_out[0], norm1_gain[0],
                  norm2_gain[0], w_router[0], router_bias[0], w_exp_gate_up[0], w_exp_down[0],
                  w_sh_gate_up[0], w_sh_down[0], final_norm_gain)
```

```python
import functools

import jax
import jax.numpy as jnp
from jax import lax
from jax.experimental import pallas as pl
from jax.experimental.pallas import tpu as pltpu
from jax.experimental.pallas import tpu_sc as plsc

F32 = jnp.float32
BF16 = jnp.bfloat16

DN_HEADS = 8
DN_DIM = 128
DN_WIDTH = DN_HEADS * DN_DIM
DN_CONV = 4
DN_CHUNK = 64
MLA_HEADS = 8
MLA_Q_RANK = 512
MLA_KV_RANK = 512
MLA_NOPE = 128
MLA_ROPE = 64
MLA_V = 128
MLA_QK_PAD = 256
ROPE_THETA = 10000.0
N_EXPERTS = 64
TOP_K = 6
N_GROUPS = 8
TOPK_GROUPS = 4
EXPERT_FF = 512
ROUTED_SCALE = 2.5
EXPERT_BLOCK = 256
RMS_EPS = 1e-6
L2_EPS = 1e-6

LANES = 128
MXU_WIDTH = 256
MAIN_WIDTH = 4 * DN_WIDTH + MLA_Q_RANK + MLA_KV_RANK
AUX_BETA = MLA_ROPE
AUX_G = MLA_ROPE + DN_HEADS
VMEM_LIMIT = 56 * 1024 * 1024


def _params(n_parallel, n_arbitrary=0):
    sem = ("parallel",) * n_parallel + ("arbitrary",) * n_arbitrary
    return pltpu.CompilerParams(dimension_semantics=sem, vmem_limit_bytes=VMEM_LIMIT)


def _silu(x):
    return x * jax.nn.sigmoid(x)


def _bdot(a, b):
    return jnp.dot(a.astype(BF16), b.astype(BF16), preferred_element_type=F32)


def _bdot_nt(a, b):
    return lax.dot_general(a.astype(BF16), b.astype(BF16), (((1,), (1,)), ((), ())),
                           preferred_element_type=F32)


def _bdot_tn(a, b):
    return lax.dot_general(a.astype(BF16), b.astype(BF16), (((0,), (0,)), ((), ())),
                           preferred_element_type=F32)


def _rms(x, gain):
    return x * lax.rsqrt(jnp.mean(x * x, axis=-1, keepdims=True) + RMS_EPS) * gain


HIGH_HALF = 0xFFFF0000


def _pack_bf16_pairs(x):
    n = x.shape[1] // 2
    bits = lax.bitcast_convert_type(x.astype(BF16).astype(F32), jnp.uint32)
    return (bits[:, :n] >> 16) | (bits[:, n:] & jnp.uint32(HIGH_HALF))


def _unpack_bf16_pairs(p):
    lo = lax.bitcast_convert_type(p << 16, F32)
    hi = lax.bitcast_convert_type(p & jnp.uint32(HIGH_HALF), F32)
    return jnp.concatenate([lo, hi], axis=1)


def _ada_kernel(c_ref, w_ref, b_ref, o_ref):
    o_ref[...] = _bdot(_silu(c_ref[...]), w_ref[...]) + b_ref[...]


def _ada(c, w_ada, b_ada):
    bsz, d = c.shape
    n = w_ada.shape[1]
    tn = 1024
    return pl.pallas_call(
        _ada_kernel,
        grid=(n // tn,),
        in_specs=[pl.BlockSpec((bsz, d), lambda j: (0, 0)),
                  pl.BlockSpec((d, tn), lambda j: (0, j)),
                  pl.BlockSpec((1, tn), lambda j: (0, j))],
        out_specs=pl.BlockSpec((bsz, tn), lambda j: (0, j)),
        out_shape=jax.ShapeDtypeStruct((bsz, n), F32),
        compiler_params=_params(1),
        name="ada",
    )(c, w_ada, b_ada.reshape(1, n))


def _in_proj_kernel(x_ref, ada_ref, gain_ref, wm_ref, wa_ref, auxp_ref,
                    main_ref, auxc_ref, auxr_ref, h_ref):
    j = pl.program_id(1)

    @pl.when(j == 0)
    def _():
        ada = ada_ref[0]
        h = _rms(x_ref[...], gain_ref[...]) * (1.0 + ada[1:2, :]) + ada[0:1, :]
        hb = h.astype(BF16)
        h_ref[...] = hb
        aux = jnp.dot(hb, wa_ref[...], preferred_element_type=F32)
        tm = aux.shape[0]
        lane = lax.broadcasted_iota(jnp.int32, aux.shape, 1)
        is_beta = (lane >= AUX_BETA) & (lane < AUX_G)
        is_g = (lane >= AUX_G) & (lane < AUX_G + DN_HEADS)
        a_log = auxp_ref[0:1, :]
        dt_bias = auxp_ref[1:2, :]
        sp_in = aux + dt_bias
        softplus = jnp.maximum(sp_in, 0.0) + jnp.log(1.0 + jnp.exp(-jnp.abs(sp_in)))
        g = -jnp.exp(a_log) * softplus
        aux = jnp.where(is_beta, jax.nn.sigmoid(aux), jnp.where(is_g, g, aux))
        r = lax.broadcasted_iota(jnp.int32, (LANES, LANES), 0)
        cidx = lax.broadcasted_iota(jnp.int32, (LANES, LANES), 1)
        tri = ((cidx <= r) & (cidx // DN_CHUNK == r // DN_CHUNK)).astype(F32)
        g_cols = (cidx >= AUX_G) & (cidx < AUX_G + DN_HEADS)
        parts = []
        for t in range(tm // LANES):
            blk = aux[t * LANES:(t + 1) * LANES, :]
            cs = jnp.dot(tri, blk, precision=lax.Precision.HIGHEST, preferred_element_type=F32)
            parts.append(jnp.where(g_cols, cs, blk))
        aux = jnp.concatenate(parts, axis=0)
        auxc_ref[...] = aux
        auxr_ref[...] = aux.T

    main_ref[...] = jnp.dot(h_ref[...], wm_ref[...], preferred_element_type=F32).astype(main_ref.dtype)


def _in_proj(x2, ada3, gain, w_main, w_aux, auxp, seq):
    t, d = x2.shape
    n = w_main.shape[1]
    tm = min(512, seq)
    tn = 1024
    per_b = seq // tm
    return pl.pallas_call(
        _in_proj_kernel,
        grid=(t // tm, n // tn),
        in_specs=[pl.BlockSpec((tm, d), lambda i, j: (i, 0)),
                  pl.BlockSpec((1, 6, d), lambda i, j: (i // per_b, 0, 0)),
                  pl.BlockSpec((1, d), lambda i, j: (0, 0)),
                  pl.BlockSpec((d, tn), lambda i, j: (0, j)),
                  pl.BlockSpec((d, LANES), lambda i, j: (0, 0)),
                  pl.BlockSpec((8, LANES), lambda i, j: (0, 0))],
        out_specs=[pl.BlockSpec((tm, tn), lambda i, j: (i, j)),
                   pl.BlockSpec((tm, LANES), lambda i, j: (i, 0)),
                   pl.BlockSpec((LANES, tm), lambda i, j: (0, i))],
        out_shape=[jax.ShapeDtypeStruct((t, n), BF16),
                   jax.ShapeDtypeStruct((t, LANES), F32),
                   jax.ShapeDtypeStruct((LANES, t), F32)],
        scratch_shapes=[pltpu.VMEM((tm, d), BF16)],
        compiler_params=_params(1, 1),
        name="in_proj",
    )(x2, ada3, gain, w_main, w_aux, auxp)


def _unit_lower_inverses(ms):
    c = ms[0].shape[0]
    r = lax.broadcasted_iota(jnp.int32, (c, c), 0)
    cc = lax.broadcasted_iota(jnp.int32, (c, c), 1)
    eye = (r == cc).astype(F32)
    same = (r // 16) == (cc // 16)
    md = [jnp.where(same, m, 0.0) for m in ms]
    mo = [jnp.where(same, 0.0, m) for m in ms]
    p1 = [_bdot(a, a) for a in md]
    p2 = [_bdot(a, a) for a in p1]
    p3 = [_bdot(a, a) for a in p2]
    td = [eye - a for a in md]
    td = [t + _bdot(t, p) for t, p in zip(td, p1)]
    td = [t + _bdot(t, p) for t, p in zip(td, p2)]
    td = [t + _bdot(t, p) for t, p in zip(td, p3)]
    n1 = [_bdot(t, o) for t, o in zip(td, mo)]
    n2 = [_bdot(n, n) for n in n1]
    left = [(eye - a) + _bdot(eye - a, b) for a, b in zip(n1, n2)]
    return [_bdot(l, t) for l, t in zip(left, td)]


DELTA_TILE = 1024
DELTA_SCAN_TILE = 512


def _delta_prep_kernel(q_ref, k_ref, v_ref, qh_ref, kh_ref, vh_ref, auxc_ref, auxr_ref, wq_ref, wk_ref, wv_ref,
                       u_ref, w_ref, qe_ref, kd_ref, a_ref):
    head = pl.program_id(1)
    first_tile = pl.program_id(2) == 0
    ts = q_ref.shape[0]
    c = DN_CHUNK
    halo = qh_ref.shape[0]

    def conv_silu(x_ref, h_ref, w_ref):
        prev = jnp.where(first_tile, 0.0, h_ref[...].astype(F32))
        x = jnp.concatenate([prev, x_ref[...].astype(F32)], axis=0)
        w = w_ref[...]
        acc = x * w[DN_CONV - 1:DN_CONV, :]
        for s in range(1, DN_CONV):
            acc = acc + pltpu.roll(x, s, 0) * w[DN_CONV - 1 - s:DN_CONV - s, :]
        return _silu(acc[halo:, :])

    q = conv_silu(q_ref, qh_ref, wq_ref)
    k = conv_silu(k_ref, kh_ref, wk_ref)
    v = conv_silu(v_ref, vh_ref, wv_ref)
    q = q * lax.rsqrt(jnp.sum(q * q, axis=-1, keepdims=True) + L2_EPS) * (DN_DIM ** -0.5)
    k = k * lax.rsqrt(jnp.sum(k * k, axis=-1, keepdims=True) + L2_EPS)

    auxc = auxc_ref[...]
    lane = lax.broadcasted_iota(jnp.int32, auxc.shape, 1)
    beta_all = jnp.sum(jnp.where(lane == AUX_BETA + head, auxc, 0.0), axis=1, keepdims=True)
    gcol_all = jnp.sum(jnp.where(lane == AUX_G + head, auxc, 0.0), axis=1, keepdims=True)
    grows = auxr_ref[AUX_G:AUX_G + DN_HEADS, :]
    head_row = lax.broadcasted_iota(jnp.int32, grows.shape, 0)
    grow_all = jnp.sum(jnp.where(head_row == head, grows, 0.0), axis=0, keepdims=True)

    ri = lax.broadcasted_iota(jnp.int32, (c, c), 0)
    ci = lax.broadcasted_iota(jnp.int32, (c, c), 1)
    causal = ri >= ci
    strict = ri > ci
    chunks = [slice(j * c, (j + 1) * c) for j in range(ts // c)]
    decay = [jnp.exp(jnp.where(causal, gcol_all[rows] - grow_all[:, rows], -jnp.inf)) for rows in chunks]
    kb = [k[rows] * beta_all[rows] for rows in chunks]
    both = [_bdot_nt(jnp.concatenate([kb_j, q[rows]], axis=0), k[rows])
            for kb_j, rows in zip(kb, chunks)]
    tinv = _unit_lower_inverses([jnp.where(strict, b[:c] * d, 0.0) for b, d in zip(both, decay)])
    eg = [jnp.exp(gcol_all[rows]) for rows in chunks]
    sol = [_bdot(t, jnp.concatenate([v[rows] * beta_all[rows], kb_j * e], axis=1))
           for t, rows, kb_j, e in zip(tinv, chunks, kb, eg)]
    for j, rows in enumerate(chunks):
        gcol = gcol_all[rows]
        u_ref[rows, :] = sol[j][:, :DN_DIM].astype(u_ref.dtype)
        w_ref[rows, :] = sol[j][:, DN_DIM:].astype(w_ref.dtype)
        qe_ref[rows, :] = (q[rows] * eg[j]).astype(qe_ref.dtype)
        kd_ref[rows, :] = (k[rows] * jnp.exp(gcol[c - 1:c, :] - gcol)).astype(kd_ref.dtype)
        a_ref[rows, :] = jnp.where(causal, both[j][c:] * decay[j], 0.0).astype(a_ref.dtype)


def _delta_scan_kernel(u_ref, w_ref, qe_ref, kd_ref, a_ref, z_ref, auxc_ref, gain_ref, o_ref, st_ref):
    seq = u_ref.shape[0]
    n_heads = a_ref.shape[0]
    c = DN_CHUNK

    @pl.when(pl.program_id(1) == 0)
    def _():
        st_ref[...] = jnp.zeros_like(st_ref)

    gain = gain_ref[...]
    lane = lax.broadcasted_iota(jnp.int32, (1, LANES), 1)

    def scan_chunk(i, carry):
        start = pl.multiple_of(i * c, c)
        rows = pl.ds(start, c)
        last = auxc_ref[pl.ds(start + c - 1, 1), :]
        heads = range(n_heads)
        cols = [slice(hh * DN_DIM, (hh + 1) * DN_DIM) for hh in heads]
        state = [st_ref[hh] for hh in heads]
        sb = [s.astype(BF16) for s in state]
        ws = [jnp.dot(w_ref[rows, cl], s, preferred_element_type=F32) for cl, s in zip(cols, sb)]
        qs = [jnp.dot(qe_ref[rows, cl], s, preferred_element_type=F32) for cl, s in zip(cols, sb)]
        vb = [(u_ref[rows, cl].astype(F32) - x).astype(BF16) for cl, x in zip(cols, ws)]
        kv = [lax.dot_general(kd_ref[rows, cl], x, (((0,), (0,)), ((), ())), preferred_element_type=F32)
              for cl, x in zip(cols, vb)]
        av = [jnp.dot(a_ref[hh, rows, :], x, preferred_element_type=F32) for hh, x in zip(heads, vb)]
        for hh in heads:
            glast = jnp.sum(jnp.where(lane == AUX_G + hh, last, 0.0), axis=1, keepdims=True)
            st_ref[hh] = state[hh] * jnp.exp(glast) + kv[hh]
            z = z_ref[rows, cols[hh]].astype(F32)
            o_ref[rows, cols[hh]] = (_rms(qs[hh] + av[hh], gain) * _silu(z)).astype(o_ref.dtype)
        return carry

    lax.fori_loop(0, seq // c, scan_chunk, 0)


def _delta(main3, auxc3, auxr, conv_w_t, gain):
    bsz, seq, _ = main3.shape
    hb = DN_WIDTH // DN_DIM
    ts = min(DELTA_TILE, seq)
    halo = 16
    tiles = seq // ts

    def col(offset):
        return pl.BlockSpec((None, ts, DN_DIM), lambda b, h, i: (b, i, offset * hb + h))

    def halo_col(offset):
        return pl.BlockSpec((None, halo, DN_DIM),
                            lambda b, h, i: (b, jnp.maximum(i * (ts // halo) - 1, 0), offset * hb + h))

    def wcol(offset):
        return pl.BlockSpec((DN_CONV, DN_DIM), lambda b, h, i: (0, offset * hb + h))

    tok = pl.BlockSpec((None, ts, DN_DIM), lambda b, h, i: (b, i, h))
    tok_shape = jax.ShapeDtypeStruct((bsz, seq, DN_WIDTH), BF16)
    u, w, qe, kd, a = pl.pallas_call(
        _delta_prep_kernel,
        grid=(bsz, DN_HEADS, tiles),
        in_specs=[col(0), col(1), col(2), halo_col(0), halo_col(1), halo_col(2),
                  pl.BlockSpec((None, ts, LANES), lambda b, h, i: (b, i, 0)),
                  pl.BlockSpec((LANES, ts), lambda b, h, i: (0, b * tiles + i)),
                  wcol(0), wcol(1), wcol(2)],
        out_specs=[tok, tok, tok, tok,
                   pl.BlockSpec((None, None, ts, DN_CHUNK), lambda b, h, i: (b, h, i, 0))],
        out_shape=[tok_shape, tok_shape, tok_shape, tok_shape,
                   jax.ShapeDtypeStruct((bsz, DN_HEADS, seq, DN_CHUNK), BF16)],
        compiler_params=_params(3),
        name="delta_prep",
    )(main3, main3, main3, main3, main3, main3, auxc3, auxr, conv_w_t, conv_w_t, conv_w_t)

    tsc = min(DELTA_SCAN_TILE, seq)
    wide = pl.BlockSpec((None, tsc, DN_WIDTH), lambda b, i: (b, i, 0))
    return pl.pallas_call(
        _delta_scan_kernel,
        grid=(bsz, seq // tsc),
        in_specs=[wide, wide, wide, wide,
                  pl.BlockSpec((None, DN_HEADS, tsc, DN_CHUNK), lambda b, i: (b, 0, i, 0)),
                  pl.BlockSpec((None, tsc, DN_WIDTH), lambda b, i: (b, i, 3)),
                  pl.BlockSpec((None, tsc, LANES), lambda b, i: (b, i, 0)),
                  pl.BlockSpec((1, DN_DIM), lambda b, i: (0, 0))],
        out_specs=wide,
        out_shape=tok_shape,
        scratch_shapes=[pltpu.VMEM((DN_HEADS, DN_DIM, DN_DIM), F32)],
        compiler_params=_params(1, 1),
        name="delta_scan",
    )(u, w, qe, kd, a, main3, auxc3, gain)


def _mla_proj_kernel(cq_ref, ckv_ref, auxc_ref, pos_ref, qg_ref, kvg_ref, wq_ref, wkv_ref, rope_ref,
                     q_ref, k_ref, v_ref):
    scale = (MLA_NOPE + MLA_ROPE) ** -0.5
    ang = pos_ref[...].astype(F32) * rope_ref[0:1, :]
    cos_t = jnp.cos(ang) * rope_ref[1:2, :]
    sin_t = jnp.sin(ang) * rope_ref[2:3, :]
    lane = lax.broadcasted_iota(jnp.int32, ang.shape, 1)
    first = lane < MLA_ROPE // 2

    def rope(a):
        swapped = jnp.where(first, pltpu.roll(a, LANES - MLA_ROPE // 2, 1), pltpu.roll(a, MLA_ROPE // 2, 1))
        return a * cos_t + swapped * sin_t

    ql = _bdot(_rms(cq_ref[...].astype(F32), qg_ref[...]), wq_ref[...]) * scale
    kv = _bdot(_rms(ckv_ref[...].astype(F32), kvg_ref[...]), wkv_ref[...])
    kpe = rope(auxc_ref[...]).astype(k_ref.dtype)
    for h in range(MLA_HEADS):
        o = h * MLA_QK_PAD
        q_ref[:, o:o + MLA_NOPE] = ql[:, o:o + MLA_NOPE].astype(q_ref.dtype)
        q_ref[:, o + MLA_NOPE:o + MLA_QK_PAD] = rope(ql[:, o + MLA_NOPE:o + MLA_QK_PAD]).astype(q_ref.dtype)
        k_ref[:, o:o + MLA_NOPE] = kv[:, h * MLA_NOPE:(h + 1) * MLA_NOPE].astype(k_ref.dtype)
        k_ref[:, o + MLA_NOPE:o + MLA_QK_PAD] = kpe
    v_ref[...] = kv[:, MLA_HEADS * MLA_NOPE:].T.astype(v_ref.dtype)


def _mla_proj(main, auxc, pos, q_gain, kv_gain, wq, wkv, rope_tab, seq):
    t = main.shape[0]
    tm = min(512, seq)
    per_b = seq // tm
    cq_blk = (4 * DN_WIDTH) // MLA_Q_RANK
    hq = MLA_HEADS * MLA_QK_PAD
    hv = MLA_HEADS * MLA_V
    return pl.pallas_call(
        _mla_proj_kernel,
        grid=(t // tm,),
        in_specs=[pl.BlockSpec((tm, MLA_Q_RANK), lambda i: (i, cq_blk)),
                  pl.BlockSpec((tm, MLA_KV_RANK), lambda i: (i, cq_blk + 1)),
                  pl.BlockSpec((tm, LANES), lambda i: (i, 0)),
                  pl.BlockSpec((tm, 1), lambda i: (i, 0)),
                  pl.BlockSpec((1, MLA_Q_RANK), lambda i: (0, 0)),
                  pl.BlockSpec((1, MLA_KV_RANK), lambda i: (0, 0)),
                  pl.BlockSpec((MLA_Q_RANK, hq), lambda i: (0, 0)),
                  pl.BlockSpec((MLA_KV_RANK, 2 * hv), lambda i: (0, 0)),
                  pl.BlockSpec((8, LANES), lambda i: (0, 0))],
        out_specs=[pl.BlockSpec((tm, hq), lambda i: (i, 0)),
                   pl.BlockSpec((tm, hq), lambda i: (i, 0)),
                   pl.BlockSpec((None, hv, tm), lambda i: (i // per_b, 0, i % per_b))],
        out_shape=[jax.ShapeDtypeStruct((t, hq), BF16),
                   jax.ShapeDtypeStruct((t, hq), BF16),
                   jax.ShapeDtypeStruct((t // seq, hv, seq), BF16)],
        compiler_params=_params(1),
        name="mla_proj",
    )(main, main, auxc, pos, q_gain, kv_gain, wq, wkv, rope_tab)


ATTN_PARTS = 2


def _attn_kernel(q_ref, k_ref, vt_ref, o_ref, m_ref, l_ref, acc_ref):
    qi = pl.program_id(2)
    n_parts = m_ref.shape[0]
    tk = m_ref.shape[2]
    q = [q_ref[p * tk:(p + 1) * tk, :] for p in range(n_parts)]
    m_ref[...] = jnp.full_like(m_ref, -jnp.inf)
    l_ref[...] = jnp.zeros_like(l_ref)
    acc_ref[...] = jnp.zeros_like(acc_ref)
    key = lax.broadcasted_iota(jnp.int32, (tk, tk), 0)
    query = lax.broadcasted_iota(jnp.int32, (tk, tk), 1)
    visible = key <= query

    def step(j, parts, masked_part):
        start = pl.multiple_of(j * tk, tk)
        kb = k_ref[pl.ds(start, tk), :]
        vtb = vt_ref[:, pl.ds(start, tk)]
        s = [lax.dot_general(kb, q[p], (((1,), (1,)), ((), ())), preferred_element_type=F32) for p in parts]
        s = [jnp.where(visible, x, -jnp.inf) if p == masked_part else x for p, x in zip(parts, s)]
        m_old = [m_ref[p] for p in parts]
        m_new = [jnp.maximum(mo, jnp.max(x, axis=0, keepdims=True)) for mo, x in zip(m_old, s)]
        e = [jnp.exp(x - mn) for x, mn in zip(s, m_new)]
        pv = [jnp.dot(vtb, x.astype(BF16), preferred_element_type=F32) for x in e]
        for i, p in enumerate(parts):
            alpha = jnp.exp(m_old[i] - m_new[i])
            l_ref[p] = alpha * l_ref[p] + jnp.sum(e[i], axis=0, keepdims=True)
            acc_ref[p] = alpha * acc_ref[p] + pv[i]
            m_ref[p] = m_new[i]

    every = tuple(range(n_parts))

    def body(j, carry):
        step(j, every, None)
        return carry

    lax.fori_loop(0, qi * n_parts, body, 0)
    for d in range(n_parts):
        step(qi * n_parts + d, every[d:], d)
    for p in every:
        o_ref[p * tk:(p + 1) * tk, :] = (acc_ref[p] / l_ref[p]).T.astype(o_ref.dtype)


def _attention(q3, k3, vt3):
    bsz, seq, _ = q3.shape
    tk = min(512, seq)
    n_parts = min(ATTN_PARTS, seq // tk)
    tq = tk * n_parts
    return pl.pallas_call(
        _attn_kernel,
        grid=(bsz, MLA_HEADS, seq // tq),
        in_specs=[pl.BlockSpec((None, tq, MLA_QK_PAD), lambda b, h, i: (b, i, h)),
                  pl.BlockSpec((None, seq, MLA_QK_PAD), lambda b, h, i: (b, 0, h)),
                  pl.BlockSpec((None, MLA_V, seq), lambda b, h, i: (b, h, 0))],
        out_specs=pl.BlockSpec((None, tq, MLA_V), lambda b, h, i: (b, i, h)),
        out_shape=jax.ShapeDtypeStruct((bsz, seq, MLA_HEADS * MLA_V), BF16),
        scratch_shapes=[pltpu.VMEM((n_parts, 1, tk), F32), pltpu.VMEM((n_parts, 1, tk), F32),
                        pltpu.VMEM((n_parts, MLA_V, tk), F32)],
        compiler_params=_params(3),
        name="attention",
    )(q3, k3, vt3)


def _out_proj_kernel(x_ref, dn_ref, mla_ref, ada_ref, gain_ref, w_ref, wr_ref, x1_ref, h2_ref, lg_ref):
    ada = ada_ref[0]
    half = dn_ref.shape[1]
    mix = (jnp.dot(dn_ref[...], w_ref[:half, :], preferred_element_type=F32)
           + jnp.dot(mla_ref[...], w_ref[half:, :], preferred_element_type=F32))
    x1 = x_ref[...] + ada[2:3, :] * mix
    x1_ref[...] = x1
    h2 = _rms(x1, gain_ref[...]) * (1.0 + ada[4:5, :]) + ada[3:4, :]
    h2_ref[...] = _pack_bf16_pairs(h2)
    h_hi = h2.astype(BF16)
    h_lo = (h2 - h_hi.astype(F32)).astype(BF16)
    both = jnp.dot(h_hi, wr_ref[...], preferred_element_type=F32)
    logits = both[:, :LANES] + both[:, LANES:] + jnp.dot(h_lo, wr_ref[:, :LANES], preferred_element_type=F32)
    lg_ref[...] = logits.T


def _out_proj(x2, dn, mla, ada3, gain, w_out, w_router, seq):
    t, d = x2.shape
    tm = min(256, seq)
    per_b = seq // tm
    half = dn.shape[1]
    return pl.pallas_call(
        _out_proj_kernel,
        grid=(t // tm,),
        in_specs=[pl.BlockSpec((tm, d), lambda i: (i, 0)),
                  pl.BlockSpec((tm, half), lambda i: (i, 0)),
                  pl.BlockSpec((tm, half), lambda i: (i, 0)),
                  pl.BlockSpec((1, 6, d), lambda i: (i // per_b, 0, 0)),
                  pl.BlockSpec((1, d), lambda i: (0, 0)),
                  pl.BlockSpec((2 * half, d), lambda i: (0, 0)),
                  pl.BlockSpec((d, 2 * LANES), lambda i: (0, 0))],
        out_specs=[pl.BlockSpec((tm, d), lambda i: (i, 0)),
                   pl.BlockSpec((tm, d // 2), lambda i: (i, 0)),
                   pl.BlockSpec((LANES, tm), lambda i: (0, i))],
        out_shape=[jax.ShapeDtypeStruct((t, d), F32),
                   jax.ShapeDtypeStruct((t, d // 2), jnp.uint32),
                   jax.ShapeDtypeStruct((LANES, t), F32)],
        compiler_params=_params(1),
        name="out_proj",
    )(x2, dn, mla, ada3, gain, w_out, w_router)


def _expert_kernel(be_ref, nv_ref, x_ref, wgu_hbm, wd_hbm, y_ref, gu_f, d_f, gu_s, d_s, slot_ref, sem):
    i = pl.program_id(0)
    n_blocks = pl.num_programs(0)
    active = nv_ref[i] > 0
    first = active & ((i == 0) | (be_ref[i] != be_ref[jnp.maximum(i - 1, 0)]))

    def fetch(e, slot):
        return (pltpu.make_async_copy(wgu_hbm.at[e], gu_f.at[slot], sem.at[slot, 0]),
                pltpu.make_async_copy(wd_hbm.at[e], d_f.at[slot], sem.at[slot, 1]))

    @pl.when(i == 0)
    def _():
        slot_ref[0] = 0
        for cp in fetch(be_ref[0], 0):
            cp.start()

    @pl.when(first)
    def _():
        cur = slot_ref[0]
        for cp in fetch(be_ref[i], cur):
            cp.wait()
        nxt = lax.while_loop(lambda j: (j < n_blocks) & (be_ref[jnp.minimum(j, n_blocks - 1)] == be_ref[i]),
                             lambda j: j + 1, i + 1)
        nxt_c = jnp.minimum(nxt, n_blocks - 1)

        @pl.when((nxt < n_blocks) & (nv_ref[nxt_c] > 0))
        def _():
            for cp in fetch(be_ref[nxt_c], 1 - cur):
                cp.start()

        gu_s[...] = gu_f[cur].astype(BF16)
        d_s[...] = d_f[cur].astype(BF16)
        slot_ref[0] = 1 - cur

    @pl.when(active)
    def _():
        x = _unpack_bf16_pairs(x_ref[...]).astype(BF16)
        ff = d_s.shape[0]
        chunks = [slice(c0, c0 + MXU_WIDTH) for c0 in range(0, ff, MXU_WIDTH)]
        gate_up = [(jnp.dot(x, gu_s[:, cs], preferred_element_type=F32),
                    jnp.dot(x, gu_s[:, slice(ff + cs.start, ff + cs.stop)], preferred_element_type=F32))
                   for cs in chunks]
        act = [(_silu(g) * u).astype(BF16) for g, u in gate_up]
        y = jnp.dot(act[0], d_s[chunks[0], :], preferred_element_type=F32)
        for a, cs in zip(act[1:], chunks[1:]):
            y = y + jnp.dot(a, d_s[cs, :], preferred_element_type=F32)
        row = lax.broadcasted_iota(jnp.int32, y.shape, 0)
        y_ref[...] = _pack_bf16_pairs(jnp.where(row < nv_ref[i], y, 0.0))

    @pl.when(jnp.logical_not(active))
    def _():
        y_ref[...] = jnp.zeros_like(y_ref)


def _experts(block_e, n_valid, x_sorted, w_gu, w_d):
    n_rows, dp = x_sorted.shape
    ff2 = w_gu.shape[2]
    n_blocks = n_rows // EXPERT_BLOCK
    grid_spec = pltpu.PrefetchScalarGridSpec(
        num_scalar_prefetch=2,
        grid=(n_blocks,),
        in_specs=[pl.BlockSpec((EXPERT_BLOCK, dp), lambda i, be, nv: (i, 0)),
                  pl.BlockSpec(memory_space=pl.ANY),
                  pl.BlockSpec(memory_space=pl.ANY)],
        out_specs=pl.BlockSpec((EXPERT_BLOCK, dp), lambda i, be, nv: (i, 0)),
        scratch_shapes=[pltpu.VMEM((2, 2 * dp, ff2), F32), pltpu.VMEM((2, ff2 // 2, 2 * dp), F32),
                        pltpu.VMEM((2 * dp, ff2), BF16), pltpu.VMEM((ff2 // 2, 2 * dp), BF16),
                        pltpu.SMEM((1,), jnp.int32), pltpu.SemaphoreType.DMA((2, 2))],
    )
    return pl.pallas_call(
        _expert_kernel,
        grid_spec=grid_spec,
        out_shape=jax.ShapeDtypeStruct((n_rows, dp), jnp.uint32),
        compiler_params=_params(0, 1),
        name="experts",
    )(block_e, n_valid, x_sorted, w_gu, w_d)


def _final_kernel(x1_ref, h2_ref, y_ref, wc_ref, ada_ref, wgu_ref, wd_ref, gain_ref, o_ref):
    ada = ada_ref[0]
    h2 = _unpack_bf16_pairs(h2_ref[...]).astype(BF16)
    gu = jnp.dot(h2, wgu_ref[...], preferred_element_type=F32)
    ff = gu.shape[1] // 2
    act = _silu(gu[:, :ff]) * gu[:, ff:]
    ffn = jnp.dot(act.astype(BF16), wd_ref[...], preferred_element_type=F32)
    wc = wc_ref[...]
    for kk in range(TOP_K):
        ffn = ffn + wc[:, kk:kk + 1] * _unpack_bf16_pairs(y_ref[kk])
    x2 = x1_ref[...] + ada[5:6, :] * ffn
    o_ref[...] = _rms(x2, gain_ref[...])


def _final(x1, h2, y_tok, wc, ada3, w_gu, w_d, gain, seq):
    t, d = x1.shape
    tm = min(256, seq)
    per_b = seq // tm
    ff2 = w_gu.shape[1]
    return pl.pallas_call(
        _final_kernel,
        grid=(t // tm,),
        in_specs=[pl.BlockSpec((tm, d), lambda i: (i, 0)),
                  pl.BlockSpec((tm, d // 2), lambda i: (i, 0)),
                  pl.BlockSpec((TOP_K, tm, d // 2), lambda i: (0, i, 0)),
                  pl.BlockSpec((tm, LANES), lambda i: (i, 0)),
                  pl.BlockSpec((1, 6, d), lambda i: (i // per_b, 0, 0)),
                  pl.BlockSpec((d, ff2), lambda i: (0, 0)),
                  pl.BlockSpec((ff2 // 2, d), lambda i: (0, 0)),
                  pl.BlockSpec((1, d), lambda i: (0, 0))],
        out_specs=pl.BlockSpec((tm, d), lambda i: (i, 0)),
        out_shape=jax.ShapeDtypeStruct((t, d), F32),
        compiler_params=_params(1),
        name="final",
    )(x1, h2, y_tok, wc, ada3, w_gu, w_d, gain)


ROUTE_TILE = 512
MAX_BLOCK_LANES = 512


def _first_argmax(x, idx_f, n):
    m = jnp.max(x, axis=0, keepdims=True)
    first = jnp.min(jnp.where(x == m, idx_f, float(n)), axis=0, keepdims=True)
    return m, first


def _route_kernel(lg_ref, bias_ref, ek_ref, rk_ref, wk_ref, wc_ref, ps_ref, be_ref, carry_ref, upper_ref):
    i = pl.program_id(0)
    tm = lg_ref.shape[1]
    per_group = N_EXPERTS // N_GROUPS

    @pl.when(i == 0)
    def _():
        carry_ref[...] = jnp.zeros_like(carry_ref)
        r = lax.broadcasted_iota(jnp.int32, (tm, tm), 0)
        cc = lax.broadcasted_iota(jnp.int32, (tm, tm), 1)
        upper_ref[...] = (r < cc).astype(BF16)

    scores = jax.nn.sigmoid(lg_ref[0:N_EXPERTS, :])
    biased = scores + bias_ref[:, 0:1]
    sub8 = lax.broadcasted_iota(jnp.int32, (per_group, tm), 0).astype(F32)
    group_rows = []
    for g in range(N_GROUPS):
        xg = biased[g * per_group:(g + 1) * per_group, :]
        m1, i1 = _first_argmax(xg, sub8, per_group)
        m2 = jnp.max(jnp.where(sub8 == i1, -jnp.inf, xg), axis=0, keepdims=True)
        group_rows.append(m1 + m2)
    cur = jnp.concatenate(group_rows, axis=0)
    gself = jnp.zeros(cur.shape, F32)
    for _ in range(TOPK_GROUPS):
        _, gi = _first_argmax(cur, sub8, N_GROUPS)
        hit = sub8 == gi
        gself = jnp.where(hit, 1.0, gself)
        cur = jnp.where(hit, -jnp.inf, cur)
    masked = jnp.concatenate(
        [jnp.where(gself[g:g + 1, :] > 0.5, biased[g * per_group:(g + 1) * per_group, :], -jnp.inf)
         for g in range(N_GROUPS)], axis=0)
    sub64 = lax.broadcasted_iota(jnp.int32, (N_EXPERTS, tm), 0).astype(F32)
    e_rows, s_rows = [], []
    sel = jnp.zeros(masked.shape, F32)
    for _ in range(TOP_K):
        _, ei = _first_argmax(masked, sub64, N_EXPERTS)
        hit = sub64 == ei
        e_rows.append(ei)
        s_rows.append(jnp.sum(jnp.where(hit, scores, 0.0), axis=0, keepdims=True))
        sel = jnp.where(hit, 1.0, sel)
        masked = jnp.where(hit, -jnp.inf, masked)
    total = s_rows[0]
    for s in s_rows[1:]:
        total = total + s
    w_rows = [s / total * ROUTED_SCALE for s in s_rows]
    before = jnp.dot(sel.astype(BF16), upper_ref[...], preferred_element_type=F32) + carry_ref[:, 0:1]
    r_rows = [jnp.sum(jnp.where(sub64 == ei, before, 0.0), axis=0, keepdims=True) for ei in e_rows]
    zrow = jnp.zeros((8 - TOP_K, tm), F32)
    ek_ref[...] = jnp.concatenate(e_rows + [zrow], axis=0).astype(jnp.int32)
    rk_ref[...] = jnp.concatenate(r_rows + [zrow], axis=0).astype(jnp.int32)
    wk = jnp.concatenate(w_rows + [zrow], axis=0)
    wk_ref[...] = wk
    wc_ref[...] = jnp.concatenate([wk, jnp.zeros((LANES - 8, tm), F32)], axis=0).T
    carry_ref[...] = carry_ref[...] + jnp.sum(sel, axis=1, keepdims=True)

    @pl.when(i == pl.num_programs(0) - 1)
    def _():
        counts = carry_ref[...]
        padded = jnp.floor((counts + (EXPERT_BLOCK - 1.0)) * (1.0 / EXPERT_BLOCK)) * EXPERT_BLOCK
        r = lax.broadcasted_iota(jnp.int32, (N_EXPERTS, N_EXPERTS), 0)
        cc = lax.broadcasted_iota(jnp.int32, (N_EXPERTS, N_EXPERTS), 1)
        pad_end = jnp.dot((cc <= r).astype(F32), padded, precision=lax.Precision.HIGHEST,
                          preferred_element_type=F32)
        pad_start = pad_end - padded
        ps_ref[...] = pad_start
        blk0 = lax.broadcasted_iota(jnp.int32, (N_EXPERTS, MAX_BLOCK_LANES), 1).astype(F32) * EXPERT_BLOCK
        e_of_blk = jnp.minimum(jnp.sum((pad_end[:, 0:1] <= blk0).astype(F32), axis=0, keepdims=True),
                               N_EXPERTS - 1.0)
        sub = lax.broadcasted_iota(jnp.int32, (N_EXPERTS, MAX_BLOCK_LANES), 0).astype(F32)
        seg_end = jnp.sum(jnp.where(sub == e_of_blk, (pad_start + counts)[:, 0:1], 0.0), axis=0, keepdims=True)
        n_valid = jnp.clip(seg_end - blk0[0:1, :], 0.0, float(EXPERT_BLOCK))
        be_ref[...] = jnp.concatenate([e_of_blk, n_valid, jnp.zeros((6, MAX_BLOCK_LANES), F32)],
                                      axis=0).astype(jnp.int32)


def _route_pallas(logits_t, bias_col):
    t = logits_t.shape[1]
    tm = min(ROUTE_TILE, t)
    row8 = pl.BlockSpec((8, tm), lambda i: (0, i))
    return pl.pallas_call(
        _route_kernel,
        grid=(t // tm,),
        in_specs=[pl.BlockSpec((LANES, tm), lambda i: (0, i)),
                  pl.BlockSpec((N_EXPERTS, 1), lambda i: (0, 0))],
        out_specs=[row8, row8, row8,
                   pl.BlockSpec((tm, LANES), lambda i: (i, 0)),
                   pl.BlockSpec((N_EXPERTS, LANES), lambda i: (0, 0)),
                   pl.BlockSpec((8, MAX_BLOCK_LANES), lambda i: (0, 0))],
        out_shape=[jax.ShapeDtypeStruct((8, t), jnp.int32),
                   jax.ShapeDtypeStruct((8, t), jnp.int32),
                   jax.ShapeDtypeStruct((8, t), F32),
                   jax.ShapeDtypeStruct((t, LANES), F32),
                   jax.ShapeDtypeStruct((N_EXPERTS, LANES), F32),
                   jax.ShapeDtypeStruct((8, MAX_BLOCK_LANES), jnp.int32)],
        scratch_shapes=[pltpu.VMEM((N_EXPERTS, LANES), F32), pltpu.VMEM((tm, tm), BF16)],
        compiler_params=_params(0, 1),
        name="route",
    )(logits_t, bias_col)


def _slot_kernel(ek_ref, rk_ref, ps_ref, slot_ref):
    tm = ek_ref.shape[1]
    sub64 = lax.broadcasted_iota(jnp.int32, (N_EXPERTS, tm), 0)
    start = ps_ref[:, 0:1]
    rows = []
    for kk in range(TOP_K):
        seg = jnp.sum(jnp.where(sub64 == ek_ref[kk:kk + 1, :], start, 0.0), axis=0, keepdims=True)
        rows.append(seg.astype(jnp.int32) + rk_ref[kk:kk + 1, :])
    slot_ref[...] = jnp.concatenate(rows + [jnp.zeros((8 - TOP_K, tm), jnp.int32)], axis=0)


def _slots(ek, rk, ps):
    t = ek.shape[1]
    tm = min(ROUTE_TILE, t)
    row8 = pl.BlockSpec((8, tm), lambda i: (0, i))
    return pl.pallas_call(
        _slot_kernel,
        grid=(t // tm,),
        in_specs=[row8, row8, pl.BlockSpec((N_EXPERTS, LANES), lambda i: (0, 0))],
        out_specs=row8,
        out_shape=jax.ShapeDtypeStruct((8, t), jnp.int32),
        compiler_params=_params(1),
        name="slots",
    )(ek, rk, ps)


SC_ROWS = 32


def _sc_mesh():
    info = plsc.get_sparse_core_info()
    mesh = plsc.VectorSubcoreMesh(core_axis_name="c", subcore_axis_name="s")
    return mesh, info.num_cores, info.num_cores * info.num_subcores


def _sc_dispatch(x, slot_flat, n_rows):
    t, dp = x.shape
    n_slots = slot_flat.shape[0] // t
    mesh, n_cores, n_workers = _sc_mesh()
    per_w = t // n_workers
    n_chunks = per_w // SC_ROWS
    assert per_w * n_workers == t and n_chunks * SC_ROWS == per_w and n_chunks % 2 == 0

    @functools.partial(
        pl.kernel, out_type=jax.ShapeDtypeStruct((n_rows, dp), x.dtype), mesh=mesh,
        scratch_types=[pltpu.VMEM((2, n_slots, SC_ROWS), jnp.int32), pltpu.VMEM((2, SC_ROWS, dp), x.dtype),
                       pltpu.SemaphoreType.DMA((2,)), pltpu.SemaphoreType.DMA((2,))])
    def dispatch(x_hbm, i_hbm, o_hbm, idx_v, rows_v, read_sem, scatter_sem):
        base = (lax.axis_index("s") * n_cores + lax.axis_index("c")) * per_w

        def read(j, b):
            return pltpu.make_async_copy(x_hbm.at[pl.ds(pl.multiple_of(base + j * SC_ROWS, 8), SC_ROWS)],
                                         rows_v.at[b], read_sem.at[b])

        def scatter(k, b):
            return pltpu.make_async_copy(rows_v.at[b], o_hbm.at[idx_v.at[b, k]], scatter_sem.at[b])

        read(0, 0).start()

        @pl.loop(0, n_chunks, step=2)
        def _(j):
            for b in range(2):
                jj = j + b

                @pl.when(jj >= 1)
                def _():
                    for k in range(n_slots):
                        scatter(k, 1 - b).wait()

                @pl.when(jj + 1 < n_chunks)
                def _():
                    read(jj + 1, 1 - b).start()

                for k in range(n_slots):
                    pltpu.sync_copy(i_hbm.at[pl.ds(pl.multiple_of(k * t + base + jj * SC_ROWS, 8), SC_ROWS)],
                                    idx_v.at[b, k])
                read(jj, b).wait()
                for k in range(n_slots):
                    scatter(k, b).start()

        for k in range(n_slots):
            scatter(k, (n_chunks - 1) % 2).wait()

    return dispatch(x, slot_flat)


def _sc_combine(y, slot_flat):
    n = slot_flat.shape[0]
    dp = y.shape[1]
    mesh, n_cores, n_workers = _sc_mesh()
    per_w = n // n_workers
    n_chunks = per_w // SC_ROWS
    assert per_w * n_workers == n and n_chunks * SC_ROWS == per_w and n_chunks % 2 == 0

    @functools.partial(
        pl.kernel, out_type=jax.ShapeDtypeStruct((n, dp), y.dtype), mesh=mesh,
        scratch_types=[pltpu.VMEM((per_w,), jnp.int32), pltpu.VMEM((2, SC_ROWS, dp), y.dtype),
                       pltpu.SemaphoreType.DMA((2,)), pltpu.SemaphoreType.DMA((2,))])
    def combine(y_hbm, i_hbm, o_hbm, idx_v, rows_v, gather_sem, write_sem):
        base = (lax.axis_index("s") * n_cores + lax.axis_index("c")) * per_w
        pltpu.sync_copy(i_hbm.at[pl.ds(pl.multiple_of(base, 8), per_w)], idx_v)

        def gather(j, b):
            return pltpu.make_async_copy(y_hbm.at[idx_v.at[pl.ds(pl.multiple_of(j * SC_ROWS, 8), SC_ROWS)]],
                                         rows_v.at[b], gather_sem.at[b])

        def write(j, b):
            return pltpu.make_async_copy(rows_v.at[b],
                                         o_hbm.at[pl.ds(pl.multiple_of(base + j * SC_ROWS, 8), SC_ROWS)],
                                         write_sem.at[b])

        gather(0, 0).start()

        @pl.loop(0, n_chunks, step=2)
        def _(j):
            for b in range(2):
                jj = j + b

                @pl.when(jj >= 1)
                def _():
                    write(jj - 1, 1 - b).wait()

                @pl.when(jj + 1 < n_chunks)
                def _():
                    gather(jj + 1, 1 - b).start()

                gather(jj, b).wait()
                write(jj, b).start()

        write(n_chunks - 1, (n_chunks - 1) % 2).wait()

    return combine(y, slot_flat)


def _layer(x, cond_ada, positions, w_in, dn_conv_w, dn_a_log, dn_dt_bias, dn_norm_gain, mla_q_norm_gain,
           w_q_up, mla_kv_norm_gain, w_kv_up, w_out, norm1_gain, norm2_gain, w_router, router_bias,
           w_exp_gate_up, w_exp_down, w_sh_gate_up, w_sh_down, out_gain):
    bsz, seq, d = x.shape
    t = bsz * seq
    x2 = x.reshape(t, d)
    ada3 = cond_ada.reshape(bsz, 6, d)

    s_z = 4 * DN_WIDTH
    s_a = s_z + 2 * DN_HEADS
    s_kpe = s_a + MLA_Q_RANK + MLA_KV_RANK
    w_main = jnp.concatenate([w_in[:, :s_z], w_in[:, s_a:s_kpe]], axis=1).astype(BF16)
    w_aux = jnp.concatenate([w_in[:, s_kpe:], w_in[:, s_z:s_a],
                             jnp.zeros((d, LANES - MLA_ROPE - 2 * DN_HEADS), F32)], axis=1).astype(BF16)
    auxp = jnp.zeros((8, LANES), F32)
    auxp = auxp.at[0, AUX_G:AUX_G + DN_HEADS].set(dn_a_log).at[1, AUX_G:AUX_G + DN_HEADS].set(dn_dt_bias)
    main, auxc, auxr = _in_proj(x2, ada3, norm1_gain.reshape(1, d), w_main, w_aux, auxp, seq)

    dn = _delta(main.reshape(bsz, seq, MAIN_WIDTH), auxc.reshape(bsz, seq, LANES), auxr,
                dn_conv_w.T, dn_norm_gain.reshape(1, DN_DIM))

    qk = MLA_NOPE + MLA_ROPE
    wq3 = w_q_up.reshape(MLA_Q_RANK, MLA_HEADS, qk)
    wq = jnp.concatenate([wq3, jnp.zeros((MLA_Q_RANK, MLA_HEADS, MLA_QK_PAD - qk), F32)], axis=2)
    wq = wq.reshape(MLA_Q_RANK, MLA_HEADS * MLA_QK_PAD).astype(BF16)
    wkv3 = w_kv_up.reshape(MLA_KV_RANK, MLA_HEADS, MLA_NOPE + MLA_V)
    wkv = jnp.concatenate([wkv3[:, :, :MLA_NOPE].reshape(MLA_KV_RANK, -1),
                           wkv3[:, :, MLA_NOPE:].reshape(MLA_KV_RANK, -1)], axis=1).astype(BF16)
    half = MLA_ROPE // 2
    inv_freq = ROPE_THETA ** (-jnp.arange(half, dtype=F32) / half)
    zeros = jnp.zeros((LANES - MLA_ROPE,), F32)
    rope_tab = jnp.zeros((8, LANES), F32)
    rope_tab = rope_tab.at[0].set(jnp.concatenate([inv_freq, inv_freq, zeros]))
    rope_tab = rope_tab.at[1].set(jnp.concatenate([jnp.ones((MLA_ROPE,), F32), zeros]))
    rope_tab = rope_tab.at[2].set(jnp.concatenate([-jnp.ones((half,), F32), jnp.ones((half,), F32), zeros]))
    q, k, v = _mla_proj(main, auxc, positions.reshape(t, 1), mla_q_norm_gain.reshape(1, -1),
                        mla_kv_norm_gain.reshape(1, -1), wq, wkv, rope_tab, seq)
    mla = _attention(q.reshape(bsz, seq, -1), k.reshape(bsz, seq, -1), v)

    w_r = jnp.concatenate([w_router, jnp.zeros((d, LANES - N_EXPERTS), F32)], axis=1)
    w_r_hi = w_r.astype(BF16)
    w_r = jnp.concatenate([w_r_hi, (w_r - w_r_hi.astype(F32)).astype(BF16)], axis=1)
    x1, h2, logits_t = _out_proj(x2, dn.reshape(t, DN_WIDTH), mla.reshape(t, -1), ada3,
                                 norm2_gain.reshape(1, d), w_out.astype(BF16), w_r, seq)

    n_rows = -(-(t * TOP_K + N_EXPERTS * (EXPERT_BLOCK - 1)) // EXPERT_BLOCK) * EXPERT_BLOCK
    n_blocks = n_rows // EXPERT_BLOCK
    assert n_blocks <= MAX_BLOCK_LANES
    ek, rk, _, wc, seg_start, blocks = _route_pallas(logits_t, router_bias.reshape(N_EXPERTS, 1))
    slot_flat = _slots(ek, rk, seg_start)[:TOP_K].reshape(-1)
    x_sorted = _sc_dispatch(h2, slot_flat, n_rows)
    y_sorted = _experts(blocks[0, :n_blocks], blocks[1, :n_blocks], x_sorted, w_exp_gate_up, w_exp_down)
    y_tok = _sc_combine(y_sorted, slot_flat).reshape(TOP_K, t, d // 2)

    out = _final(x1, h2, y_tok, wc, ada3, w_sh_gate_up.astype(BF16), w_sh_down.astype(BF16),
                 out_gain.reshape(1, d), seq)
    return out.reshape(bsz, seq, d)


def kernel(x, c, positions, w_ada, b_ada, norm1_gain, w_in, dn_conv_w, dn_a_log, dn_dt_bias, dn_norm_gain,
           mla_q_norm_gain, w_q_up, mla_kv_norm_gain, w_kv_up, w_out, norm2_gain, w_router, router_bias,
           w_exp_gate_up, w_exp_down, w_sh_gate_up, w_sh_down, final_norm_gain):
    depth = w_ada.shape[0]
    assert depth == 1, "the final RMSNorm is fused into the single layer's last kernel"
    ada = _ada(c, w_ada[0], b_ada[0])
    return _layer(x, ada, positions, w_in[0], dn_conv_w[0], dn_a_log[0], dn_dt_bias[0], dn_norm_gain[0],
                  mla_q_norm_gain[0], w_q_up[0], mla_kv_norm_gain[0], w_kv_up[0], w_out[0], norm1_gain[0],
                  norm2_gain[0], w_router[0], router_bias[0], w_exp_gate_up[0], w_exp_down[0],
                  w_sh_gate_up[0], w_sh_down[0], final_norm_gain)
```

```python
import functools

import jax
import jax.numpy as jnp
from jax import lax
from jax.experimental import pallas as pl
from jax.experimental.pallas import tpu as pltpu
from jax.experimental.pallas import tpu_sc as plsc

F32 = jnp.float32
BF16 = jnp.bfloat16

DN_HEADS = 8
DN_DIM = 128
DN_WIDTH = DN_HEADS * DN_DIM
DN_CONV = 4
DN_CHUNK = 64
MLA_HEADS = 8
MLA_Q_RANK = 512
MLA_KV_RANK = 512
MLA_NOPE = 128
MLA_ROPE = 64
MLA_V = 128
MLA_QK_PAD = 256
ROPE_THETA = 10000.0
N_EXPERTS = 64
TOP_K = 6
N_GROUPS = 8
TOPK_GROUPS = 4
EXPERT_FF = 512
ROUTED_SCALE = 2.5
EXPERT_BLOCK = 256
RMS_EPS = 1e-6
L2_EPS = 1e-6

LANES = 128
MXU_WIDTH = 256
MAIN_WIDTH = 4 * DN_WIDTH + MLA_Q_RANK + MLA_KV_RANK
AUX_BETA = MLA_ROPE
AUX_G = MLA_ROPE + DN_HEADS
VMEM_LIMIT = 56 * 1024 * 1024


def _params(n_parallel, n_arbitrary=0):
    sem = ("parallel",) * n_parallel + ("arbitrary",) * n_arbitrary
    return pltpu.CompilerParams(dimension_semantics=sem, vmem_limit_bytes=VMEM_LIMIT)


def _silu(x):
    return x * jax.nn.sigmoid(x)


def _bdot(a, b):
    return jnp.dot(a.astype(BF16), b.astype(BF16), preferred_element_type=F32)


def _bdot_nt(a, b):
    return lax.dot_general(a.astype(BF16), b.astype(BF16), (((1,), (1,)), ((), ())),
                           preferred_element_type=F32)


def _bdot_tn(a, b):
    return lax.dot_general(a.astype(BF16), b.astype(BF16), (((0,), (0,)), ((), ())),
                           preferred_element_type=F32)


def _rms(x, gain):
    return x * lax.rsqrt(jnp.mean(x * x, axis=-1, keepdims=True) + RMS_EPS) * gain


HIGH_HALF = 0xFFFF0000


def _pack_bf16_pairs(x):
    n = x.shape[1] // 2
    bits = lax.bitcast_convert_type(x.astype(BF16).astype(F32), jnp.uint32)
    return (bits[:, :n] >> 16) | (bits[:, n:] & jnp.uint32(HIGH_HALF))


def _unpack_bf16_pairs(p):
    lo = lax.bitcast_convert_type(p << 16, F32)
    hi = lax.bitcast_convert_type(p & jnp.uint32(HIGH_HALF), F32)
    return jnp.concatenate([lo, hi], axis=1)


def _ada_kernel(c_ref, w_ref, b_ref, o_ref):
    o_ref[...] = _bdot(_silu(c_ref[...]), w_ref[...]) + b_ref[...]


def _ada(c, w_ada, b_ada):
    bsz, d = c.shape
    n = w_ada.shape[1]
    tn = 1024
    return pl.pallas_call(
        _ada_kernel,
        grid=(n // tn,),
        in_specs=[pl.BlockSpec((bsz, d), lambda j: (0, 0)),
                  pl.BlockSpec((d, tn), lambda j: (0, j)),
                  pl.BlockSpec((1, tn), lambda j: (0, j))],
        out_specs=pl.BlockSpec((bsz, tn), lambda j: (0, j)),
        out_shape=jax.ShapeDtypeStruct((bsz, n), F32),
        compiler_params=_params(1),
        name="ada",
    )(c, w_ada, b_ada.reshape(1, n))


def _in_proj_kernel(x_ref, ada_ref, gain_ref, wm_ref, wa_ref, auxp_ref,
                    main_ref, auxc_ref, auxr_ref, h_ref):
    j = pl.program_id(1)

    @pl.when(j == 0)
    def _():
        ada = ada_ref[0]
        h = _rms(x_ref[...], gain_ref[...]) * (1.0 + ada[1:2, :]) + ada[0:1, :]
        hb = h.astype(BF16)
        h_ref[...] = hb
        aux = jnp.dot(hb, wa_ref[...], preferred_element_type=F32)
        tm = aux.shape[0]
        lane = lax.broadcasted_iota(jnp.int32, aux.shape, 1)
        is_beta = (lane >= AUX_BETA) & (lane < AUX_G)
        is_g = (lane >= AUX_G) & (lane < AUX_G + DN_HEADS)
        a_log = auxp_ref[0:1, :]
        dt_bias = auxp_ref[1:2, :]
        sp_in = aux + dt_bias
        softplus = jnp.maximum(sp_in, 0.0) + jnp.log(1.0 + jnp.exp(-jnp.abs(sp_in)))
        g = -jnp.exp(a_log) * softplus
        aux = jnp.where(is_beta, jax.nn.sigmoid(aux), jnp.where(is_g, g, aux))
        r = lax.broadcasted_iota(jnp.int32, (LANES, LANES), 0)
        cidx = lax.broadcasted_iota(jnp.int32, (LANES, LANES), 1)
        tri = ((cidx <= r) & (cidx // DN_CHUNK == r // DN_CHUNK)).astype(F32)
        g_cols = (cidx >= AUX_G) & (cidx < AUX_G + DN_HEADS)
        parts = []
        for t in range(tm // LANES):
            blk = aux[t * LANES:(t + 1) * LANES, :]
            cs = jnp.dot(tri, blk, precision=lax.Precision.HIGHEST, preferred_element_type=F32)
            parts.append(jnp.where(g_cols, cs, blk))
        aux = jnp.concatenate(parts, axis=0)
        auxc_ref[...] = aux
        auxr_ref[...] = aux.T

    main_ref[...] = jnp.dot(h_ref[...], wm_ref[...], preferred_element_type=F32).astype(main_ref.dtype)


def _in_proj(x2, ada3, gain, w_main, w_aux, auxp, seq):
    t, d = x2.shape
    n = w_main.shape[1]
    tm = min(1024, seq)
    tn = 1024
    per_b = seq // tm
    return pl.pallas_call(
        _in_proj_kernel,
        grid=(t // tm, n // tn),
        in_specs=[pl.BlockSpec((tm, d), lambda i, j: (i, 0)),
                  pl.BlockSpec((1, 6, d), lambda i, j: (i // per_b, 0, 0)),
                  pl.BlockSpec((1, d), lambda i, j: (0, 0)),
                  pl.BlockSpec((d, tn), lambda i, j: (0, j)),
                  pl.BlockSpec((d, LANES), lambda i, j: (0, 0)),
                  pl.BlockSpec((8, LANES), lambda i, j: (0, 0))],
        out_specs=[pl.BlockSpec((tm, tn), lambda i, j: (i, j)),
                   pl.BlockSpec((tm, LANES), lambda i, j: (i, 0)),
                   pl.BlockSpec((LANES, tm), lambda i, j: (0, i))],
        out_shape=[jax.ShapeDtypeStruct((t, n), BF16),
                   jax.ShapeDtypeStruct((t, LANES), F32),
                   jax.ShapeDtypeStruct((LANES, t), F32)],
        scratch_shapes=[pltpu.VMEM((tm, d), BF16)],
        compiler_params=_params(1, 1),
        name="in_proj",
    )(x2, ada3, gain, w_main, w_aux, auxp)


def _unit_lower_inverses(ms):
    c = ms[0].shape[0]
    r = lax.broadcasted_iota(jnp.int32, (c, c), 0)
    cc = lax.broadcasted_iota(jnp.int32, (c, c), 1)
    eye = (r == cc).astype(F32)
    same = (r // 16) == (cc // 16)
    md = [jnp.where(same, m, 0.0) for m in ms]
    mo = [jnp.where(same, 0.0, m) for m in ms]
    p1 = [_bdot(a, a) for a in md]
    p2 = [_bdot(a, a) for a in p1]
    p3 = [_bdot(a, a) for a in p2]
    td = [eye - a for a in md]
    td = [t + _bdot(t, p) for t, p in zip(td, p1)]
    td = [t + _bdot(t, p) for t, p in zip(td, p2)]
    td = [t + _bdot(t, p) for t, p in zip(td, p3)]
    n1 = [_bdot(t, o) for t, o in zip(td, mo)]
    n2 = [_bdot(n, n) for n in n1]
    left = [(eye - a) + _bdot(eye - a, b) for a, b in zip(n1, n2)]
    return [_bdot(l, t) for l, t in zip(left, td)]


DELTA_TILE = 2048
DELTA_SCAN_TILE = 512


def _delta_prep_kernel(q_ref, k_ref, v_ref, qh_ref, kh_ref, vh_ref, auxc_ref, auxr_ref, wq_ref, wk_ref, wv_ref,
                       u_ref, w_ref, qe_ref, kd_ref, a_ref):
    head = pl.program_id(1)
    first_tile = pl.program_id(2) == 0
    ts = q_ref.shape[0]
    c = DN_CHUNK
    halo = qh_ref.shape[0]

    def conv_silu(x_ref, h_ref, w_ref):
        prev = jnp.where(first_tile, 0.0, h_ref[...].astype(F32))
        x = jnp.concatenate([prev, x_ref[...].astype(F32)], axis=0)
        w = w_ref[...]
        acc = x * w[DN_CONV - 1:DN_CONV, :]
        for s in range(1, DN_CONV):
            acc = acc + pltpu.roll(x, s, 0) * w[DN_CONV - 1 - s:DN_CONV - s, :]
        return _silu(acc[halo:, :])

    q = conv_silu(q_ref, qh_ref, wq_ref)
    k = conv_silu(k_ref, kh_ref, wk_ref)
    v = conv_silu(v_ref, vh_ref, wv_ref)
    q = q * lax.rsqrt(jnp.sum(q * q, axis=-1, keepdims=True) + L2_EPS) * (DN_DIM ** -0.5)
    k = k * lax.rsqrt(jnp.sum(k * k, axis=-1, keepdims=True) + L2_EPS)

    auxc = auxc_ref[...]
    lane = lax.broadcasted_iota(jnp.int32, auxc.shape, 1)
    beta_all = jnp.sum(jnp.where(lane == AUX_BETA + head, auxc, 0.0), axis=1, keepdims=True)
    gcol_all = jnp.sum(jnp.where(lane == AUX_G + head, auxc, 0.0), axis=1, keepdims=True)
    grows = auxr_ref[AUX_G:AUX_G + DN_HEADS, :]
    head_row = lax.broadcasted_iota(jnp.int32, grows.shape, 0)
    grow_all = jnp.sum(jnp.where(head_row == head, grows, 0.0), axis=0, keepdims=True)

    ri = lax.broadcasted_iota(jnp.int32, (c, c), 0)
    ci = lax.broadcasted_iota(jnp.int32, (c, c), 1)
    causal = ri >= ci
    strict = ri > ci
    chunks = [slice(j * c, (j + 1) * c) for j in range(ts // c)]
    decay = [jnp.exp(jnp.where(causal, gcol_all[rows] - grow_all[:, rows], -jnp.inf)) for rows in chunks]
    kb = [k[rows] * beta_all[rows] for rows in chunks]
    both = [_bdot_nt(jnp.concatenate([kb_j, q[rows]], axis=0), k[rows])
            for kb_j, rows in zip(kb, chunks)]
    tinv = _unit_lower_inverses([jnp.where(strict, b[:c] * d, 0.0) for b, d in zip(both, decay)])
    eg = [jnp.exp(gcol_all[rows]) for rows in chunks]
    sol = [_bdot(t, jnp.concatenate([v[rows] * beta_all[rows], kb_j * e], axis=1))
           for t, rows, kb_j, e in zip(tinv, chunks, kb, eg)]
    for j, rows in enumerate(chunks):
        gcol = gcol_all[rows]
        u_ref[rows, :] = sol[j][:, :DN_DIM].astype(u_ref.dtype)
        w_ref[rows, :] = sol[j][:, DN_DIM:].astype(w_ref.dtype)
        qe_ref[rows, :] = (q[rows] * eg[j]).astype(qe_ref.dtype)
        kd_ref[rows, :] = (k[rows] * jnp.exp(gcol[c - 1:c, :] - gcol)).astype(kd_ref.dtype)
        a_ref[rows, :] = jnp.where(causal, both[j][c:] * decay[j], 0.0).astype(a_ref.dtype)


def _delta_scan_kernel(u_ref, w_ref, qe_ref, kd_ref, a_ref, z_ref, auxc_ref, gain_ref, o_ref, st_ref):
    seq = u_ref.shape[0]
    n_heads = a_ref.shape[0]
    c = DN_CHUNK

    @pl.when(pl.program_id(1) == 0)
    def _():
        st_ref[...] = jnp.zeros_like(st_ref)

    gain = gain_ref[...]
    lane = lax.broadcasted_iota(jnp.int32, (1, LANES), 1)

    def scan_chunk(i, carry):
        start = pl.multiple_of(i * c, c)
        rows = pl.ds(start, c)
        last = auxc_ref[pl.ds(start + c - 1, 1), :]
        heads = range(n_heads)
        cols = [slice(hh * DN_DIM, (hh + 1) * DN_DIM) for hh in heads]
        state = [st_ref[hh] for hh in heads]
        sb = [s.astype(BF16) for s in state]
        ws = [jnp.dot(w_ref[rows, cl], s, preferred_element_type=F32) for cl, s in zip(cols, sb)]
        qs = [jnp.dot(qe_ref[rows, cl], s, preferred_element_type=F32) for cl, s in zip(cols, sb)]
        vb = [(u_ref[rows, cl].astype(F32) - x).astype(BF16) for cl, x in zip(cols, ws)]
        kv = [lax.dot_general(kd_ref[rows, cl], x, (((0,), (0,)), ((), ())), preferred_element_type=F32)
              for cl, x in zip(cols, vb)]
        av = [jnp.dot(a_ref[hh, rows, :], x, preferred_element_type=F32) for hh, x in zip(heads, vb)]
        for hh in heads:
            glast = jnp.sum(jnp.where(lane == AUX_G + hh, last, 0.0), axis=1, keepdims=True)
            st_ref[hh] = state[hh] * jnp.exp(glast) + kv[hh]
            z = z_ref[rows, cols[hh]].astype(F32)
            o_ref[rows, cols[hh]] = (_rms(qs[hh] + av[hh], gain) * _silu(z)).astype(o_ref.dtype)
        return carry

    lax.fori_loop(0, seq // c, scan_chunk, 0)


def _delta(main3, auxc3, auxr, conv_w_t, gain):
    bsz, seq, _ = main3.shape
    hb = DN_WIDTH // DN_DIM
    ts = min(DELTA_TILE, seq)
    halo = 16
    tiles = seq // ts

    def col(offset):
        return pl.BlockSpec((None, ts, DN_DIM), lambda b, h, i: (b, i, offset * hb + h))

    def halo_col(offset):
        return pl.BlockSpec((None, halo, DN_DIM),
                            lambda b, h, i: (b, jnp.maximum(i * (ts // halo) - 1, 0), offset * hb + h))

    def wcol(offset):
        return pl.BlockSpec((DN_CONV, DN_DIM), lambda b, h, i: (0, offset * hb + h))

    tok = pl.BlockSpec((None, ts, DN_DIM), lambda b, h, i: (b, i, h))
    tok_shape = jax.ShapeDtypeStruct((bsz, seq, DN_WIDTH), BF16)
    u, w, qe, kd, a = pl.pallas_call(
        _delta_prep_kernel,
        grid=(bsz, DN_HEADS, tiles),
        in_specs=[col(0), col(1), col(2), halo_col(0), halo_col(1), halo_col(2),
                  pl.BlockSpec((None, ts, LANES), lambda b, h, i: (b, i, 0)),
                  pl.BlockSpec((LANES, ts), lambda b, h, i: (0, b * tiles + i)),
                  wcol(0), wcol(1), wcol(2)],
        out_specs=[tok, tok, tok, tok,
                   pl.BlockSpec((None, None, ts, DN_CHUNK), lambda b, h, i: (b, h, i, 0))],
        out_shape=[tok_shape, tok_shape, tok_shape, tok_shape,
                   jax.ShapeDtypeStruct((bsz, DN_HEADS, seq, DN_CHUNK), BF16)],
        compiler_params=_params(3),
        name="delta_prep",
    )(main3, main3, main3, main3, main3, main3, auxc3, auxr, conv_w_t, conv_w_t, conv_w_t)

    tsc = min(DELTA_SCAN_TILE, seq)
    wide = pl.BlockSpec((None, tsc, DN_WIDTH), lambda b, i: (b, i, 0))
    return pl.pallas_call(
        _delta_scan_kernel,
        grid=(bsz, seq // tsc),
        in_specs=[wide, wide, wide, wide,
                  pl.BlockSpec((None, DN_HEADS, tsc, DN_CHUNK), lambda b, i: (b, 0, i, 0)),
                  pl.BlockSpec((None, tsc, DN_WIDTH), lambda b, i: (b, i, 3)),
                  pl.BlockSpec((None, tsc, LANES), lambda b, i: (b, i, 0)),
                  pl.BlockSpec((1, DN_DIM), lambda b, i: (0, 0))],
        out_specs=wide,
        out_shape=tok_shape,
        scratch_shapes=[pltpu.VMEM((DN_HEADS, DN_DIM, DN_DIM), F32)],
        compiler_params=_params(1, 1),
        name="delta_scan",
    )(u, w, qe, kd, a, main3, auxc3, gain)


def _mla_proj_kernel(cq_ref, ckv_ref, auxc_ref, pos_ref, qg_ref, kvg_ref, wq_ref, wkv_ref, rope_ref,
                     q_ref, k_ref, v_ref):
    scale = (MLA_NOPE + MLA_ROPE) ** -0.5
    ang = pos_ref[...].astype(F32) * rope_ref[0:1, :]
    cos_t = jnp.cos(ang) * rope_ref[1:2, :]
    sin_t = jnp.sin(ang) * rope_ref[2:3, :]
    lane = lax.broadcasted_iota(jnp.int32, ang.shape, 1)
    first = lane < MLA_ROPE // 2

    def rope(a):
        swapped = jnp.where(first, pltpu.roll(a, LANES - MLA_ROPE // 2, 1), pltpu.roll(a, MLA_ROPE // 2, 1))
        return a * cos_t + swapped * sin_t

    ql = _bdot(_rms(cq_ref[...].astype(F32), qg_ref[...]), wq_ref[...]) * scale
    kv = _bdot(_rms(ckv_ref[...].astype(F32), kvg_ref[...]), wkv_ref[...])
    kpe = rope(auxc_ref[...]).astype(k_ref.dtype)
    for h in range(MLA_HEADS):
        o = h * MLA_QK_PAD
        q_ref[:, o:o + MLA_NOPE] = ql[:, o:o + MLA_NOPE].astype(q_ref.dtype)
        q_ref[:, o + MLA_NOPE:o + MLA_QK_PAD] = rope(ql[:, o + MLA_NOPE:o + MLA_QK_PAD]).astype(q_ref.dtype)
        k_ref[:, o:o + MLA_NOPE] = kv[:, h * MLA_NOPE:(h + 1) * MLA_NOPE].astype(k_ref.dtype)
        k_ref[:, o + MLA_NOPE:o + MLA_QK_PAD] = kpe
    v_ref[...] = kv[:, MLA_HEADS * MLA_NOPE:].T.astype(v_ref.dtype)


def _mla_proj(main, auxc, pos, q_gain, kv_gain, wq, wkv, rope_tab, seq):
    t = main.shape[0]
    tm = min(512, seq)
    per_b = seq // tm
    cq_blk = (4 * DN_WIDTH) // MLA_Q_RANK
    hq = MLA_HEADS * MLA_QK_PAD
    hv = MLA_HEADS * MLA_V
    return pl.pallas_call(
        _mla_proj_kernel,
        grid=(t // tm,),
        in_specs=[pl.BlockSpec((tm, MLA_Q_RANK), lambda i: (i, cq_blk)),
                  pl.BlockSpec((tm, MLA_KV_RANK), lambda i: (i, cq_blk + 1)),
                  pl.BlockSpec((tm, LANES), lambda i: (i, 0)),
                  pl.BlockSpec((tm, 1), lambda i: (i, 0)),
                  pl.BlockSpec((1, MLA_Q_RANK), lambda i: (0, 0)),
                  pl.BlockSpec((1, MLA_KV_RANK), lambda i: (0, 0)),
                  pl.BlockSpec((MLA_Q_RANK, hq), lambda i: (0, 0)),
                  pl.BlockSpec((MLA_KV_RANK, 2 * hv), lambda i: (0, 0)),
                  pl.BlockSpec((8, LANES), lambda i: (0, 0))],
        out_specs=[pl.BlockSpec((tm, hq), lambda i: (i, 0)),
                   pl.BlockSpec((tm, hq), lambda i: (i, 0)),
                   pl.BlockSpec((None, hv, tm), lambda i: (i // per_b, 0, i % per_b))],
        out_shape=[jax.ShapeDtypeStruct((t, hq), BF16),
                   jax.ShapeDtypeStruct((t, hq), BF16),
                   jax.ShapeDtypeStruct((t // seq, hv, seq), BF16)],
        compiler_params=_params(1),
        name="mla_proj",
    )(main, main, auxc, pos, q_gain, kv_gain, wq, wkv, rope_tab)


ATTN_PARTS = 2


def _attn_kernel(q_ref, k_ref, vt_ref, o_ref, m_ref, l_ref, acc_ref):
    qi = pl.program_id(2)
    n_parts = m_ref.shape[0]
    tk = m_ref.shape[2]
    q = [q_ref[p * tk:(p + 1) * tk, :] for p in range(n_parts)]
    m_ref[...] = jnp.full_like(m_ref, -jnp.inf)
    l_ref[...] = jnp.zeros_like(l_ref)
    acc_ref[...] = jnp.zeros_like(acc_ref)
    key = lax.broadcasted_iota(jnp.int32, (tk, tk), 0)
    query = lax.broadcasted_iota(jnp.int32, (tk, tk), 1)
    visible = key <= query

    def step(j, parts, masked_part):
        start = pl.multiple_of(j * tk, tk)
        kb = k_ref[pl.ds(start, tk), :]
        vtb = vt_ref[:, pl.ds(start, tk)]
        s = [lax.dot_general(kb, q[p], (((1,), (1,)), ((), ())), preferred_element_type=F32) for p in parts]
        s = [jnp.where(visible, x, -jnp.inf) if p == masked_part else x for p, x in zip(parts, s)]
        m_old = [m_ref[p] for p in parts]
        m_new = [jnp.maximum(mo, jnp.max(x, axis=0, keepdims=True)) for mo, x in zip(m_old, s)]
        e = [jnp.exp(x - mn) for x, mn in zip(s, m_new)]
        pv = [jnp.dot(vtb, x.astype(BF16), preferred_element_type=F32) for x in e]
        for i, p in enumerate(parts):
            alpha = jnp.exp(m_old[i] - m_new[i])
            l_ref[p] = alpha * l_ref[p] + jnp.sum(e[i], axis=0, keepdims=True)
            acc_ref[p] = alpha * acc_ref[p] + pv[i]
            m_ref[p] = m_new[i]

    every = tuple(range(n_parts))

    def body(j, carry):
        step(j, every, None)
        return carry

    lax.fori_loop(0, qi * n_parts, body, 0)
    for d in range(n_parts):
        step(qi * n_parts + d, every[d:], d)
    for p in every:
        o_ref[p * tk:(p + 1) * tk, :] = (acc_ref[p] / l_ref[p]).T.astype(o_ref.dtype)


def _attention(q3, k3, vt3):
    bsz, seq, _ = q3.shape
    tk = min(512, seq)
    n_parts = min(ATTN_PARTS, seq // tk)
    tq = tk * n_parts
    return pl.pallas_call(
        _attn_kernel,
        grid=(bsz, MLA_HEADS, seq // tq),
        in_specs=[pl.BlockSpec((None, tq, MLA_QK_PAD), lambda b, h, i: (b, i, h)),
                  pl.BlockSpec((None, seq, MLA_QK_PAD), lambda b, h, i: (b, 0, h)),
                  pl.BlockSpec((None, MLA_V, seq), lambda b, h, i: (b, h, 0))],
        out_specs=pl.BlockSpec((None, tq, MLA_V), lambda b, h, i: (b, i, h)),
        out_shape=jax.ShapeDtypeStruct((bsz, seq, MLA_HEADS * MLA_V), BF16),
        scratch_shapes=[pltpu.VMEM((n_parts, 1, tk), F32), pltpu.VMEM((n_parts, 1, tk), F32),
                        pltpu.VMEM((n_parts, MLA_V, tk), F32)],
        compiler_params=_params(3),
        name="attention",
    )(q3, k3, vt3)


def _out_proj_kernel(x_ref, dn_ref, mla_ref, ada_ref, gain_ref, w_ref, wr_ref, x1_ref, h2_ref, lg_ref):
    ada = ada_ref[0]
    half = dn_ref.shape[1]
    mix = (jnp.dot(dn_ref[...], w_ref[:half, :], preferred_element_type=F32)
           + jnp.dot(mla_ref[...], w_ref[half:, :], preferred_element_type=F32))
    x1 = x_ref[...] + ada[2:3, :] * mix
    x1_ref[...] = x1
    h2 = _rms(x1, gain_ref[...]) * (1.0 + ada[4:5, :]) + ada[3:4, :]
    h2_ref[...] = _pack_bf16_pairs(h2)
    h_hi = h2.astype(BF16)
    h_lo = (h2 - h_hi.astype(F32)).astype(BF16)
    both = jnp.dot(h_hi, wr_ref[...], preferred_element_type=F32)
    logits = both[:, :LANES] + both[:, LANES:] + jnp.dot(h_lo, wr_ref[:, :LANES], preferred_element_type=F32)
    lg_ref[...] = logits.T


def _out_proj(x2, dn, mla, ada3, gain, w_out, w_router, seq):
    t, d = x2.shape
    tm = min(256, seq)
    per_b = seq // tm
    half = dn.shape[1]
    return pl.pallas_call(
        _out_proj_kernel,
        grid=(t // tm,),
        in_specs=[pl.BlockSpec((tm, d), lambda i: (i, 0)),
                  pl.BlockSpec((tm, half), lambda i: (i, 0)),
                  pl.BlockSpec((tm, half), lambda i: (i, 0)),
                  pl.BlockSpec((1, 6, d), lambda i: (i // per_b, 0, 0)),
                  pl.BlockSpec((1, d), lambda i: (0, 0)),
                  pl.BlockSpec((2 * half, d), lambda i: (0, 0)),
                  pl.BlockSpec((d, 2 * LANES), lambda i: (0, 0))],
        out_specs=[pl.BlockSpec((tm, d), lambda i: (i, 0)),
                   pl.BlockSpec((tm, d // 2), lambda i: (i, 0)),
                   pl.BlockSpec((LANES, tm), lambda i: (0, i))],
        out_shape=[jax.ShapeDtypeStruct((t, d), F32),
                   jax.ShapeDtypeStruct((t, d // 2), jnp.uint32),
                   jax.ShapeDtypeStruct((LANES, t), F32)],
        compiler_params=_params(1),
        name="out_proj",
    )(x2, dn, mla, ada3, gain, w_out, w_router)


def _expert_kernel(be_ref, nv_ref, x_ref, wgu_hbm, wd_hbm, y_ref, gu_f, d_f, gu_s, d_s, slot_ref, sem):
    i = pl.program_id(0)
    n_blocks = pl.num_programs(0)
    active = nv_ref[i] > 0
    first = active & ((i == 0) | (be_ref[i] != be_ref[jnp.maximum(i - 1, 0)]))

    def fetch(e, slot):
        return (pltpu.make_async_copy(wgu_hbm.at[e], gu_f.at[slot], sem.at[slot, 0]),
                pltpu.make_async_copy(wd_hbm.at[e], d_f.at[slot], sem.at[slot, 1]))

    @pl.when(i == 0)
    def _():
        slot_ref[0] = 0
        for cp in fetch(be_ref[0], 0):
            cp.start()

    @pl.when(first)
    def _():
        cur = slot_ref[0]
        for cp in fetch(be_ref[i], cur):
            cp.wait()
        nxt = lax.while_loop(lambda j: (j < n_blocks) & (be_ref[jnp.minimum(j, n_blocks - 1)] == be_ref[i]),
                             lambda j: j + 1, i + 1)
        nxt_c = jnp.minimum(nxt, n_blocks - 1)

        @pl.when((nxt < n_blocks) & (nv_ref[nxt_c] > 0))
        def _():
            for cp in fetch(be_ref[nxt_c], 1 - cur):
                cp.start()

        gu_s[...] = gu_f[cur].astype(BF16)
        d_s[...] = d_f[cur].astype(BF16)
        slot_ref[0] = 1 - cur

    @pl.when(active)
    def _():
        x = _unpack_bf16_pairs(x_ref[...]).astype(BF16)
        ff = d_s.shape[0]
        chunks = [slice(c0, c0 + MXU_WIDTH) for c0 in range(0, ff, MXU_WIDTH)]
        gate_up = [(jnp.dot(x, gu_s[:, cs], preferred_element_type=F32),
                    jnp.dot(x, gu_s[:, slice(ff + cs.start, ff + cs.stop)], preferred_element_type=F32))
                   for cs in chunks]
        act = [(_silu(g) * u).astype(BF16) for g, u in gate_up]
        y = jnp.dot(act[0], d_s[chunks[0], :], preferred_element_type=F32)
        for a, cs in zip(act[1:], chunks[1:]):
            y = y + jnp.dot(a, d_s[cs, :], preferred_element_type=F32)
        row = lax.broadcasted_iota(jnp.int32, y.shape, 0)
        y_ref[...] = _pack_bf16_pairs(jnp.where(row < nv_ref[i], y, 0.0))

    @pl.when(jnp.logical_not(active))
    def _():
        y_ref[...] = jnp.zeros_like(y_ref)


def _experts(block_e, n_valid, x_sorted, w_gu, w_d):
    n_rows, dp = x_sorted.shape
    ff2 = w_gu.shape[2]
    n_blocks = n_rows // EXPERT_BLOCK
    grid_spec = pltpu.PrefetchScalarGridSpec(
        num_scalar_prefetch=2,
        grid=(n_blocks,),
        in_specs=[pl.BlockSpec((EXPERT_BLOCK, dp), lambda i, be, nv: (i, 0)),
                  pl.BlockSpec(memory_space=pl.ANY),
                  pl.BlockSpec(memory_space=pl.ANY)],
        out_specs=pl.BlockSpec((EXPERT_BLOCK, dp), lambda i, be, nv: (i, 0)),
        scratch_shapes=[pltpu.VMEM((2, 2 * dp, ff2), F32), pltpu.VMEM((2, ff2 // 2, 2 * dp), F32),
                        pltpu.VMEM((2 * dp, ff2), BF16), pltpu.VMEM((ff2 // 2, 2 * dp), BF16),
                        pltpu.SMEM((1,), jnp.int32), pltpu.SemaphoreType.DMA((2, 2))],
    )
    return pl.pallas_call(
        _expert_kernel,
        grid_spec=grid_spec,
        out_shape=jax.ShapeDtypeStruct((n_rows, dp), jnp.uint32),
        compiler_params=_params(0, 1),
        name="experts",
    )(block_e, n_valid, x_sorted, w_gu, w_d)


def _shared_ffn_kernel(h2_ref, wgu_ref, wd_ref, o_ref):
    h2 = _unpack_bf16_pairs(h2_ref[...]).astype(BF16)
    gu = jnp.dot(h2, wgu_ref[...], preferred_element_type=F32)
    ff = gu.shape[1] // 2
    act = _silu(gu[:, :ff]) * gu[:, ff:]
    o_ref[...] = jnp.dot(act.astype(BF16), wd_ref[...], preferred_element_type=F32).astype(o_ref.dtype)


def _shared_ffn(h2, w_gu, w_d):
    t, dp = h2.shape
    d = 2 * dp
    tm = min(512, t)
    ff2 = w_gu.shape[1]
    return pl.pallas_call(
        _shared_ffn_kernel,
        grid=(t // tm,),
        in_specs=[pl.BlockSpec((tm, dp), lambda i: (i, 0)),
                  pl.BlockSpec((d, ff2), lambda i: (0, 0)),
                  pl.BlockSpec((ff2 // 2, d), lambda i: (0, 0))],
        out_specs=pl.BlockSpec((tm, d), lambda i: (i, 0)),
        out_shape=jax.ShapeDtypeStruct((t, d), BF16),
        compiler_params=_params(1),
        name="shared_ffn",
    )(h2, w_gu, w_d)


def _final_kernel(x1_ref, sh_ref, y_ref, wc_ref, ada_ref, gain_ref, *rest):
    o_ref = rest[-1]
    ada = ada_ref[0]
    ffn = sh_ref[...].astype(F32)
    wc = wc_ref[...]
    for kk in range(TOP_K):
        ffn = ffn + wc[:, kk:kk + 1] * _unpack_bf16_pairs(y_ref[kk])
    x2 = x1_ref[...] + ada[5:6, :] * ffn
    o_ref[...] = _rms(x2, gain_ref[...])


def _final_part(x1, shared, y_part, wc, ada3, gain, seq, part, n_parts, prev_out):
    t, d = x1.shape
    tm = min(256, seq)
    per_b = seq // tm
    steps = t // tm // n_parts
    off = part * steps
    in_specs = [pl.BlockSpec((tm, d), lambda i: (i + off, 0)),
                pl.BlockSpec((tm, d), lambda i: (i + off, 0)),
                pl.BlockSpec((TOP_K, tm, d // 2), lambda i: (0, i, 0)),
                pl.BlockSpec((tm, LANES), lambda i: (i + off, 0)),
                pl.BlockSpec((1, 6, d), lambda i: ((i + off) // per_b, 0, 0)),
                pl.BlockSpec((1, d), lambda i: (0, 0))]
    args = [x1, shared, y_part, wc, ada3, gain]
    aliases = {}
    if prev_out is not None:
        in_specs.append(pl.BlockSpec(memory_space=pl.ANY))
        args.append(prev_out)
        aliases = {len(args) - 1: 0}
    return pl.pallas_call(
        _final_kernel,
        grid=(steps,),
        in_specs=in_specs,
        out_specs=pl.BlockSpec((tm, d), lambda i: (i + off, 0)),
        out_shape=jax.ShapeDtypeStruct((t, d), F32),
        input_output_aliases=aliases,
        compiler_params=_params(1),
        name="final",
    )(*args)


ROUTE_TILE = 512
MAX_BLOCK_LANES = 512


def _first_argmax(x, idx_f, n):
    m = jnp.max(x, axis=0, keepdims=True)
    first = jnp.min(jnp.where(x == m, idx_f, float(n)), axis=0, keepdims=True)
    return m, first


def _route_kernel(lg_ref, bias_ref, ek_ref, rk_ref, wk_ref, wc_ref, ps_ref, be_ref, carry_ref, upper_ref):
    i = pl.program_id(0)
    tm = lg_ref.shape[1]
    per_group = N_EXPERTS // N_GROUPS

    @pl.when(i == 0)
    def _():
        carry_ref[...] = jnp.zeros_like(carry_ref)
        r = lax.broadcasted_iota(jnp.int32, (tm, tm), 0)
        cc = lax.broadcasted_iota(jnp.int32, (tm, tm), 1)
        upper_ref[...] = (r < cc).astype(BF16)

    scores = jax.nn.sigmoid(lg_ref[0:N_EXPERTS, :])
    biased = scores + bias_ref[:, 0:1]
    sub8 = lax.broadcasted_iota(jnp.int32, (per_group, tm), 0).astype(F32)
    group_rows = []
    for g in range(N_GROUPS):
        xg = biased[g * per_group:(g + 1) * per_group, :]
        m1, i1 = _first_argmax(xg, sub8, per_group)
        m2 = jnp.max(jnp.where(sub8 == i1, -jnp.inf, xg), axis=0, keepdims=True)
        group_rows.append(m1 + m2)
    cur = jnp.concatenate(group_rows, axis=0)
    gself = jnp.zeros(cur.shape, F32)
    for _ in range(TOPK_GROUPS):
        _, gi = _first_argmax(cur, sub8, N_GROUPS)
        hit = sub8 == gi
        gself = jnp.where(hit, 1.0, gself)
        cur = jnp.where(hit, -jnp.inf, cur)
    masked = jnp.concatenate(
        [jnp.where(gself[g:g + 1, :] > 0.5, biased[g * per_group:(g + 1) * per_group, :], -jnp.inf)
         for g in range(N_GROUPS)], axis=0)
    sub64 = lax.broadcasted_iota(jnp.int32, (N_EXPERTS, tm), 0).astype(F32)
    e_rows, s_rows = [], []
    sel = jnp.zeros(masked.shape, F32)
    for _ in range(TOP_K):
        _, ei = _first_argmax(masked, sub64, N_EXPERTS)
        hit = sub64 == ei
        e_rows.append(ei)
        s_rows.append(jnp.sum(jnp.where(hit, scores, 0.0), axis=0, keepdims=True))
        sel = jnp.where(hit, 1.0, sel)
        masked = jnp.where(hit, -jnp.inf, masked)
    total = s_rows[0]
    for s in s_rows[1:]:
        total = total + s
    w_rows = [s / total * ROUTED_SCALE for s in s_rows]
    before = jnp.dot(sel.astype(BF16), upper_ref[...], preferred_element_type=F32) + carry_ref[:, 0:1]
    r_rows = [jnp.sum(jnp.where(sub64 == ei, before, 0.0), axis=0, keepdims=True) for ei in e_rows]
    zrow = jnp.zeros((8 - TOP_K, tm), F32)
    ek_ref[...] = jnp.concatenate(e_rows + [zrow], axis=0).astype(jnp.int32)
    rk_ref[...] = jnp.concatenate(r_rows + [zrow], axis=0).astype(jnp.int32)
    wk = jnp.concatenate(w_rows + [zrow], axis=0)
    wk_ref[...] = wk
    wc_ref[...] = jnp.concatenate([wk, jnp.zeros((LANES - 8, tm), F32)], axis=0).T
    carry_ref[...] = carry_ref[...] + jnp.sum(sel, axis=1, keepdims=True)

    @pl.when(i == pl.num_programs(0) - 1)
    def _():
        counts = carry_ref[...]
        padded = jnp.floor((counts + (EXPERT_BLOCK - 1.0)) * (1.0 / EXPERT_BLOCK)) * EXPERT_BLOCK
        r = lax.broadcasted_iota(jnp.int32, (N_EXPERTS, N_EXPERTS), 0)
        cc = lax.broadcasted_iota(jnp.int32, (N_EXPERTS, N_EXPERTS), 1)
        pad_end = jnp.dot((cc <= r).astype(F32), padded, precision=lax.Precision.HIGHEST,
                          preferred_element_type=F32)
        pad_start = pad_end - padded
        ps_ref[...] = pad_start
        blk0 = lax.broadcasted_iota(jnp.int32, (N_EXPERTS, MAX_BLOCK_LANES), 1).astype(F32) * EXPERT_BLOCK
        e_of_blk = jnp.minimum(jnp.sum((pad_end[:, 0:1] <= blk0).astype(F32), axis=0, keepdims=True),
                               N_EXPERTS - 1.0)
        sub = lax.broadcasted_iota(jnp.int32, (N_EXPERTS, MAX_BLOCK_LANES), 0).astype(F32)
        seg_end = jnp.sum(jnp.where(sub == e_of_blk, (pad_start + counts)[:, 0:1], 0.0), axis=0, keepdims=True)
        n_valid = jnp.clip(seg_end - blk0[0:1, :], 0.0, float(EXPERT_BLOCK))
        be_ref[...] = jnp.concatenate([e_of_blk, n_valid, jnp.zeros((6, MAX_BLOCK_LANES), F32)],
                                      axis=0).astype(jnp.int32)


def _route_pallas(logits_t, bias_col):
    t = logits_t.shape[1]
    tm = min(ROUTE_TILE, t)
    row8 = pl.BlockSpec((8, tm), lambda i: (0, i))
    return pl.pallas_call(
        _route_kernel,
        grid=(t // tm,),
        in_specs=[pl.BlockSpec((LANES, tm), lambda i: (0, i)),
                  pl.BlockSpec((N_EXPERTS, 1), lambda i: (0, 0))],
        out_specs=[row8, row8, row8,
                   pl.BlockSpec((tm, LANES), lambda i: (i, 0)),
                   pl.BlockSpec((N_EXPERTS, LANES), lambda i: (0, 0)),
                   pl.BlockSpec((8, MAX_BLOCK_LANES), lambda i: (0, 0))],
        out_shape=[jax.ShapeDtypeStruct((8, t), jnp.int32),
                   jax.ShapeDtypeStruct((8, t), jnp.int32),
                   jax.ShapeDtypeStruct((8, t), F32),
                   jax.ShapeDtypeStruct((t, LANES), F32),
                   jax.ShapeDtypeStruct((N_EXPERTS, LANES), F32),
                   jax.ShapeDtypeStruct((8, MAX_BLOCK_LANES), jnp.int32)],
        scratch_shapes=[pltpu.VMEM((N_EXPERTS, LANES), F32), pltpu.VMEM((tm, tm), BF16)],
        compiler_params=_params(0, 1),
        name="route",
    )(logits_t, bias_col)


def _slot_kernel(ek_ref, rk_ref, ps_ref, slot_ref):
    tm = ek_ref.shape[1]
    sub64 = lax.broadcasted_iota(jnp.int32, (N_EXPERTS, tm), 0)
    start = ps_ref[:, 0:1]
    rows = []
    for kk in range(TOP_K):
        seg = jnp.sum(jnp.where(sub64 == ek_ref[kk:kk + 1, :], start, 0.0), axis=0, keepdims=True)
        rows.append(seg.astype(jnp.int32) + rk_ref[kk:kk + 1, :])
    slot_ref[...] = jnp.concatenate(rows + [jnp.zeros((8 - TOP_K, tm), jnp.int32)], axis=0)


def _slots(ek, rk, ps):
    t = ek.shape[1]
    tm = min(ROUTE_TILE, t)
    row8 = pl.BlockSpec((8, tm), lambda i: (0, i))
    return pl.pallas_call(
        _slot_kernel,
        grid=(t // tm,),
        in_specs=[row8, row8, pl.BlockSpec((N_EXPERTS, LANES), lambda i: (0, 0))],
        out_specs=row8,
        out_shape=jax.ShapeDtypeStruct((8, t), jnp.int32),
        compiler_params=_params(1),
        name="slots",
    )(ek, rk, ps)


SC_ROWS = 32
FINAL_PARTS = 2


def _sc_mesh():
    info = plsc.get_sparse_core_info()
    mesh = plsc.VectorSubcoreMesh(core_axis_name="c", subcore_axis_name="s")
    return mesh, info.num_cores, info.num_cores * info.num_subcores


def _sc_dispatch(x, slot_flat, n_rows):
    t, dp = x.shape
    n_slots = slot_flat.shape[0] // t
    mesh, n_cores, n_workers = _sc_mesh()
    per_w = t // n_workers
    n_chunks = per_w // SC_ROWS
    assert per_w * n_workers == t and n_chunks * SC_ROWS == per_w and n_chunks % 2 == 0

    @functools.partial(
        pl.kernel, out_type=jax.ShapeDtypeStruct((n_rows, dp), x.dtype), mesh=mesh,
        scratch_types=[pltpu.VMEM((2, n_slots, SC_ROWS), jnp.int32), pltpu.VMEM((2, SC_ROWS, dp), x.dtype),
                       pltpu.SemaphoreType.DMA((2,)), pltpu.SemaphoreType.DMA((2,))])
    def dispatch(x_hbm, i_hbm, o_hbm, idx_v, rows_v, read_sem, scatter_sem):
        base = (lax.axis_index("s") * n_cores + lax.axis_index("c")) * per_w

        def read(j, b):
            return pltpu.make_async_copy(x_hbm.at[pl.ds(pl.multiple_of(base + j * SC_ROWS, 8), SC_ROWS)],
                                         rows_v.at[b], read_sem.at[b])

        def scatter(k, b):
            return pltpu.make_async_copy(rows_v.at[b], o_hbm.at[idx_v.at[b, k]], scatter_sem.at[b])

        read(0, 0).start()

        @pl.loop(0, n_chunks, step=2)
        def _(j):
            for b in range(2):
                jj = j + b

                @pl.when(jj >= 1)
                def _():
                    for k in range(n_slots):
                        scatter(k, 1 - b).wait()

                @pl.when(jj + 1 < n_chunks)
                def _():
                    read(jj + 1, 1 - b).start()

                for k in range(n_slots):
                    pltpu.sync_copy(i_hbm.at[pl.ds(pl.multiple_of(k * t + base + jj * SC_ROWS, 8), SC_ROWS)],
                                    idx_v.at[b, k])
                read(jj, b).wait()
                for k in range(n_slots):
                    scatter(k, b).start()

        for k in range(n_slots):
            scatter(k, (n_chunks - 1) % 2).wait()

    return dispatch(x, slot_flat)


def _sc_combine(y, slot_flat):
    n = slot_flat.shape[0]
    dp = y.shape[1]
    mesh, n_cores, n_workers = _sc_mesh()
    per_w = n // n_workers
    n_chunks = per_w // SC_ROWS
    assert per_w * n_workers == n and n_chunks * SC_ROWS == per_w and n_chunks % 2 == 0

    @functools.partial(
        pl.kernel, out_type=jax.ShapeDtypeStruct((n, dp), y.dtype), mesh=mesh,
        scratch_types=[pltpu.VMEM((per_w,), jnp.int32), pltpu.VMEM((2, SC_ROWS, dp), y.dtype),
                       pltpu.SemaphoreType.DMA((2,)), pltpu.SemaphoreType.DMA((2,))])
    def combine(y_hbm, i_hbm, o_hbm, idx_v, rows_v, gather_sem, write_sem):
        base = (lax.axis_index("s") * n_cores + lax.axis_index("c")) * per_w
        pltpu.sync_copy(i_hbm.at[pl.ds(pl.multiple_of(base, 8), per_w)], idx_v)

        def gather(j, b):
            return pltpu.make_async_copy(y_hbm.at[idx_v.at[pl.ds(pl.multiple_of(j * SC_ROWS, 8), SC_ROWS)]],
                                         rows_v.at[b], gather_sem.at[b])

        def write(j, b):
            return pltpu.make_async_copy(rows_v.at[b],
                                         o_hbm.at[pl.ds(pl.multiple_of(base + j * SC_ROWS, 8), SC_ROWS)],
                                         write_sem.at[b])

        gather(0, 0).start()

        @pl.loop(0, n_chunks, step=2)
        def _(j):
            for b in range(2):
                jj = j + b

                @pl.when(jj >= 1)
                def _():
                    write(jj - 1, 1 - b).wait()

                @pl.when(jj + 1 < n_chunks)
                def _():
                    gather(jj + 1, 1 - b).start()

                gather(jj, b).wait()
                write(jj, b).start()

        write(n_chunks - 1, (n_chunks - 1) % 2).wait()

    return combine(y, slot_flat)


def _layer(x, cond_ada, positions, w_in, dn_conv_w, dn_a_log, dn_dt_bias, dn_norm_gain, mla_q_norm_gain,
           w_q_up, mla_kv_norm_gain, w_kv_up, w_out, norm1_gain, norm2_gain, w_router, router_bias,
           w_exp_gate_up, w_exp_down, w_sh_gate_up, w_sh_down, out_gain):
    bsz, seq, d = x.shape
    t = bsz * seq
    x2 = x.reshape(t, d)
    ada3 = cond_ada.reshape(bsz, 6, d)

    s_z = 4 * DN_WIDTH
    s_a = s_z + 2 * DN_HEADS
    s_kpe = s_a + MLA_Q_RANK + MLA_KV_RANK
    w_main = jnp.concatenate([w_in[:, :s_z], w_in[:, s_a:s_kpe]], axis=1).astype(BF16)
    w_aux = jnp.concatenate([w_in[:, s_kpe:], w_in[:, s_z:s_a],
                             jnp.zeros((d, LANES - MLA_ROPE - 2 * DN_HEADS), F32)], axis=1).astype(BF16)
    auxp = jnp.zeros((8, LANES), F32)
    auxp = auxp.at[0, AUX_G:AUX_G + DN_HEADS].set(dn_a_log).at[1, AUX_G:AUX_G + DN_HEADS].set(dn_dt_bias)
    main, auxc, auxr = _in_proj(x2, ada3, norm1_gain.reshape(1, d), w_main, w_aux, auxp, seq)

    dn = _delta(main.reshape(bsz, seq, MAIN_WIDTH), auxc.reshape(bsz, seq, LANES), auxr,
                dn_conv_w.T, dn_norm_gain.reshape(1, DN_DIM))

    qk = MLA_NOPE + MLA_ROPE
    wq3 = w_q_up.reshape(MLA_Q_RANK, MLA_HEADS, qk)
    wq = jnp.concatenate([wq3, jnp.zeros((MLA_Q_RANK, MLA_HEADS, MLA_QK_PAD - qk), F32)], axis=2)
    wq = wq.reshape(MLA_Q_RANK, MLA_HEADS * MLA_QK_PAD).astype(BF16)
    wkv3 = w_kv_up.reshape(MLA_KV_RANK, MLA_HEADS, MLA_NOPE + MLA_V)
    wkv = jnp.concatenate([wkv3[:, :, :MLA_NOPE].reshape(MLA_KV_RANK, -1),
                           wkv3[:, :, MLA_NOPE:].reshape(MLA_KV_RANK, -1)], axis=1).astype(BF16)
    half = MLA_ROPE // 2
    inv_freq = ROPE_THETA ** (-jnp.arange(half, dtype=F32) / half)
    zeros = jnp.zeros((LANES - MLA_ROPE,), F32)
    rope_tab = jnp.zeros((8, LANES), F32)
    rope_tab = rope_tab.at[0].set(jnp.concatenate([inv_freq, inv_freq, zeros]))
    rope_tab = rope_tab.at[1].set(jnp.concatenate([jnp.ones((MLA_ROPE,), F32), zeros]))
    rope_tab = rope_tab.at[2].set(jnp.concatenate([-jnp.ones((half,), F32), jnp.ones((half,), F32), zeros]))
    q, k, v = _mla_proj(main, auxc, positions.reshape(t, 1), mla_q_norm_gain.reshape(1, -1),
                        mla_kv_norm_gain.reshape(1, -1), wq, wkv, rope_tab, seq)
    mla = _attention(q.reshape(bsz, seq, -1), k.reshape(bsz, seq, -1), v)

    w_r = jnp.concatenate([w_router, jnp.zeros((d, LANES - N_EXPERTS), F32)], axis=1)
    w_r_hi = w_r.astype(BF16)
    w_r = jnp.concatenate([w_r_hi, (w_r - w_r_hi.astype(F32)).astype(BF16)], axis=1)
    x1, h2, logits_t = _out_proj(x2, dn.reshape(t, DN_WIDTH), mla.reshape(t, -1), ada3,
                                 norm2_gain.reshape(1, d), w_out.astype(BF16), w_r, seq)

    n_rows = -(-(t * TOP_K + N_EXPERTS * (EXPERT_BLOCK - 1)) // EXPERT_BLOCK) * EXPERT_BLOCK
    n_blocks = n_rows // EXPERT_BLOCK
    assert n_blocks <= MAX_BLOCK_LANES
    ek, rk, _, wc, seg_start, blocks = _route_pallas(logits_t, router_bias.reshape(N_EXPERTS, 1))
    slot = _slots(ek, rk, seg_start)[:TOP_K]
    x_sorted = _sc_dispatch(h2, slot.reshape(-1), n_rows)
    shared = _shared_ffn(h2, w_sh_gate_up.astype(BF16), w_sh_down.astype(BF16))
    y_sorted = _experts(blocks[0, :n_blocks], blocks[1, :n_blocks], x_sorted, w_exp_gate_up, w_exp_down)

    n_parts = FINAL_PARTS if (t // min(256, seq)) % FINAL_PARTS == 0 else 1
    out = None
    for part in range(n_parts):
        part_slots = slot[:, part * (t // n_parts):(part + 1) * (t // n_parts)].reshape(-1)
        y_part = _sc_combine(y_sorted, part_slots).reshape(TOP_K, t // n_parts, d // 2)
        out = _final_part(x1, shared, y_part, wc, ada3, out_gain.reshape(1, d), seq, part, n_parts, out)
    return out.reshape(bsz, seq, d)


def kernel(x, c, positions, w_ada, b_ada, norm1_gain, w_in, dn_conv_w, dn_a_log, dn_dt_bias, dn_norm_gain,
           mla_q_norm_gain, w_q_up, mla_kv_norm_gain, w_kv_up, w_out, norm2_gain, w_router, router_bias,
           w_exp_gate_up, w_exp_down, w_sh_gate_up, w_sh_down, final_norm_gain):
    depth = w_ada.shape[0]
    assert depth == 1, "the final RMSNorm is fused into the single layer's last kernel"
    ada = _ada(c, w_ada[0], b_ada[0])
    return _layer(x, ada, positions, w_in[0], dn_conv_w[0], dn_a_log[0], dn_dt_bias[0], dn_norm_gain[0],
                  mla_q_norm_gain[0], w_q_up[0], mla_kv_norm_gain[0], w_kv_up[0], w_out[0], norm1_gain[0],
                  norm2_gain[0], w_router[0], router_bias[0], w_exp_gate_up[0], w_exp_down[0],
                  w_sh_gate_up[0], w_sh_down[0], final_norm_gain)
```

```python
import functools

import jax
import jax.numpy as jnp
from jax import lax
from jax.experimental import pallas as pl
from jax.experimental.pallas import tpu as pltpu
from jax.experimental.pallas import tpu_sc as plsc

F32 = jnp.float32
BF16 = jnp.bfloat16

DN_HEADS = 8
DN_DIM = 128
DN_WIDTH = DN_HEADS * DN_DIM
DN_CONV = 4
DN_CHUNK = 64
MLA_HEADS = 8
MLA_Q_RANK = 512
MLA_KV_RANK = 512
MLA_NOPE = 128
MLA_ROPE = 64
MLA_V = 128
MLA_QK_PAD = 256
ROPE_THETA = 10000.0
N_EXPERTS = 64
TOP_K = 6
N_GROUPS = 8
TOPK_GROUPS = 4
EXPERT_FF = 512
ROUTED_SCALE = 2.5
EXPERT_BLOCK = 512
RMS_EPS = 1e-6
L2_EPS = 1e-6

LANES = 128
MXU_WIDTH = 256
MAIN_WIDTH = 4 * DN_WIDTH + MLA_Q_RANK + MLA_KV_RANK
AUX_BETA = MLA_ROPE
AUX_G = MLA_ROPE + DN_HEADS
VMEM_LIMIT = 56 * 1024 * 1024


def _params(n_parallel, n_arbitrary=0):
    sem = ("parallel",) * n_parallel + ("arbitrary",) * n_arbitrary
    return pltpu.CompilerParams(dimension_semantics=sem, vmem_limit_bytes=VMEM_LIMIT)


def _silu(x):
    return x * jax.nn.sigmoid(x)


def _bdot(a, b):
    return jnp.dot(a.astype(BF16), b.astype(BF16), preferred_element_type=F32)


def _bdot_nt(a, b):
    return lax.dot_general(a.astype(BF16), b.astype(BF16), (((1,), (1,)), ((), ())),
                           preferred_element_type=F32)


def _bdot_tn(a, b):
    return lax.dot_general(a.astype(BF16), b.astype(BF16), (((0,), (0,)), ((), ())),
                           preferred_element_type=F32)


def _rms(x, gain):
    return x * lax.rsqrt(jnp.mean(x * x, axis=-1, keepdims=True) + RMS_EPS) * gain


HIGH_HALF = 0xFFFF0000


def _pack_bf16_pairs(x):
    n = x.shape[1] // 2
    bits = lax.bitcast_convert_type(x.astype(BF16).astype(F32), jnp.uint32)
    return (bits[:, :n] >> 16) | (bits[:, n:] & jnp.uint32(HIGH_HALF))


def _unpack_bf16_pairs(p):
    lo = lax.bitcast_convert_type(p << 16, F32)
    hi = lax.bitcast_convert_type(p & jnp.uint32(HIGH_HALF), F32)
    return jnp.concatenate([lo, hi], axis=1)


def _ada_kernel(c_ref, w_ref, b_ref, o_ref):
    o_ref[...] = _bdot(_silu(c_ref[...]), w_ref[...]) + b_ref[...]


def _ada(c, w_ada, b_ada):
    bsz, d = c.shape
    n = w_ada.shape[1]
    tn = 1024
    return pl.pallas_call(
        _ada_kernel,
        grid=(n // tn,),
        in_specs=[pl.BlockSpec((bsz, d), lambda j: (0, 0)),
                  pl.BlockSpec((d, tn), lambda j: (0, j)),
                  pl.BlockSpec((1, tn), lambda j: (0, j))],
        out_specs=pl.BlockSpec((bsz, tn), lambda j: (0, j)),
        out_shape=jax.ShapeDtypeStruct((bsz, n), F32),
        compiler_params=_params(1),
        name="ada",
    )(c, w_ada, b_ada.reshape(1, n))


def _in_proj_kernel(x_ref, ada_ref, gain_ref, wm_ref, wa_ref, auxp_ref,
                    main_ref, auxc_ref, auxr_ref, h_ref):
    j = pl.program_id(1)

    @pl.when(j == 0)
    def _():
        ada = ada_ref[0]
        h = _rms(x_ref[...], gain_ref[...]) * (1.0 + ada[1:2, :]) + ada[0:1, :]
        hb = h.astype(BF16)
        h_ref[...] = hb
        aux = jnp.dot(hb, wa_ref[...], preferred_element_type=F32)
        tm = aux.shape[0]
        lane = lax.broadcasted_iota(jnp.int32, aux.shape, 1)
        is_beta = (lane >= AUX_BETA) & (lane < AUX_G)
        is_g = (lane >= AUX_G) & (lane < AUX_G + DN_HEADS)
        a_log = auxp_ref[0:1, :]
        dt_bias = auxp_ref[1:2, :]
        sp_in = aux + dt_bias
        softplus = jnp.maximum(sp_in, 0.0) + jnp.log(1.0 + jnp.exp(-jnp.abs(sp_in)))
        g = -jnp.exp(a_log) * softplus
        aux = jnp.where(is_beta, jax.nn.sigmoid(aux), jnp.where(is_g, g, aux))
        r = lax.broadcasted_iota(jnp.int32, (LANES, LANES), 0)
        cidx = lax.broadcasted_iota(jnp.int32, (LANES, LANES), 1)
        tri = ((cidx <= r) & (cidx // DN_CHUNK == r // DN_CHUNK)).astype(F32)
        g_cols = (cidx >= AUX_G) & (cidx < AUX_G + DN_HEADS)
        parts = []
        for t in range(tm // LANES):
            blk = aux[t * LANES:(t + 1) * LANES, :]
            cs = jnp.dot(tri, blk, precision=lax.Precision.HIGHEST, preferred_element_type=F32)
            parts.append(jnp.where(g_cols, cs, blk))
        aux = jnp.concatenate(parts, axis=0)
        auxc_ref[...] = aux
        auxr_ref[...] = aux.T

    main_ref[...] = jnp.dot(h_ref[...], wm_ref[...], preferred_element_type=F32).astype(main_ref.dtype)


def _in_proj(x2, ada3, gain, w_main, w_aux, auxp, seq):
    t, d = x2.shape
    n = w_main.shape[1]
    tm = min(1024, seq)
    tn = 1024
    per_b = seq // tm
    return pl.pallas_call(
        _in_proj_kernel,
        grid=(t // tm, n // tn),
        in_specs=[pl.BlockSpec((tm, d), lambda i, j: (i, 0)),
                  pl.BlockSpec((1, 6, d), lambda i, j: (i // per_b, 0, 0)),
                  pl.BlockSpec((1, d), lambda i, j: (0, 0)),
                  pl.BlockSpec((d, tn), lambda i, j: (0, j)),
                  pl.BlockSpec((d, LANES), lambda i, j: (0, 0)),
                  pl.BlockSpec((8, LANES), lambda i, j: (0, 0))],
        out_specs=[pl.BlockSpec((tm, tn), lambda i, j: (i, j)),
                   pl.BlockSpec((tm, LANES), lambda i, j: (i, 0)),
                   pl.BlockSpec((LANES, tm), lambda i, j: (0, i))],
        out_shape=[jax.ShapeDtypeStruct((t, n), BF16),
                   jax.ShapeDtypeStruct((t, LANES), F32),
                   jax.ShapeDtypeStruct((LANES, t), F32)],
        scratch_shapes=[pltpu.VMEM((tm, d), BF16)],
        compiler_params=_params(1, 1),
        name="in_proj",
    )(x2, ada3, gain, w_main, w_aux, auxp)


def _unit_lower_inverses(ms):
    c = ms[0].shape[0]
    r = lax.broadcasted_iota(jnp.int32, (c, c), 0)
    cc = lax.broadcasted_iota(jnp.int32, (c, c), 1)
    eye = (r == cc).astype(F32)
    same = (r // 16) == (cc // 16)
    md = [jnp.where(same, m, 0.0) for m in ms]
    mo = [jnp.where(same, 0.0, m) for m in ms]
    p1 = [_bdot(a, a) for a in md]
    p2 = [_bdot(a, a) for a in p1]
    p3 = [_bdot(a, a) for a in p2]
    td = [eye - a for a in md]
    td = [t + _bdot(t, p) for t, p in zip(td, p1)]
    td = [t + _bdot(t, p) for t, p in zip(td, p2)]
    td = [t + _bdot(t, p) for t, p in zip(td, p3)]
    n1 = [_bdot(t, o) for t, o in zip(td, mo)]
    n2 = [_bdot(n, n) for n in n1]
    left = [(eye - a) + _bdot(eye - a, b) for a, b in zip(n1, n2)]
    return [_bdot(l, t) for l, t in zip(left, td)]


DELTA_TILE = 2048
DELTA_SCAN_TILE = 512


def _delta_prep_kernel(q_ref, k_ref, v_ref, qh_ref, kh_ref, vh_ref, auxc_ref, auxr_ref, wq_ref, wk_ref, wv_ref,
                       u_ref, w_ref, qe_ref, kd_ref, a_ref):
    head = pl.program_id(1)
    first_tile = pl.program_id(2) == 0
    ts = q_ref.shape[0]
    c = DN_CHUNK
    halo = qh_ref.shape[0]

    def conv_silu(x_ref, h_ref, w_ref):
        prev = jnp.where(first_tile, 0.0, h_ref[...].astype(F32))
        x = jnp.concatenate([prev, x_ref[...].astype(F32)], axis=0)
        w = w_ref[...]
        acc = x * w[DN_CONV - 1:DN_CONV, :]
        for s in range(1, DN_CONV):
            acc = acc + pltpu.roll(x, s, 0) * w[DN_CONV - 1 - s:DN_CONV - s, :]
        return _silu(acc[halo:, :])

    q = conv_silu(q_ref, qh_ref, wq_ref)
    k = conv_silu(k_ref, kh_ref, wk_ref)
    v = conv_silu(v_ref, vh_ref, wv_ref)
    q = q * lax.rsqrt(jnp.sum(q * q, axis=-1, keepdims=True) + L2_EPS) * (DN_DIM ** -0.5)
    k = k * lax.rsqrt(jnp.sum(k * k, axis=-1, keepdims=True) + L2_EPS)

    auxc = auxc_ref[...]
    lane = lax.broadcasted_iota(jnp.int32, auxc.shape, 1)
    beta_all = jnp.sum(jnp.where(lane == AUX_BETA + head, auxc, 0.0), axis=1, keepdims=True)
    gcol_all = jnp.sum(jnp.where(lane == AUX_G + head, auxc, 0.0), axis=1, keepdims=True)
    grows = auxr_ref[AUX_G:AUX_G + DN_HEADS, :]
    head_row = lax.broadcasted_iota(jnp.int32, grows.shape, 0)
    grow_all = jnp.sum(jnp.where(head_row == head, grows, 0.0), axis=0, keepdims=True)

    ri = lax.broadcasted_iota(jnp.int32, (c, c), 0)
    ci = lax.broadcasted_iota(jnp.int32, (c, c), 1)
    causal = ri >= ci
    strict = ri > ci
    chunks = [slice(j * c, (j + 1) * c) for j in range(ts // c)]
    decay = [jnp.exp(jnp.where(causal, gcol_all[rows] - grow_all[:, rows], -jnp.inf)) for rows in chunks]
    kb = [k[rows] * beta_all[rows] for rows in chunks]
    both = [_bdot_nt(jnp.concatenate([kb_j, q[rows]], axis=0), k[rows])
            for kb_j, rows in zip(kb, chunks)]
    tinv = _unit_lower_inverses([jnp.where(strict, b[:c] * d, 0.0) for b, d in zip(both, decay)])
    eg = [jnp.exp(gcol_all[rows]) for rows in chunks]
    sol = [_bdot(t, jnp.concatenate([v[rows] * beta_all[rows], kb_j * e], axis=1))
           for t, rows, kb_j, e in zip(tinv, chunks, kb, eg)]
    for j, rows in enumerate(chunks):
        gcol = gcol_all[rows]
        u_ref[rows, :] = sol[j][:, :DN_DIM].astype(u_ref.dtype)
        w_ref[rows, :] = sol[j][:, DN_DIM:].astype(w_ref.dtype)
        qe_ref[rows, :] = (q[rows] * eg[j]).astype(qe_ref.dtype)
        kd_ref[rows, :] = (k[rows] * jnp.exp(gcol[c - 1:c, :] - gcol)).astype(kd_ref.dtype)
        a_ref[rows, :] = jnp.where(causal, both[j][c:] * decay[j], 0.0).astype(a_ref.dtype)


def _delta_scan_kernel(u_ref, w_ref, qe_ref, kd_ref, a_ref, z_ref, auxc_ref, gain_ref, o_ref, st_ref):
    seq = u_ref.shape[0]
    n_heads = a_ref.shape[0]
    c = DN_CHUNK

    @pl.when(pl.program_id(1) == 0)
    def _():
        st_ref[...] = jnp.zeros_like(st_ref)

    gain = gain_ref[...]
    lane = lax.broadcasted_iota(jnp.int32, (1, LANES), 1)

    def scan_chunk(i, carry):
        start = pl.multiple_of(i * c, c)
        rows = pl.ds(start, c)
        last = auxc_ref[pl.ds(start + c - 1, 1), :]
        heads = range(n_heads)
        cols = [slice(hh * DN_DIM, (hh + 1) * DN_DIM) for hh in heads]
        state = [st_ref[hh] for hh in heads]
        sb = [s.astype(BF16) for s in state]
        ws = [jnp.dot(w_ref[rows, cl], s, preferred_element_type=F32) for cl, s in zip(cols, sb)]
        qs = [jnp.dot(qe_ref[rows, cl], s, preferred_element_type=F32) for cl, s in zip(cols, sb)]
        vb = [(u_ref[rows, cl].astype(F32) - x).astype(BF16) for cl, x in zip(cols, ws)]
        kv = [lax.dot_general(kd_ref[rows, cl], x, (((0,), (0,)), ((), ())), preferred_element_type=F32)
              for cl, x in zip(cols, vb)]
        av = [jnp.dot(a_ref[hh, rows, :], x, preferred_element_type=F32) for hh, x in zip(heads, vb)]
        for hh in heads:
            glast = jnp.sum(jnp.where(lane == AUX_G + hh, last, 0.0), axis=1, keepdims=True)
            st_ref[hh] = state[hh] * jnp.exp(glast) + kv[hh]
            z = z_ref[rows, cols[hh]].astype(F32)
            o_ref[rows, cols[hh]] = (_rms(qs[hh] + av[hh], gain) * _silu(z)).astype(o_ref.dtype)
        return carry

    lax.fori_loop(0, seq // c, scan_chunk, 0)


def _delta(main3, auxc3, auxr, conv_w_t, gain):
    bsz, seq, _ = main3.shape
    hb = DN_WIDTH // DN_DIM
    ts = min(DELTA_TILE, seq)
    halo = 16
    tiles = seq // ts

    def col(offset):
        return pl.BlockSpec((None, ts, DN_DIM), lambda b, h, i: (b, i, offset * hb + h))

    def halo_col(offset):
        return pl.BlockSpec((None, halo, DN_DIM),
                            lambda b, h, i: (b, jnp.maximum(i * (ts // halo) - 1, 0), offset * hb + h))

    def wcol(offset):
        return pl.BlockSpec((DN_CONV, DN_DIM), lambda b, h, i: (0, offset * hb + h))

    tok = pl.BlockSpec((None, ts, DN_DIM), lambda b, h, i: (b, i, h))
    tok_shape = jax.ShapeDtypeStruct((bsz, seq, DN_WIDTH), BF16)
    u, w, qe, kd, a = pl.pallas_call(
        _delta_prep_kernel,
        grid=(bsz, DN_HEADS, tiles),
        in_specs=[col(0), col(1), col(2), halo_col(0), halo_col(1), halo_col(2),
                  pl.BlockSpec((None, ts, LANES), lambda b, h, i: (b, i, 0)),
                  pl.BlockSpec((LANES, ts), lambda b, h, i: (0, b * tiles + i)),
                  wcol(0), wcol(1), wcol(2)],
        out_specs=[tok, tok, tok, tok,
                   pl.BlockSpec((None, None, ts, DN_CHUNK), lambda b, h, i: (b, h, i, 0))],
        out_shape=[tok_shape, tok_shape, tok_shape, tok_shape,
                   jax.ShapeDtypeStruct((bsz, DN_HEADS, seq, DN_CHUNK), BF16)],
        compiler_params=_params(3),
        name="delta_prep",
    )(main3, main3, main3, main3, main3, main3, auxc3, auxr, conv_w_t, conv_w_t, conv_w_t)

    tsc = min(DELTA_SCAN_TILE, seq)
    wide = pl.BlockSpec((None, tsc, DN_WIDTH), lambda b, i: (b, i, 0))
    return pl.pallas_call(
        _delta_scan_kernel,
        grid=(bsz, seq // tsc),
        in_specs=[wide, wide, wide, wide,
                  pl.BlockSpec((None, DN_HEADS, tsc, DN_CHUNK), lambda b, i: (b, 0, i, 0)),
                  pl.BlockSpec((None, tsc, DN_WIDTH), lambda b, i: (b, i, 3)),
                  pl.BlockSpec((None, tsc, LANES), lambda b, i: (b, i, 0)),
                  pl.BlockSpec((1, DN_DIM), lambda b, i: (0, 0))],
        out_specs=wide,
        out_shape=tok_shape,
        scratch_shapes=[pltpu.VMEM((DN_HEADS, DN_DIM, DN_DIM), F32)],
        compiler_params=_params(1, 1),
        name="delta_scan",
    )(u, w, qe, kd, a, main3, auxc3, gain)


def _mla_proj_kernel(cq_ref, ckv_ref, auxc_ref, pos_ref, qg_ref, kvg_ref, wq_ref, wkv_ref, rope_ref,
                     q_ref, k_ref, v_ref):
    scale = (MLA_NOPE + MLA_ROPE) ** -0.5
    ang = pos_ref[...].astype(F32) * rope_ref[0:1, :]
    cos_t = jnp.cos(ang) * rope_ref[1:2, :]
    sin_t = jnp.sin(ang) * rope_ref[2:3, :]
    lane = lax.broadcasted_iota(jnp.int32, ang.shape, 1)
    first = lane < MLA_ROPE // 2

    def rope(a):
        swapped = jnp.where(first, pltpu.roll(a, LANES - MLA_ROPE // 2, 1), pltpu.roll(a, MLA_ROPE // 2, 1))
        return a * cos_t + swapped * sin_t

    ql = _bdot(_rms(cq_ref[...].astype(F32), qg_ref[...]), wq_ref[...]) * scale
    kv = _bdot(_rms(ckv_ref[...].astype(F32), kvg_ref[...]), wkv_ref[...])
    kpe = rope(auxc_ref[...]).astype(k_ref.dtype)
    for h in range(MLA_HEADS):
        o = h * MLA_QK_PAD
        q_ref[:, o:o + MLA_NOPE] = ql[:, o:o + MLA_NOPE].astype(q_ref.dtype)
        q_ref[:, o + MLA_NOPE:o + MLA_QK_PAD] = rope(ql[:, o + MLA_NOPE:o + MLA_QK_PAD]).astype(q_ref.dtype)
        k_ref[:, o:o + MLA_NOPE] = kv[:, h * MLA_NOPE:(h + 1) * MLA_NOPE].astype(k_ref.dtype)
        k_ref[:, o + MLA_NOPE:o + MLA_QK_PAD] = kpe
    v_ref[...] = kv[:, MLA_HEADS * MLA_NOPE:].T.astype(v_ref.dtype)


def _mla_proj(main, auxc, pos, q_gain, kv_gain, wq, wkv, rope_tab, seq):
    t = main.shape[0]
    tm = min(512, seq)
    per_b = seq // tm
    cq_blk = (4 * DN_WIDTH) // MLA_Q_RANK
    hq = MLA_HEADS * MLA_QK_PAD
    hv = MLA_HEADS * MLA_V
    return pl.pallas_call(
        _mla_proj_kernel,
        grid=(t // tm,),
        in_specs=[pl.BlockSpec((tm, MLA_Q_RANK), lambda i: (i, cq_blk)),
                  pl.BlockSpec((tm, MLA_KV_RANK), lambda i: (i, cq_blk + 1)),
                  pl.BlockSpec((tm, LANES), lambda i: (i, 0)),
                  pl.BlockSpec((tm, 1), lambda i: (i, 0)),
                  pl.BlockSpec((1, MLA_Q_RANK), lambda i: (0, 0)),
                  pl.BlockSpec((1, MLA_KV_RANK), lambda i: (0, 0)),
                  pl.BlockSpec((MLA_Q_RANK, hq), lambda i: (0, 0)),
                  pl.BlockSpec((MLA_KV_RANK, 2 * hv), lambda i: (0, 0)),
                  pl.BlockSpec((8, LANES), lambda i: (0, 0))],
        out_specs=[pl.BlockSpec((tm, hq), lambda i: (i, 0)),
                   pl.BlockSpec((tm, hq), lambda i: (i, 0)),
                   pl.BlockSpec((None, hv, tm), lambda i: (i // per_b, 0, i % per_b))],
        out_shape=[jax.ShapeDtypeStruct((t, hq), BF16),
                   jax.ShapeDtypeStruct((t, hq), BF16),
                   jax.ShapeDtypeStruct((t // seq, hv, seq), BF16)],
        compiler_params=_params(1),
        name="mla_proj",
    )(main, main, auxc, pos, q_gain, kv_gain, wq, wkv, rope_tab)


ATTN_PARTS = 2


def _attn_kernel(q_ref, k_ref, vt_ref, o_ref, m_ref, l_ref, acc_ref):
    qi = pl.program_id(2)
    n_parts = m_ref.shape[0]
    tk = m_ref.shape[2]
    q = [q_ref[p * tk:(p + 1) * tk, :] for p in range(n_parts)]
    m_ref[...] = jnp.full_like(m_ref, -jnp.inf)
    l_ref[...] = jnp.zeros_like(l_ref)
    acc_ref[...] = jnp.zeros_like(acc_ref)
    key = lax.broadcasted_iota(jnp.int32, (tk, tk), 0)
    query = lax.broadcasted_iota(jnp.int32, (tk, tk), 1)
    visible = key <= query

    def step(j, parts, masked_part):
        start = pl.multiple_of(j * tk, tk)
        kb = k_ref[pl.ds(start, tk), :]
        vtb = vt_ref[:, pl.ds(start, tk)]
        s = [lax.dot_general(kb, q[p], (((1,), (1,)), ((), ())), preferred_element_type=F32) for p in parts]
        s = [jnp.where(visible, x, -jnp.inf) if p == masked_part else x for p, x in zip(parts, s)]
        m_old = [m_ref[p] for p in parts]
        m_new = [jnp.maximum(mo, jnp.max(x, axis=0, keepdims=True)) for mo, x in zip(m_old, s)]
        e = [jnp.exp(x - mn) for x, mn in zip(s, m_new)]
        pv = [jnp.dot(vtb, x.astype(BF16), preferred_element_type=F32) for x in e]
        for i, p in enumerate(parts):
            alpha = jnp.exp(m_old[i] - m_new[i])
            l_ref[p] = alpha * l_ref[p] + jnp.sum(e[i], axis=0, keepdims=True)
            acc_ref[p] = alpha * acc_ref[p] + pv[i]
            m_ref[p] = m_new[i]

    every = tuple(range(n_parts))

    def body(j, carry):
        step(j, every, None)
        return carry

    lax.fori_loop(0, qi * n_parts, body, 0)
    for d in range(n_parts):
        step(qi * n_parts + d, every[d:], d)
    for p in every:
        o_ref[p * tk:(p + 1) * tk, :] = (acc_ref[p] / l_ref[p]).T.astype(o_ref.dtype)


def _attention(q3, k3, vt3):
    bsz, seq, _ = q3.shape
    tk = min(512, seq)
    n_parts = min(ATTN_PARTS, seq // tk)
    tq = tk * n_parts
    return pl.pallas_call(
        _attn_kernel,
        grid=(bsz, MLA_HEADS, seq // tq),
        in_specs=[pl.BlockSpec((None, tq, MLA_QK_PAD), lambda b, h, i: (b, i, h)),
                  pl.BlockSpec((None, seq, MLA_QK_PAD), lambda b, h, i: (b, 0, h)),
                  pl.BlockSpec((None, MLA_V, seq), lambda b, h, i: (b, h, 0))],
        out_specs=pl.BlockSpec((None, tq, MLA_V), lambda b, h, i: (b, i, h)),
        out_shape=jax.ShapeDtypeStruct((bsz, seq, MLA_HEADS * MLA_V), BF16),
        scratch_shapes=[pltpu.VMEM((n_parts, 1, tk), F32), pltpu.VMEM((n_parts, 1, tk), F32),
                        pltpu.VMEM((n_parts, MLA_V, tk), F32)],
        compiler_params=_params(3),
        name="attention",
    )(q3, k3, vt3)


def _out_proj_kernel(x_ref, dn_ref, mla_ref, ada_ref, gain_ref, w_ref, wr_ref, x1_ref, h2_ref, lg_ref):
    ada = ada_ref[0]
    half = dn_ref.shape[1]
    mix = (jnp.dot(dn_ref[...], w_ref[:half, :], preferred_element_type=F32)
           + jnp.dot(mla_ref[...], w_ref[half:, :], preferred_element_type=F32))
    x1 = x_ref[...] + ada[2:3, :] * mix
    x1_ref[...] = x1
    h2 = _rms(x1, gain_ref[...]) * (1.0 + ada[4:5, :]) + ada[3:4, :]
    h2_ref[...] = _pack_bf16_pairs(h2)
    h_hi = h2.astype(BF16)
    h_lo = (h2 - h_hi.astype(F32)).astype(BF16)
    both = jnp.dot(h_hi, wr_ref[...], preferred_element_type=F32)
    logits = both[:, :LANES] + both[:, LANES:] + jnp.dot(h_lo, wr_ref[:, :LANES], preferred_element_type=F32)
    lg_ref[...] = logits.T


def _out_proj(x2, dn, mla, ada3, gain, w_out, w_router, seq):
    t, d = x2.shape
    tm = min(256, seq)
    per_b = seq // tm
    half = dn.shape[1]
    return pl.pallas_call(
        _out_proj_kernel,
        grid=(t // tm,),
        in_specs=[pl.BlockSpec((tm, d), lambda i: (i, 0)),
                  pl.BlockSpec((tm, half), lambda i: (i, 0)),
                  pl.BlockSpec((tm, half), lambda i: (i, 0)),
                  pl.BlockSpec((1, 6, d), lambda i: (i // per_b, 0, 0)),
                  pl.BlockSpec((1, d), lambda i: (0, 0)),
                  pl.BlockSpec((2 * half, d), lambda i: (0, 0)),
                  pl.BlockSpec((d, 2 * LANES), lambda i: (0, 0))],
        out_specs=[pl.BlockSpec((tm, d), lambda i: (i, 0)),
                   pl.BlockSpec((tm, d // 2), lambda i: (i, 0)),
                   pl.BlockSpec((LANES, tm), lambda i: (0, i))],
        out_shape=[jax.ShapeDtypeStruct((t, d), F32),
                   jax.ShapeDtypeStruct((t, d // 2), jnp.uint32),
                   jax.ShapeDtypeStruct((LANES, t), F32)],
        compiler_params=_params(1),
        name="out_proj",
    )(x2, dn, mla, ada3, gain, w_out, w_router)


def _expert_kernel(be_ref, nv_ref, x_ref, wgu_hbm, wd_hbm, y_ref, gu_f, d_f, gu_s, d_s, slot_ref, sem):
    i = pl.program_id(0)
    n_blocks = pl.num_programs(0)
    active = nv_ref[i] > 0
    first = active & ((i == 0) | (be_ref[i] != be_ref[jnp.maximum(i - 1, 0)]))

    def fetch(e, slot):
        return (pltpu.make_async_copy(wgu_hbm.at[e], gu_f.at[slot], sem.at[slot, 0]),
                pltpu.make_async_copy(wd_hbm.at[e], d_f.at[slot], sem.at[slot, 1]))

    @pl.when(i == 0)
    def _():
        slot_ref[0] = 0
        for cp in fetch(be_ref[0], 0):
            cp.start()

    @pl.when(first)
    def _():
        cur = slot_ref[0]
        for cp in fetch(be_ref[i], cur):
            cp.wait()
        nxt = lax.while_loop(lambda j: (j < n_blocks) & (be_ref[jnp.minimum(j, n_blocks - 1)] == be_ref[i]),
                             lambda j: j + 1, i + 1)
        nxt_c = jnp.minimum(nxt, n_blocks - 1)

        @pl.when((nxt < n_blocks) & (nv_ref[nxt_c] > 0))
        def _():
            for cp in fetch(be_ref[nxt_c], 1 - cur):
                cp.start()

        gu_s[...] = gu_f[cur].astype(BF16)
        d_s[...] = d_f[cur].astype(BF16)
        slot_ref[0] = 1 - cur

    @pl.when(active)
    def _():
        x = _unpack_bf16_pairs(x_ref[...]).astype(BF16)
        ff = d_s.shape[0]
        chunks = [slice(c0, c0 + MXU_WIDTH) for c0 in range(0, ff, MXU_WIDTH)]
        gate_up = [(jnp.dot(x, gu_s[:, cs], preferred_element_type=F32),
                    jnp.dot(x, gu_s[:, slice(ff + cs.start, ff + cs.stop)], preferred_element_type=F32))
                   for cs in chunks]
        act = [(_silu(g) * u).astype(BF16) for g, u in gate_up]
        y = jnp.dot(act[0], d_s[chunks[0], :], preferred_element_type=F32)
        for a, cs in zip(act[1:], chunks[1:]):
            y = y + jnp.dot(a, d_s[cs, :], preferred_element_type=F32)
        row = lax.broadcasted_iota(jnp.int32, y.shape, 0)
        y_ref[...] = _pack_bf16_pairs(jnp.where(row < nv_ref[i], y, 0.0))

    @pl.when(jnp.logical_not(active))
    def _():
        y_ref[...] = jnp.zeros_like(y_ref)


def _experts(block_e, n_valid, x_sorted, w_gu, w_d):
    n_rows, dp = x_sorted.shape
    ff2 = w_gu.shape[2]
    n_blocks = n_rows // EXPERT_BLOCK
    grid_spec = pltpu.PrefetchScalarGridSpec(
        num_scalar_prefetch=2,
        grid=(n_blocks,),
        in_specs=[pl.BlockSpec((EXPERT_BLOCK, dp), lambda i, be, nv: (i, 0)),
                  pl.BlockSpec(memory_space=pl.ANY),
                  pl.BlockSpec(memory_space=pl.ANY)],
        out_specs=pl.BlockSpec((EXPERT_BLOCK, dp), lambda i, be, nv: (i, 0)),
        scratch_shapes=[pltpu.VMEM((2, 2 * dp, ff2), F32), pltpu.VMEM((2, ff2 // 2, 2 * dp), F32),
                        pltpu.VMEM((2 * dp, ff2), BF16), pltpu.VMEM((ff2 // 2, 2 * dp), BF16),
                        pltpu.SMEM((1,), jnp.int32), pltpu.SemaphoreType.DMA((2, 2))],
    )
    return pl.pallas_call(
        _expert_kernel,
        grid_spec=grid_spec,
        out_shape=jax.ShapeDtypeStruct((n_rows, dp), jnp.uint32),
        compiler_params=_params(0, 1),
        name="experts",
    )(block_e, n_valid, x_sorted, w_gu, w_d)


def _shared_ffn_kernel(h2_ref, wgu_ref, wd_ref, o_ref):
    h2 = _unpack_bf16_pairs(h2_ref[...]).astype(BF16)
    gu = jnp.dot(h2, wgu_ref[...], preferred_element_type=F32)
    ff = gu.shape[1] // 2
    act = _silu(gu[:, :ff]) * gu[:, ff:]
    o_ref[...] = jnp.dot(act.astype(BF16), wd_ref[...], preferred_element_type=F32).astype(o_ref.dtype)


def _shared_ffn(h2, w_gu, w_d):
    t, dp = h2.shape
    d = 2 * dp
    tm = min(512, t)
    ff2 = w_gu.shape[1]
    return pl.pallas_call(
        _shared_ffn_kernel,
        grid=(t // tm,),
        in_specs=[pl.BlockSpec((tm, dp), lambda i: (i, 0)),
                  pl.BlockSpec((d, ff2), lambda i: (0, 0)),
                  pl.BlockSpec((ff2 // 2, d), lambda i: (0, 0))],
        out_specs=pl.BlockSpec((tm, d), lambda i: (i, 0)),
        out_shape=jax.ShapeDtypeStruct((t, d), BF16),
        compiler_params=_params(1),
        name="shared_ffn",
    )(h2, w_gu, w_d)


def _final_kernel(x1_ref, sh_ref, y_ref, wc_ref, ada_ref, gain_ref, *rest):
    o_ref = rest[-1]
    ada = ada_ref[0]
    ffn = sh_ref[...].astype(F32)
    wc = wc_ref[...]
    for kk in range(TOP_K):
        ffn = ffn + wc[:, kk:kk + 1] * _unpack_bf16_pairs(y_ref[kk])
    x2 = x1_ref[...] + ada[5:6, :] * ffn
    o_ref[...] = _rms(x2, gain_ref[...])


def _final_part(x1, shared, y_part, wc, ada3, gain, seq, part, n_parts, prev_out):
    t, d = x1.shape
    tm = min(256, seq)
    per_b = seq // tm
    steps = t // tm // n_parts
    off = part * steps
    in_specs = [pl.BlockSpec((tm, d), lambda i: (i + off, 0)),
                pl.BlockSpec((tm, d), lambda i: (i + off, 0)),
                pl.BlockSpec((TOP_K, tm, d // 2), lambda i: (0, i, 0)),
                pl.BlockSpec((tm, LANES), lambda i: (i + off, 0)),
                pl.BlockSpec((1, 6, d), lambda i: ((i + off) // per_b, 0, 0)),
                pl.BlockSpec((1, d), lambda i: (0, 0))]
    args = [x1, shared, y_part, wc, ada3, gain]
    aliases = {}
    if prev_out is not None:
        in_specs.append(pl.BlockSpec(memory_space=pl.ANY))
        args.append(prev_out)
        aliases = {len(args) - 1: 0}
    return pl.pallas_call(
        _final_kernel,
        grid=(steps,),
        in_specs=in_specs,
        out_specs=pl.BlockSpec((tm, d), lambda i: (i + off, 0)),
        out_shape=jax.ShapeDtypeStruct((t, d), F32),
        input_output_aliases=aliases,
        compiler_params=_params(1),
        name="final",
    )(*args)


ROUTE_TILE = 512
MAX_BLOCK_LANES = 512


def _first_argmax(x, idx_f, n):
    m = jnp.max(x, axis=0, keepdims=True)
    first = jnp.min(jnp.where(x == m, idx_f, float(n)), axis=0, keepdims=True)
    return m, first


def _route_kernel(lg_ref, bias_ref, ek_ref, rk_ref, wk_ref, wc_ref, ps_ref, be_ref, carry_ref, upper_ref):
    i = pl.program_id(0)
    tm = lg_ref.shape[1]
    per_group = N_EXPERTS // N_GROUPS

    @pl.when(i == 0)
    def _():
        carry_ref[...] = jnp.zeros_like(carry_ref)
        r = lax.broadcasted_iota(jnp.int32, (tm, tm), 0)
        cc = lax.broadcasted_iota(jnp.int32, (tm, tm), 1)
        upper_ref[...] = (r < cc).astype(BF16)

    scores = jax.nn.sigmoid(lg_ref[0:N_EXPERTS, :])
    biased = scores + bias_ref[:, 0:1]
    sub8 = lax.broadcasted_iota(jnp.int32, (per_group, tm), 0).astype(F32)
    group_rows = []
    for g in range(N_GROUPS):
        xg = biased[g * per_group:(g + 1) * per_group, :]
        m1, i1 = _first_argmax(xg, sub8, per_group)
        m2 = jnp.max(jnp.where(sub8 == i1, -jnp.inf, xg), axis=0, keepdims=True)
        group_rows.append(m1 + m2)
    cur = jnp.concatenate(group_rows, axis=0)
    gself = jnp.zeros(cur.shape, F32)
    for _ in range(TOPK_GROUPS):
        _, gi = _first_argmax(cur, sub8, N_GROUPS)
        hit = sub8 == gi
        gself = jnp.where(hit, 1.0, gself)
        cur = jnp.where(hit, -jnp.inf, cur)
    masked = jnp.concatenate(
        [jnp.where(gself[g:g + 1, :] > 0.5, biased[g * per_group:(g + 1) * per_group, :], -jnp.inf)
         for g in range(N_GROUPS)], axis=0)
    sub64 = lax.broadcasted_iota(jnp.int32, (N_EXPERTS, tm), 0).astype(F32)
    e_rows, s_rows = [], []
    sel = jnp.zeros(masked.shape, F32)
    for _ in range(TOP_K):
        _, ei = _first_argmax(masked, sub64, N_EXPERTS)
        hit = sub64 == ei
        e_rows.append(ei)
        s_rows.append(jnp.sum(jnp.where(hit, scores, 0.0), axis=0, keepdims=True))
        sel = jnp.where(hit, 1.0, sel)
        masked = jnp.where(hit, -jnp.inf, masked)
    total = s_rows[0]
    for s in s_rows[1:]:
        total = total + s
    w_rows = [s / total * ROUTED_SCALE for s in s_rows]
    before = jnp.dot(sel.astype(BF16), upper_ref[...], preferred_element_type=F32) + carry_ref[:, 0:1]
    r_rows = [jnp.sum(jnp.where(sub64 == ei, before, 0.0), axis=0, keepdims=True) for ei in e_rows]
    zrow = jnp.zeros((8 - TOP_K, tm), F32)
    ek_ref[...] = jnp.concatenate(e_rows + [zrow], axis=0).astype(jnp.int32)
    rk_ref[...] = jnp.concatenate(r_rows + [zrow], axis=0).astype(jnp.int32)
    wk = jnp.concatenate(w_rows + [zrow], axis=0)
    wk_ref[...] = wk
    wc_ref[...] = jnp.concatenate([wk, jnp.zeros((LANES - 8, tm), F32)], axis=0).T
    carry_ref[...] = carry_ref[...] + jnp.sum(sel, axis=1, keepdims=True)

    @pl.when(i == pl.num_programs(0) - 1)
    def _():
        counts = carry_ref[...]
        padded = jnp.floor((counts + (EXPERT_BLOCK - 1.0)) * (1.0 / EXPERT_BLOCK)) * EXPERT_BLOCK
        r = lax.broadcasted_iota(jnp.int32, (N_EXPERTS, N_EXPERTS), 0)
        cc = lax.broadcasted_iota(jnp.int32, (N_EXPERTS, N_EXPERTS), 1)
        pad_end = jnp.dot((cc <= r).astype(F32), padded, precision=lax.Precision.HIGHEST,
                          preferred_element_type=F32)
        pad_start = pad_end - padded
        ps_ref[...] = pad_start
        blk0 = lax.broadcasted_iota(jnp.int32, (N_EXPERTS, MAX_BLOCK_LANES), 1).astype(F32) * EXPERT_BLOCK
        e_of_blk = jnp.minimum(jnp.sum((pad_end[:, 0:1] <= blk0).astype(F32), axis=0, keepdims=True),
                               N_EXPERTS - 1.0)
        sub = lax.broadcasted_iota(jnp.int32, (N_EXPERTS, MAX_BLOCK_LANES), 0).astype(F32)
        seg_end = jnp.sum(jnp.where(sub == e_of_blk, (pad_start + counts)[:, 0:1], 0.0), axis=0, keepdims=True)
        n_valid = jnp.clip(seg_end - blk0[0:1, :], 0.0, float(EXPERT_BLOCK))
        be_ref[...] = jnp.concatenate([e_of_blk, n_valid, jnp.zeros((6, MAX_BLOCK_LANES), F32)],
                                      axis=0).astype(jnp.int32)


def _route_pallas(logits_t, bias_col):
    t = logits_t.shape[1]
    tm = min(ROUTE_TILE, t)
    row8 = pl.BlockSpec((8, tm), lambda i: (0, i))
    return pl.pallas_call(
        _route_kernel,
        grid=(t // tm,),
        in_specs=[pl.BlockSpec((LANES, tm), lambda i: (0, i)),
                  pl.BlockSpec((N_EXPERTS, 1), lambda i: (0, 0))],
        out_specs=[row8, row8, row8,
                   pl.BlockSpec((tm, LANES), lambda i: (i, 0)),
                   pl.BlockSpec((N_EXPERTS, LANES), lambda i: (0, 0)),
                   pl.BlockSpec((8, MAX_BLOCK_LANES), lambda i: (0, 0))],
        out_shape=[jax.ShapeDtypeStruct((8, t), jnp.int32),
                   jax.ShapeDtypeStruct((8, t), jnp.int32),
                   jax.ShapeDtypeStruct((8, t), F32),
                   jax.ShapeDtypeStruct((t, LANES), F32),
                   jax.ShapeDtypeStruct((N_EXPERTS, LANES), F32),
                   jax.ShapeDtypeStruct((8, MAX_BLOCK_LANES), jnp.int32)],
        scratch_shapes=[pltpu.VMEM((N_EXPERTS, LANES), F32), pltpu.VMEM((tm, tm), BF16)],
        compiler_params=_params(0, 1),
        name="route",
    )(logits_t, bias_col)


def _slot_kernel(ek_ref, rk_ref, ps_ref, slot_ref):
    tm = ek_ref.shape[1]
    sub64 = lax.broadcasted_iota(jnp.int32, (N_EXPERTS, tm), 0)
    start = ps_ref[:, 0:1]
    rows = []
    for kk in range(TOP_K):
        seg = jnp.sum(jnp.where(sub64 == ek_ref[kk:kk + 1, :], start, 0.0), axis=0, keepdims=True)
        rows.append(seg.astype(jnp.int32) + rk_ref[kk:kk + 1, :])
    slot_ref[...] = jnp.concatenate(rows + [jnp.zeros((8 - TOP_K, tm), jnp.int32)], axis=0)


def _slots(ek, rk, ps):
    t = ek.shape[1]
    tm = min(ROUTE_TILE, t)
    row8 = pl.BlockSpec((8, tm), lambda i: (0, i))
    return pl.pallas_call(
        _slot_kernel,
        grid=(t // tm,),
        in_specs=[row8, row8, pl.BlockSpec((N_EXPERTS, LANES), lambda i: (0, 0))],
        out_specs=row8,
        out_shape=jax.ShapeDtypeStruct((8, t), jnp.int32),
        compiler_params=_params(1),
        name="slots",
    )(ek, rk, ps)


SC_ROWS = 32
FINAL_PARTS = 2


def _sc_mesh():
    info = plsc.get_sparse_core_info()
    mesh = plsc.VectorSubcoreMesh(core_axis_name="c", subcore_axis_name="s")
    return mesh, info.num_cores, info.num_cores * info.num_subcores


def _sc_dispatch(x, slot_flat, n_rows):
    t, dp = x.shape
    n_slots = slot_flat.shape[0] // t
    mesh, n_cores, n_workers = _sc_mesh()
    per_w = t // n_workers
    n_chunks = per_w // SC_ROWS
    assert per_w * n_workers == t and n_chunks * SC_ROWS == per_w and n_chunks % 2 == 0

    @functools.partial(
        pl.kernel, out_type=jax.ShapeDtypeStruct((n_rows, dp), x.dtype), mesh=mesh,
        scratch_types=[pltpu.VMEM((2, n_slots, SC_ROWS), jnp.int32), pltpu.VMEM((2, SC_ROWS, dp), x.dtype),
                       pltpu.SemaphoreType.DMA((2,)), pltpu.SemaphoreType.DMA((2,))])
    def dispatch(x_hbm, i_hbm, o_hbm, idx_v, rows_v, read_sem, scatter_sem):
        base = (lax.axis_index("s") * n_cores + lax.axis_index("c")) * per_w

        def read(j, b):
            return pltpu.make_async_copy(x_hbm.at[pl.ds(pl.multiple_of(base + j * SC_ROWS, 8), SC_ROWS)],
                                         rows_v.at[b], read_sem.at[b])

        def scatter(k, b):
            return pltpu.make_async_copy(rows_v.at[b], o_hbm.at[idx_v.at[b, k]], scatter_sem.at[b])

        read(0, 0).start()

        @pl.loop(0, n_chunks, step=2)
        def _(j):
            for b in range(2):
                jj = j + b

                @pl.when(jj >= 1)
                def _():
                    for k in range(n_slots):
                        scatter(k, 1 - b).wait()

                @pl.when(jj + 1 < n_chunks)
                def _():
                    read(jj + 1, 1 - b).start()

                for k in range(n_slots):
                    pltpu.sync_copy(i_hbm.at[pl.ds(pl.multiple_of(k * t + base + jj * SC_ROWS, 8), SC_ROWS)],
                                    idx_v.at[b, k])
                read(jj, b).wait()
                for k in range(n_slots):
                    scatter(k, b).start()

        for k in range(n_slots):
            scatter(k, (n_chunks - 1) % 2).wait()

    return dispatch(x, slot_flat)


def _sc_combine(y, slot_flat):
    n = slot_flat.shape[0]
    dp = y.shape[1]
    mesh, n_cores, n_workers = _sc_mesh()
    per_w = n // n_workers
    n_chunks = per_w // SC_ROWS
    assert per_w * n_workers == n and n_chunks * SC_ROWS == per_w and n_chunks % 2 == 0

    @functools.partial(
        pl.kernel, out_type=jax.ShapeDtypeStruct((n, dp), y.dtype), mesh=mesh,
        scratch_types=[pltpu.VMEM((per_w,), jnp.int32), pltpu.VMEM((2, SC_ROWS, dp), y.dtype),
                       pltpu.SemaphoreType.DMA((2,)), pltpu.SemaphoreType.DMA((2,))])
    def combine(y_hbm, i_hbm, o_hbm, idx_v, rows_v, gather_sem, write_sem):
        base = (lax.axis_index("s") * n_cores + lax.axis_index("c")) * per_w
        pltpu.sync_copy(i_hbm.at[pl.ds(pl.multiple_of(base, 8), per_w)], idx_v)

        def gather(j, b):
            return pltpu.make_async_copy(y_hbm.at[idx_v.at[pl.ds(pl.multiple_of(j * SC_ROWS, 8), SC_ROWS)]],
                                         rows_v.at[b], gather_sem.at[b])

        def write(j, b):
            return pltpu.make_async_copy(rows_v.at[b],
                                         o_hbm.at[pl.ds(pl.multiple_of(base + j * SC_ROWS, 8), SC_ROWS)],
                                         write_sem.at[b])

        gather(0, 0).start()

        @pl.loop(0, n_chunks, step=2)
        def _(j):
            for b in range(2):
                jj = j + b

                @pl.when(jj >= 1)
                def _():
                    write(jj - 1, 1 - b).wait()

                @pl.when(jj + 1 < n_chunks)
                def _():
                    gather(jj + 1, 1 - b).start()

                gather(jj, b).wait()
                write(jj, b).start()

        write(n_chunks - 1, (n_chunks - 1) % 2).wait()

    return combine(y, slot_flat)


def _layer(x, cond_ada, positions, w_in, dn_conv_w, dn_a_log, dn_dt_bias, dn_norm_gain, mla_q_norm_gain,
           w_q_up, mla_kv_norm_gain, w_kv_up, w_out, norm1_gain, norm2_gain, w_router, router_bias,
           w_exp_gate_up, w_exp_down, w_sh_gate_up, w_sh_down, out_gain):
    bsz, seq, d = x.shape
    t = bsz * seq
    x2 = x.reshape(t, d)
    ada3 = cond_ada.reshape(bsz, 6, d)

    s_z = 4 * DN_WIDTH
    s_a = s_z + 2 * DN_HEADS
    s_kpe = s_a + MLA_Q_RANK + MLA_KV_RANK
    w_main = jnp.concatenate([w_in[:, :s_z], w_in[:, s_a:s_kpe]], axis=1).astype(BF16)
    w_aux = jnp.concatenate([w_in[:, s_kpe:], w_in[:, s_z:s_a],
                             jnp.zeros((d, LANES - MLA_ROPE - 2 * DN_HEADS), F32)], axis=1).astype(BF16)
    auxp = jnp.zeros((8, LANES), F32)
    auxp = auxp.at[0, AUX_G:AUX_G + DN_HEADS].set(dn_a_log).at[1, AUX_G:AUX_G + DN_HEADS].set(dn_dt_bias)
    main, auxc, auxr = _in_proj(x2, ada3, norm1_gain.reshape(1, d), w_main, w_aux, auxp, seq)

    dn = _delta(main.reshape(bsz, seq, MAIN_WIDTH), auxc.reshape(bsz, seq, LANES), auxr,
                dn_conv_w.T, dn_norm_gain.reshape(1, DN_DIM))

    qk = MLA_NOPE + MLA_ROPE
    wq3 = w_q_up.reshape(MLA_Q_RANK, MLA_HEADS, qk)
    wq = jnp.concatenate([wq3, jnp.zeros((MLA_Q_RANK, MLA_HEADS, MLA_QK_PAD - qk), F32)], axis=2)
    wq = wq.reshape(MLA_Q_RANK, MLA_HEADS * MLA_QK_PAD).astype(BF16)
    wkv3 = w_kv_up.reshape(MLA_KV_RANK, MLA_HEADS, MLA_NOPE + MLA_V)
    wkv = jnp.concatenate([wkv3[:, :, :MLA_NOPE].reshape(MLA_KV_RANK, -1),
                           wkv3[:, :, MLA_NOPE:].reshape(MLA_KV_RANK, -1)], axis=1).astype(BF16)
    half = MLA_ROPE // 2
    inv_freq = ROPE_THETA ** (-jnp.arange(half, dtype=F32) / half)
    zeros = jnp.zeros((LANES - MLA_ROPE,), F32)
    rope_tab = jnp.zeros((8, LANES), F32)
    rope_tab = rope_tab.at[0].set(jnp.concatenate([inv_freq, inv_freq, zeros]))
    rope_tab = rope_tab.at[1].set(jnp.concatenate([jnp.ones((MLA_ROPE,), F32), zeros]))
    rope_tab = rope_tab.at[2].set(jnp.concatenate([-jnp.ones((half,), F32), jnp.ones((half,), F32), zeros]))
    q, k, v = _mla_proj(main, auxc, positions.reshape(t, 1), mla_q_norm_gain.reshape(1, -1),
                        mla_kv_norm_gain.reshape(1, -1), wq, wkv, rope_tab, seq)
    mla = _attention(q.reshape(bsz, seq, -1), k.reshape(bsz, seq, -1), v)

    w_r = jnp.concatenate([w_router, jnp.zeros((d, LANES - N_EXPERTS), F32)], axis=1)
    w_r_hi = w_r.astype(BF16)
    w_r = jnp.concatenate([w_r_hi, (w_r - w_r_hi.astype(F32)).astype(BF16)], axis=1)
    x1, h2, logits_t = _out_proj(x2, dn.reshape(t, DN_WIDTH), mla.reshape(t, -1), ada3,
                                 norm2_gain.reshape(1, d), w_out.astype(BF16), w_r, seq)

    n_rows = -(-(t * TOP_K + N_EXPERTS * (EXPERT_BLOCK - 1)) // EXPERT_BLOCK) * EXPERT_BLOCK
    n_blocks = n_rows // EXPERT_BLOCK
    assert n_blocks <= MAX_BLOCK_LANES
    ek, rk, _, wc, seg_start, blocks = _route_pallas(logits_t, router_bias.reshape(N_EXPERTS, 1))
    slot = _slots(ek, rk, seg_start)[:TOP_K]
    x_sorted = _sc_dispatch(h2, slot.reshape(-1), n_rows)
    shared = _shared_ffn(h2, w_sh_gate_up.astype(BF16), w_sh_down.astype(BF16))
    y_sorted = _experts(blocks[0, :n_blocks], blocks[1, :n_blocks], x_sorted, w_exp_gate_up, w_exp_down)

    n_parts = FINAL_PARTS if (t // min(256, seq)) % FINAL_PARTS == 0 else 1
    out = None
    for part in range(n_parts):
        part_slots = slot[:, part * (t // n_parts):(part + 1) * (t // n_parts)].reshape(-1)
        y_part = _sc_combine(y_sorted, part_slots).reshape(TOP_K, t // n_parts, d // 2)
        out = _final_part(x1, shared, y_part, wc, ada3, out_gain.reshape(1, d), seq, part, n_parts, out)
    return out.reshape(bsz, seq, d)


def kernel(x, c, positions, w_ada, b_ada, norm1_gain, w_in, dn_conv_w, dn_a_log, dn_dt_bias, dn_norm_gain,
           mla_q_norm_gain, w_q_up, mla_kv_norm_gain, w_kv_up, w_out, norm2_gain, w_router, router_bias,
           w_exp_gate_up, w_exp_down, w_sh_gate_up, w_sh_down, final_norm_gain):
    depth = w_ada.shape[0]
    assert depth == 1, "the final RMSNorm is fused into the single layer's last kernel"
    ada = _ada(c, w_ada[0], b_ada[0])
    return _layer(x, ada, positions, w_in[0], dn_conv_w[0], dn_a_log[0], dn_dt_bias[0], dn_norm_gain[0],
                  mla_q_norm_gain[0], w_q_up[0], mla_kv_norm_gain[0], w_kv_up[0], w_out[0], norm1_gain[0],
                  norm2_gain[0], w_router[0], router_bias[0], w_exp_gate_up[0], w_exp_down[0],
                  w_sh_gate_up[0], w_sh_down[0], final_norm_gain)
```

```python
import functools

import jax
import jax.numpy as jnp
from jax import lax
from jax.experimental import pallas as pl
from jax.experimental.pallas import tpu as pltpu
from jax.experimental.pallas import tpu_sc as plsc

F32 = jnp.float32
BF16 = jnp.bfloat16

DN_HEADS = 8
DN_DIM = 128
DN_WIDTH = DN_HEADS * DN_DIM
DN_CONV = 4
DN_CHUNK = 64
MLA_HEADS = 8
MLA_Q_RANK = 512
MLA_KV_RANK = 512
MLA_NOPE = 128
MLA_ROPE = 64
MLA_V = 128
MLA_QK_PAD = 256
ROPE_THETA = 10000.0
N_EXPERTS = 64
TOP_K = 6
N_GROUPS = 8
TOPK_GROUPS = 4
EXPERT_FF = 512
ROUTED_SCALE = 2.5
EXPERT_BLOCK = 512
RMS_EPS = 1e-6
L2_EPS = 1e-6

LANES = 128
MXU_WIDTH = 256
MAIN_WIDTH = 4 * DN_WIDTH + MLA_Q_RANK + MLA_KV_RANK
AUX_BETA = MLA_ROPE
AUX_G = MLA_ROPE + DN_HEADS
VMEM_LIMIT = 56 * 1024 * 1024


def _params(n_parallel, n_arbitrary=0):
    sem = ("parallel",) * n_parallel + ("arbitrary",) * n_arbitrary
    return pltpu.CompilerParams(dimension_semantics=sem, vmem_limit_bytes=VMEM_LIMIT)


def _silu(x):
    return x * jax.nn.sigmoid(x)


def _bdot(a, b):
    return jnp.dot(a.astype(BF16), b.astype(BF16), preferred_element_type=F32)


def _bdot_nt(a, b):
    return lax.dot_general(a.astype(BF16), b.astype(BF16), (((1,), (1,)), ((), ())),
                           preferred_element_type=F32)


def _rms(x, gain):
    return x * lax.rsqrt(jnp.mean(x * x, axis=-1, keepdims=True) + RMS_EPS) * gain


HIGH_HALF = 0xFFFF0000


def _pack_bf16_pairs(x):
    n = x.shape[1] // 2
    bits = lax.bitcast_convert_type(x.astype(BF16).astype(F32), jnp.uint32)
    return (bits[:, :n] >> 16) | (bits[:, n:] & jnp.uint32(HIGH_HALF))


def _unpack_bf16_pairs(p):
    lo = lax.bitcast_convert_type(p << 16, F32)
    hi = lax.bitcast_convert_type(p & jnp.uint32(HIGH_HALF), F32)
    return jnp.concatenate([lo, hi], axis=1)


def _ada_kernel(c_ref, w_ref, b_ref, o_ref):
    o_ref[...] = _bdot(_silu(c_ref[...]), w_ref[...]) + b_ref[...]


def _ada(c, w_ada, b_ada):
    bsz, d = c.shape
    n = w_ada.shape[1]
    tn = 1024
    return pl.pallas_call(
        _ada_kernel,
        grid=(n // tn,),
        in_specs=[pl.BlockSpec((bsz, d), lambda j: (0, 0)),
                  pl.BlockSpec((d, tn), lambda j: (0, j)),
                  pl.BlockSpec((1, tn), lambda j: (0, j))],
        out_specs=pl.BlockSpec((bsz, tn), lambda j: (0, j)),
        out_shape=jax.ShapeDtypeStruct((bsz, n), F32),
        compiler_params=_params(1),
        name="ada",
    )(c, w_ada, b_ada.reshape(1, n))


def _in_proj_kernel(x_ref, ada_ref, gain_ref, wm_ref, wa_ref, auxp_ref,
                    main_ref, auxc_ref, auxr_ref, h_ref):
    j = pl.program_id(1)

    @pl.when(j == 0)
    def _():
        ada = ada_ref[0]
        h = _rms(x_ref[...], gain_ref[...]) * (1.0 + ada[1:2, :]) + ada[0:1, :]
        hb = h.astype(BF16)
        h_ref[...] = hb
        aux = jnp.dot(hb, wa_ref[...], preferred_element_type=F32)
        tm = aux.shape[0]
        lane = lax.broadcasted_iota(jnp.int32, aux.shape, 1)
        is_beta = (lane >= AUX_BETA) & (lane < AUX_G)
        is_g = (lane >= AUX_G) & (lane < AUX_G + DN_HEADS)
        a_log = auxp_ref[0:1, :]
        dt_bias = auxp_ref[1:2, :]
        sp_in = aux + dt_bias
        softplus = jnp.maximum(sp_in, 0.0) + jnp.log(1.0 + jnp.exp(-jnp.abs(sp_in)))
        g = -jnp.exp(a_log) * softplus
        aux = jnp.where(is_beta, jax.nn.sigmoid(aux), jnp.where(is_g, g, aux))
        r = lax.broadcasted_iota(jnp.int32, (LANES, LANES), 0)
        cidx = lax.broadcasted_iota(jnp.int32, (LANES, LANES), 1)
        tri = ((cidx <= r) & (cidx // DN_CHUNK == r // DN_CHUNK)).astype(F32)
        g_cols = (cidx >= AUX_G) & (cidx < AUX_G + DN_HEADS)
        parts = []
        for t in range(tm // LANES):
            blk = aux[t * LANES:(t + 1) * LANES, :]
            cs = jnp.dot(tri, blk, precision=lax.Precision.HIGHEST, preferred_element_type=F32)
            parts.append(jnp.where(g_cols, cs, blk))
        aux = jnp.concatenate(parts, axis=0)
        auxc_ref[...] = aux
        auxr_ref[...] = aux.T

    main_ref[...] = jnp.dot(h_ref[...], wm_ref[...], preferred_element_type=F32).astype(main_ref.dtype)


def _in_proj(x2, ada3, gain, w_main, w_aux, auxp, seq):
    t, d = x2.shape
    n = w_main.shape[1]
    tm = min(1024, seq)
    tn = 1024
    per_b = seq // tm
    return pl.pallas_call(
        _in_proj_kernel,
        grid=(t // tm, n // tn),
        in_specs=[pl.BlockSpec((tm, d), lambda i, j: (i, 0)),
                  pl.BlockSpec((1, 6, d), lambda i, j: (i // per_b, 0, 0)),
                  pl.BlockSpec((1, d), lambda i, j: (0, 0)),
                  pl.BlockSpec((d, tn), lambda i, j: (0, j)),
                  pl.BlockSpec((d, LANES), lambda i, j: (0, 0)),
                  pl.BlockSpec((8, LANES), lambda i, j: (0, 0))],
        out_specs=[pl.BlockSpec((tm, tn), lambda i, j: (i, j)),
                   pl.BlockSpec((tm, LANES), lambda i, j: (i, 0)),
                   pl.BlockSpec((LANES, tm), lambda i, j: (0, i))],
        out_shape=[jax.ShapeDtypeStruct((t, n), BF16),
                   jax.ShapeDtypeStruct((t, LANES), F32),
                   jax.ShapeDtypeStruct((LANES, t), F32)],
        scratch_shapes=[pltpu.VMEM((tm, d), BF16)],
        compiler_params=_params(1, 1),
        name="in_proj",
    )(x2, ada3, gain, w_main, w_aux, auxp)


def _unit_lower_inverses(ms):
    c = ms[0].shape[0]
    r = lax.broadcasted_iota(jnp.int32, (c, c), 0)
    cc = lax.broadcasted_iota(jnp.int32, (c, c), 1)
    eye = (r == cc).astype(F32)
    same = (r // 16) == (cc // 16)
    md = [jnp.where(same, m, 0.0) for m in ms]
    mo = [jnp.where(same, 0.0, m) for m in ms]
    p1 = [_bdot(a, a) for a in md]
    p2 = [_bdot(a, a) for a in p1]
    p3 = [_bdot(a, a) for a in p2]
    td = [eye - a for a in md]
    td = [t + _bdot(t, p) for t, p in zip(td, p1)]
    td = [t + _bdot(t, p) for t, p in zip(td, p2)]
    td = [t + _bdot(t, p) for t, p in zip(td, p3)]
    n1 = [_bdot(t, o) for t, o in zip(td, mo)]
    n2 = [_bdot(n, n) for n in n1]
    left = [(eye - a) + _bdot(eye - a, b) for a, b in zip(n1, n2)]
    return [_bdot(l, t) for l, t in zip(left, td)]


DELTA_TILE = 2048
DELTA_SCAN_TILE = 512


def _delta_prep_kernel(q_ref, k_ref, v_ref, qh_ref, kh_ref, vh_ref, auxc_ref, auxr_ref, wq_ref, wk_ref, wv_ref,
                       u_ref, w_ref, qe_ref, kd_ref, a_ref):
    head = pl.program_id(1)
    first_tile = pl.program_id(2) == 0
    ts = q_ref.shape[0]
    c = DN_CHUNK
    halo = qh_ref.shape[0]

    def conv_silu(x_ref, h_ref, w_ref):
        prev = jnp.where(first_tile, 0.0, h_ref[...].astype(F32))
        x = jnp.concatenate([prev, x_ref[...].astype(F32)], axis=0)
        w = w_ref[...]
        acc = x * w[DN_CONV - 1:DN_CONV, :]
        for s in range(1, DN_CONV):
            acc = acc + pltpu.roll(x, s, 0) * w[DN_CONV - 1 - s:DN_CONV - s, :]
        return _silu(acc[halo:, :])

    q = conv_silu(q_ref, qh_ref, wq_ref)
    k = conv_silu(k_ref, kh_ref, wk_ref)
    v = conv_silu(v_ref, vh_ref, wv_ref)
    q = q * lax.rsqrt(jnp.sum(q * q, axis=-1, keepdims=True) + L2_EPS) * (DN_DIM ** -0.5)
    k = k * lax.rsqrt(jnp.sum(k * k, axis=-1, keepdims=True) + L2_EPS)

    auxc = auxc_ref[...]
    lane = lax.broadcasted_iota(jnp.int32, auxc.shape, 1)
    beta_all = jnp.sum(jnp.where(lane == AUX_BETA + head, auxc, 0.0), axis=1, keepdims=True)
    gcol_all = jnp.sum(jnp.where(lane == AUX_G + head, auxc, 0.0), axis=1, keepdims=True)
    grows = auxr_ref[AUX_G:AUX_G + DN_HEADS, :]
    head_row = lax.broadcasted_iota(jnp.int32, grows.shape, 0)
    grow_all = jnp.sum(jnp.where(head_row == head, grows, 0.0), axis=0, keepdims=True)

    ri = lax.broadcasted_iota(jnp.int32, (c, c), 0)
    ci = lax.broadcasted_iota(jnp.int32, (c, c), 1)
    causal = ri >= ci
    strict = ri > ci
    chunks = [slice(j * c, (j + 1) * c) for j in range(ts // c)]
    decay = [jnp.exp(jnp.where(causal, gcol_all[rows] - grow_all[:, rows], -jnp.inf)) for rows in chunks]
    kb = [k[rows] * beta_all[rows] for rows in chunks]
    both = [_bdot_nt(jnp.concatenate([kb_j, q[rows]], axis=0), k[rows])
            for kb_j, rows in zip(kb, chunks)]
    tinv = _unit_lower_inverses([jnp.where(strict, b[:c] * d, 0.0) for b, d in zip(both, decay)])
    eg = [jnp.exp(gcol_all[rows]) for rows in chunks]
    sol = [_bdot(t, jnp.concatenate([v[rows] * beta_all[rows], kb_j * e], axis=1))
           for t, rows, kb_j, e in zip(tinv, chunks, kb, eg)]
    for j, rows in enumerate(chunks):
        gcol = gcol_all[rows]
        u_ref[rows, :] = sol[j][:, :DN_DIM].astype(u_ref.dtype)
        w_ref[rows, :] = sol[j][:, DN_DIM:].astype(w_ref.dtype)
        qe_ref[rows, :] = (q[rows] * eg[j]).astype(qe_ref.dtype)
        kd_ref[rows, :] = (k[rows] * jnp.exp(gcol[c - 1:c, :] - gcol)).astype(kd_ref.dtype)
        a_ref[rows, :] = jnp.where(causal, both[j][c:] * decay[j], 0.0).astype(a_ref.dtype)


def _delta_scan_kernel(u_ref, w_ref, qe_ref, kd_ref, a_ref, z_ref, auxc_ref, gain_ref, o_ref, st_ref):
    seq = u_ref.shape[0]
    n_heads = a_ref.shape[0]
    c = DN_CHUNK

    @pl.when(pl.program_id(1) == 0)
    def _():
        st_ref[...] = jnp.zeros_like(st_ref)

    gain = gain_ref[...]
    lane = lax.broadcasted_iota(jnp.int32, (1, LANES), 1)

    def scan_chunk(i, carry):
        start = pl.multiple_of(i * c, c)
        rows = pl.ds(start, c)
        last = auxc_ref[pl.ds(start + c - 1, 1), :]
        heads = range(n_heads)
        cols = [slice(hh * DN_DIM, (hh + 1) * DN_DIM) for hh in heads]
        state = [st_ref[hh] for hh in heads]
        sb = [s.astype(BF16) for s in state]
        ws = [jnp.dot(w_ref[rows, cl], s, preferred_element_type=F32) for cl, s in zip(cols, sb)]
        qs = [jnp.dot(qe_ref[rows, cl], s, preferred_element_type=F32) for cl, s in zip(cols, sb)]
        vb = [(u_ref[rows, cl].astype(F32) - x).astype(BF16) for cl, x in zip(cols, ws)]
        kv = [lax.dot_general(kd_ref[rows, cl], x, (((0,), (0,)), ((), ())), preferred_element_type=F32)
              for cl, x in zip(cols, vb)]
        av = [jnp.dot(a_ref[hh, rows, :], x, preferred_element_type=F32) for hh, x in zip(heads, vb)]
        for hh in heads:
            glast = jnp.sum(jnp.where(lane == AUX_G + hh, last, 0.0), axis=1, keepdims=True)
            st_ref[hh] = state[hh] * jnp.exp(glast) + kv[hh]
            z = z_ref[rows, cols[hh]].astype(F32)
            o_ref[rows, cols[hh]] = (_rms(qs[hh] + av[hh], gain) * _silu(z)).astype(o_ref.dtype)
        return carry

    lax.fori_loop(0, seq // c, scan_chunk, 0)


def _delta(main3, auxc3, auxr, conv_w_t, gain):
    bsz, seq, _ = main3.shape
    hb = DN_WIDTH // DN_DIM
    ts = min(DELTA_TILE, seq)
    halo = 16
    tiles = seq // ts

    def col(offset):
        return pl.BlockSpec((None, ts, DN_DIM), lambda b, h, i: (b, i, offset * hb + h))

    def halo_col(offset):
        return pl.BlockSpec((None, halo, DN_DIM),
                            lambda b, h, i: (b, jnp.maximum(i * (ts // halo) - 1, 0), offset * hb + h))

    def wcol(offset):
        return pl.BlockSpec((DN_CONV, DN_DIM), lambda b, h, i: (0, offset * hb + h))

    tok = pl.BlockSpec((None, ts, DN_DIM), lambda b, h, i: (b, i, h))
    tok_shape = jax.ShapeDtypeStruct((bsz, seq, DN_WIDTH), BF16)
    u, w, qe, kd, a = pl.pallas_call(
        _delta_prep_kernel,
        grid=(bsz, DN_HEADS, tiles),
        in_specs=[col(0), col(1), col(2), halo_col(0), halo_col(1), halo_col(2),
                  pl.BlockSpec((None, ts, LANES), lambda b, h, i: (b, i, 0)),
                  pl.BlockSpec((LANES, ts), lambda b, h, i: (0, b * tiles + i)),
                  wcol(0), wcol(1), wcol(2)],
        out_specs=[tok, tok, tok, tok,
                   pl.BlockSpec((None, None, ts, DN_CHUNK), lambda b, h, i: (b, h, i, 0))],
        out_shape=[tok_shape, tok_shape, tok_shape, tok_shape,
                   jax.ShapeDtypeStruct((bsz, DN_HEADS, seq, DN_CHUNK), BF16)],
        compiler_params=_params(3),
        name="delta_prep",
    )(main3, main3, main3, main3, main3, main3, auxc3, auxr, conv_w_t, conv_w_t, conv_w_t)

    tsc = min(DELTA_SCAN_TILE, seq)
    wide = pl.BlockSpec((None, tsc, DN_WIDTH), lambda b, i: (b, i, 0))
    return pl.pallas_call(
        _delta_scan_kernel,
        grid=(bsz, seq // tsc),
        in_specs=[wide, wide, wide, wide,
                  pl.BlockSpec((None, DN_HEADS, tsc, DN_CHUNK), lambda b, i: (b, 0, i, 0)),
                  pl.BlockSpec((None, tsc, DN_WIDTH), lambda b, i: (b, i, 3)),
                  pl.BlockSpec((None, tsc, LANES), lambda b, i: (b, i, 0)),
                  pl.BlockSpec((1, DN_DIM), lambda b, i: (0, 0))],
        out_specs=wide,
        out_shape=tok_shape,
        scratch_shapes=[pltpu.VMEM((DN_HEADS, DN_DIM, DN_DIM), F32)],
        compiler_params=_params(1, 1),
        name="delta_scan",
    )(u, w, qe, kd, a, main3, auxc3, gain)


def _mla_proj_kernel(cq_ref, ckv_ref, auxc_ref, pos_ref, qg_ref, kvg_ref, wq_ref, wkv_ref, rope_ref,
                     q_ref, k_ref, v_ref):
    scale = (MLA_NOPE + MLA_ROPE) ** -0.5
    ang = pos_ref[...].astype(F32) * rope_ref[0:1, :]
    cos_t = jnp.cos(ang) * rope_ref[1:2, :]
    sin_t = jnp.sin(ang) * rope_ref[2:3, :]
    lane = lax.broadcasted_iota(jnp.int32, ang.shape, 1)
    first = lane < MLA_ROPE // 2

    def rope(a):
        swapped = jnp.where(first, pltpu.roll(a, LANES - MLA_ROPE // 2, 1), pltpu.roll(a, MLA_ROPE // 2, 1))
        return a * cos_t + swapped * sin_t

    ql = _bdot(_rms(cq_ref[...].astype(F32), qg_ref[...]), wq_ref[...]) * scale
    kv = _bdot(_rms(ckv_ref[...].astype(F32), kvg_ref[...]), wkv_ref[...])
    kpe = rope(auxc_ref[...]).astype(k_ref.dtype)
    for h in range(MLA_HEADS):
        o = h * MLA_QK_PAD
        q_ref[:, o:o + MLA_NOPE] = ql[:, o:o + MLA_NOPE].astype(q_ref.dtype)
        q_ref[:, o + MLA_NOPE:o + MLA_QK_PAD] = rope(ql[:, o + MLA_NOPE:o + MLA_QK_PAD]).astype(q_ref.dtype)
        k_ref[:, o:o + MLA_NOPE] = kv[:, h * MLA_NOPE:(h + 1) * MLA_NOPE].astype(k_ref.dtype)
        k_ref[:, o + MLA_NOPE:o + MLA_QK_PAD] = kpe
    v_ref[...] = kv[:, MLA_HEADS * MLA_NOPE:].T.astype(v_ref.dtype)


def _mla_proj(main, auxc, pos, q_gain, kv_gain, wq, wkv, rope_tab, seq):
    t = main.shape[0]
    tm = min(512, seq)
    per_b = seq // tm
    cq_blk = (4 * DN_WIDTH) // MLA_Q_RANK
    hq = MLA_HEADS * MLA_QK_PAD
    hv = MLA_HEADS * MLA_V
    return pl.pallas_call(
        _mla_proj_kernel,
        grid=(t // tm,),
        in_specs=[pl.BlockSpec((tm, MLA_Q_RANK), lambda i: (i, cq_blk)),
                  pl.BlockSpec((tm, MLA_KV_RANK), lambda i: (i, cq_blk + 1)),
                  pl.BlockSpec((tm, LANES), lambda i: (i, 0)),
                  pl.BlockSpec((tm, 1), lambda i: (i, 0)),
                  pl.BlockSpec((1, MLA_Q_RANK), lambda i: (0, 0)),
                  pl.BlockSpec((1, MLA_KV_RANK), lambda i: (0, 0)),
                  pl.BlockSpec((MLA_Q_RANK, hq), lambda i: (0, 0)),
                  pl.BlockSpec((MLA_KV_RANK, 2 * hv), lambda i: (0, 0)),
                  pl.BlockSpec((8, LANES), lambda i: (0, 0))],
        out_specs=[pl.BlockSpec((tm, hq), lambda i: (i, 0)),
                   pl.BlockSpec((tm, hq), lambda i: (i, 0)),
                   pl.BlockSpec((None, hv, tm), lambda i: (i // per_b, 0, i % per_b))],
        out_shape=[jax.ShapeDtypeStruct((t, hq), BF16),
                   jax.ShapeDtypeStruct((t, hq), BF16),
                   jax.ShapeDtypeStruct((t // seq, hv, seq), BF16)],
        compiler_params=_params(1),
        name="mla_proj",
    )(main, main, auxc, pos, q_gain, kv_gain, wq, wkv, rope_tab)


ATTN_PARTS = 4


def _attn_kernel(q_ref, k_ref, vt_ref, o_ref, m_ref, l_ref, acc_ref):
    qi = pl.program_id(2)
    n_parts = m_ref.shape[0]
    tk = m_ref.shape[2]
    q = [q_ref[p * tk:(p + 1) * tk, :] for p in range(n_parts)]
    m_ref[...] = jnp.full_like(m_ref, -jnp.inf)
    l_ref[...] = jnp.zeros_like(l_ref)
    acc_ref[...] = jnp.zeros_like(acc_ref)
    key = lax.broadcasted_iota(jnp.int32, (tk, tk), 0)
    query = lax.broadcasted_iota(jnp.int32, (tk, tk), 1)
    visible = key <= query

    def step(j, parts, masked_part):
        start = pl.multiple_of(j * tk, tk)
        kb = k_ref[pl.ds(start, tk), :]
        vtb = vt_ref[:, pl.ds(start, tk)]
        s = [lax.dot_general(kb, q[p], (((1,), (1,)), ((), ())), preferred_element_type=F32) for p in parts]
        s = [jnp.where(visible, x, -jnp.inf) if p == masked_part else x for p, x in zip(parts, s)]
        m_old = [m_ref[p] for p in parts]
        m_new = [jnp.maximum(mo, jnp.max(x, axis=0, keepdims=True)) for mo, x in zip(m_old, s)]
        e = [jnp.exp(x - mn) for x, mn in zip(s, m_new)]
        pv = [jnp.dot(vtb, x.astype(BF16), preferred_element_type=F32) for x in e]
        for i, p in enumerate(parts):
            alpha = jnp.exp(m_old[i] - m_new[i])
            l_ref[p] = alpha * l_ref[p] + jnp.sum(e[i], axis=0, keepdims=True)
            acc_ref[p] = alpha * acc_ref[p] + pv[i]
            m_ref[p] = m_new[i]

    every = tuple(range(n_parts))

    def body(j, carry):
        step(j, every, None)
        return carry

    lax.fori_loop(0, qi * n_parts, body, 0)
    for d in range(n_parts):
        step(qi * n_parts + d, every[d:], d)
    for p in every:
        o_ref[p * tk:(p + 1) * tk, :] = (acc_ref[p] / l_ref[p]).T.astype(o_ref.dtype)


def _attention(q3, k3, vt3):
    bsz, seq, _ = q3.shape
    tk = min(512, seq)
    n_parts = min(ATTN_PARTS, seq // tk)
    tq = tk * n_parts
    return pl.pallas_call(
        _attn_kernel,
        grid=(bsz, MLA_HEADS, seq // tq),
        in_specs=[pl.BlockSpec((None, tq, MLA_QK_PAD), lambda b, h, i: (b, i, h)),
                  pl.BlockSpec((None, seq, MLA_QK_PAD), lambda b, h, i: (b, 0, h)),
                  pl.BlockSpec((None, MLA_V, seq), lambda b, h, i: (b, h, 0))],
        out_specs=pl.BlockSpec((None, tq, MLA_V), lambda b, h, i: (b, i, h)),
        out_shape=jax.ShapeDtypeStruct((bsz, seq, MLA_HEADS * MLA_V), BF16),
        scratch_shapes=[pltpu.VMEM((n_parts, 1, tk), F32), pltpu.VMEM((n_parts, 1, tk), F32),
                        pltpu.VMEM((n_parts, MLA_V, tk), F32)],
        compiler_params=_params(3),
        name="attention",
    )(q3, k3, vt3)


def _out_proj_kernel(x_ref, dn_ref, mla_ref, ada_ref, gain_ref, w_ref, wr_ref, x1_ref, h2_ref, lg_ref):
    ada = ada_ref[0]
    half = dn_ref.shape[1]
    mix = (jnp.dot(dn_ref[...], w_ref[:half, :], preferred_element_type=F32)
           + jnp.dot(mla_ref[...], w_ref[half:, :], preferred_element_type=F32))
    x1 = x_ref[...] + ada[2:3, :] * mix
    x1_ref[...] = x1
    h2 = _rms(x1, gain_ref[...]) * (1.0 + ada[4:5, :]) + ada[3:4, :]
    h2_ref[...] = _pack_bf16_pairs(h2)
    h_hi = h2.astype(BF16)
    h_lo = (h2 - h_hi.astype(F32)).astype(BF16)
    both = jnp.dot(h_hi, wr_ref[...], preferred_element_type=F32)
    logits = both[:, :LANES] + both[:, LANES:] + jnp.dot(h_lo, wr_ref[:, :LANES], preferred_element_type=F32)
    lg_ref[...] = logits.T


def _out_proj(x2, dn, mla, ada3, gain, w_out, w_router, seq):
    t, d = x2.shape
    tm = min(256, seq)
    per_b = seq // tm
    half = dn.shape[1]
    return pl.pallas_call(
        _out_proj_kernel,
        grid=(t // tm,),
        in_specs=[pl.BlockSpec((tm, d), lambda i: (i, 0)),
                  pl.BlockSpec((tm, half), lambda i: (i, 0)),
                  pl.BlockSpec((tm, half), lambda i: (i, 0)),
                  pl.BlockSpec((1, 6, d), lambda i: (i // per_b, 0, 0)),
                  pl.BlockSpec((1, d), lambda i: (0, 0)),
                  pl.BlockSpec((2 * half, d), lambda i: (0, 0)),
                  pl.BlockSpec((d, 2 * LANES), lambda i: (0, 0))],
        out_specs=[pl.BlockSpec((tm, d), lambda i: (i, 0)),
                   pl.BlockSpec((tm, d // 2), lambda i: (i, 0)),
                   pl.BlockSpec((LANES, tm), lambda i: (0, i))],
        out_shape=[jax.ShapeDtypeStruct((t, d), F32),
                   jax.ShapeDtypeStruct((t, d // 2), jnp.uint32),
                   jax.ShapeDtypeStruct((LANES, t), F32)],
        compiler_params=_params(1),
        name="out_proj",
    )(x2, dn, mla, ada3, gain, w_out, w_router)


def _expert_kernel(be_ref, nv_ref, x_ref, wgu_hbm, wd_hbm, y_ref, gu_f, d_f, gu_s, d_s, slot_ref, sem):
    i = pl.program_id(0)
    n_blocks = pl.num_programs(0)
    active = nv_ref[i] > 0
    first = active & ((i == 0) | (be_ref[i] != be_ref[jnp.maximum(i - 1, 0)]))

    def fetch(e, slot):
        return (pltpu.make_async_copy(wgu_hbm.at[e], gu_f.at[slot], sem.at[slot, 0]),
                pltpu.make_async_copy(wd_hbm.at[e], d_f.at[slot], sem.at[slot, 1]))

    @pl.when(i == 0)
    def _():
        slot_ref[0] = 0
        for cp in fetch(be_ref[0], 0):
            cp.start()

    @pl.when(first)
    def _():
        cur = slot_ref[0]
        for cp in fetch(be_ref[i], cur):
            cp.wait()
        nxt = lax.while_loop(lambda j: (j < n_blocks) & (be_ref[jnp.minimum(j, n_blocks - 1)] == be_ref[i]),
                             lambda j: j + 1, i + 1)
        nxt_c = jnp.minimum(nxt, n_blocks - 1)

        @pl.when((nxt < n_blocks) & (nv_ref[nxt_c] > 0))
        def _():
            for cp in fetch(be_ref[nxt_c], 1 - cur):
                cp.start()

        gu_s[...] = gu_f[cur].astype(BF16)
        d_s[...] = d_f[cur].astype(BF16)
        slot_ref[0] = 1 - cur

    @pl.when(active)
    def _():
        x = _unpack_bf16_pairs(x_ref[...]).astype(BF16)
        ff = d_s.shape[0]
        chunks = [slice(c0, c0 + MXU_WIDTH) for c0 in range(0, ff, MXU_WIDTH)]
        gate_up = [(jnp.dot(x, gu_s[:, cs], preferred_element_type=F32),
                    jnp.dot(x, gu_s[:, slice(ff + cs.start, ff + cs.stop)], preferred_element_type=F32))
                   for cs in chunks]
        act = [(_silu(g) * u).astype(BF16) for g, u in gate_up]
        y = jnp.dot(act[0], d_s[chunks[0], :], preferred_element_type=F32)
        for a, cs in zip(act[1:], chunks[1:]):
            y = y + jnp.dot(a, d_s[cs, :], preferred_element_type=F32)
        row = lax.broadcasted_iota(jnp.int32, y.shape, 0)
        y_ref[...] = _pack_bf16_pairs(jnp.where(row < nv_ref[i], y, 0.0))

    @pl.when(jnp.logical_not(active))
    def _():
        y_ref[...] = jnp.zeros_like(y_ref)


def _experts(block_e, n_valid, x_sorted, w_gu, w_d):
    n_rows, dp = x_sorted.shape
    ff2 = w_gu.shape[2]
    n_blocks = n_rows // EXPERT_BLOCK
    grid_spec = pltpu.PrefetchScalarGridSpec(
        num_scalar_prefetch=2,
        grid=(n_blocks,),
        in_specs=[pl.BlockSpec((EXPERT_BLOCK, dp), lambda i, be, nv: (i, 0)),
                  pl.BlockSpec(memory_space=pl.ANY),
                  pl.BlockSpec(memory_space=pl.ANY)],
        out_specs=pl.BlockSpec((EXPERT_BLOCK, dp), lambda i, be, nv: (i, 0)),
        scratch_shapes=[pltpu.VMEM((2, 2 * dp, ff2), F32), pltpu.VMEM((2, ff2 // 2, 2 * dp), F32),
                        pltpu.VMEM((2 * dp, ff2), BF16), pltpu.VMEM((ff2 // 2, 2 * dp), BF16),
                        pltpu.SMEM((1,), jnp.int32), pltpu.SemaphoreType.DMA((2, 2))],
    )
    return pl.pallas_call(
        _expert_kernel,
        grid_spec=grid_spec,
        out_shape=jax.ShapeDtypeStruct((n_rows, dp), jnp.uint32),
        compiler_params=_params(0, 1),
        name="experts",
    )(block_e, n_valid, x_sorted, w_gu, w_d)


def _shared_ffn_kernel(h2_ref, wgu_ref, wd_ref, o_ref):
    h2 = _unpack_bf16_pairs(h2_ref[...]).astype(BF16)
    gu = jnp.dot(h2, wgu_ref[...], preferred_element_type=F32)
    ff = gu.shape[1] // 2
    act = _silu(gu[:, :ff]) * gu[:, ff:]
    o_ref[...] = jnp.dot(act.astype(BF16), wd_ref[...], preferred_element_type=F32).astype(o_ref.dtype)


def _shared_ffn(h2, w_gu, w_d):
    t, dp = h2.shape
    d = 2 * dp
    tm = min(512, t)
    ff2 = w_gu.shape[1]
    return pl.pallas_call(
        _shared_ffn_kernel,
        grid=(t // tm,),
        in_specs=[pl.BlockSpec((tm, dp), lambda i: (i, 0)),
                  pl.BlockSpec((d, ff2), lambda i: (0, 0)),
                  pl.BlockSpec((ff2 // 2, d), lambda i: (0, 0))],
        out_specs=pl.BlockSpec((tm, d), lambda i: (i, 0)),
        out_shape=jax.ShapeDtypeStruct((t, d), BF16),
        compiler_params=_params(1),
        name="shared_ffn",
    )(h2, w_gu, w_d)


def _final_kernel(x1_ref, sh_ref, y_ref, wc_ref, ada_ref, gain_ref, *rest):
    o_ref = rest[-1]
    ada = ada_ref[0]
    ffn = sh_ref[...].astype(F32)
    wc = wc_ref[...]
    for kk in range(TOP_K):
        ffn = ffn + wc[:, kk:kk + 1] * _unpack_bf16_pairs(y_ref[kk])
    x2 = x1_ref[...] + ada[5:6, :] * ffn
    o_ref[...] = _rms(x2, gain_ref[...])


def _final_part(x1, shared, y_part, wc, ada3, gain, seq, part, n_parts, prev_out):
    t, d = x1.shape
    tm = min(256, seq)
    per_b = seq // tm
    steps = t // tm // n_parts
    off = part * steps
    in_specs = [pl.BlockSpec((tm, d), lambda i: (i + off, 0)),
                pl.BlockSpec((tm, d), lambda i: (i + off, 0)),
                pl.BlockSpec((TOP_K, tm, d // 2), lambda i: (0, i, 0)),
                pl.BlockSpec((tm, LANES), lambda i: (i + off, 0)),
                pl.BlockSpec((1, 6, d), lambda i: ((i + off) // per_b, 0, 0)),
                pl.BlockSpec((1, d), lambda i: (0, 0))]
    args = [x1, shared, y_part, wc, ada3, gain]
    aliases = {}
    if prev_out is not None:
        in_specs.append(pl.BlockSpec(memory_space=pl.ANY))
        args.append(prev_out)
        aliases = {len(args) - 1: 0}
    return pl.pallas_call(
        _final_kernel,
        grid=(steps,),
        in_specs=in_specs,
        out_specs=pl.BlockSpec((tm, d), lambda i: (i + off, 0)),
        out_shape=jax.ShapeDtypeStruct((t, d), F32),
        input_output_aliases=aliases,
        compiler_params=_params(1),
        name="final",
    )(*args)


ROUTE_TILE = 512
MAX_BLOCK_LANES = 512


def _first_argmax(x, idx_f, n):
    m = jnp.max(x, axis=0, keepdims=True)
    first = jnp.min(jnp.where(x == m, idx_f, float(n)), axis=0, keepdims=True)
    return m, first


def _route_kernel(lg_ref, bias_ref, ek_ref, rk_ref, wc_ref, ps_ref, be_ref, carry_ref, upper_ref):
    i = pl.program_id(0)
    tm = lg_ref.shape[1]
    per_group = N_EXPERTS // N_GROUPS

    @pl.when(i == 0)
    def _():
        carry_ref[...] = jnp.zeros_like(carry_ref)
        r = lax.broadcasted_iota(jnp.int32, (tm, tm), 0)
        cc = lax.broadcasted_iota(jnp.int32, (tm, tm), 1)
        upper_ref[...] = (r < cc).astype(BF16)

    scores = jax.nn.sigmoid(lg_ref[0:N_EXPERTS, :])
    biased = scores + bias_ref[:, 0:1]
    sub8 = lax.broadcasted_iota(jnp.int32, (per_group, tm), 0).astype(F32)
    group_rows = []
    for g in range(N_GROUPS):
        xg = biased[g * per_group:(g + 1) * per_group, :]
        m1, i1 = _first_argmax(xg, sub8, per_group)
        m2 = jnp.max(jnp.where(sub8 == i1, -jnp.inf, xg), axis=0, keepdims=True)
        group_rows.append(m1 + m2)
    cur = jnp.concatenate(group_rows, axis=0)
    gself = jnp.zeros(cur.shape, F32)
    for _ in range(TOPK_GROUPS):
        _, gi = _first_argmax(cur, sub8, N_GROUPS)
        hit = sub8 == gi
        gself = jnp.where(hit, 1.0, gself)
        cur = jnp.where(hit, -jnp.inf, cur)
    masked = jnp.concatenate(
        [jnp.where(gself[g:g + 1, :] > 0.5, biased[g * per_group:(g + 1) * per_group, :], -jnp.inf)
         for g in range(N_GROUPS)], axis=0)
    sub64 = lax.broadcasted_iota(jnp.int32, (N_EXPERTS, tm), 0).astype(F32)
    e_rows, s_rows = [], []
    sel = jnp.zeros(masked.shape, F32)
    for _ in range(TOP_K):
        _, ei = _first_argmax(masked, sub64, N_EXPERTS)
        hit = sub64 == ei
        e_rows.append(ei)
        s_rows.append(jnp.sum(jnp.where(hit, scores, 0.0), axis=0, keepdims=True))
        sel = jnp.where(hit, 1.0, sel)
        masked = jnp.where(hit, -jnp.inf, masked)
    total = s_rows[0]
    for s in s_rows[1:]:
        total = total + s
    w_rows = [s / total * ROUTED_SCALE for s in s_rows]
    before = jnp.dot(sel.astype(BF16), upper_ref[...], preferred_element_type=F32) + carry_ref[:, 0:1]
    r_rows = [jnp.sum(jnp.where(sub64 == ei, before, 0.0), axis=0, keepdims=True) for ei in e_rows]
    zrow = jnp.zeros((8 - TOP_K, tm), F32)
    ek_ref[...] = jnp.concatenate(e_rows + [zrow], axis=0).astype(jnp.int32)
    rk_ref[...] = jnp.concatenate(r_rows + [zrow], axis=0).astype(jnp.int32)
    wc_ref[...] = jnp.concatenate(w_rows + [jnp.zeros((LANES - TOP_K, tm), F32)], axis=0).T
    carry_ref[...] = carry_ref[...] + jnp.sum(sel, axis=1, keepdims=True)

    @pl.when(i == pl.num_programs(0) - 1)
    def _():
        counts = carry_ref[...]
        padded = jnp.floor((counts + (EXPERT_BLOCK - 1.0)) * (1.0 / EXPERT_BLOCK)) * EXPERT_BLOCK
        r = lax.broadcasted_iota(jnp.int32, (N_EXPERTS, N_EXPERTS), 0)
        cc = lax.broadcasted_iota(jnp.int32, (N_EXPERTS, N_EXPERTS), 1)
        pad_end = jnp.dot((cc <= r).astype(F32), padded, precision=lax.Precision.HIGHEST,
                          preferred_element_type=F32)
        pad_start = pad_end - padded
        ps_ref[...] = pad_start
        blk0 = lax.broadcasted_iota(jnp.int32, (N_EXPERTS, MAX_BLOCK_LANES), 1).astype(F32) * EXPERT_BLOCK
        e_of_blk = jnp.minimum(jnp.sum((pad_end[:, 0:1] <= blk0).astype(F32), axis=0, keepdims=True),
                               N_EXPERTS - 1.0)
        sub = lax.broadcasted_iota(jnp.int32, (N_EXPERTS, MAX_BLOCK_LANES), 0).astype(F32)
        seg_end = jnp.sum(jnp.where(sub == e_of_blk, (pad_start + counts)[:, 0:1], 0.0), axis=0, keepdims=True)
        n_valid = jnp.clip(seg_end - blk0[0:1, :], 0.0, float(EXPERT_BLOCK))
        be_ref[...] = jnp.concatenate([e_of_blk, n_valid, jnp.zeros((6, MAX_BLOCK_LANES), F32)],
                                      axis=0).astype(jnp.int32)


def _route_pallas(logits_t, bias_col):
    t = logits_t.shape[1]
    tm = min(ROUTE_TILE, t)
    row8 = pl.BlockSpec((8, tm), lambda i: (0, i))
    return pl.pallas_call(
        _route_kernel,
        grid=(t // tm,),
        in_specs=[pl.BlockSpec((LANES, tm), lambda i: (0, i)),
                  pl.BlockSpec((N_EXPERTS, 1), lambda i: (0, 0))],
        out_specs=[row8, row8,
                   pl.BlockSpec((tm, LANES), lambda i: (i, 0)),
                   pl.BlockSpec((N_EXPERTS, LANES), lambda i: (0, 0)),
                   pl.BlockSpec((8, MAX_BLOCK_LANES), lambda i: (0, 0))],
        out_shape=[jax.ShapeDtypeStruct((8, t), jnp.int32),
                   jax.ShapeDtypeStruct((8, t), jnp.int32),
                   jax.ShapeDtypeStruct((t, LANES), F32),
                   jax.ShapeDtypeStruct((N_EXPERTS, LANES), F32),
                   jax.ShapeDtypeStruct((8, MAX_BLOCK_LANES), jnp.int32)],
        scratch_shapes=[pltpu.VMEM((N_EXPERTS, LANES), F32), pltpu.VMEM((tm, tm), BF16)],
        compiler_params=_params(0, 1),
        name="route",
    )(logits_t, bias_col)


def _slot_kernel(ek_ref, rk_ref, ps_ref, slot_ref):
    tm = ek_ref.shape[1]
    sub64 = lax.broadcasted_iota(jnp.int32, (N_EXPERTS, tm), 0)
    start = ps_ref[:, 0:1]
    rows = []
    for kk in range(TOP_K):
        seg = jnp.sum(jnp.where(sub64 == ek_ref[kk:kk + 1, :], start, 0.0), axis=0, keepdims=True)
        rows.append(seg.astype(jnp.int32) + rk_ref[kk:kk + 1, :])
    slot_ref[...] = jnp.concatenate(rows + [jnp.zeros((8 - TOP_K, tm), jnp.int32)], axis=0)


def _slots(ek, rk, ps):
    t = ek.shape[1]
    tm = min(ROUTE_TILE, t)
    row8 = pl.BlockSpec((8, tm), lambda i: (0, i))
    return pl.pallas_call(
        _slot_kernel,
        grid=(t // tm,),
        in_specs=[row8, row8, pl.BlockSpec((N_EXPERTS, LANES), lambda i: (0, 0))],
        out_specs=row8,
        out_shape=jax.ShapeDtypeStruct((8, t), jnp.int32),
        compiler_params=_params(1),
        name="slots",
    )(ek, rk, ps)


SC_ROWS = 32
FINAL_PARTS = 2


def _sc_mesh():
    info = plsc.get_sparse_core_info()
    mesh = plsc.VectorSubcoreMesh(core_axis_name="c", subcore_axis_name="s")
    return mesh, info.num_cores, info.num_cores * info.num_subcores


def _sc_dispatch(x, slot_flat, n_rows):
    t, dp = x.shape
    n_slots = slot_flat.shape[0] // t
    mesh, n_cores, n_workers = _sc_mesh()
    per_w = t // n_workers
    n_chunks = per_w // SC_ROWS
    assert per_w * n_workers == t and n_chunks * SC_ROWS == per_w and n_chunks % 2 == 0

    @functools.partial(
        pl.kernel, out_type=jax.ShapeDtypeStruct((n_rows, dp), x.dtype), mesh=mesh,
        scratch_types=[pltpu.VMEM((2, n_slots, SC_ROWS), jnp.int32), pltpu.VMEM((2, SC_ROWS, dp), x.dtype),
                       pltpu.SemaphoreType.DMA((2,)), pltpu.SemaphoreType.DMA((2,))])
    def dispatch(x_hbm, i_hbm, o_hbm, idx_v, rows_v, read_sem, scatter_sem):
        base = (lax.axis_index("s") * n_cores + lax.axis_index("c")) * per_w

        def read(j, b):
            return pltpu.make_async_copy(x_hbm.at[pl.ds(pl.multiple_of(base + j * SC_ROWS, 8), SC_ROWS)],
                                         rows_v.at[b], read_sem.at[b])

        def scatter(k, b):
            return pltpu.make_async_copy(rows_v.at[b], o_hbm.at[idx_v.at[b, k]], scatter_sem.at[b])

        read(0, 0).start()

        @pl.loop(0, n_chunks, step=2)
        def _(j):
            for b in range(2):
                jj = j + b

                @pl.when(jj >= 1)
                def _():
                    for k in range(n_slots):
                        scatter(k, 1 - b).wait()

                @pl.when(jj + 1 < n_chunks)
                def _():
                    read(jj + 1, 1 - b).start()

                for k in range(n_slots):
                    pltpu.sync_copy(i_hbm.at[pl.ds(pl.multiple_of(k * t + base + jj * SC_ROWS, 8), SC_ROWS)],
                                    idx_v.at[b, k])
                read(jj, b).wait()
                for k in range(n_slots):
                    scatter(k, b).start()

        for k in range(n_slots):
            scatter(k, (n_chunks - 1) % 2).wait()

    return dispatch(x, slot_flat)


def _sc_combine(y, slot_flat):
    n = slot_flat.shape[0]
    dp = y.shape[1]
    mesh, n_cores, n_workers = _sc_mesh()
    per_w = n // n_workers
    n_chunks = per_w // SC_ROWS
    assert per_w * n_workers == n and n_chunks * SC_ROWS == per_w and n_chunks % 2 == 0

    @functools.partial(
        pl.kernel, out_type=jax.ShapeDtypeStruct((n, dp), y.dtype), mesh=mesh,
        scratch_types=[pltpu.VMEM((per_w,), jnp.int32), pltpu.VMEM((2, SC_ROWS, dp), y.dtype),
                       pltpu.SemaphoreType.DMA((2,)), pltpu.SemaphoreType.DMA((2,))])
    def combine(y_hbm, i_hbm, o_hbm, idx_v, rows_v, gather_sem, write_sem):
        base = (lax.axis_index("s") * n_cores + lax.axis_index("c")) * per_w
        pltpu.sync_copy(i_hbm.at[pl.ds(pl.multiple_of(base, 8), per_w)], idx_v)

        def gather(j, b):
            return pltpu.make_async_copy(y_hbm.at[idx_v.at[pl.ds(pl.multiple_of(j * SC_ROWS, 8), SC_ROWS)]],
                                         rows_v.at[b], gather_sem.at[b])

        def write(j, b):
            return pltpu.make_async_copy(rows_v.at[b],
                                         o_hbm.at[pl.ds(pl.multiple_of(base + j * SC_ROWS, 8), SC_ROWS)],
                                         write_sem.at[b])

        gather(0, 0).start()

        @pl.loop(0, n_chunks, step=2)
        def _(j):
            for b in range(2):
                jj = j + b

                @pl.when(jj >= 1)
                def _():
                    write(jj - 1, 1 - b).wait()

                @pl.when(jj + 1 < n_chunks)
                def _():
                    gather(jj + 1, 1 - b).start()

                gather(jj, b).wait()
                write(jj, b).start()

        write(n_chunks - 1, (n_chunks - 1) % 2).wait()

    return combine(y, slot_flat)


def _layer(x, cond_ada, positions, w_in, dn_conv_w, dn_a_log, dn_dt_bias, dn_norm_gain, mla_q_norm_gain,
           w_q_up, mla_kv_norm_gain, w_kv_up, w_out, norm1_gain, norm2_gain, w_router, router_bias,
           w_exp_gate_up, w_exp_down, w_sh_gate_up, w_sh_down, out_gain):
    bsz, seq, d = x.shape
    t = bsz * seq
    x2 = x.reshape(t, d)
    ada3 = cond_ada.reshape(bsz, 6, d)

    s_z = 4 * DN_WIDTH
    s_a = s_z + 2 * DN_HEADS
    s_kpe = s_a + MLA_Q_RANK + MLA_KV_RANK
    w_main = jnp.concatenate([w_in[:, :s_z], w_in[:, s_a:s_kpe]], axis=1).astype(BF16)
    w_aux = jnp.concatenate([w_in[:, s_kpe:], w_in[:, s_z:s_a],
                             jnp.zeros((d, LANES - MLA_ROPE - 2 * DN_HEADS), F32)], axis=1).astype(BF16)
    auxp = jnp.zeros((8, LANES), F32)
    auxp = auxp.at[0, AUX_G:AUX_G + DN_HEADS].set(dn_a_log).at[1, AUX_G:AUX_G + DN_HEADS].set(dn_dt_bias)
    main, auxc, auxr = _in_proj(x2, ada3, norm1_gain.reshape(1, d), w_main, w_aux, auxp, seq)

    dn = _delta(main.reshape(bsz, seq, MAIN_WIDTH), auxc.reshape(bsz, seq, LANES), auxr,
                dn_conv_w.T, dn_norm_gain.reshape(1, DN_DIM))

    qk = MLA_NOPE + MLA_ROPE
    wq3 = w_q_up.reshape(MLA_Q_RANK, MLA_HEADS, qk)
    wq = jnp.concatenate([wq3, jnp.zeros((MLA_Q_RANK, MLA_HEADS, MLA_QK_PAD - qk), F32)], axis=2)
    wq = wq.reshape(MLA_Q_RANK, MLA_HEADS * MLA_QK_PAD).astype(BF16)
    wkv3 = w_kv_up.reshape(MLA_KV_RANK, MLA_HEADS, MLA_NOPE + MLA_V)
    wkv = jnp.concatenate([wkv3[:, :, :MLA_NOPE].reshape(MLA_KV_RANK, -1),
                           wkv3[:, :, MLA_NOPE:].reshape(MLA_KV_RANK, -1)], axis=1).astype(BF16)
    half = MLA_ROPE // 2
    inv_freq = ROPE_THETA ** (-jnp.arange(half, dtype=F32) / half)
    zeros = jnp.zeros((LANES - MLA_ROPE,), F32)
    rope_tab = jnp.zeros((8, LANES), F32)
    rope_tab = rope_tab.at[0].set(jnp.concatenate([inv_freq, inv_freq, zeros]))
    rope_tab = rope_tab.at[1].set(jnp.concatenate([jnp.ones((MLA_ROPE,), F32), zeros]))
    rope_tab = rope_tab.at[2].set(jnp.concatenate([-jnp.ones((half,), F32), jnp.ones((half,), F32), zeros]))
    q, k, v = _mla_proj(main, auxc, positions.reshape(t, 1), mla_q_norm_gain.reshape(1, -1),
                        mla_kv_norm_gain.reshape(1, -1), wq, wkv, rope_tab, seq)
    mla = _attention(q.reshape(bsz, seq, -1), k.reshape(bsz, seq, -1), v)

    w_r = jnp.concatenate([w_router, jnp.zeros((d, LANES - N_EXPERTS), F32)], axis=1)
    w_r_hi = w_r.astype(BF16)
    w_r = jnp.concatenate([w_r_hi, (w_r - w_r_hi.astype(F32)).astype(BF16)], axis=1)
    x1, h2, logits_t = _out_proj(x2, dn.reshape(t, DN_WIDTH), mla.reshape(t, -1), ada3,
                                 norm2_gain.reshape(1, d), w_out.astype(BF16), w_r, seq)

    n_rows = -(-(t * TOP_K + N_EXPERTS * (EXPERT_BLOCK - 1)) // EXPERT_BLOCK) * EXPERT_BLOCK
    n_blocks = n_rows // EXPERT_BLOCK
    assert n_blocks <= MAX_BLOCK_LANES
    ek, rk, wc, seg_start, blocks = _route_pallas(logits_t, router_bias.reshape(N_EXPERTS, 1))
    slot = _slots(ek, rk, seg_start)[:TOP_K]
    x_sorted = _sc_dispatch(h2, slot.reshape(-1), n_rows)
    shared = _shared_ffn(h2, w_sh_gate_up.astype(BF16), w_sh_down.astype(BF16))
    y_sorted = _experts(blocks[0, :n_blocks], blocks[1, :n_blocks], x_sorted, w_exp_gate_up, w_exp_down)

    n_parts = FINAL_PARTS if (t // min(256, seq)) % FINAL_PARTS == 0 else 1
    out = None
    for part in range(n_parts):
        part_slots = slot[:, part * (t // n_parts):(part + 1) * (t // n_parts)].reshape(-1)
        y_part = _sc_combine(y_sorted, part_slots).reshape(TOP_K, t // n_parts, d // 2)
        out = _final_part(x1, shared, y_part, wc, ada3, out_gain.reshape(1, d), seq, part, n_parts, out)
    return out.reshape(bsz, seq, d)


def kernel(x, c, positions, w_ada, b_ada, norm1_gain, w_in, dn_conv_w, dn_a_log, dn_dt_bias, dn_norm_gain,
           mla_q_norm_gain, w_q_up, mla_kv_norm_gain, w_kv_up, w_out, norm2_gain, w_router, router_bias,
           w_exp_gate_up, w_exp_down, w_sh_gate_up, w_sh_down, final_norm_gain):
    depth = w_ada.shape[0]
    assert depth == 1, "the final RMSNorm is fused into the single layer's last kernel"
    ada = _ada(c, w_ada[0], b_ada[0])
    return _layer(x, ada, positions, w_in[0], dn_conv_w[0], dn_a_log[0], dn_dt_bias[0], dn_norm_gain[0],
                  mla_q_norm_gain[0], w_q_up[0], mla_kv_norm_gain[0], w_kv_up[0], w_out[0], norm1_gain[0],
                  norm2_gain[0], w_router[0], router_bias[0], w_exp_gate_up[0], w_exp_down[0],
                  w_sh_gate_up[0], w_sh_down[0], final_norm_gain)
```

```python
import functools

import jax
import jax.numpy as jnp
from jax import lax
from jax.experimental import pallas as pl
from jax.experimental.pallas import tpu as pltpu
from jax.experimental.pallas import tpu_sc as plsc

F32 = jnp.float32
BF16 = jnp.bfloat16

DN_HEADS = 8
DN_DIM = 128
DN_WIDTH = DN_HEADS * DN_DIM
DN_CONV = 4
DN_CHUNK = 64
MLA_HEADS = 8
MLA_Q_RANK = 512
MLA_KV_RANK = 512
MLA_NOPE = 128
MLA_ROPE = 64
MLA_V = 128
MLA_QK_PAD = 256
ROPE_THETA = 10000.0
N_EXPERTS = 64
TOP_K = 6
N_GROUPS = 8
TOPK_GROUPS = 4
EXPERT_FF = 512
ROUTED_SCALE = 2.5
EXPERT_BLOCK = 512
RMS_EPS = 1e-6
L2_EPS = 1e-6

LANES = 128
MXU_WIDTH = 256
MAIN_WIDTH = 4 * DN_WIDTH + MLA_Q_RANK + MLA_KV_RANK
AUX_BETA = MLA_ROPE
AUX_G = MLA_ROPE + DN_HEADS
VMEM_LIMIT = 56 * 1024 * 1024


def _params(n_parallel, n_arbitrary=0):
    sem = ("parallel",) * n_parallel + ("arbitrary",) * n_arbitrary
    return pltpu.CompilerParams(dimension_semantics=sem, vmem_limit_bytes=VMEM_LIMIT)


def _silu(x):
    return x * jax.nn.sigmoid(x)


def _bdot(a, b):
    return jnp.dot(a.astype(BF16), b.astype(BF16), preferred_element_type=F32)


def _bdot_nt(a, b):
    return lax.dot_general(a.astype(BF16), b.astype(BF16), (((1,), (1,)), ((), ())),
                           preferred_element_type=F32)


def _rms(x, gain):
    return x * lax.rsqrt(jnp.mean(x * x, axis=-1, keepdims=True) + RMS_EPS) * gain


HIGH_HALF = 0xFFFF0000


def _pack_bf16_pairs(x):
    n = x.shape[1] // 2
    bits = lax.bitcast_convert_type(x.astype(BF16).astype(F32), jnp.uint32)
    return (bits[:, :n] >> 16) | (bits[:, n:] & jnp.uint32(HIGH_HALF))


def _unpack_bf16_pairs(p):
    lo = lax.bitcast_convert_type(p << 16, F32)
    hi = lax.bitcast_convert_type(p & jnp.uint32(HIGH_HALF), F32)
    return jnp.concatenate([lo, hi], axis=1)


def _ada_kernel(c_ref, w_ref, b_ref, o_ref):
    o_ref[...] = _bdot(_silu(c_ref[...]), w_ref[...]) + b_ref[...]


def _ada(c, w_ada, b_ada):
    bsz, d = c.shape
    n = w_ada.shape[1]
    tn = 1024
    return pl.pallas_call(
        _ada_kernel,
        grid=(n // tn,),
        in_specs=[pl.BlockSpec((bsz, d), lambda j: (0, 0)),
                  pl.BlockSpec((d, tn), lambda j: (0, j)),
                  pl.BlockSpec((1, tn), lambda j: (0, j))],
        out_specs=pl.BlockSpec((bsz, tn), lambda j: (0, j)),
        out_shape=jax.ShapeDtypeStruct((bsz, n), F32),
        compiler_params=_params(1),
        name="ada",
    )(c, w_ada, b_ada.reshape(1, n))


def _in_proj_kernel(x_ref, ada_ref, gain_ref, wm_ref, wa_ref, auxp_ref,
                    main_ref, auxc_ref, auxr_ref, h_ref):
    j = pl.program_id(1)

    @pl.when(j == 0)
    def _():
        ada = ada_ref[0]
        h = _rms(x_ref[...], gain_ref[...]) * (1.0 + ada[1:2, :]) + ada[0:1, :]
        hb = h.astype(BF16)
        h_ref[...] = hb
        aux = jnp.dot(hb, wa_ref[...], preferred_element_type=F32)
        tm = aux.shape[0]
        lane = lax.broadcasted_iota(jnp.int32, aux.shape, 1)
        is_beta = (lane >= AUX_BETA) & (lane < AUX_G)
        is_g = (lane >= AUX_G) & (lane < AUX_G + DN_HEADS)
        a_log = auxp_ref[0:1, :]
        dt_bias = auxp_ref[1:2, :]
        sp_in = aux + dt_bias
        softplus = jnp.maximum(sp_in, 0.0) + jnp.log(1.0 + jnp.exp(-jnp.abs(sp_in)))
        g = -jnp.exp(a_log) * softplus
        aux = jnp.where(is_beta, jax.nn.sigmoid(aux), jnp.where(is_g, g, aux))
        r = lax.broadcasted_iota(jnp.int32, (LANES, LANES), 0)
        cidx = lax.broadcasted_iota(jnp.int32, (LANES, LANES), 1)
        tri = ((cidx <= r) & (cidx // DN_CHUNK == r // DN_CHUNK)).astype(F32)
        g_cols = (cidx >= AUX_G) & (cidx < AUX_G + DN_HEADS)
        parts = []
        for t in range(tm // LANES):
            blk = aux[t * LANES:(t + 1) * LANES, :]
            cs = jnp.dot(tri, blk, precision=lax.Precision.HIGHEST, preferred_element_type=F32)
            parts.append(jnp.where(g_cols, cs, blk))
        aux = jnp.concatenate(parts, axis=0)
        auxc_ref[...] = aux
        auxr_ref[...] = aux.T

    main_ref[...] = jnp.dot(h_ref[...], wm_ref[...], preferred_element_type=F32).astype(main_ref.dtype)


def _in_proj(x2, ada3, gain, w_main, w_aux, auxp, seq):
    t, d = x2.shape
    n = w_main.shape[1]
    tm = min(1024, seq)
    tn = 1024
    per_b = seq // tm
    return pl.pallas_call(
        _in_proj_kernel,
        grid=(t // tm, n // tn),
        in_specs=[pl.BlockSpec((tm, d), lambda i, j: (i, 0)),
                  pl.BlockSpec((1, 6, d), lambda i, j: (i // per_b, 0, 0)),
                  pl.BlockSpec((1, d), lambda i, j: (0, 0)),
                  pl.BlockSpec((d, tn), lambda i, j: (0, j)),
                  pl.BlockSpec((d, LANES), lambda i, j: (0, 0)),
                  pl.BlockSpec((8, LANES), lambda i, j: (0, 0))],
        out_specs=[pl.BlockSpec((tm, tn), lambda i, j: (i, j)),
                   pl.BlockSpec((tm, LANES), lambda i, j: (i, 0)),
                   pl.BlockSpec((LANES, tm), lambda i, j: (0, i))],
        out_shape=[jax.ShapeDtypeStruct((t, n), BF16),
                   jax.ShapeDtypeStruct((t, LANES), F32),
                   jax.ShapeDtypeStruct((LANES, t), F32)],
        scratch_shapes=[pltpu.VMEM((tm, d), BF16)],
        compiler_params=_params(1, 1),
        name="in_proj",
    )(x2, ada3, gain, w_main, w_aux, auxp)


def _unit_lower_inverses(ms):
    c = ms[0].shape[0]
    r = lax.broadcasted_iota(jnp.int32, (c, c), 0)
    cc = lax.broadcasted_iota(jnp.int32, (c, c), 1)
    eye = (r == cc).astype(F32)
    same = (r // 16) == (cc // 16)
    md = [jnp.where(same, m, 0.0) for m in ms]
    mo = [jnp.where(same, 0.0, m) for m in ms]
    p1 = [_bdot(a, a) for a in md]
    p2 = [_bdot(a, a) for a in p1]
    p3 = [_bdot(a, a) for a in p2]
    td = [eye - a for a in md]
    td = [t + _bdot(t, p) for t, p in zip(td, p1)]
    td = [t + _bdot(t, p) for t, p in zip(td, p2)]
    td = [t + _bdot(t, p) for t, p in zip(td, p3)]
    n1 = [_bdot(t, o) for t, o in zip(td, mo)]
    n2 = [_bdot(n, n) for n in n1]
    left = [(eye - a) + _bdot(eye - a, b) for a, b in zip(n1, n2)]
    return [_bdot(l, t) for l, t in zip(left, td)]


DELTA_TILE = 2048
DELTA_SCAN_TILE = 512


def _delta_prep_kernel(q_ref, k_ref, v_ref, qh_ref, kh_ref, vh_ref, auxc_ref, auxr_ref, wq_ref, wk_ref, wv_ref,
                       u_ref, w_ref, qe_ref, kd_ref, a_ref):
    head = pl.program_id(1)
    first_tile = pl.program_id(2) == 0
    ts = q_ref.shape[0]
    c = DN_CHUNK
    halo = qh_ref.shape[0]

    def conv_silu(x_ref, h_ref, w_ref):
        prev = jnp.where(first_tile, 0.0, h_ref[...].astype(F32))
        x = jnp.concatenate([prev, x_ref[...].astype(F32)], axis=0)
        w = w_ref[...]
        acc = x * w[DN_CONV - 1:DN_CONV, :]
        for s in range(1, DN_CONV):
            acc = acc + pltpu.roll(x, s, 0) * w[DN_CONV - 1 - s:DN_CONV - s, :]
        return _silu(acc[halo:, :])

    q = conv_silu(q_ref, qh_ref, wq_ref)
    k = conv_silu(k_ref, kh_ref, wk_ref)
    v = conv_silu(v_ref, vh_ref, wv_ref)
    q = q * lax.rsqrt(jnp.sum(q * q, axis=-1, keepdims=True) + L2_EPS) * (DN_DIM ** -0.5)
    k = k * lax.rsqrt(jnp.sum(k * k, axis=-1, keepdims=True) + L2_EPS)

    auxc = auxc_ref[...]
    lane = lax.broadcasted_iota(jnp.int32, auxc.shape, 1)
    beta_all = jnp.sum(jnp.where(lane == AUX_BETA + head, auxc, 0.0), axis=1, keepdims=True)
    gcol_all = jnp.sum(jnp.where(lane == AUX_G + head, auxc, 0.0), axis=1, keepdims=True)
    grows = auxr_ref[AUX_G:AUX_G + DN_HEADS, :]
    head_row = lax.broadcasted_iota(jnp.int32, grows.shape, 0)
    grow_all = jnp.sum(jnp.where(head_row == head, grows, 0.0), axis=0, keepdims=True)

    ri = lax.broadcasted_iota(jnp.int32, (c, c), 0)
    ci = lax.broadcasted_iota(jnp.int32, (c, c), 1)
    causal = ri >= ci
    strict = ri > ci
    chunks = [slice(j * c, (j + 1) * c) for j in range(ts // c)]
    decay = [jnp.exp(jnp.where(causal, gcol_all[rows] - grow_all[:, rows], -jnp.inf)) for rows in chunks]
    kb = [k[rows] * beta_all[rows] for rows in chunks]
    both = [_bdot_nt(jnp.concatenate([kb_j, q[rows]], axis=0), k[rows])
            for kb_j, rows in zip(kb, chunks)]
    tinv = _unit_lower_inverses([jnp.where(strict, b[:c] * d, 0.0) for b, d in zip(both, decay)])
    eg = [jnp.exp(gcol_all[rows]) for rows in chunks]
    sol = [_bdot(t, jnp.concatenate([v[rows] * beta_all[rows], kb_j * e], axis=1))
           for t, rows, kb_j, e in zip(tinv, chunks, kb, eg)]
    for j, rows in enumerate(chunks):
        gcol = gcol_all[rows]
        u_ref[rows, :] = sol[j][:, :DN_DIM].astype(u_ref.dtype)
        w_ref[rows, :] = sol[j][:, DN_DIM:].astype(w_ref.dtype)
        qe_ref[rows, :] = (q[rows] * eg[j]).astype(qe_ref.dtype)
        kd_ref[rows, :] = (k[rows] * jnp.exp(gcol[c - 1:c, :] - gcol)).astype(kd_ref.dtype)
        a_ref[rows, :] = jnp.where(causal, both[j][c:] * decay[j], 0.0).astype(a_ref.dtype)


def _delta_scan_kernel(u_ref, w_ref, qe_ref, kd_ref, a_ref, z_ref, auxc_ref, gain_ref, o_ref, st_ref):
    seq = u_ref.shape[0]
    n_heads = a_ref.shape[0]
    c = DN_CHUNK

    @pl.when(pl.program_id(1) == 0)
    def _():
        st_ref[...] = jnp.zeros_like(st_ref)

    gain = gain_ref[...]
    lane = lax.broadcasted_iota(jnp.int32, (1, LANES), 1)

    def scan_chunk(i, carry):
        start = pl.multiple_of(i * c, c)
        rows = pl.ds(start, c)
        last = auxc_ref[pl.ds(start + c - 1, 1), :]
        heads = range(n_heads)
        cols = [slice(hh * DN_DIM, (hh + 1) * DN_DIM) for hh in heads]
        state = [st_ref[hh] for hh in heads]
        sb = [s.astype(BF16) for s in state]
        ws = [jnp.dot(w_ref[rows, cl], s, preferred_element_type=F32) for cl, s in zip(cols, sb)]
        qs = [jnp.dot(qe_ref[rows, cl], s, preferred_element_type=F32) for cl, s in zip(cols, sb)]
        vb = [(u_ref[rows, cl].astype(F32) - x).astype(BF16) for cl, x in zip(cols, ws)]
        kv = [lax.dot_general(kd_ref[rows, cl], x, (((0,), (0,)), ((), ())), preferred_element_type=F32)
              for cl, x in zip(cols, vb)]
        av = [jnp.dot(a_ref[hh, rows, :], x, preferred_element_type=F32) for hh, x in zip(heads, vb)]
        for hh in heads:
            glast = jnp.sum(jnp.where(lane == AUX_G + hh, last, 0.0), axis=1, keepdims=True)
            st_ref[hh] = state[hh] * jnp.exp(glast) + kv[hh]
            z = z_ref[rows, cols[hh]].astype(F32)
            o_ref[rows, cols[hh]] = (_rms(qs[hh] + av[hh], gain) * _silu(z)).astype(o_ref.dtype)
        return carry

    lax.fori_loop(0, seq // c, scan_chunk, 0)


def _delta(main3, auxc3, auxr, conv_w_t, gain):
    bsz, seq, _ = main3.shape
    hb = DN_WIDTH // DN_DIM
    ts = min(DELTA_TILE, seq)
    halo = 16
    tiles = seq // ts

    def col(offset):
        return pl.BlockSpec((None, ts, DN_DIM), lambda b, h, i: (b, i, offset * hb + h))

    def halo_col(offset):
        return pl.BlockSpec((None, halo, DN_DIM),
                            lambda b, h, i: (b, jnp.maximum(i * (ts // halo) - 1, 0), offset * hb + h))

    def wcol(offset):
        return pl.BlockSpec((DN_CONV, DN_DIM), lambda b, h, i: (0, offset * hb + h))

    tok = pl.BlockSpec((None, ts, DN_DIM), lambda b, h, i: (b, i, h))
    tok_shape = jax.ShapeDtypeStruct((bsz, seq, DN_WIDTH), BF16)
    u, w, qe, kd, a = pl.pallas_call(
        _delta_prep_kernel,
        grid=(bsz, DN_HEADS, tiles),
        in_specs=[col(0), col(1), col(2), halo_col(0), halo_col(1), halo_col(2),
                  pl.BlockSpec((None, ts, LANES), lambda b, h, i: (b, i, 0)),
                  pl.BlockSpec((LANES, ts), lambda b, h, i: (0, b * tiles + i)),
                  wcol(0), wcol(1), wcol(2)],
        out_specs=[tok, tok, tok, tok,
                   pl.BlockSpec((None, None, ts, DN_CHUNK), lambda b, h, i: (b, h, i, 0))],
        out_shape=[tok_shape, tok_shape, tok_shape, tok_shape,
                   jax.ShapeDtypeStruct((bsz, DN_HEADS, seq, DN_CHUNK), BF16)],
        compiler_params=_params(3),
        name="delta_prep",
    )(main3, main3, main3, main3, main3, main3, auxc3, auxr, conv_w_t, conv_w_t, conv_w_t)

    tsc = min(DELTA_SCAN_TILE, seq)
    wide = pl.BlockSpec((None, tsc, DN_WIDTH), lambda b, i: (b, i, 0))
    return pl.pallas_call(
        _delta_scan_kernel,
        grid=(bsz, seq // tsc),
        in_specs=[wide, wide, wide, wide,
                  pl.BlockSpec((None, DN_HEADS, tsc, DN_CHUNK), lambda b, i: (b, 0, i, 0)),
                  pl.BlockSpec((None, tsc, DN_WIDTH), lambda b, i: (b, i, 3)),
                  pl.BlockSpec((None, tsc, LANES), lambda b, i: (b, i, 0)),
                  pl.BlockSpec((1, DN_DIM), lambda b, i: (0, 0))],
        out_specs=wide,
        out_shape=tok_shape,
        scratch_shapes=[pltpu.VMEM((DN_HEADS, DN_DIM, DN_DIM), F32)],
        compiler_params=_params(1, 1),
        name="delta_scan",
    )(u, w, qe, kd, a, main3, auxc3, gain)


def _mla_proj_kernel(cq_ref, ckv_ref, auxc_ref, pos_ref, qg_ref, kvg_ref, wq_ref, wkv_ref, rope_ref,
                     q_ref, k_ref, v_ref):
    scale = (MLA_NOPE + MLA_ROPE) ** -0.5
    ang = pos_ref[...].astype(F32) * rope_ref[0:1, :]
    cos_t = jnp.cos(ang) * rope_ref[1:2, :]
    sin_t = jnp.sin(ang) * rope_ref[2:3, :]
    lane = lax.broadcasted_iota(jnp.int32, ang.shape, 1)
    first = lane < MLA_ROPE // 2

    def rope(a):
        swapped = jnp.where(first, pltpu.roll(a, LANES - MLA_ROPE // 2, 1), pltpu.roll(a, MLA_ROPE // 2, 1))
        return a * cos_t + swapped * sin_t

    ql = _bdot(_rms(cq_ref[...].astype(F32), qg_ref[...]), wq_ref[...]) * scale
    kv = _bdot(_rms(ckv_ref[...].astype(F32), kvg_ref[...]), wkv_ref[...])
    kpe = rope(auxc_ref[...]).astype(k_ref.dtype)
    for h in range(MLA_HEADS):
        o = h * MLA_QK_PAD
        q_ref[:, o:o + MLA_NOPE] = ql[:, o:o + MLA_NOPE].astype(q_ref.dtype)
        q_ref[:, o + MLA_NOPE:o + MLA_QK_PAD] = rope(ql[:, o + MLA_NOPE:o + MLA_QK_PAD]).astype(q_ref.dtype)
        k_ref[:, o:o + MLA_NOPE] = kv[:, h * MLA_NOPE:(h + 1) * MLA_NOPE].astype(k_ref.dtype)
        k_ref[:, o + MLA_NOPE:o + MLA_QK_PAD] = kpe
    v_ref[...] = kv[:, MLA_HEADS * MLA_NOPE:].T.astype(v_ref.dtype)


def _mla_proj(main, auxc, pos, q_gain, kv_gain, wq, wkv, rope_tab, seq):
    t = main.shape[0]
    tm = min(512, seq)
    per_b = seq // tm
    cq_blk = (4 * DN_WIDTH) // MLA_Q_RANK
    hq = MLA_HEADS * MLA_QK_PAD
    hv = MLA_HEADS * MLA_V
    return pl.pallas_call(
        _mla_proj_kernel,
        grid=(t // tm,),
        in_specs=[pl.BlockSpec((tm, MLA_Q_RANK), lambda i: (i, cq_blk)),
                  pl.BlockSpec((tm, MLA_KV_RANK), lambda i: (i, cq_blk + 1)),
                  pl.BlockSpec((tm, LANES), lambda i: (i, 0)),
                  pl.BlockSpec((tm, 1), lambda i: (i, 0)),
                  pl.BlockSpec((1, MLA_Q_RANK), lambda i: (0, 0)),
                  pl.BlockSpec((1, MLA_KV_RANK), lambda i: (0, 0)),
                  pl.BlockSpec((MLA_Q_RANK, hq), lambda i: (0, 0)),
                  pl.BlockSpec((MLA_KV_RANK, 2 * hv), lambda i: (0, 0)),
                  pl.BlockSpec((8, LANES), lambda i: (0, 0))],
        out_specs=[pl.BlockSpec((tm, hq), lambda i: (i, 0)),
                   pl.BlockSpec((tm, hq), lambda i: (i, 0)),
                   pl.BlockSpec((None, hv, tm), lambda i: (i // per_b, 0, i % per_b))],
        out_shape=[jax.ShapeDtypeStruct((t, hq), BF16),
                   jax.ShapeDtypeStruct((t, hq), BF16),
                   jax.ShapeDtypeStruct((t // seq, hv, seq), BF16)],
        compiler_params=_params(1),
        name="mla_proj",
    )(main, main, auxc, pos, q_gain, kv_gain, wq, wkv, rope_tab)


ATTN_PARTS = 4


def _attn_kernel(q_ref, k_ref, vt_ref, o_ref, m_ref, l_ref, acc_ref):
    qi = pl.program_id(2)
    n_parts = m_ref.shape[0]
    tk = m_ref.shape[2]
    q = [q_ref[p * tk:(p + 1) * tk, :] for p in range(n_parts)]
    m_ref[...] = jnp.full_like(m_ref, -jnp.inf)
    l_ref[...] = jnp.zeros_like(l_ref)
    acc_ref[...] = jnp.zeros_like(acc_ref)
    key = lax.broadcasted_iota(jnp.int32, (tk, tk), 0)
    query = lax.broadcasted_iota(jnp.int32, (tk, tk), 1)
    visible = key <= query

    def step(j, parts, masked_part):
        start = pl.multiple_of(j * tk, tk)
        kb = k_ref[pl.ds(start, tk), :]
        vtb = vt_ref[:, pl.ds(start, tk)]
        s = [lax.dot_general(kb, q[p], (((1,), (1,)), ((), ())), preferred_element_type=F32) for p in parts]
        s = [jnp.where(visible, x, -jnp.inf) if p == masked_part else x for p, x in zip(parts, s)]
        m_old = [m_ref[p] for p in parts]
        m_new = [jnp.maximum(mo, jnp.max(x, axis=0, keepdims=True)) for mo, x in zip(m_old, s)]
        e = [jnp.exp(x - mn) for x, mn in zip(s, m_new)]
        pv = [jnp.dot(vtb, x.astype(BF16), preferred_element_type=F32) for x in e]
        for i, p in enumerate(parts):
            alpha = jnp.exp(m_old[i] - m_new[i])
            l_ref[p] = alpha * l_ref[p] + jnp.sum(e[i], axis=0, keepdims=True)
            acc_ref[p] = alpha * acc_ref[p] + pv[i]
            m_ref[p] = m_new[i]

    every = tuple(range(n_parts))

    def body(j, carry):
        step(j, every, None)
        return carry

    lax.fori_loop(0, qi * n_parts, body, 0)
    for d in range(n_parts):
        step(qi * n_parts + d, every[d:], d)
    for p in every:
        o_ref[p * tk:(p + 1) * tk, :] = (acc_ref[p] / l_ref[p]).T.astype(o_ref.dtype)


def _attention(q3, k3, vt3):
    bsz, seq, _ = q3.shape
    tk = min(512, seq)
    n_parts = min(ATTN_PARTS, seq // tk)
    tq = tk * n_parts
    return pl.pallas_call(
        _attn_kernel,
        grid=(bsz, MLA_HEADS, seq // tq),
        in_specs=[pl.BlockSpec((None, tq, MLA_QK_PAD), lambda b, h, i: (b, i, h)),
                  pl.BlockSpec((None, seq, MLA_QK_PAD), lambda b, h, i: (b, 0, h)),
                  pl.BlockSpec((None, MLA_V, seq), lambda b, h, i: (b, h, 0))],
        out_specs=pl.BlockSpec((None, tq, MLA_V), lambda b, h, i: (b, i, h)),
        out_shape=jax.ShapeDtypeStruct((bsz, seq, MLA_HEADS * MLA_V), BF16),
        scratch_shapes=[pltpu.VMEM((n_parts, 1, tk), F32), pltpu.VMEM((n_parts, 1, tk), F32),
                        pltpu.VMEM((n_parts, MLA_V, tk), F32)],
        compiler_params=_params(3),
        name="attention",
    )(q3, k3, vt3)


def _out_proj_kernel(x_ref, dn_ref, mla_ref, ada_ref, gain_ref, w_ref, wr_ref, x1_ref, h2_ref, lg_ref):
    ada = ada_ref[0]
    half = dn_ref.shape[1]
    mix = (jnp.dot(dn_ref[...], w_ref[:half, :], preferred_element_type=F32)
           + jnp.dot(mla_ref[...], w_ref[half:, :], preferred_element_type=F32))
    x1 = x_ref[...] + ada[2:3, :] * mix
    x1_ref[...] = x1
    h2 = _rms(x1, gain_ref[...]) * (1.0 + ada[4:5, :]) + ada[3:4, :]
    h2_ref[...] = _pack_bf16_pairs(h2)
    h_hi = h2.astype(BF16)
    h_lo = (h2 - h_hi.astype(F32)).astype(BF16)
    both = jnp.dot(h_hi, wr_ref[...], preferred_element_type=F32)
    logits = both[:, :LANES] + both[:, LANES:] + jnp.dot(h_lo, wr_ref[:, :LANES], preferred_element_type=F32)
    lg_ref[...] = logits.T


def _out_proj(x2, dn, mla, ada3, gain, w_out, w_router, seq):
    t, d = x2.shape
    tm = min(256, seq)
    per_b = seq // tm
    half = dn.shape[1]
    return pl.pallas_call(
        _out_proj_kernel,
        grid=(t // tm,),
        in_specs=[pl.BlockSpec((tm, d), lambda i: (i, 0)),
                  pl.BlockSpec((tm, half), lambda i: (i, 0)),
                  pl.BlockSpec((tm, half), lambda i: (i, 0)),
                  pl.BlockSpec((1, 6, d), lambda i: (i // per_b, 0, 0)),
                  pl.BlockSpec((1, d), lambda i: (0, 0)),
                  pl.BlockSpec((2 * half, d), lambda i: (0, 0)),
                  pl.BlockSpec((d, 2 * LANES), lambda i: (0, 0))],
        out_specs=[pl.BlockSpec((tm, d), lambda i: (i, 0)),
                   pl.BlockSpec((tm, d // 2), lambda i: (i, 0)),
                   pl.BlockSpec((LANES, tm), lambda i: (0, i))],
        out_shape=[jax.ShapeDtypeStruct((t, d), F32),
                   jax.ShapeDtypeStruct((t, d // 2), jnp.uint32),
                   jax.ShapeDtypeStruct((LANES, t), F32)],
        compiler_params=_params(1),
        name="out_proj",
    )(x2, dn, mla, ada3, gain, w_out, w_router)


def _expert_kernel(be_ref, nv_ref, na_ref, x_ref, wgu_hbm, wd_hbm, after_hbm, y_ref,
                   gu_f, d_f, gu_s, d_s, slot_ref, sem):
    i = pl.program_id(0)
    n_blocks = pl.num_programs(0)
    active = nv_ref[i] > 0
    first = active & ((i == 0) | (be_ref[i] != be_ref[jnp.maximum(i - 1, 0)]))

    def fetch(e, slot):
        return (pltpu.make_async_copy(wgu_hbm.at[e], gu_f.at[slot], sem.at[slot, 0]),
                pltpu.make_async_copy(wd_hbm.at[e], d_f.at[slot], sem.at[slot, 1]))

    @pl.when(i == 0)
    def _():
        slot_ref[0] = 0
        for cp in fetch(be_ref[0], 0):
            cp.start()

    @pl.when(first)
    def _():
        cur = slot_ref[0]
        for cp in fetch(be_ref[i], cur):
            cp.wait()
        nxt = lax.while_loop(lambda j: (j < n_blocks) & (be_ref[jnp.minimum(j, n_blocks - 1)] == be_ref[i]),
                             lambda j: j + 1, i + 1)
        nxt_c = jnp.minimum(nxt, n_blocks - 1)

        @pl.when((nxt < n_blocks) & (nv_ref[nxt_c] > 0))
        def _():
            for cp in fetch(be_ref[nxt_c], 1 - cur):
                cp.start()

        gu_s[...] = gu_f[cur].astype(BF16)
        d_s[...] = d_f[cur].astype(BF16)
        slot_ref[0] = 1 - cur

    @pl.when(active)
    def _():
        x = _unpack_bf16_pairs(x_ref[...]).astype(BF16)
        ff = d_s.shape[0]
        chunks = [slice(c0, c0 + MXU_WIDTH) for c0 in range(0, ff, MXU_WIDTH)]
        gate_up = [(jnp.dot(x, gu_s[:, cs], preferred_element_type=F32),
                    jnp.dot(x, gu_s[:, slice(ff + cs.start, ff + cs.stop)], preferred_element_type=F32))
                   for cs in chunks]
        act = [(_silu(g) * u).astype(BF16) for g, u in gate_up]
        y = jnp.dot(act[0], d_s[chunks[0], :], preferred_element_type=F32)
        for a, cs in zip(act[1:], chunks[1:]):
            y = y + jnp.dot(a, d_s[cs, :], preferred_element_type=F32)
        row = lax.broadcasted_iota(jnp.int32, y.shape, 0)
        y_ref[...] = _pack_bf16_pairs(jnp.where(row < nv_ref[i], y, 0.0))


def _experts(block_e, n_valid, n_active, x_sorted, w_gu, w_d, after):
    n_rows, dp = x_sorted.shape
    ff2 = w_gu.shape[2]
    n_blocks = n_rows // EXPERT_BLOCK

    def block(i, be, nv, na):
        return (jnp.minimum(i, na[0] - 1), 0)

    grid_spec = pltpu.PrefetchScalarGridSpec(
        num_scalar_prefetch=3,
        grid=(n_blocks,),
        in_specs=[pl.BlockSpec((EXPERT_BLOCK, dp), block),
                  pl.BlockSpec(memory_space=pl.ANY),
                  pl.BlockSpec(memory_space=pl.ANY),
                  pl.BlockSpec(memory_space=pl.ANY)],
        out_specs=pl.BlockSpec((EXPERT_BLOCK, dp), block),
        scratch_shapes=[pltpu.VMEM((2, 2 * dp, ff2), F32), pltpu.VMEM((2, ff2 // 2, 2 * dp), F32),
                        pltpu.VMEM((2 * dp, ff2), BF16), pltpu.VMEM((ff2 // 2, 2 * dp), BF16),
                        pltpu.SMEM((1,), jnp.int32), pltpu.SemaphoreType.DMA((2, 2))],
    )
    return pl.pallas_call(
        _expert_kernel,
        grid_spec=grid_spec,
        out_shape=jax.ShapeDtypeStruct((n_rows, dp), jnp.uint32),
        compiler_params=_params(0, 1),
        name="experts",
    )(block_e, n_valid, n_active, x_sorted, w_gu, w_d, after)


def _shared_ffn_kernel(h2_ref, wgu_ref, wd_ref, o_ref):
    h2 = _unpack_bf16_pairs(h2_ref[...]).astype(BF16)
    gu = jnp.dot(h2, wgu_ref[...], preferred_element_type=F32)
    ff = gu.shape[1] // 2
    act = _silu(gu[:, :ff]) * gu[:, ff:]
    o_ref[...] = jnp.dot(act.astype(BF16), wd_ref[...], preferred_element_type=F32).astype(o_ref.dtype)


def _shared_ffn(h2, w_gu, w_d):
    t, dp = h2.shape
    d = 2 * dp
    tm = min(512, t)
    ff2 = w_gu.shape[1]
    return pl.pallas_call(
        _shared_ffn_kernel,
        grid=(t // tm,),
        in_specs=[pl.BlockSpec((tm, dp), lambda i: (i, 0)),
                  pl.BlockSpec((d, ff2), lambda i: (0, 0)),
                  pl.BlockSpec((ff2 // 2, d), lambda i: (0, 0))],
        out_specs=pl.BlockSpec((tm, d), lambda i: (i, 0)),
        out_shape=jax.ShapeDtypeStruct((t, d), BF16),
        compiler_params=_params(1),
        name="shared_ffn",
    )(h2, w_gu, w_d)


def _final_kernel(x1_ref, sh_ref, y_ref, wc_ref, ada_ref, gain_ref, *rest):
    o_ref = rest[-1]
    ada = ada_ref[0]
    ffn = sh_ref[...].astype(F32)
    wc = wc_ref[...]
    for kk in range(TOP_K):
        ffn = ffn + wc[:, kk:kk + 1] * _unpack_bf16_pairs(y_ref[kk])
    x2 = x1_ref[...] + ada[5:6, :] * ffn
    o_ref[...] = _rms(x2, gain_ref[...])


def _final_part(x1, shared, y_part, wc, ada3, gain, seq, part, n_parts, prev_out):
    t, d = x1.shape
    tm = min(256, seq)
    per_b = seq // tm
    steps = t // tm // n_parts
    off = part * steps
    in_specs = [pl.BlockSpec((tm, d), lambda i: (i + off, 0)),
                pl.BlockSpec((tm, d), lambda i: (i + off, 0)),
                pl.BlockSpec((TOP_K, tm, d // 2), lambda i: (0, i, 0)),
                pl.BlockSpec((tm, LANES), lambda i: (i + off, 0)),
                pl.BlockSpec((1, 6, d), lambda i: ((i + off) // per_b, 0, 0)),
                pl.BlockSpec((1, d), lambda i: (0, 0))]
    args = [x1, shared, y_part, wc, ada3, gain]
    aliases = {}
    if prev_out is not None:
        in_specs.append(pl.BlockSpec(memory_space=pl.ANY))
        args.append(prev_out)
        aliases = {len(args) - 1: 0}
    return pl.pallas_call(
        _final_kernel,
        grid=(steps,),
        in_specs=in_specs,
        out_specs=pl.BlockSpec((tm, d), lambda i: (i + off, 0)),
        out_shape=jax.ShapeDtypeStruct((t, d), F32),
        input_output_aliases=aliases,
        compiler_params=_params(1),
        name="final",
    )(*args)


ROUTE_TILE = 512
MAX_BLOCK_LANES = 512


def _first_argmax(x, idx_f, n):
    m = jnp.max(x, axis=0, keepdims=True)
    first = jnp.min(jnp.where(x == m, idx_f, float(n)), axis=0, keepdims=True)
    return m, first


def _route_kernel(lg_ref, bias_ref, ek_ref, rk_ref, wc_ref, ps_ref, be_ref, carry_ref, upper_ref):
    i = pl.program_id(0)
    tm = lg_ref.shape[1]
    per_group = N_EXPERTS // N_GROUPS

    @pl.when(i == 0)
    def _():
        carry_ref[...] = jnp.zeros_like(carry_ref)
        r = lax.broadcasted_iota(jnp.int32, (tm, tm), 0)
        cc = lax.broadcasted_iota(jnp.int32, (tm, tm), 1)
        upper_ref[...] = (r < cc).astype(BF16)

    scores = jax.nn.sigmoid(lg_ref[0:N_EXPERTS, :])
    biased = scores + bias_ref[:, 0:1]
    sub8 = lax.broadcasted_iota(jnp.int32, (per_group, tm), 0).astype(F32)
    group_rows = []
    for g in range(N_GROUPS):
        xg = biased[g * per_group:(g + 1) * per_group, :]
        m1, i1 = _first_argmax(xg, sub8, per_group)
        m2 = jnp.max(jnp.where(sub8 == i1, -jnp.inf, xg), axis=0, keepdims=True)
        group_rows.append(m1 + m2)
    cur = jnp.concatenate(group_rows, axis=0)
    gself = jnp.zeros(cur.shape, F32)
    for _ in range(TOPK_GROUPS):
        _, gi = _first_argmax(cur, sub8, N_GROUPS)
        hit = sub8 == gi
        gself = jnp.where(hit, 1.0, gself)
        cur = jnp.where(hit, -jnp.inf, cur)
    masked = jnp.concatenate(
        [jnp.where(gself[g:g + 1, :] > 0.5, biased[g * per_group:(g + 1) * per_group, :], -jnp.inf)
         for g in range(N_GROUPS)], axis=0)
    sub64 = lax.broadcasted_iota(jnp.int32, (N_EXPERTS, tm), 0).astype(F32)
    e_rows, s_rows = [], []
    sel = jnp.zeros(masked.shape, F32)
    for _ in range(TOP_K):
        _, ei = _first_argmax(masked, sub64, N_EXPERTS)
        hit = sub64 == ei
        e_rows.append(ei)
        s_rows.append(jnp.sum(jnp.where(hit, scores, 0.0), axis=0, keepdims=True))
        sel = jnp.where(hit, 1.0, sel)
        masked = jnp.where(hit, -jnp.inf, masked)
    total = s_rows[0]
    for s in s_rows[1:]:
        total = total + s
    w_rows = [s / total * ROUTED_SCALE for s in s_rows]
    before = jnp.dot(sel.astype(BF16), upper_ref[...], preferred_element_type=F32) + carry_ref[:, 0:1]
    r_rows = [jnp.sum(jnp.where(sub64 == ei, before, 0.0), axis=0, keepdims=True) for ei in e_rows]
    zrow = jnp.zeros((8 - TOP_K, tm), F32)
    ek_ref[...] = jnp.concatenate(e_rows + [zrow], axis=0).astype(jnp.int32)
    rk_ref[...] = jnp.concatenate(r_rows + [zrow], axis=0).astype(jnp.int32)
    wc_ref[...] = jnp.concatenate(w_rows + [jnp.zeros((LANES - TOP_K, tm), F32)], axis=0).T
    carry_ref[...] = carry_ref[...] + jnp.sum(sel, axis=1, keepdims=True)

    @pl.when(i == pl.num_programs(0) - 1)
    def _():
        counts = carry_ref[...]
        padded = jnp.floor((counts + (EXPERT_BLOCK - 1.0)) * (1.0 / EXPERT_BLOCK)) * EXPERT_BLOCK
        r = lax.broadcasted_iota(jnp.int32, (N_EXPERTS, N_EXPERTS), 0)
        cc = lax.broadcasted_iota(jnp.int32, (N_EXPERTS, N_EXPERTS), 1)
        pad_end = jnp.dot((cc <= r).astype(F32), padded, precision=lax.Precision.HIGHEST,
                          preferred_element_type=F32)
        pad_start = pad_end - padded
        ps_ref[...] = pad_start
        blk0 = lax.broadcasted_iota(jnp.int32, (N_EXPERTS, MAX_BLOCK_LANES), 1).astype(F32) * EXPERT_BLOCK
        e_of_blk = jnp.minimum(jnp.sum((pad_end[:, 0:1] <= blk0).astype(F32), axis=0, keepdims=True),
                               N_EXPERTS - 1.0)
        sub = lax.broadcasted_iota(jnp.int32, (N_EXPERTS, MAX_BLOCK_LANES), 0).astype(F32)
        seg_end = jnp.sum(jnp.where(sub == e_of_blk, (pad_start + counts)[:, 0:1], 0.0), axis=0, keepdims=True)
        n_valid = jnp.clip(seg_end - blk0[0:1, :], 0.0, float(EXPERT_BLOCK))
        n_active = jnp.broadcast_to(jnp.sum((n_valid > 0.0).astype(F32), axis=1, keepdims=True), n_valid.shape)
        be_ref[...] = jnp.concatenate([e_of_blk, n_valid, n_active, jnp.zeros((5, MAX_BLOCK_LANES), F32)],
                                      axis=0).astype(jnp.int32)


def _route_pallas(logits_t, bias_col):
    t = logits_t.shape[1]
    tm = min(ROUTE_TILE, t)
    row8 = pl.BlockSpec((8, tm), lambda i: (0, i))
    return pl.pallas_call(
        _route_kernel,
        grid=(t // tm,),
        in_specs=[pl.BlockSpec((LANES, tm), lambda i: (0, i)),
                  pl.BlockSpec((N_EXPERTS, 1), lambda i: (0, 0))],
        out_specs=[row8, row8,
                   pl.BlockSpec((tm, LANES), lambda i: (i, 0)),
                   pl.BlockSpec((N_EXPERTS, LANES), lambda i: (0, 0)),
                   pl.BlockSpec((8, MAX_BLOCK_LANES), lambda i: (0, 0))],
        out_shape=[jax.ShapeDtypeStruct((8, t), jnp.int32),
                   jax.ShapeDtypeStruct((8, t), jnp.int32),
                   jax.ShapeDtypeStruct((t, LANES), F32),
                   jax.ShapeDtypeStruct((N_EXPERTS, LANES), F32),
                   jax.ShapeDtypeStruct((8, MAX_BLOCK_LANES), jnp.int32)],
        scratch_shapes=[pltpu.VMEM((N_EXPERTS, LANES), F32), pltpu.VMEM((tm, tm), BF16)],
        compiler_params=_params(0, 1),
        name="route",
    )(logits_t, bias_col)


def _slot_kernel(ek_ref, rk_ref, ps_ref, slot_ref):
    tm = ek_ref.shape[1]
    sub64 = lax.broadcasted_iota(jnp.int32, (N_EXPERTS, tm), 0)
    start = ps_ref[:, 0:1]
    rows = []
    for kk in range(TOP_K):
        seg = jnp.sum(jnp.where(sub64 == ek_ref[kk:kk + 1, :], start, 0.0), axis=0, keepdims=True)
        rows.append(seg.astype(jnp.int32) + rk_ref[kk:kk + 1, :])
    slot_ref[...] = jnp.concatenate(rows + [jnp.zeros((8 - TOP_K, tm), jnp.int32)], axis=0)


def _slots(ek, rk, ps):
    t = ek.shape[1]
    tm = min(ROUTE_TILE, t)
    row8 = pl.BlockSpec((8, tm), lambda i: (0, i))
    return pl.pallas_call(
        _slot_kernel,
        grid=(t // tm,),
        in_specs=[row8, row8, pl.BlockSpec((N_EXPERTS, LANES), lambda i: (0, 0))],
        out_specs=row8,
        out_shape=jax.ShapeDtypeStruct((8, t), jnp.int32),
        compiler_params=_params(1),
        name="slots",
    )(ek, rk, ps)


SC_ROWS = 32
FINAL_PARTS = 2


def _sc_mesh():
    info = plsc.get_sparse_core_info()
    mesh = plsc.VectorSubcoreMesh(core_axis_name="c", subcore_axis_name="s")
    return mesh, info.num_cores, info.num_cores * info.num_subcores


def _sc_dispatch(x, slot_flat, n_rows):
    t, dp = x.shape
    n_slots = slot_flat.shape[0] // t
    mesh, n_cores, n_workers = _sc_mesh()
    per_w = t // n_workers
    n_chunks = per_w // SC_ROWS
    assert per_w * n_workers == t and n_chunks * SC_ROWS == per_w and n_chunks % 2 == 0

    @functools.partial(
        pl.kernel, out_type=jax.ShapeDtypeStruct((n_rows, dp), x.dtype), mesh=mesh,
        scratch_types=[pltpu.VMEM((2, n_slots, SC_ROWS), jnp.int32), pltpu.VMEM((2, SC_ROWS, dp), x.dtype),
                       pltpu.SemaphoreType.DMA((2,)), pltpu.SemaphoreType.DMA((2,))])
    def dispatch(x_hbm, i_hbm, o_hbm, idx_v, rows_v, read_sem, scatter_sem):
        base = (lax.axis_index("s") * n_cores + lax.axis_index("c")) * per_w

        def read(j, b):
            return pltpu.make_async_copy(x_hbm.at[pl.ds(pl.multiple_of(base + j * SC_ROWS, 8), SC_ROWS)],
                                         rows_v.at[b], read_sem.at[b])

        def scatter(k, b):
            return pltpu.make_async_copy(rows_v.at[b], o_hbm.at[idx_v.at[b, k]], scatter_sem.at[b])

        read(0, 0).start()

        @pl.loop(0, n_chunks, step=2)
        def _(j):
            for b in range(2):
                jj = j + b

                @pl.when(jj >= 1)
                def _():
                    for k in range(n_slots):
                        scatter(k, 1 - b).wait()

                @pl.when(jj + 1 < n_chunks)
                def _():
                    read(jj + 1, 1 - b).start()

                for k in range(n_slots):
                    pltpu.sync_copy(i_hbm.at[pl.ds(pl.multiple_of(k * t + base + jj * SC_ROWS, 8), SC_ROWS)],
                                    idx_v.at[b, k])
                read(jj, b).wait()
                for k in range(n_slots):
                    scatter(k, b).start()

        for k in range(n_slots):
            scatter(k, (n_chunks - 1) % 2).wait()

    return dispatch(x, slot_flat)


def _sc_combine(y, slot_flat):
    n = slot_flat.shape[0]
    dp = y.shape[1]
    mesh, n_cores, n_workers = _sc_mesh()
    per_w = n // n_workers
    n_chunks = per_w // SC_ROWS
    assert per_w * n_workers == n and n_chunks * SC_ROWS == per_w and n_chunks % 2 == 0

    @functools.partial(
        pl.kernel, out_type=jax.ShapeDtypeStruct((n, dp), y.dtype), mesh=mesh,
        scratch_types=[pltpu.VMEM((per_w,), jnp.int32), pltpu.VMEM((2, SC_ROWS, dp), y.dtype),
                       pltpu.SemaphoreType.DMA((2,)), pltpu.SemaphoreType.DMA((2,))])
    def combine(y_hbm, i_hbm, o_hbm, idx_v, rows_v, gather_sem, write_sem):
        base = (lax.axis_index("s") * n_cores + lax.axis_index("c")) * per_w
        pltpu.sync_copy(i_hbm.at[pl.ds(pl.multiple_of(base, 8), per_w)], idx_v)

        def gather(j, b):
            return pltpu.make_async_copy(y_hbm.at[idx_v.at[pl.ds(pl.multiple_of(j * SC_ROWS, 8), SC_ROWS)]],
                                         rows_v.at[b], gather_sem.at[b])

        def write(j, b):
            return pltpu.make_async_copy(rows_v.at[b],
                                         o_hbm.at[pl.ds(pl.multiple_of(base + j * SC_ROWS, 8), SC_ROWS)],
                                         write_sem.at[b])

        gather(0, 0).start()

        @pl.loop(0, n_chunks, step=2)
        def _(j):
            for b in range(2):
                jj = j + b

                @pl.when(jj >= 1)
                def _():
                    write(jj - 1, 1 - b).wait()

                @pl.when(jj + 1 < n_chunks)
                def _():
                    gather(jj + 1, 1 - b).start()

                gather(jj, b).wait()
                write(jj, b).start()

        write(n_chunks - 1, (n_chunks - 1) % 2).wait()

    return combine(y, slot_flat)


def _layer(x, cond_ada, positions, w_in, dn_conv_w, dn_a_log, dn_dt_bias, dn_norm_gain, mla_q_norm_gain,
           w_q_up, mla_kv_norm_gain, w_kv_up, w_out, norm1_gain, norm2_gain, w_router, router_bias,
           w_exp_gate_up, w_exp_down, w_sh_gate_up, w_sh_down, out_gain):
    bsz, seq, d = x.shape
    t = bsz * seq
    x2 = x.reshape(t, d)
    ada3 = cond_ada.reshape(bsz, 6, d)

    s_z = 4 * DN_WIDTH
    s_a = s_z + 2 * DN_HEADS
    s_kpe = s_a + MLA_Q_RANK + MLA_KV_RANK
    w_main = jnp.concatenate([w_in[:, :s_z], w_in[:, s_a:s_kpe]], axis=1).astype(BF16)
    w_aux = jnp.concatenate([w_in[:, s_kpe:], w_in[:, s_z:s_a],
                             jnp.zeros((d, LANES - MLA_ROPE - 2 * DN_HEADS), F32)], axis=1).astype(BF16)
    auxp = jnp.zeros((8, LANES), F32)
    auxp = auxp.at[0, AUX_G:AUX_G + DN_HEADS].set(dn_a_log).at[1, AUX_G:AUX_G + DN_HEADS].set(dn_dt_bias)
    main, auxc, auxr = _in_proj(x2, ada3, norm1_gain.reshape(1, d), w_main, w_aux, auxp, seq)

    dn = _delta(main.reshape(bsz, seq, MAIN_WIDTH), auxc.reshape(bsz, seq, LANES), auxr,
                dn_conv_w.T, dn_norm_gain.reshape(1, DN_DIM))

    qk = MLA_NOPE + MLA_ROPE
    wq3 = w_q_up.reshape(MLA_Q_RANK, MLA_HEADS, qk)
    wq = jnp.concatenate([wq3, jnp.zeros((MLA_Q_RANK, MLA_HEADS, MLA_QK_PAD - qk), F32)], axis=2)
    wq = wq.reshape(MLA_Q_RANK, MLA_HEADS * MLA_QK_PAD).astype(BF16)
    wkv3 = w_kv_up.reshape(MLA_KV_RANK, MLA_HEADS, MLA_NOPE + MLA_V)
    wkv = jnp.concatenate([wkv3[:, :, :MLA_NOPE].reshape(MLA_KV_RANK, -1),
                           wkv3[:, :, MLA_NOPE:].reshape(MLA_KV_RANK, -1)], axis=1).astype(BF16)
    half = MLA_ROPE // 2
    inv_freq = ROPE_THETA ** (-jnp.arange(half, dtype=F32) / half)
    zeros = jnp.zeros((LANES - MLA_ROPE,), F32)
    rope_tab = jnp.zeros((8, LANES), F32)
    rope_tab = rope_tab.at[0].set(jnp.concatenate([inv_freq, inv_freq, zeros]))
    rope_tab = rope_tab.at[1].set(jnp.concatenate([jnp.ones((MLA_ROPE,), F32), zeros]))
    rope_tab = rope_tab.at[2].set(jnp.concatenate([-jnp.ones((half,), F32), jnp.ones((half,), F32), zeros]))
    q, k, v = _mla_proj(main, auxc, positions.reshape(t, 1), mla_q_norm_gain.reshape(1, -1),
                        mla_kv_norm_gain.reshape(1, -1), wq, wkv, rope_tab, seq)
    mla = _attention(q.reshape(bsz, seq, -1), k.reshape(bsz, seq, -1), v)

    w_r = jnp.concatenate([w_router, jnp.zeros((d, LANES - N_EXPERTS), F32)], axis=1)
    w_r_hi = w_r.astype(BF16)
    w_r = jnp.concatenate([w_r_hi, (w_r - w_r_hi.astype(F32)).astype(BF16)], axis=1)
    x1, h2, logits_t = _out_proj(x2, dn.reshape(t, DN_WIDTH), mla.reshape(t, -1), ada3,
                                 norm2_gain.reshape(1, d), w_out.astype(BF16), w_r, seq)

    n_rows = -(-(t * TOP_K + N_EXPERTS * (EXPERT_BLOCK - 1)) // EXPERT_BLOCK) * EXPERT_BLOCK
    n_blocks = n_rows // EXPERT_BLOCK
    assert n_blocks <= MAX_BLOCK_LANES
    ek, rk, wc, seg_start, blocks = _route_pallas(logits_t, router_bias.reshape(N_EXPERTS, 1))
    slot = _slots(ek, rk, seg_start)[:TOP_K]
    x_sorted = _sc_dispatch(h2, slot.reshape(-1), n_rows)
    shared = _shared_ffn(h2, w_sh_gate_up.astype(BF16), w_sh_down.astype(BF16))
    y_sorted = _experts(blocks[0, :n_blocks], blocks[1, :n_blocks], blocks[2, :1], x_sorted,
                        w_exp_gate_up, w_exp_down, shared)

    n_parts = FINAL_PARTS if (t // min(256, seq)) % FINAL_PARTS == 0 else 1
    out = None
    for part in range(n_parts):
        part_slots = slot[:, part * (t // n_parts):(part + 1) * (t // n_parts)].reshape(-1)
        y_part = _sc_combine(y_sorted, part_slots).reshape(TOP_K, t // n_parts, d // 2)
        out = _final_part(x1, shared, y_part, wc, ada3, out_gain.reshape(1, d), seq, part, n_parts, out)
    return out.reshape(bsz, seq, d)


def kernel(x, c, positions, w_ada, b_ada, norm1_gain, w_in, dn_conv_w, dn_a_log, dn_dt_bias, dn_norm_gain,
           mla_q_norm_gain, w_q_up, mla_kv_norm_gain, w_kv_up, w_out, norm2_gain, w_router, router_bias,
           w_exp_gate_up, w_exp_down, w_sh_gate_up, w_sh_down, final_norm_gain):
    depth = w_ada.shape[0]
    assert depth == 1, "the final RMSNorm is fused into the single layer's last kernel"
    ada = _ada(c, w_ada[0], b_ada[0])
    return _layer(x, ada, positions, w_in[0], dn_conv_w[0], dn_a_log[0], dn_dt_bias[0], dn_norm_gain[0],
                  mla_q_norm_gain[0], w_q_up[0], mla_kv_norm_gain[0], w_kv_up[0], w_out[0], norm1_gain[0],
                  norm2_gain[0], w_router[0], router_bias[0], w_exp_gate_up[0], w_exp_down[0],
                  w_sh_gate_up[0], w_sh_down[0], final_norm_gain)
```

```python
import functools

import jax
import jax.numpy as jnp
from jax import lax
from jax.experimental import pallas as pl
from jax.experimental.pallas import tpu as pltpu
from jax.experimental.pallas import tpu_sc as plsc

F32 = jnp.float32
BF16 = jnp.bfloat16

DN_HEADS = 8
DN_DIM = 128
DN_WIDTH = DN_HEADS * DN_DIM
DN_CONV = 4
DN_CHUNK = 64
MLA_HEADS = 8
MLA_Q_RANK = 512
MLA_KV_RANK = 512
MLA_NOPE = 128
MLA_ROPE = 64
MLA_V = 128
MLA_QK_PAD = 256
ROPE_THETA = 10000.0
N_EXPERTS = 64
TOP_K = 6
N_GROUPS = 8
TOPK_GROUPS = 4
EXPERT_FF = 512
ROUTED_SCALE = 2.5
EXPERT_BLOCK = 512
RMS_EPS = 1e-6
L2_EPS = 1e-6

LANES = 128
MXU_WIDTH = 256
MAIN_WIDTH = 4 * DN_WIDTH + MLA_Q_RANK + MLA_KV_RANK
AUX_BETA = MLA_ROPE
AUX_G = MLA_ROPE + DN_HEADS
VMEM_LIMIT = 56 * 1024 * 1024


def _params(n_parallel, n_arbitrary=0):
    sem = ("parallel",) * n_parallel + ("arbitrary",) * n_arbitrary
    return pltpu.CompilerParams(dimension_semantics=sem, vmem_limit_bytes=VMEM_LIMIT)


def _silu(x):
    return x * jax.nn.sigmoid(x)


def _bdot(a, b):
    return jnp.dot(a.astype(BF16), b.astype(BF16), preferred_element_type=F32)


def _bdot_nt(a, b):
    return lax.dot_general(a.astype(BF16), b.astype(BF16), (((1,), (1,)), ((), ())),
                           preferred_element_type=F32)


def _rms(x, gain):
    return x * lax.rsqrt(jnp.mean(x * x, axis=-1, keepdims=True) + RMS_EPS) * gain


HIGH_HALF = 0xFFFF0000


def _pack_bf16_pairs(x):
    n = x.shape[1] // 2
    bits = lax.bitcast_convert_type(x.astype(BF16).astype(F32), jnp.uint32)
    return (bits[:, :n] >> 16) | (bits[:, n:] & jnp.uint32(HIGH_HALF))


def _unpack_bf16_pairs(p):
    lo = lax.bitcast_convert_type(p << 16, F32)
    hi = lax.bitcast_convert_type(p & jnp.uint32(HIGH_HALF), F32)
    return jnp.concatenate([lo, hi], axis=1)


def _ada_kernel(c_ref, w_ref, b_ref, o_ref):
    o_ref[...] = _bdot(_silu(c_ref[...]), w_ref[...]) + b_ref[...]


def _ada(c, w_ada, b_ada):
    bsz, d = c.shape
    n = w_ada.shape[1]
    tn = 1024
    return pl.pallas_call(
        _ada_kernel,
        grid=(n // tn,),
        in_specs=[pl.BlockSpec((bsz, d), lambda j: (0, 0)),
                  pl.BlockSpec((d, tn), lambda j: (0, j)),
                  pl.BlockSpec((1, tn), lambda j: (0, j))],
        out_specs=pl.BlockSpec((bsz, tn), lambda j: (0, j)),
        out_shape=jax.ShapeDtypeStruct((bsz, n), F32),
        compiler_params=_params(1),
        name="ada",
    )(c, w_ada, b_ada.reshape(1, n))


def _in_proj_kernel(x_ref, ada_ref, gain_ref, wm_ref, wa_ref, auxp_ref,
                    main_ref, auxc_ref, auxr_ref, h_ref):
    j = pl.program_id(1)

    @pl.when(j == 0)
    def _():
        ada = ada_ref[0]
        h = _rms(x_ref[...], gain_ref[...]) * (1.0 + ada[1:2, :]) + ada[0:1, :]
        hb = h.astype(BF16)
        h_ref[...] = hb
        aux = jnp.dot(hb, wa_ref[...], preferred_element_type=F32)
        tm = aux.shape[0]
        lane = lax.broadcasted_iota(jnp.int32, aux.shape, 1)
        is_beta = (lane >= AUX_BETA) & (lane < AUX_G)
        is_g = (lane >= AUX_G) & (lane < AUX_G + DN_HEADS)
        a_log = auxp_ref[0:1, :]
        dt_bias = auxp_ref[1:2, :]
        sp_in = aux + dt_bias
        softplus = jnp.maximum(sp_in, 0.0) + jnp.log(1.0 + jnp.exp(-jnp.abs(sp_in)))
        g = -jnp.exp(a_log) * softplus
        aux = jnp.where(is_beta, jax.nn.sigmoid(aux), jnp.where(is_g, g, aux))
        r = lax.broadcasted_iota(jnp.int32, (LANES, LANES), 0)
        cidx = lax.broadcasted_iota(jnp.int32, (LANES, LANES), 1)
        tri = ((cidx <= r) & (cidx // DN_CHUNK == r // DN_CHUNK)).astype(F32)
        g_cols = (cidx >= AUX_G) & (cidx < AUX_G + DN_HEADS)
        parts = []
        for t in range(tm // LANES):
            blk = aux[t * LANES:(t + 1) * LANES, :]
            cs = jnp.dot(tri, blk, precision=lax.Precision.HIGHEST, preferred_element_type=F32)
            parts.append(jnp.where(g_cols, cs, blk))
        aux = jnp.concatenate(parts, axis=0)
        auxc_ref[...] = aux
        auxr_ref[...] = aux.T

    main_ref[...] = jnp.dot(h_ref[...], wm_ref[...], preferred_element_type=F32).astype(main_ref.dtype)


def _in_proj(x2, ada3, gain, w_main, w_aux, auxp, seq):
    t, d = x2.shape
    n = w_main.shape[1]
    tm = min(1024, seq)
    tn = 1024
    per_b = seq // tm
    return pl.pallas_call(
        _in_proj_kernel,
        grid=(t // tm, n // tn),
        in_specs=[pl.BlockSpec((tm, d), lambda i, j: (i, 0)),
                  pl.BlockSpec((1, 6, d), lambda i, j: (i // per_b, 0, 0)),
                  pl.BlockSpec((1, d), lambda i, j: (0, 0)),
                  pl.BlockSpec((d, tn), lambda i, j: (0, j)),
                  pl.BlockSpec((d, LANES), lambda i, j: (0, 0)),
                  pl.BlockSpec((8, LANES), lambda i, j: (0, 0))],
        out_specs=[pl.BlockSpec((tm, tn), lambda i, j: (i, j)),
                   pl.BlockSpec((tm, LANES), lambda i, j: (i, 0)),
                   pl.BlockSpec((LANES, tm), lambda i, j: (0, i))],
        out_shape=[jax.ShapeDtypeStruct((t, n), BF16),
                   jax.ShapeDtypeStruct((t, LANES), F32),
                   jax.ShapeDtypeStruct((LANES, t), F32)],
        scratch_shapes=[pltpu.VMEM((tm, d), BF16)],
        compiler_params=_params(1, 1),
        name="in_proj",
    )(x2, ada3, gain, w_main, w_aux, auxp)


def _unit_lower_inverses(ms):
    c = ms[0].shape[0]
    r = lax.broadcasted_iota(jnp.int32, (c, c), 0)
    cc = lax.broadcasted_iota(jnp.int32, (c, c), 1)
    eye = (r == cc).astype(F32)
    same = (r // 16) == (cc // 16)
    md = [jnp.where(same, m, 0.0) for m in ms]
    mo = [jnp.where(same, 0.0, m) for m in ms]
    p1 = [_bdot(a, a) for a in md]
    p2 = [_bdot(a, a) for a in p1]
    p3 = [_bdot(a, a) for a in p2]
    td = [eye - a for a in md]
    td = [t + _bdot(t, p) for t, p in zip(td, p1)]
    td = [t + _bdot(t, p) for t, p in zip(td, p2)]
    td = [t + _bdot(t, p) for t, p in zip(td, p3)]
    n1 = [_bdot(t, o) for t, o in zip(td, mo)]
    n2 = [_bdot(n, n) for n in n1]
    left = [(eye - a) + _bdot(eye - a, b) for a, b in zip(n1, n2)]
    return [_bdot(l, t) for l, t in zip(left, td)]


DELTA_TILE = 2048
DELTA_SCAN_TILE = 512


def _delta_prep_kernel(q_ref, k_ref, v_ref, qh_ref, kh_ref, vh_ref, auxc_ref, auxr_ref, wq_ref, wk_ref, wv_ref,
                       u_ref, w_ref, qe_ref, kd_ref, a_ref, pad_ref):
    head = pl.program_id(1)
    first_tile = pl.program_id(2) == 0
    ts = q_ref.shape[0]
    c = DN_CHUNK
    halo = qh_ref.shape[0]

    def conv_silu(x_ref, h_ref, w_ref):
        top = pad_ref.shape[0] - ts
        pad_ref[0:top, :] = jnp.where(first_tile, 0.0, h_ref[halo - top:, :].astype(F32))
        pad_ref[top:, :] = x_ref[...].astype(F32)
        w = w_ref[...]
        acc = pad_ref[top:, :] * w[DN_CONV - 1:DN_CONV, :]
        for s in range(1, DN_CONV):
            acc = acc + pad_ref[top - s:top - s + ts, :] * w[DN_CONV - 1 - s:DN_CONV - s, :]
        return _silu(acc)

    q = conv_silu(q_ref, qh_ref, wq_ref)
    k = conv_silu(k_ref, kh_ref, wk_ref)
    v = conv_silu(v_ref, vh_ref, wv_ref)
    q = q * lax.rsqrt(jnp.sum(q * q, axis=-1, keepdims=True) + L2_EPS) * (DN_DIM ** -0.5)
    k = k * lax.rsqrt(jnp.sum(k * k, axis=-1, keepdims=True) + L2_EPS)

    auxc = auxc_ref[...]
    lane = lax.broadcasted_iota(jnp.int32, auxc.shape, 1)
    beta_all = jnp.sum(jnp.where(lane == AUX_BETA + head, auxc, 0.0), axis=1, keepdims=True)
    gcol_all = jnp.sum(jnp.where(lane == AUX_G + head, auxc, 0.0), axis=1, keepdims=True)
    grows = auxr_ref[AUX_G:AUX_G + DN_HEADS, :]
    head_row = lax.broadcasted_iota(jnp.int32, grows.shape, 0)
    grow_all = jnp.sum(jnp.where(head_row == head, grows, 0.0), axis=0, keepdims=True)

    ri = lax.broadcasted_iota(jnp.int32, (c, c), 0)
    ci = lax.broadcasted_iota(jnp.int32, (c, c), 1)
    causal = ri >= ci
    strict = ri > ci
    chunks = [slice(j * c, (j + 1) * c) for j in range(ts // c)]
    decay = [jnp.exp(jnp.where(causal, gcol_all[rows] - grow_all[:, rows], -jnp.inf)) for rows in chunks]
    kb = [k[rows] * beta_all[rows] for rows in chunks]
    both = [_bdot_nt(jnp.concatenate([kb_j, q[rows]], axis=0), k[rows])
            for kb_j, rows in zip(kb, chunks)]
    tinv = _unit_lower_inverses([jnp.where(strict, b[:c] * d, 0.0) for b, d in zip(both, decay)])
    eg = [jnp.exp(gcol_all[rows]) for rows in chunks]
    sol = [_bdot(t, jnp.concatenate([v[rows] * beta_all[rows], kb_j * e], axis=1))
           for t, rows, kb_j, e in zip(tinv, chunks, kb, eg)]
    for j, rows in enumerate(chunks):
        gcol = gcol_all[rows]
        u_ref[rows, :] = sol[j][:, :DN_DIM].astype(u_ref.dtype)
        w_ref[rows, :] = sol[j][:, DN_DIM:].astype(w_ref.dtype)
        qe_ref[rows, :] = (q[rows] * eg[j]).astype(qe_ref.dtype)
        kd_ref[rows, :] = (k[rows] * jnp.exp(gcol[c - 1:c, :] - gcol)).astype(kd_ref.dtype)
        a_ref[rows, :] = jnp.where(causal, both[j][c:] * decay[j], 0.0).astype(a_ref.dtype)


def _delta_scan_kernel(u_ref, w_ref, qe_ref, kd_ref, a_ref, z_ref, auxc_ref, gain_ref, o_ref, st_ref):
    seq = u_ref.shape[0]
    n_heads = a_ref.shape[0]
    c = DN_CHUNK

    @pl.when(pl.program_id(1) == 0)
    def _():
        st_ref[...] = jnp.zeros_like(st_ref)

    gain = gain_ref[...]
    lane = lax.broadcasted_iota(jnp.int32, (1, LANES), 1)

    def scan_chunk(i, carry):
        start = pl.multiple_of(i * c, c)
        rows = pl.ds(start, c)
        last = auxc_ref[pl.ds(start + c - 1, 1), :]
        heads = range(n_heads)
        cols = [slice(hh * DN_DIM, (hh + 1) * DN_DIM) for hh in heads]
        state = [st_ref[hh] for hh in heads]
        sb = [s.astype(BF16) for s in state]
        ws = [jnp.dot(w_ref[rows, cl], s, preferred_element_type=F32) for cl, s in zip(cols, sb)]
        qs = [jnp.dot(qe_ref[rows, cl], s, preferred_element_type=F32) for cl, s in zip(cols, sb)]
        vb = [(u_ref[rows, cl].astype(F32) - x).astype(BF16) for cl, x in zip(cols, ws)]
        kv = [lax.dot_general(kd_ref[rows, cl], x, (((0,), (0,)), ((), ())), preferred_element_type=F32)
              for cl, x in zip(cols, vb)]
        av = [jnp.dot(a_ref[hh, rows, :], x, preferred_element_type=F32) for hh, x in zip(heads, vb)]
        for hh in heads:
            glast = jnp.sum(jnp.where(lane == AUX_G + hh, last, 0.0), axis=1, keepdims=True)
            st_ref[hh] = state[hh] * jnp.exp(glast) + kv[hh]
            z = z_ref[rows, cols[hh]].astype(F32)
            o_ref[rows, cols[hh]] = (_rms(qs[hh] + av[hh], gain) * _silu(z)).astype(o_ref.dtype)
        return carry

    lax.fori_loop(0, seq // c, scan_chunk, 0)


def _delta(main3, auxc3, auxr, conv_w_t, gain):
    bsz, seq, _ = main3.shape
    hb = DN_WIDTH // DN_DIM
    ts = min(DELTA_TILE, seq)
    halo = 16
    tiles = seq // ts

    def col(offset):
        return pl.BlockSpec((None, ts, DN_DIM), lambda b, h, i: (b, i, offset * hb + h))

    def halo_col(offset):
        return pl.BlockSpec((None, halo, DN_DIM),
                            lambda b, h, i: (b, jnp.maximum(i * (ts // halo) - 1, 0), offset * hb + h))

    def wcol(offset):
        return pl.BlockSpec((DN_CONV, DN_DIM), lambda b, h, i: (0, offset * hb + h))

    tok = pl.BlockSpec((None, ts, DN_DIM), lambda b, h, i: (b, i, h))
    tok_shape = jax.ShapeDtypeStruct((bsz, seq, DN_WIDTH), BF16)
    u, w, qe, kd, a = pl.pallas_call(
        _delta_prep_kernel,
        grid=(bsz, DN_HEADS, tiles),
        in_specs=[col(0), col(1), col(2), halo_col(0), halo_col(1), halo_col(2),
                  pl.BlockSpec((None, ts, LANES), lambda b, h, i: (b, i, 0)),
                  pl.BlockSpec((LANES, ts), lambda b, h, i: (0, b * tiles + i)),
                  wcol(0), wcol(1), wcol(2)],
        out_specs=[tok, tok, tok, tok,
                   pl.BlockSpec((None, None, ts, DN_CHUNK), lambda b, h, i: (b, h, i, 0))],
        out_shape=[tok_shape, tok_shape, tok_shape, tok_shape,
                   jax.ShapeDtypeStruct((bsz, DN_HEADS, seq, DN_CHUNK), BF16)],
        scratch_shapes=[pltpu.VMEM((ts + 8, DN_DIM), F32)],
        compiler_params=_params(3),
        name="delta_prep",
    )(main3, main3, main3, main3, main3, main3, auxc3, auxr, conv_w_t, conv_w_t, conv_w_t)

    tsc = min(DELTA_SCAN_TILE, seq)
    wide = pl.BlockSpec((None, tsc, DN_WIDTH), lambda b, i: (b, i, 0))
    return pl.pallas_call(
        _delta_scan_kernel,
        grid=(bsz, seq // tsc),
        in_specs=[wide, wide, wide, wide,
                  pl.BlockSpec((None, DN_HEADS, tsc, DN_CHUNK), lambda b, i: (b, 0, i, 0)),
                  pl.BlockSpec((None, tsc, DN_WIDTH), lambda b, i: (b, i, 3)),
                  pl.BlockSpec((None, tsc, LANES), lambda b, i: (b, i, 0)),
                  pl.BlockSpec((1, DN_DIM), lambda b, i: (0, 0))],
        out_specs=wide,
        out_shape=tok_shape,
        scratch_shapes=[pltpu.VMEM((DN_HEADS, DN_DIM, DN_DIM), F32)],
        compiler_params=_params(1, 1),
        name="delta_scan",
    )(u, w, qe, kd, a, main3, auxc3, gain)


def _mla_proj_kernel(cq_ref, ckv_ref, auxc_ref, pos_ref, qg_ref, kvg_ref, wq_ref, wkv_ref, rope_ref,
                     q_ref, k_ref, v_ref):
    scale = (MLA_NOPE + MLA_ROPE) ** -0.5
    ang = pos_ref[...].astype(F32) * rope_ref[0:1, :]
    cos_t = jnp.cos(ang) * rope_ref[1:2, :]
    sin_t = jnp.sin(ang) * rope_ref[2:3, :]
    lane = lax.broadcasted_iota(jnp.int32, ang.shape, 1)
    first = lane < MLA_ROPE // 2

    def rope(a):
        swapped = jnp.where(first, pltpu.roll(a, LANES - MLA_ROPE // 2, 1), pltpu.roll(a, MLA_ROPE // 2, 1))
        return a * cos_t + swapped * sin_t

    ql = _bdot(_rms(cq_ref[...].astype(F32), qg_ref[...]), wq_ref[...]) * scale
    kv = _bdot(_rms(ckv_ref[...].astype(F32), kvg_ref[...]), wkv_ref[...])
    kpe = rope(auxc_ref[...]).astype(k_ref.dtype)
    for h in range(MLA_HEADS):
        o = h * MLA_QK_PAD
        q_ref[:, o:o + MLA_NOPE] = ql[:, o:o + MLA_NOPE].astype(q_ref.dtype)
        q_ref[:, o + MLA_NOPE:o + MLA_QK_PAD] = rope(ql[:, o + MLA_NOPE:o + MLA_QK_PAD]).astype(q_ref.dtype)
        k_ref[:, o:o + MLA_NOPE] = kv[:, h * MLA_NOPE:(h + 1) * MLA_NOPE].astype(k_ref.dtype)
        k_ref[:, o + MLA_NOPE:o + MLA_QK_PAD] = kpe
    v_ref[...] = kv[:, MLA_HEADS * MLA_NOPE:].T.astype(v_ref.dtype)


def _mla_proj(main, auxc, pos, q_gain, kv_gain, wq, wkv, rope_tab, seq):
    t = main.shape[0]
    tm = min(512, seq)
    per_b = seq // tm
    cq_blk = (4 * DN_WIDTH) // MLA_Q_RANK
    hq = MLA_HEADS * MLA_QK_PAD
    hv = MLA_HEADS * MLA_V
    return pl.pallas_call(
        _mla_proj_kernel,
        grid=(t // tm,),
        in_specs=[pl.BlockSpec((tm, MLA_Q_RANK), lambda i: (i, cq_blk)),
                  pl.BlockSpec((tm, MLA_KV_RANK), lambda i: (i, cq_blk + 1)),
                  pl.BlockSpec((tm, LANES), lambda i: (i, 0)),
                  pl.BlockSpec((tm, 1), lambda i: (i, 0)),
                  pl.BlockSpec((1, MLA_Q_RANK), lambda i: (0, 0)),
                  pl.BlockSpec((1, MLA_KV_RANK), lambda i: (0, 0)),
                  pl.BlockSpec((MLA_Q_RANK, hq), lambda i: (0, 0)),
                  pl.BlockSpec((MLA_KV_RANK, 2 * hv), lambda i: (0, 0)),
                  pl.BlockSpec((8, LANES), lambda i: (0, 0))],
        out_specs=[pl.BlockSpec((tm, hq), lambda i: (i, 0)),
                   pl.BlockSpec((tm, hq), lambda i: (i, 0)),
                   pl.BlockSpec((None, hv, tm), lambda i: (i // per_b, 0, i % per_b))],
        out_shape=[jax.ShapeDtypeStruct((t, hq), BF16),
                   jax.ShapeDtypeStruct((t, hq), BF16),
                   jax.ShapeDtypeStruct((t // seq, hv, seq), BF16)],
        compiler_params=_params(1),
        name="mla_proj",
    )(main, main, auxc, pos, q_gain, kv_gain, wq, wkv, rope_tab)


ATTN_PARTS = 4


def _attn_kernel(q_ref, k_ref, vt_ref, o_ref, m_ref, l_ref, acc_ref):
    qi = pl.program_id(2)
    n_parts = m_ref.shape[0]
    tk = m_ref.shape[2]
    q = [q_ref[p * tk:(p + 1) * tk, :] for p in range(n_parts)]
    m_ref[...] = jnp.full_like(m_ref, -jnp.inf)
    l_ref[...] = jnp.zeros_like(l_ref)
    acc_ref[...] = jnp.zeros_like(acc_ref)
    key = lax.broadcasted_iota(jnp.int32, (tk, tk), 0)
    query = lax.broadcasted_iota(jnp.int32, (tk, tk), 1)
    visible = key <= query

    def step(j, parts, masked_part):
        start = pl.multiple_of(j * tk, tk)
        kb = k_ref[pl.ds(start, tk), :]
        vtb = vt_ref[:, pl.ds(start, tk)]
        s = [lax.dot_general(kb, q[p], (((1,), (1,)), ((), ())), preferred_element_type=F32) for p in parts]
        s = [jnp.where(visible, x, -jnp.inf) if p == masked_part else x for p, x in zip(parts, s)]
        m_old = [m_ref[p] for p in parts]
        m_new = [jnp.maximum(mo, jnp.max(x, axis=0, keepdims=True)) for mo, x in zip(m_old, s)]
        e = [jnp.exp(x - mn) for x, mn in zip(s, m_new)]
        pv = [jnp.dot(vtb, x.astype(BF16), preferred_element_type=F32) for x in e]
        for i, p in enumerate(parts):
            alpha = jnp.exp(m_old[i] - m_new[i])
            l_ref[p] = alpha * l_ref[p] + jnp.sum(e[i], axis=0, keepdims=True)
            acc_ref[p] = alpha * acc_ref[p] + pv[i]
            m_ref[p] = m_new[i]

    every = tuple(range(n_parts))

    def body(j, carry):
        step(j, every, None)
        return carry

    lax.fori_loop(0, qi * n_parts, body, 0)
    for d in range(n_parts):
        step(qi * n_parts + d, every[d:], d)
    for p in every:
        o_ref[p * tk:(p + 1) * tk, :] = (acc_ref[p] / l_ref[p]).T.astype(o_ref.dtype)


def _attention(q3, k3, vt3):
    bsz, seq, _ = q3.shape
    tk = min(512, seq)
    n_parts = min(ATTN_PARTS, seq // tk)
    tq = tk * n_parts
    return pl.pallas_call(
        _attn_kernel,
        grid=(bsz, MLA_HEADS, seq // tq),
        in_specs=[pl.BlockSpec((None, tq, MLA_QK_PAD), lambda b, h, i: (b, i, h)),
                  pl.BlockSpec((None, seq, MLA_QK_PAD), lambda b, h, i: (b, 0, h)),
                  pl.BlockSpec((None, MLA_V, seq), lambda b, h, i: (b, h, 0))],
        out_specs=pl.BlockSpec((None, tq, MLA_V), lambda b, h, i: (b, i, h)),
        out_shape=jax.ShapeDtypeStruct((bsz, seq, MLA_HEADS * MLA_V), BF16),
        scratch_shapes=[pltpu.VMEM((n_parts, 1, tk), F32), pltpu.VMEM((n_parts, 1, tk), F32),
                        pltpu.VMEM((n_parts, MLA_V, tk), F32)],
        compiler_params=_params(3),
        name="attention",
    )(q3, k3, vt3)


def _out_proj_kernel(x_ref, dn_ref, mla_ref, ada_ref, gain_ref, w_ref, wr_ref, x1_ref, h2_ref, lg_ref):
    ada = ada_ref[0]
    half = dn_ref.shape[1]
    mix = (jnp.dot(dn_ref[...], w_ref[:half, :], preferred_element_type=F32)
           + jnp.dot(mla_ref[...], w_ref[half:, :], preferred_element_type=F32))
    x1 = x_ref[...] + ada[2:3, :] * mix
    x1_ref[...] = x1
    h2 = _rms(x1, gain_ref[...]) * (1.0 + ada[4:5, :]) + ada[3:4, :]
    h2_ref[...] = _pack_bf16_pairs(h2)
    h_hi = h2.astype(BF16)
    h_lo = (h2 - h_hi.astype(F32)).astype(BF16)
    both = jnp.dot(h_hi, wr_ref[...], preferred_element_type=F32)
    logits = both[:, :LANES] + both[:, LANES:] + jnp.dot(h_lo, wr_ref[:, :LANES], preferred_element_type=F32)
    lg_ref[...] = logits.T


def _out_proj(x2, dn, mla, ada3, gain, w_out, w_router, seq):
    t, d = x2.shape
    tm = min(256, seq)
    per_b = seq // tm
    half = dn.shape[1]
    return pl.pallas_call(
        _out_proj_kernel,
        grid=(t // tm,),
        in_specs=[pl.BlockSpec((tm, d), lambda i: (i, 0)),
                  pl.BlockSpec((tm, half), lambda i: (i, 0)),
                  pl.BlockSpec((tm, half), lambda i: (i, 0)),
                  pl.BlockSpec((1, 6, d), lambda i: (i // per_b, 0, 0)),
                  pl.BlockSpec((1, d), lambda i: (0, 0)),
                  pl.BlockSpec((2 * half, d), lambda i: (0, 0)),
                  pl.BlockSpec((d, 2 * LANES), lambda i: (0, 0))],
        out_specs=[pl.BlockSpec((tm, d), lambda i: (i, 0)),
                   pl.BlockSpec((tm, d // 2), lambda i: (i, 0)),
                   pl.BlockSpec((LANES, tm), lambda i: (0, i))],
        out_shape=[jax.ShapeDtypeStruct((t, d), F32),
                   jax.ShapeDtypeStruct((t, d // 2), jnp.uint32),
                   jax.ShapeDtypeStruct((LANES, t), F32)],
        compiler_params=_params(1),
        name="out_proj",
    )(x2, dn, mla, ada3, gain, w_out, w_router)


def _expert_kernel(be_ref, nv_ref, na_ref, x_ref, wgu_hbm, wd_hbm, after_hbm, y_ref,
                   gu_f, d_f, gu_s, d_s, slot_ref, sem):
    i = pl.program_id(0)
    n_blocks = pl.num_programs(0)
    active = nv_ref[i] > 0
    first = active & ((i == 0) | (be_ref[i] != be_ref[jnp.maximum(i - 1, 0)]))

    def fetch(e, slot):
        return (pltpu.make_async_copy(wgu_hbm.at[e], gu_f.at[slot], sem.at[slot, 0]),
                pltpu.make_async_copy(wd_hbm.at[e], d_f.at[slot], sem.at[slot, 1]))

    @pl.when(i == 0)
    def _():
        slot_ref[0] = 0
        for cp in fetch(be_ref[0], 0):
            cp.start()

    @pl.when(first)
    def _():
        cur = slot_ref[0]
        for cp in fetch(be_ref[i], cur):
            cp.wait()
        nxt = lax.while_loop(lambda j: (j < n_blocks) & (be_ref[jnp.minimum(j, n_blocks - 1)] == be_ref[i]),
                             lambda j: j + 1, i + 1)
        nxt_c = jnp.minimum(nxt, n_blocks - 1)

        @pl.when((nxt < n_blocks) & (nv_ref[nxt_c] > 0))
        def _():
            for cp in fetch(be_ref[nxt_c], 1 - cur):
                cp.start()

        gu_s[...] = gu_f[cur].astype(BF16)
        d_s[...] = d_f[cur].astype(BF16)
        slot_ref[0] = 1 - cur

    @pl.when(active)
    def _():
        x = _unpack_bf16_pairs(x_ref[...]).astype(BF16)
        ff = d_s.shape[0]
        chunks = [slice(c0, c0 + MXU_WIDTH) for c0 in range(0, ff, MXU_WIDTH)]
        gate_up = [(jnp.dot(x, gu_s[:, cs], preferred_element_type=F32),
                    jnp.dot(x, gu_s[:, slice(ff + cs.start, ff + cs.stop)], preferred_element_type=F32))
                   for cs in chunks]
        act = [(_silu(g) * u).astype(BF16) for g, u in gate_up]
        y = jnp.dot(act[0], d_s[chunks[0], :], preferred_element_type=F32)
        for a, cs in zip(act[1:], chunks[1:]):
            y = y + jnp.dot(a, d_s[cs, :], preferred_element_type=F32)
        row = lax.broadcasted_iota(jnp.int32, y.shape, 0)
        y_ref[...] = _pack_bf16_pairs(jnp.where(row < nv_ref[i], y, 0.0))


def _experts(block_e, n_valid, n_active, x_sorted, w_gu, w_d, after):
    n_rows, dp = x_sorted.shape
    ff2 = w_gu.shape[2]
    n_blocks = n_rows // EXPERT_BLOCK

    def block(i, be, nv, na):
        return (jnp.minimum(i, na[0] - 1), 0)

    grid_spec = pltpu.PrefetchScalarGridSpec(
        num_scalar_prefetch=3,
        grid=(n_blocks,),
        in_specs=[pl.BlockSpec((EXPERT_BLOCK, dp), block),
                  pl.BlockSpec(memory_space=pl.ANY),
                  pl.BlockSpec(memory_space=pl.ANY),
                  pl.BlockSpec(memory_space=pl.ANY)],
        out_specs=pl.BlockSpec((EXPERT_BLOCK, dp), block),
        scratch_shapes=[pltpu.VMEM((2, 2 * dp, ff2), F32), pltpu.VMEM((2, ff2 // 2, 2 * dp), F32),
                        pltpu.VMEM((2 * dp, ff2), BF16), pltpu.VMEM((ff2 // 2, 2 * dp), BF16),
                        pltpu.SMEM((1,), jnp.int32), pltpu.SemaphoreType.DMA((2, 2))],
    )
    return pl.pallas_call(
        _expert_kernel,
        grid_spec=grid_spec,
        out_shape=jax.ShapeDtypeStruct((n_rows, dp), jnp.uint32),
        compiler_params=_params(0, 1),
        name="experts",
    )(block_e, n_valid, n_active, x_sorted, w_gu, w_d, after)


def _shared_ffn_kernel(h2_ref, wgu_ref, wd_ref, o_ref):
    h2 = _unpack_bf16_pairs(h2_ref[...]).astype(BF16)
    gu = jnp.dot(h2, wgu_ref[...], preferred_element_type=F32)
    ff = gu.shape[1] // 2
    act = _silu(gu[:, :ff]) * gu[:, ff:]
    o_ref[...] = jnp.dot(act.astype(BF16), wd_ref[...], preferred_element_type=F32).astype(o_ref.dtype)


def _shared_ffn(h2, w_gu, w_d):
    t, dp = h2.shape
    d = 2 * dp
    tm = min(512, t)
    ff2 = w_gu.shape[1]
    return pl.pallas_call(
        _shared_ffn_kernel,
        grid=(t // tm,),
        in_specs=[pl.BlockSpec((tm, dp), lambda i: (i, 0)),
                  pl.BlockSpec((d, ff2), lambda i: (0, 0)),
                  pl.BlockSpec((ff2 // 2, d), lambda i: (0, 0))],
        out_specs=pl.BlockSpec((tm, d), lambda i: (i, 0)),
        out_shape=jax.ShapeDtypeStruct((t, d), BF16),
        compiler_params=_params(1),
        name="shared_ffn",
    )(h2, w_gu, w_d)


def _final_kernel(x1_ref, sh_ref, y_ref, wc_ref, ada_ref, gain_ref, *rest):
    o_ref = rest[-1]
    ada = ada_ref[0]
    ffn = sh_ref[...].astype(F32)
    wc = wc_ref[...]
    for kk in range(TOP_K):
        ffn = ffn + wc[:, kk:kk + 1] * _unpack_bf16_pairs(y_ref[kk])
    x2 = x1_ref[...] + ada[5:6, :] * ffn
    o_ref[...] = _rms(x2, gain_ref[...])


def _final_part(x1, shared, y_part, wc, ada3, gain, seq, part, n_parts, prev_out):
    t, d = x1.shape
    tm = min(256, seq)
    per_b = seq // tm
    steps = t // tm // n_parts
    off = part * steps
    in_specs = [pl.BlockSpec((tm, d), lambda i: (i + off, 0)),
                pl.BlockSpec((tm, d), lambda i: (i + off, 0)),
                pl.BlockSpec((TOP_K, tm, d // 2), lambda i: (0, i, 0)),
                pl.BlockSpec((tm, LANES), lambda i: (i + off, 0)),
                pl.BlockSpec((1, 6, d), lambda i: ((i + off) // per_b, 0, 0)),
                pl.BlockSpec((1, d), lambda i: (0, 0))]
    args = [x1, shared, y_part, wc, ada3, gain]
    aliases = {}
    if prev_out is not None:
        in_specs.append(pl.BlockSpec(memory_space=pl.ANY))
        args.append(prev_out)
        aliases = {len(args) - 1: 0}
    return pl.pallas_call(
        _final_kernel,
        grid=(steps,),
        in_specs=in_specs,
        out_specs=pl.BlockSpec((tm, d), lambda i: (i + off, 0)),
        out_shape=jax.ShapeDtypeStruct((t, d), F32),
        input_output_aliases=aliases,
        compiler_params=_params(1),
        name="final",
    )(*args)


ROUTE_TILE = 512
MAX_BLOCK_LANES = 512


def _first_argmax(x, idx_f, n):
    m = jnp.max(x, axis=0, keepdims=True)
    first = jnp.min(jnp.where(x == m, idx_f, float(n)), axis=0, keepdims=True)
    return m, first


def _route_kernel(lg_ref, bias_ref, ek_ref, rk_ref, wc_ref, ps_ref, be_ref, carry_ref, upper_ref):
    i = pl.program_id(0)
    tm = lg_ref.shape[1]
    per_group = N_EXPERTS // N_GROUPS

    @pl.when(i == 0)
    def _():
        carry_ref[...] = jnp.zeros_like(carry_ref)
        r = lax.broadcasted_iota(jnp.int32, (tm, tm), 0)
        cc = lax.broadcasted_iota(jnp.int32, (tm, tm), 1)
        upper_ref[...] = (r < cc).astype(BF16)

    scores = jax.nn.sigmoid(lg_ref[0:N_EXPERTS, :])
    biased = scores + bias_ref[:, 0:1]
    sub8 = lax.broadcasted_iota(jnp.int32, (per_group, tm), 0).astype(F32)
    group_rows = []
    for g in range(N_GROUPS):
        xg = biased[g * per_group:(g + 1) * per_group, :]
        m1, i1 = _first_argmax(xg, sub8, per_group)
        m2 = jnp.max(jnp.where(sub8 == i1, -jnp.inf, xg), axis=0, keepdims=True)
        group_rows.append(m1 + m2)
    cur = jnp.concatenate(group_rows, axis=0)
    gself = jnp.zeros(cur.shape, F32)
    for _ in range(TOPK_GROUPS):
        _, gi = _first_argmax(cur, sub8, N_GROUPS)
        hit = sub8 == gi
        gself = jnp.where(hit, 1.0, gself)
        cur = jnp.where(hit, -jnp.inf, cur)
    masked = jnp.concatenate(
        [jnp.where(gself[g:g + 1, :] > 0.5, biased[g * per_group:(g + 1) * per_group, :], -jnp.inf)
         for g in range(N_GROUPS)], axis=0)
    sub64 = lax.broadcasted_iota(jnp.int32, (N_EXPERTS, tm), 0).astype(F32)
    e_rows, s_rows = [], []
    sel = jnp.zeros(masked.shape, F32)
    for _ in range(TOP_K):
        _, ei = _first_argmax(masked, sub64, N_EXPERTS)
        hit = sub64 == ei
        e_rows.append(ei)
        s_rows.append(jnp.sum(jnp.where(hit, scores, 0.0), axis=0, keepdims=True))
        sel = jnp.where(hit, 1.0, sel)
        masked = jnp.where(hit, -jnp.inf, masked)
    total = s_rows[0]
    for s in s_rows[1:]:
        total = total + s
    w_rows = [s / total * ROUTED_SCALE for s in s_rows]
    before = jnp.dot(sel.astype(BF16), upper_ref[...], preferred_element_type=F32) + carry_ref[:, 0:1]
    r_rows = [jnp.sum(jnp.where(sub64 == ei, before, 0.0), axis=0, keepdims=True) for ei in e_rows]
    zrow = jnp.zeros((8 - TOP_K, tm), F32)
    ek_ref[...] = jnp.concatenate(e_rows + [zrow], axis=0).astype(jnp.int32)
    rk_ref[...] = jnp.concatenate(r_rows + [zrow], axis=0).astype(jnp.int32)
    wc_ref[...] = jnp.concatenate(w_rows + [jnp.zeros((LANES - TOP_K, tm), F32)], axis=0).T
    carry_ref[...] = carry_ref[...] + jnp.sum(sel, axis=1, keepdims=True)

    @pl.when(i == pl.num_programs(0) - 1)
    def _():
        counts = carry_ref[...]
        padded = jnp.floor((counts + (EXPERT_BLOCK - 1.0)) * (1.0 / EXPERT_BLOCK)) * EXPERT_BLOCK
        r = lax.broadcasted_iota(jnp.int32, (N_EXPERTS, N_EXPERTS), 0)
        cc = lax.broadcasted_iota(jnp.int32, (N_EXPERTS, N_EXPERTS), 1)
        pad_end = jnp.dot((cc <= r).astype(F32), padded, precision=lax.Precision.HIGHEST,
                          preferred_element_type=F32)
        pad_start = pad_end - padded
        ps_ref[...] = pad_start
        blk0 = lax.broadcasted_iota(jnp.int32, (N_EXPERTS, MAX_BLOCK_LANES), 1).astype(F32) * EXPERT_BLOCK
        e_of_blk = jnp.minimum(jnp.sum((pad_end[:, 0:1] <= blk0).astype(F32), axis=0, keepdims=True),
                               N_EXPERTS - 1.0)
        sub = lax.broadcasted_iota(jnp.int32, (N_EXPERTS, MAX_BLOCK_LANES), 0).astype(F32)
        seg_end = jnp.sum(jnp.where(sub == e_of_blk, (pad_start + counts)[:, 0:1], 0.0), axis=0, keepdims=True)
        n_valid = jnp.clip(seg_end - blk0[0:1, :], 0.0, float(EXPERT_BLOCK))
        n_active = jnp.broadcast_to(jnp.sum((n_valid > 0.0).astype(F32), axis=1, keepdims=True), n_valid.shape)
        be_ref[...] = jnp.concatenate([e_of_blk, n_valid, n_active, jnp.zeros((5, MAX_BLOCK_LANES), F32)],
                                      axis=0).astype(jnp.int32)


def _route_pallas(logits_t, bias_col):
    t = logits_t.shape[1]
    tm = min(ROUTE_TILE, t)
    row8 = pl.BlockSpec((8, tm), lambda i: (0, i))
    return pl.pallas_call(
        _route_kernel,
        grid=(t // tm,),
        in_specs=[pl.BlockSpec((LANES, tm), lambda i: (0, i)),
                  pl.BlockSpec((N_EXPERTS, 1), lambda i: (0, 0))],
        out_specs=[row8, row8,
                   pl.BlockSpec((tm, LANES), lambda i: (i, 0)),
                   pl.BlockSpec((N_EXPERTS, LANES), lambda i: (0, 0)),
                   pl.BlockSpec((8, MAX_BLOCK_LANES), lambda i: (0, 0))],
        out_shape=[jax.ShapeDtypeStruct((8, t), jnp.int32),
                   jax.ShapeDtypeStruct((8, t), jnp.int32),
                   jax.ShapeDtypeStruct((t, LANES), F32),
                   jax.ShapeDtypeStruct((N_EXPERTS, LANES), F32),
                   jax.ShapeDtypeStruct((8, MAX_BLOCK_LANES), jnp.int32)],
        scratch_shapes=[pltpu.VMEM((N_EXPERTS, LANES), F32), pltpu.VMEM((tm, tm), BF16)],
        compiler_params=_params(0, 1),
        name="route",
    )(logits_t, bias_col)


def _slot_kernel(ek_ref, rk_ref, ps_ref, slot_ref):
    tm = ek_ref.shape[1]
    sub64 = lax.broadcasted_iota(jnp.int32, (N_EXPERTS, tm), 0)
    start = ps_ref[:, 0:1]
    rows = []
    for kk in range(TOP_K):
        seg = jnp.sum(jnp.where(sub64 == ek_ref[kk:kk + 1, :], start, 0.0), axis=0, keepdims=True)
        rows.append(seg.astype(jnp.int32) + rk_ref[kk:kk + 1, :])
    slot_ref[...] = jnp.concatenate(rows + [jnp.zeros((8 - TOP_K, tm), jnp.int32)], axis=0)


def _slots(ek, rk, ps):
    t = ek.shape[1]
    tm = min(ROUTE_TILE, t)
    row8 = pl.BlockSpec((8, tm), lambda i: (0, i))
    return pl.pallas_call(
        _slot_kernel,
        grid=(t // tm,),
        in_specs=[row8, row8, pl.BlockSpec((N_EXPERTS, LANES), lambda i: (0, 0))],
        out_specs=row8,
        out_shape=jax.ShapeDtypeStruct((8, t), jnp.int32),
        compiler_params=_params(1),
        name="slots",
    )(ek, rk, ps)


SC_ROWS = 32
FINAL_PARTS = 4


def _sc_mesh():
    info = plsc.get_sparse_core_info()
    mesh = plsc.VectorSubcoreMesh(core_axis_name="c", subcore_axis_name="s")
    return mesh, info.num_cores, info.num_cores * info.num_subcores


def _sc_dispatch(x, slot_flat, n_rows):
    t, dp = x.shape
    n_slots = slot_flat.shape[0] // t
    mesh, n_cores, n_workers = _sc_mesh()
    per_w = t // n_workers
    n_chunks = per_w // SC_ROWS
    assert per_w * n_workers == t and n_chunks * SC_ROWS == per_w and n_chunks % 2 == 0

    @functools.partial(
        pl.kernel, out_type=jax.ShapeDtypeStruct((n_rows, dp), x.dtype), mesh=mesh,
        scratch_types=[pltpu.VMEM((2, n_slots, SC_ROWS), jnp.int32), pltpu.VMEM((2, SC_ROWS, dp), x.dtype),
                       pltpu.SemaphoreType.DMA((2,)), pltpu.SemaphoreType.DMA((2,))])
    def dispatch(x_hbm, i_hbm, o_hbm, idx_v, rows_v, read_sem, scatter_sem):
        base = (lax.axis_index("s") * n_cores + lax.axis_index("c")) * per_w

        def read(j, b):
            return pltpu.make_async_copy(x_hbm.at[pl.ds(pl.multiple_of(base + j * SC_ROWS, 8), SC_ROWS)],
                                         rows_v.at[b], read_sem.at[b])

        def scatter(k, b):
            return pltpu.make_async_copy(rows_v.at[b], o_hbm.at[idx_v.at[b, k]], scatter_sem.at[b])

        read(0, 0).start()

        @pl.loop(0, n_chunks, step=2)
        def _(j):
            for b in range(2):
                jj = j + b

                @pl.when(jj >= 1)
                def _():
                    for k in range(n_slots):
                        scatter(k, 1 - b).wait()

                @pl.when(jj + 1 < n_chunks)
                def _():
                    read(jj + 1, 1 - b).start()

                for k in range(n_slots):
                    pltpu.sync_copy(i_hbm.at[pl.ds(pl.multiple_of(k * t + base + jj * SC_ROWS, 8), SC_ROWS)],
                                    idx_v.at[b, k])
                read(jj, b).wait()
                for k in range(n_slots):
                    scatter(k, b).start()

        for k in range(n_slots):
            scatter(k, (n_chunks - 1) % 2).wait()

    return dispatch(x, slot_flat)


def _sc_combine(y, slot_flat):
    n = slot_flat.shape[0]
    dp = y.shape[1]
    mesh, n_cores, n_workers = _sc_mesh()
    per_w = n // n_workers
    n_chunks = per_w // SC_ROWS
    assert per_w * n_workers == n and n_chunks * SC_ROWS == per_w and n_chunks % 2 == 0

    @functools.partial(
        pl.kernel, out_type=jax.ShapeDtypeStruct((n, dp), y.dtype), mesh=mesh,
        scratch_types=[pltpu.VMEM((per_w,), jnp.int32), pltpu.VMEM((2, SC_ROWS, dp), y.dtype),
                       pltpu.SemaphoreType.DMA((2,)), pltpu.SemaphoreType.DMA((2,))])
    def combine(y_hbm, i_hbm, o_hbm, idx_v, rows_v, gather_sem, write_sem):
        base = (lax.axis_index("s") * n_cores + lax.axis_index("c")) * per_w
        pltpu.sync_copy(i_hbm.at[pl.ds(pl.multiple_of(base, 8), per_w)], idx_v)

        def gather(j, b):
            return pltpu.make_async_copy(y_hbm.at[idx_v.at[pl.ds(pl.multiple_of(j * SC_ROWS, 8), SC_ROWS)]],
                                         rows_v.at[b], gather_sem.at[b])

        def write(j, b):
            return pltpu.make_async_copy(rows_v.at[b],
                                         o_hbm.at[pl.ds(pl.multiple_of(base + j * SC_ROWS, 8), SC_ROWS)],
                                         write_sem.at[b])

        gather(0, 0).start()

        @pl.loop(0, n_chunks, step=2)
        def _(j):
            for b in range(2):
                jj = j + b

                @pl.when(jj >= 1)
                def _():
                    write(jj - 1, 1 - b).wait()

                @pl.when(jj + 1 < n_chunks)
                def _():
                    gather(jj + 1, 1 - b).start()

                gather(jj, b).wait()
                write(jj, b).start()

        write(n_chunks - 1, (n_chunks - 1) % 2).wait()

    return combine(y, slot_flat)


def _layer(x, cond_ada, positions, w_in, dn_conv_w, dn_a_log, dn_dt_bias, dn_norm_gain, mla_q_norm_gain,
           w_q_up, mla_kv_norm_gain, w_kv_up, w_out, norm1_gain, norm2_gain, w_router, router_bias,
           w_exp_gate_up, w_exp_down, w_sh_gate_up, w_sh_down, out_gain):
    bsz, seq, d = x.shape
    t = bsz * seq
    x2 = x.reshape(t, d)
    ada3 = cond_ada.reshape(bsz, 6, d)

    s_z = 4 * DN_WIDTH
    s_a = s_z + 2 * DN_HEADS
    s_kpe = s_a + MLA_Q_RANK + MLA_KV_RANK
    w_main = jnp.concatenate([w_in[:, :s_z], w_in[:, s_a:s_kpe]], axis=1).astype(BF16)
    w_aux = jnp.concatenate([w_in[:, s_kpe:], w_in[:, s_z:s_a],
                             jnp.zeros((d, LANES - MLA_ROPE - 2 * DN_HEADS), F32)], axis=1).astype(BF16)
    auxp = jnp.zeros((8, LANES), F32)
    auxp = auxp.at[0, AUX_G:AUX_G + DN_HEADS].set(dn_a_log).at[1, AUX_G:AUX_G + DN_HEADS].set(dn_dt_bias)
    main, auxc, auxr = _in_proj(x2, ada3, norm1_gain.reshape(1, d), w_main, w_aux, auxp, seq)

    dn = _delta(main.reshape(bsz, seq, MAIN_WIDTH), auxc.reshape(bsz, seq, LANES), auxr,
                dn_conv_w.T, dn_norm_gain.reshape(1, DN_DIM))

    qk = MLA_NOPE + MLA_ROPE
    wq3 = w_q_up.reshape(MLA_Q_RANK, MLA_HEADS, qk)
    wq = jnp.concatenate([wq3, jnp.zeros((MLA_Q_RANK, MLA_HEADS, MLA_QK_PAD - qk), F32)], axis=2)
    wq = wq.reshape(MLA_Q_RANK, MLA_HEADS * MLA_QK_PAD).astype(BF16)
    wkv3 = w_kv_up.reshape(MLA_KV_RANK, MLA_HEADS, MLA_NOPE + MLA_V)
    wkv = jnp.concatenate([wkv3[:, :, :MLA_NOPE].reshape(MLA_KV_RANK, -1),
                           wkv3[:, :, MLA_NOPE:].reshape(MLA_KV_RANK, -1)], axis=1).astype(BF16)
    half = MLA_ROPE // 2
    inv_freq = ROPE_THETA ** (-jnp.arange(half, dtype=F32) / half)
    zeros = jnp.zeros((LANES - MLA_ROPE,), F32)
    rope_tab = jnp.zeros((8, LANES), F32)
    rope_tab = rope_tab.at[0].set(jnp.concatenate([inv_freq, inv_freq, zeros]))
    rope_tab = rope_tab.at[1].set(jnp.concatenate([jnp.ones((MLA_ROPE,), F32), zeros]))
    rope_tab = rope_tab.at[2].set(jnp.concatenate([-jnp.ones((half,), F32), jnp.ones((half,), F32), zeros]))
    q, k, v = _mla_proj(main, auxc, positions.reshape(t, 1), mla_q_norm_gain.reshape(1, -1),
                        mla_kv_norm_gain.reshape(1, -1), wq, wkv, rope_tab, seq)
    mla = _attention(q.reshape(bsz, seq, -1), k.reshape(bsz, seq, -1), v)

    w_r = jnp.concatenate([w_router, jnp.zeros((d, LANES - N_EXPERTS), F32)], axis=1)
    w_r_hi = w_r.astype(BF16)
    w_r = jnp.concatenate([w_r_hi, (w_r - w_r_hi.astype(F32)).astype(BF16)], axis=1)
    x1, h2, logits_t = _out_proj(x2, dn.reshape(t, DN_WIDTH), mla.reshape(t, -1), ada3,
                                 norm2_gain.reshape(1, d), w_out.astype(BF16), w_r, seq)

    n_rows = -(-(t * TOP_K + N_EXPERTS * (EXPERT_BLOCK - 1)) // EXPERT_BLOCK) * EXPERT_BLOCK
    n_blocks = n_rows // EXPERT_BLOCK
    assert n_blocks <= MAX_BLOCK_LANES
    ek, rk, wc, seg_start, blocks = _route_pallas(logits_t, router_bias.reshape(N_EXPERTS, 1))
    slot = _slots(ek, rk, seg_start)[:TOP_K]
    x_sorted = _sc_dispatch(h2, slot.reshape(-1), n_rows)
    shared = _shared_ffn(h2, w_sh_gate_up.astype(BF16), w_sh_down.astype(BF16))
    y_sorted = _experts(blocks[0, :n_blocks], blocks[1, :n_blocks], blocks[2, :1], x_sorted,
                        w_exp_gate_up, w_exp_down, shared)

    n_parts = FINAL_PARTS if (t // min(256, seq)) % FINAL_PARTS == 0 else 1
    out = None
    for part in range(n_parts):
        part_slots = slot[:, part * (t // n_parts):(part + 1) * (t // n_parts)].reshape(-1)
        y_part = _sc_combine(y_sorted, part_slots).reshape(TOP_K, t // n_parts, d // 2)
        out = _final_part(x1, shared, y_part, wc, ada3, out_gain.reshape(1, d), seq, part, n_parts, out)
    return out.reshape(bsz, seq, d)


def kernel(x, c, positions, w_ada, b_ada, norm1_gain, w_in, dn_conv_w, dn_a_log, dn_dt_bias, dn_norm_gain,
           mla_q_norm_gain, w_q_up, mla_kv_norm_gain, w_kv_up, w_out, norm2_gain, w_router, router_bias,
           w_exp_gate_up, w_exp_down, w_sh_gate_up, w_sh_down, final_norm_gain):
    depth = w_ada.shape[0]
    assert depth == 1, "the final RMSNorm is fused into the single layer's last kernel"
    ada = _ada(c, w_ada[0], b_ada[0])
    return _layer(x, ada, positions, w_in[0], dn_conv_w[0], dn_a_log[0], dn_dt_bias[0], dn_norm_gain[0],
                  mla_q_norm_gain[0], w_q_up[0], mla_kv_norm_gain[0], w_kv_up[0], w_out[0], norm1_gain[0],
                  norm2_gain[0], w_router[0], router_bias[0], w_exp_gate_up[0], w_exp_down[0],
                  w_sh_gate_up[0], w_sh_down[0], final_norm_gain)
```

```python
import functools

import jax
import jax.numpy as jnp
from jax import lax
from jax.experimental import pallas as pl
from jax.experimental.pallas import tpu as pltpu
from jax.experimental.pallas import tpu_sc as plsc

F32 = jnp.float32
BF16 = jnp.bfloat16

DN_HEADS = 8
DN_DIM = 128
DN_WIDTH = DN_HEADS * DN_DIM
DN_CONV = 4
DN_CHUNK = 64
MLA_HEADS = 8
MLA_Q_RANK = 512
MLA_KV_RANK = 512
MLA_NOPE = 128
MLA_ROPE = 64
MLA_V = 128
MLA_QK_PAD = 256
ROPE_THETA = 10000.0
N_EXPERTS = 64
TOP_K = 6
N_GROUPS = 8
TOPK_GROUPS = 4
EXPERT_FF = 512
ROUTED_SCALE = 2.5
EXPERT_BLOCK = 512
RMS_EPS = 1e-6
L2_EPS = 1e-6

LANES = 128
MXU_WIDTH = 256
MAIN_WIDTH = 4 * DN_WIDTH + MLA_Q_RANK + MLA_KV_RANK
AUX_BETA = MLA_ROPE
AUX_G = MLA_ROPE + DN_HEADS
VMEM_LIMIT = 56 * 1024 * 1024


def _params(n_parallel, n_arbitrary=0):
    sem = ("parallel",) * n_parallel + ("arbitrary",) * n_arbitrary
    return pltpu.CompilerParams(dimension_semantics=sem, vmem_limit_bytes=VMEM_LIMIT)


def _silu(x):
    return x * jax.nn.sigmoid(x)


def _bdot(a, b):
    return jnp.dot(a.astype(BF16), b.astype(BF16), preferred_element_type=F32)


def _bdot_nt(a, b):
    return lax.dot_general(a.astype(BF16), b.astype(BF16), (((1,), (1,)), ((), ())),
                           preferred_element_type=F32)


def _rms(x, gain):
    return x * lax.rsqrt(jnp.mean(x * x, axis=-1, keepdims=True) + RMS_EPS) * gain


HIGH_HALF = 0xFFFF0000


def _pack_bf16_pairs(x):
    n = x.shape[1] // 2
    bits = lax.bitcast_convert_type(x.astype(BF16).astype(F32), jnp.uint32)
    return (bits[:, :n] >> 16) | (bits[:, n:] & jnp.uint32(HIGH_HALF))


def _unpack_bf16_pairs(p):
    lo = lax.bitcast_convert_type(p << 16, F32)
    hi = lax.bitcast_convert_type(p & jnp.uint32(HIGH_HALF), F32)
    return jnp.concatenate([lo, hi], axis=1)


def _ada_kernel(c_ref, w_ref, b_ref, o_ref):
    o_ref[...] = _bdot(_silu(c_ref[...]), w_ref[...]) + b_ref[...]


def _ada(c, w_ada, b_ada):
    bsz, d = c.shape
    n = w_ada.shape[1]
    tn = 1024
    return pl.pallas_call(
        _ada_kernel,
        grid=(n // tn,),
        in_specs=[pl.BlockSpec((bsz, d), lambda j: (0, 0)),
                  pl.BlockSpec((d, tn), lambda j: (0, j)),
                  pl.BlockSpec((1, tn), lambda j: (0, j))],
        out_specs=pl.BlockSpec((bsz, tn), lambda j: (0, j)),
        out_shape=jax.ShapeDtypeStruct((bsz, n), F32),
        compiler_params=_params(1),
        name="ada",
    )(c, w_ada, b_ada.reshape(1, n))


def _in_proj_kernel(x_ref, ada_ref, gain_ref, wm_ref, wa_ref, auxp_ref,
                    main_ref, auxc_ref, auxr_ref, h_ref):
    j = pl.program_id(1)

    @pl.when(j == 0)
    def _():
        ada = ada_ref[0]
        h = _rms(x_ref[...], gain_ref[...]) * (1.0 + ada[1:2, :]) + ada[0:1, :]
        hb = h.astype(BF16)
        h_ref[...] = hb
        aux = jnp.dot(hb, wa_ref[...], preferred_element_type=F32)
        tm = aux.shape[0]
        lane = lax.broadcasted_iota(jnp.int32, aux.shape, 1)
        is_beta = (lane >= AUX_BETA) & (lane < AUX_G)
        is_g = (lane >= AUX_G) & (lane < AUX_G + DN_HEADS)
        a_log = auxp_ref[0:1, :]
        dt_bias = auxp_ref[1:2, :]
        sp_in = aux + dt_bias
        softplus = jnp.maximum(sp_in, 0.0) + jnp.log(1.0 + jnp.exp(-jnp.abs(sp_in)))
        g = -jnp.exp(a_log) * softplus
        aux = jnp.where(is_beta, jax.nn.sigmoid(aux), jnp.where(is_g, g, aux))
        r = lax.broadcasted_iota(jnp.int32, (LANES, LANES), 0)
        cidx = lax.broadcasted_iota(jnp.int32, (LANES, LANES), 1)
        tri = ((cidx <= r) & (cidx // DN_CHUNK == r // DN_CHUNK)).astype(F32)
        g_cols = (cidx >= AUX_G) & (cidx < AUX_G + DN_HEADS)
        parts = []
        for t in range(tm // LANES):
            blk = aux[t * LANES:(t + 1) * LANES, :]
            cs = jnp.dot(tri, blk, precision=lax.Precision.HIGHEST, preferred_element_type=F32)
            parts.append(jnp.where(g_cols, cs, blk))
        aux = jnp.concatenate(parts, axis=0)
        auxc_ref[...] = aux
        auxr_ref[...] = aux.T

    main_ref[...] = jnp.dot(h_ref[...], wm_ref[...], preferred_element_type=F32).astype(main_ref.dtype)


def _in_proj(x2, ada3, gain, w_main, w_aux, auxp, seq):
    t, d = x2.shape
    n = w_main.shape[1]
    tm = min(1024, seq)
    tn = 1024
    per_b = seq // tm
    return pl.pallas_call(
        _in_proj_kernel,
        grid=(t // tm, n // tn),
        in_specs=[pl.BlockSpec((tm, d), lambda i, j: (i, 0)),
                  pl.BlockSpec((1, 6, d), lambda i, j: (i // per_b, 0, 0)),
                  pl.BlockSpec((1, d), lambda i, j: (0, 0)),
                  pl.BlockSpec((d, tn), lambda i, j: (0, j)),
                  pl.BlockSpec((d, LANES), lambda i, j: (0, 0)),
                  pl.BlockSpec((8, LANES), lambda i, j: (0, 0))],
        out_specs=[pl.BlockSpec((tm, tn), lambda i, j: (i, j)),
                   pl.BlockSpec((tm, LANES), lambda i, j: (i, 0)),
                   pl.BlockSpec((LANES, tm), lambda i, j: (0, i))],
        out_shape=[jax.ShapeDtypeStruct((t, n), BF16),
                   jax.ShapeDtypeStruct((t, LANES), F32),
                   jax.ShapeDtypeStruct((LANES, t), F32)],
        scratch_shapes=[pltpu.VMEM((tm, d), BF16)],
        compiler_params=_params(1, 1),
        name="in_proj",
    )(x2, ada3, gain, w_main, w_aux, auxp)


def _unit_lower_inverses(ms):
    c = ms[0].shape[0]
    r = lax.broadcasted_iota(jnp.int32, (c, c), 0)
    cc = lax.broadcasted_iota(jnp.int32, (c, c), 1)
    eye = (r == cc).astype(F32)
    same = (r // 16) == (cc // 16)
    md = [jnp.where(same, m, 0.0) for m in ms]
    mo = [jnp.where(same, 0.0, m) for m in ms]
    p1 = [_bdot(a, a) for a in md]
    p2 = [_bdot(a, a) for a in p1]
    p3 = [_bdot(a, a) for a in p2]
    td = [eye - a for a in md]
    td = [t + _bdot(t, p) for t, p in zip(td, p1)]
    td = [t + _bdot(t, p) for t, p in zip(td, p2)]
    td = [t + _bdot(t, p) for t, p in zip(td, p3)]
    n1 = [_bdot(t, o) for t, o in zip(td, mo)]
    n2 = [_bdot(n, n) for n in n1]
    left = [(eye - a) + _bdot(eye - a, b) for a, b in zip(n1, n2)]
    return [_bdot(l, t) for l, t in zip(left, td)]


DELTA_TILE = 2048
DELTA_SCAN_TILE = 512


def _delta_prep_kernel(q_ref, k_ref, v_ref, qh_ref, kh_ref, vh_ref, auxc_ref, auxr_ref, wq_ref, wk_ref, wv_ref,
                       u_ref, w_ref, qe_ref, kd_ref, a_ref, pad_ref):
    head = pl.program_id(1)
    first_tile = pl.program_id(2) == 0
    ts = q_ref.shape[0]
    c = DN_CHUNK
    halo = qh_ref.shape[0]

    def conv_silu(x_ref, h_ref, w_ref):
        top = pad_ref.shape[0] - ts
        pad_ref[0:top, :] = jnp.where(first_tile, 0.0, h_ref[halo - top:, :].astype(F32))
        pad_ref[top:, :] = x_ref[...].astype(F32)
        w = w_ref[...]
        acc = pad_ref[top:, :] * w[DN_CONV - 1:DN_CONV, :]
        for s in range(1, DN_CONV):
            acc = acc + pad_ref[top - s:top - s + ts, :] * w[DN_CONV - 1 - s:DN_CONV - s, :]
        return _silu(acc)

    q = conv_silu(q_ref, qh_ref, wq_ref)
    k = conv_silu(k_ref, kh_ref, wk_ref)
    v = conv_silu(v_ref, vh_ref, wv_ref)
    q = q * lax.rsqrt(jnp.sum(q * q, axis=-1, keepdims=True) + L2_EPS) * (DN_DIM ** -0.5)
    k = k * lax.rsqrt(jnp.sum(k * k, axis=-1, keepdims=True) + L2_EPS)

    auxc = auxc_ref[...]
    lane = lax.broadcasted_iota(jnp.int32, auxc.shape, 1)
    beta_all = jnp.sum(jnp.where(lane == AUX_BETA + head, auxc, 0.0), axis=1, keepdims=True)
    gcol_all = jnp.sum(jnp.where(lane == AUX_G + head, auxc, 0.0), axis=1, keepdims=True)
    grows = auxr_ref[AUX_G:AUX_G + DN_HEADS, :]
    head_row = lax.broadcasted_iota(jnp.int32, grows.shape, 0)
    grow_all = jnp.sum(jnp.where(head_row == head, grows, 0.0), axis=0, keepdims=True)

    ri = lax.broadcasted_iota(jnp.int32, (c, c), 0)
    ci = lax.broadcasted_iota(jnp.int32, (c, c), 1)
    causal = ri >= ci
    strict = ri > ci
    chunks = [slice(j * c, (j + 1) * c) for j in range(ts // c)]
    decay = [jnp.exp(jnp.where(causal, gcol_all[rows] - grow_all[:, rows], -jnp.inf)) for rows in chunks]
    kb = [k[rows] * beta_all[rows] for rows in chunks]
    both = [_bdot_nt(jnp.concatenate([kb_j, q[rows]], axis=0), k[rows])
            for kb_j, rows in zip(kb, chunks)]
    tinv = _unit_lower_inverses([jnp.where(strict, b[:c] * d, 0.0) for b, d in zip(both, decay)])
    eg = [jnp.exp(gcol_all[rows]) for rows in chunks]
    sol = [_bdot(t, jnp.concatenate([v[rows] * beta_all[rows], kb_j * e], axis=1))
           for t, rows, kb_j, e in zip(tinv, chunks, kb, eg)]
    for j, rows in enumerate(chunks):
        gcol = gcol_all[rows]
        u_ref[rows, :] = sol[j][:, :DN_DIM].astype(u_ref.dtype)
        w_ref[rows, :] = sol[j][:, DN_DIM:].astype(w_ref.dtype)
        qe_ref[rows, :] = (q[rows] * eg[j]).astype(qe_ref.dtype)
        kd_ref[rows, :] = (k[rows] * jnp.exp(gcol[c - 1:c, :] - gcol)).astype(kd_ref.dtype)
        a_ref[rows, :] = jnp.where(causal, both[j][c:] * decay[j], 0.0).astype(a_ref.dtype)


def _delta_scan_kernel(u_ref, w_ref, qe_ref, kd_ref, a_ref, z_ref, auxc_ref, gain_ref, o_ref, st_ref):
    seq = u_ref.shape[0]
    n_heads = a_ref.shape[0]
    c = DN_CHUNK

    @pl.when(pl.program_id(1) == 0)
    def _():
        st_ref[...] = jnp.zeros_like(st_ref)

    gain = gain_ref[...]
    lane = lax.broadcasted_iota(jnp.int32, (1, LANES), 1)

    def scan_chunk(i, carry):
        start = pl.multiple_of(i * c, c)
        rows = pl.ds(start, c)
        last = auxc_ref[pl.ds(start + c - 1, 1), :]
        heads = range(n_heads)
        cols = [slice(hh * DN_DIM, (hh + 1) * DN_DIM) for hh in heads]
        state = [st_ref[hh] for hh in heads]
        sb = [s.astype(BF16) for s in state]
        ws = [jnp.dot(w_ref[rows, cl], s, preferred_element_type=F32) for cl, s in zip(cols, sb)]
        qs = [jnp.dot(qe_ref[rows, cl], s, preferred_element_type=F32) for cl, s in zip(cols, sb)]
        vb = [(u_ref[rows, cl].astype(F32) - x).astype(BF16) for cl, x in zip(cols, ws)]
        kv = [lax.dot_general(kd_ref[rows, cl], x, (((0,), (0,)), ((), ())), preferred_element_type=F32)
              for cl, x in zip(cols, vb)]
        av = [jnp.dot(a_ref[hh, rows, :], x, preferred_element_type=F32) for hh, x in zip(heads, vb)]
        for hh in heads:
            glast = jnp.sum(jnp.where(lane == AUX_G + hh, last, 0.0), axis=1, keepdims=True)
            st_ref[hh] = state[hh] * jnp.exp(glast) + kv[hh]
            z = z_ref[rows, cols[hh]].astype(F32)
            o_ref[rows, cols[hh]] = (_rms(qs[hh] + av[hh], gain) * _silu(z)).astype(o_ref.dtype)
        return carry

    lax.fori_loop(0, seq // c, scan_chunk, 0)


def _delta(main3, auxc3, auxr, conv_w_t, gain):
    bsz, seq, _ = main3.shape
    hb = DN_WIDTH // DN_DIM
    ts = min(DELTA_TILE, seq)
    halo = 16
    tiles = seq // ts

    def col(offset):
        return pl.BlockSpec((None, ts, DN_DIM), lambda b, h, i: (b, i, offset * hb + h))

    def halo_col(offset):
        return pl.BlockSpec((None, halo, DN_DIM),
                            lambda b, h, i: (b, jnp.maximum(i * (ts // halo) - 1, 0), offset * hb + h))

    def wcol(offset):
        return pl.BlockSpec((DN_CONV, DN_DIM), lambda b, h, i: (0, offset * hb + h))

    tok = pl.BlockSpec((None, ts, DN_DIM), lambda b, h, i: (b, i, h))
    tok_shape = jax.ShapeDtypeStruct((bsz, seq, DN_WIDTH), BF16)
    u, w, qe, kd, a = pl.pallas_call(
        _delta_prep_kernel,
        grid=(bsz, DN_HEADS, tiles),
        in_specs=[col(0), col(1), col(2), halo_col(0), halo_col(1), halo_col(2),
                  pl.BlockSpec((None, ts, LANES), lambda b, h, i: (b, i, 0)),
                  pl.BlockSpec((LANES, ts), lambda b, h, i: (0, b * tiles + i)),
                  wcol(0), wcol(1), wcol(2)],
        out_specs=[tok, tok, tok, tok,
                   pl.BlockSpec((None, None, ts, DN_CHUNK), lambda b, h, i: (b, h, i, 0))],
        out_shape=[tok_shape, tok_shape, tok_shape, tok_shape,
                   jax.ShapeDtypeStruct((bsz, DN_HEADS, seq, DN_CHUNK), BF16)],
        scratch_shapes=[pltpu.VMEM((ts + 8, DN_DIM), F32)],
        compiler_params=_params(3),
        name="delta_prep",
    )(main3, main3, main3, main3, main3, main3, auxc3, auxr, conv_w_t, conv_w_t, conv_w_t)

    tsc = min(DELTA_SCAN_TILE, seq)
    wide = pl.BlockSpec((None, tsc, DN_WIDTH), lambda b, i: (b, i, 0))
    return pl.pallas_call(
        _delta_scan_kernel,
        grid=(bsz, seq // tsc),
        in_specs=[wide, wide, wide, wide,
                  pl.BlockSpec((None, DN_HEADS, tsc, DN_CHUNK), lambda b, i: (b, 0, i, 0)),
                  pl.BlockSpec((None, tsc, DN_WIDTH), lambda b, i: (b, i, 3)),
                  pl.BlockSpec((None, tsc, LANES), lambda b, i: (b, i, 0)),
                  pl.BlockSpec((1, DN_DIM), lambda b, i: (0, 0))],
        out_specs=wide,
        out_shape=tok_shape,
        scratch_shapes=[pltpu.VMEM((DN_HEADS, DN_DIM, DN_DIM), F32)],
        compiler_params=_params(1, 1),
        name="delta_scan",
    )(u, w, qe, kd, a, main3, auxc3, gain)


def _mla_proj_kernel(cq_ref, ckv_ref, auxc_ref, pos_ref, qg_ref, kvg_ref, wq_ref, wkv_ref, rope_ref,
                     q_ref, k_ref, v_ref):
    scale = (MLA_NOPE + MLA_ROPE) ** -0.5
    ang = pos_ref[...].astype(F32) * rope_ref[0:1, :]
    cos_t = jnp.cos(ang) * rope_ref[1:2, :]
    sin_t = jnp.sin(ang) * rope_ref[2:3, :]
    lane = lax.broadcasted_iota(jnp.int32, ang.shape, 1)
    first = lane < MLA_ROPE // 2

    def rope(a):
        swapped = jnp.where(first, pltpu.roll(a, LANES - MLA_ROPE // 2, 1), pltpu.roll(a, MLA_ROPE // 2, 1))
        return a * cos_t + swapped * sin_t

    ql = _bdot(_rms(cq_ref[...].astype(F32), qg_ref[...]), wq_ref[...]) * scale
    kv = _bdot(_rms(ckv_ref[...].astype(F32), kvg_ref[...]), wkv_ref[...])
    kpe = rope(auxc_ref[...]).astype(k_ref.dtype)
    for h in range(MLA_HEADS):
        o = h * MLA_QK_PAD
        q_ref[:, o:o + MLA_NOPE] = ql[:, o:o + MLA_NOPE].astype(q_ref.dtype)
        q_ref[:, o + MLA_NOPE:o + MLA_QK_PAD] = rope(ql[:, o + MLA_NOPE:o + MLA_QK_PAD]).astype(q_ref.dtype)
        k_ref[:, o:o + MLA_NOPE] = kv[:, h * MLA_NOPE:(h + 1) * MLA_NOPE].astype(k_ref.dtype)
        k_ref[:, o + MLA_NOPE:o + MLA_QK_PAD] = kpe
    v_ref[...] = kv[:, MLA_HEADS * MLA_NOPE:].T.astype(v_ref.dtype)


def _mla_proj(main, auxc, pos, q_gain, kv_gain, wq, wkv, rope_tab, seq):
    t = main.shape[0]
    tm = min(512, seq)
    per_b = seq // tm
    cq_blk = (4 * DN_WIDTH) // MLA_Q_RANK
    hq = MLA_HEADS * MLA_QK_PAD
    hv = MLA_HEADS * MLA_V
    return pl.pallas_call(
        _mla_proj_kernel,
        grid=(t // tm,),
        in_specs=[pl.BlockSpec((tm, MLA_Q_RANK), lambda i: (i, cq_blk)),
                  pl.BlockSpec((tm, MLA_KV_RANK), lambda i: (i, cq_blk + 1)),
                  pl.BlockSpec((tm, LANES), lambda i: (i, 0)),
                  pl.BlockSpec((tm, 1), lambda i: (i, 0)),
                  pl.BlockSpec((1, MLA_Q_RANK), lambda i: (0, 0)),
                  pl.BlockSpec((1, MLA_KV_RANK), lambda i: (0, 0)),
                  pl.BlockSpec((MLA_Q_RANK, hq), lambda i: (0, 0)),
                  pl.BlockSpec((MLA_KV_RANK, 2 * hv), lambda i: (0, 0)),
                  pl.BlockSpec((8, LANES), lambda i: (0, 0))],
        out_specs=[pl.BlockSpec((tm, hq), lambda i: (i, 0)),
                   pl.BlockSpec((tm, hq), lambda i: (i, 0)),
                   pl.BlockSpec((None, hv, tm), lambda i: (i // per_b, 0, i % per_b))],
        out_shape=[jax.ShapeDtypeStruct((t, hq), BF16),
                   jax.ShapeDtypeStruct((t, hq), BF16),
                   jax.ShapeDtypeStruct((t // seq, hv, seq), BF16)],
        compiler_params=_params(1),
        name="mla_proj",
    )(main, main, auxc, pos, q_gain, kv_gain, wq, wkv, rope_tab)


ATTN_PARTS = 4


def _attn_kernel(q_ref, k_ref, vt_ref, o_ref, m_ref, l_ref, acc_ref):
    qi = pl.program_id(2)
    n_parts = m_ref.shape[0]
    tk = m_ref.shape[2]
    q = [q_ref[p * tk:(p + 1) * tk, :] for p in range(n_parts)]
    m_ref[...] = jnp.full_like(m_ref, -jnp.inf)
    l_ref[...] = jnp.zeros_like(l_ref)
    acc_ref[...] = jnp.zeros_like(acc_ref)
    key = lax.broadcasted_iota(jnp.int32, (tk, tk), 0)
    query = lax.broadcasted_iota(jnp.int32, (tk, tk), 1)
    visible = key <= query

    def step(j, parts, masked_part):
        start = pl.multiple_of(j * tk, tk)
        kb = k_ref[pl.ds(start, tk), :]
        vtb = vt_ref[:, pl.ds(start, tk)]
        s = [lax.dot_general(kb, q[p], (((1,), (1,)), ((), ())), preferred_element_type=F32) for p in parts]
        s = [jnp.where(visible, x, -jnp.inf) if p == masked_part else x for p, x in zip(parts, s)]
        m_old = [m_ref[p] for p in parts]
        m_new = [jnp.maximum(mo, jnp.max(x, axis=0, keepdims=True)) for mo, x in zip(m_old, s)]
        e = [jnp.exp(x - mn) for x, mn in zip(s, m_new)]
        pv = [jnp.dot(vtb, x.astype(BF16), preferred_element_type=F32) for x in e]
        for i, p in enumerate(parts):
            alpha = jnp.exp(m_old[i] - m_new[i])
            l_ref[p] = alpha * l_ref[p] + jnp.sum(e[i], axis=0, keepdims=True)
            acc_ref[p] = alpha * acc_ref[p] + pv[i]
            m_ref[p] = m_new[i]

    every = tuple(range(n_parts))

    def body(j, carry):
        step(j, every, None)
        return carry

    lax.fori_loop(0, qi * n_parts, body, 0)
    for d in range(n_parts):
        step(qi * n_parts + d, every[d:], d)
    for p in every:
        o_ref[p * tk:(p + 1) * tk, :] = (acc_ref[p] / l_ref[p]).T.astype(o_ref.dtype)


def _attention(q3, k3, vt3):
    bsz, seq, _ = q3.shape
    tk = min(512, seq)
    n_parts = min(ATTN_PARTS, seq // tk)
    tq = tk * n_parts
    return pl.pallas_call(
        _attn_kernel,
        grid=(bsz, MLA_HEADS, seq // tq),
        in_specs=[pl.BlockSpec((None, tq, MLA_QK_PAD), lambda b, h, i: (b, i, h)),
                  pl.BlockSpec((None, seq, MLA_QK_PAD), lambda b, h, i: (b, 0, h)),
                  pl.BlockSpec((None, MLA_V, seq), lambda b, h, i: (b, h, 0))],
        out_specs=pl.BlockSpec((None, tq, MLA_V), lambda b, h, i: (b, i, h)),
        out_shape=jax.ShapeDtypeStruct((bsz, seq, MLA_HEADS * MLA_V), BF16),
        scratch_shapes=[pltpu.VMEM((n_parts, 1, tk), F32), pltpu.VMEM((n_parts, 1, tk), F32),
                        pltpu.VMEM((n_parts, MLA_V, tk), F32)],
        compiler_params=_params(3),
        name="attention",
    )(q3, k3, vt3)


def _out_proj_kernel(x_ref, dn_ref, mla_ref, ada_ref, gain_ref, w_ref, wr_ref, x1_ref, h2_ref, lg_ref):
    ada = ada_ref[0]
    half = dn_ref.shape[1]
    mix = (jnp.dot(dn_ref[...], w_ref[:half, :], preferred_element_type=F32)
           + jnp.dot(mla_ref[...], w_ref[half:, :], preferred_element_type=F32))
    x1 = x_ref[...] + ada[2:3, :] * mix
    x1_ref[...] = x1
    h2 = _rms(x1, gain_ref[...]) * (1.0 + ada[4:5, :]) + ada[3:4, :]
    h2_ref[...] = _pack_bf16_pairs(h2)
    h_hi = h2.astype(BF16)
    h_lo = (h2 - h_hi.astype(F32)).astype(BF16)
    both = jnp.dot(h_hi, wr_ref[...], preferred_element_type=F32)
    logits = both[:, :LANES] + both[:, LANES:] + jnp.dot(h_lo, wr_ref[:, :LANES], preferred_element_type=F32)
    lg_ref[...] = logits.T


def _out_proj(x2, dn, mla, ada3, gain, w_out, w_router, seq):
    t, d = x2.shape
    tm = min(256, seq)
    per_b = seq // tm
    half = dn.shape[1]
    return pl.pallas_call(
        _out_proj_kernel,
        grid=(t // tm,),
        in_specs=[pl.BlockSpec((tm, d), lambda i: (i, 0)),
                  pl.BlockSpec((tm, half), lambda i: (i, 0)),
                  pl.BlockSpec((tm, half), lambda i: (i, 0)),
                  pl.BlockSpec((1, 6, d), lambda i: (i // per_b, 0, 0)),
                  pl.BlockSpec((1, d), lambda i: (0, 0)),
                  pl.BlockSpec((2 * half, d), lambda i: (0, 0)),
                  pl.BlockSpec((d, 2 * LANES), lambda i: (0, 0))],
        out_specs=[pl.BlockSpec((tm, d), lambda i: (i, 0)),
                   pl.BlockSpec((tm, d // 2), lambda i: (i, 0)),
                   pl.BlockSpec((LANES, tm), lambda i: (0, i))],
        out_shape=[jax.ShapeDtypeStruct((t, d), F32),
                   jax.ShapeDtypeStruct((t, d // 2), jnp.uint32),
                   jax.ShapeDtypeStruct((LANES, t), F32)],
        compiler_params=_params(1),
        name="out_proj",
    )(x2, dn, mla, ada3, gain, w_out, w_router)


def _expert_kernel(be_ref, nv_ref, na_ref, x_ref, wgu_hbm, wd_hbm, y_ref, gu_f, d_f, gu_s, d_s, slot_ref, sem):
    i = pl.program_id(0)
    n_blocks = pl.num_programs(0)
    active = nv_ref[i] > 0
    first = active & ((i == 0) | (be_ref[i] != be_ref[jnp.maximum(i - 1, 0)]))

    def fetch(e, slot):
        return (pltpu.make_async_copy(wgu_hbm.at[e], gu_f.at[slot], sem.at[slot, 0]),
                pltpu.make_async_copy(wd_hbm.at[e], d_f.at[slot], sem.at[slot, 1]))

    @pl.when((i == 0) & active)
    def _():
        slot_ref[0] = 0
        for cp in fetch(be_ref[0], 0):
            cp.start()

    @pl.when(first)
    def _():
        cur = slot_ref[0]
        for cp in fetch(be_ref[i], cur):
            cp.wait()
        nxt = lax.while_loop(lambda j: (j < n_blocks) & (be_ref[jnp.minimum(j, n_blocks - 1)] == be_ref[i]),
                             lambda j: j + 1, i + 1)
        nxt_c = jnp.minimum(nxt, n_blocks - 1)

        @pl.when((nxt < n_blocks) & (nv_ref[nxt_c] > 0))
        def _():
            for cp in fetch(be_ref[nxt_c], 1 - cur):
                cp.start()

        gu_s[...] = gu_f[cur].astype(BF16)
        d_s[...] = d_f[cur].astype(BF16)
        slot_ref[0] = 1 - cur

    @pl.when(active)
    def _():
        x = _unpack_bf16_pairs(x_ref[...]).astype(BF16)
        ff = d_s.shape[0]
        chunks = [slice(c0, c0 + MXU_WIDTH) for c0 in range(0, ff, MXU_WIDTH)]
        gate_up = [(jnp.dot(x, gu_s[:, cs], preferred_element_type=F32),
                    jnp.dot(x, gu_s[:, slice(ff + cs.start, ff + cs.stop)], preferred_element_type=F32))
                   for cs in chunks]
        act = [(_silu(g) * u).astype(BF16) for g, u in gate_up]
        y = jnp.dot(act[0], d_s[chunks[0], :], preferred_element_type=F32)
        for a, cs in zip(act[1:], chunks[1:]):
            y = y + jnp.dot(a, d_s[cs, :], preferred_element_type=F32)
        row = lax.broadcasted_iota(jnp.int32, y.shape, 0)
        y_ref[...] = _pack_bf16_pairs(jnp.where(row < nv_ref[i], y, 0.0))


def _experts(block_e, n_valid, n_active, x_sorted, w_gu, w_d):
    n_rows, dp = x_sorted.shape
    ff2 = w_gu.shape[2]
    n_blocks = n_rows // EXPERT_BLOCK

    def block(i, be, nv, na):
        return (jnp.maximum(jnp.minimum(i, na[0] - 1), 0), 0)

    grid_spec = pltpu.PrefetchScalarGridSpec(
        num_scalar_prefetch=3,
        grid=(n_blocks,),
        in_specs=[pl.BlockSpec((EXPERT_BLOCK, dp), block),
                  pl.BlockSpec(memory_space=pl.ANY),
                  pl.BlockSpec(memory_space=pl.ANY)],
        out_specs=pl.BlockSpec((EXPERT_BLOCK, dp), block),
        scratch_shapes=[pltpu.VMEM((2, 2 * dp, ff2), F32), pltpu.VMEM((2, ff2 // 2, 2 * dp), F32),
                        pltpu.VMEM((2 * dp, ff2), BF16), pltpu.VMEM((ff2 // 2, 2 * dp), BF16),
                        pltpu.SMEM((1,), jnp.int32), pltpu.SemaphoreType.DMA((2, 2))],
    )
    return pl.pallas_call(
        _expert_kernel,
        grid_spec=grid_spec,
        out_shape=jax.ShapeDtypeStruct((n_rows, dp), jnp.uint32),
        compiler_params=_params(0, 1),
        name="experts",
    )(block_e, n_valid, n_active, x_sorted, w_gu, w_d)


def _shared_ffn_kernel(h2_ref, wgu_ref, wd_ref, o_ref):
    h2 = _unpack_bf16_pairs(h2_ref[...]).astype(BF16)
    gu = jnp.dot(h2, wgu_ref[...], preferred_element_type=F32)
    ff = gu.shape[1] // 2
    act = _silu(gu[:, :ff]) * gu[:, ff:]
    o_ref[...] = jnp.dot(act.astype(BF16), wd_ref[...], preferred_element_type=F32).astype(o_ref.dtype)


def _shared_ffn(h2, w_gu, w_d):
    t, dp = h2.shape
    d = 2 * dp
    tm = min(512, t)
    ff2 = w_gu.shape[1]
    return pl.pallas_call(
        _shared_ffn_kernel,
        grid=(t // tm,),
        in_specs=[pl.BlockSpec((tm, dp), lambda i: (i, 0)),
                  pl.BlockSpec((d, ff2), lambda i: (0, 0)),
                  pl.BlockSpec((ff2 // 2, d), lambda i: (0, 0))],
        out_specs=pl.BlockSpec((tm, d), lambda i: (i, 0)),
        out_shape=jax.ShapeDtypeStruct((t, d), BF16),
        compiler_params=_params(1),
        name="shared_ffn",
    )(h2, w_gu, w_d)


def _final_kernel(x1_ref, sh_ref, wc_ref, ada_ref, gain_ref, *rest):
    y_refs, o_ref = rest[:-1], rest[-1]
    ada = ada_ref[0]
    ffn = sh_ref[...].astype(F32)
    wc = wc_ref[...]
    for kk, y_ref in enumerate(y_refs):
        ffn = ffn + wc[:, kk:kk + 1] * _unpack_bf16_pairs(y_ref[...])
    x2 = x1_ref[...] + ada[5:6, :] * ffn
    o_ref[...] = _rms(x2, gain_ref[...])


def _final(x1, shared, y_tok, wc, ada3, gain, seq):
    t, d = x1.shape
    tm = min(256, seq)
    per_b = seq // tm
    steps = t // tm
    slot_specs = [pl.BlockSpec((tm, d // 2), lambda i, kk=kk: (kk * steps + i, 0)) for kk in range(TOP_K)]
    return pl.pallas_call(
        _final_kernel,
        grid=(steps,),
        in_specs=[pl.BlockSpec((tm, d), lambda i: (i, 0)),
                  pl.BlockSpec((tm, d), lambda i: (i, 0)),
                  pl.BlockSpec((tm, LANES), lambda i: (i, 0)),
                  pl.BlockSpec((1, 6, d), lambda i: (i // per_b, 0, 0)),
                  pl.BlockSpec((1, d), lambda i: (0, 0))] + slot_specs,
        out_specs=pl.BlockSpec((tm, d), lambda i: (i, 0)),
        out_shape=jax.ShapeDtypeStruct((t, d), F32),
        compiler_params=_params(1),
        name="final",
    )(x1, shared, wc, ada3, gain, *([y_tok] * TOP_K))


ROUTE_TILE = 512
MAX_BLOCK_LANES = 512


def _first_argmax(x, idx_f, n):
    m = jnp.max(x, axis=0, keepdims=True)
    first = jnp.min(jnp.where(x == m, idx_f, float(n)), axis=0, keepdims=True)
    return m, first


def _route_kernel(lg_ref, bias_ref, ek_ref, rk_ref, wc_ref, ps_ref, be_ref, carry_ref, upper_ref):
    i = pl.program_id(0)
    tm = lg_ref.shape[1]
    per_group = N_EXPERTS // N_GROUPS

    @pl.when(i == 0)
    def _():
        carry_ref[...] = jnp.zeros_like(carry_ref)
        r = lax.broadcasted_iota(jnp.int32, (tm, tm), 0)
        cc = lax.broadcasted_iota(jnp.int32, (tm, tm), 1)
        upper_ref[...] = (r < cc).astype(BF16)

    scores = jax.nn.sigmoid(lg_ref[0:N_EXPERTS, :])
    biased = scores + bias_ref[:, 0:1]
    sub8 = lax.broadcasted_iota(jnp.int32, (per_group, tm), 0).astype(F32)
    group_rows = []
    for g in range(N_GROUPS):
        xg = biased[g * per_group:(g + 1) * per_group, :]
        m1, i1 = _first_argmax(xg, sub8, per_group)
        m2 = jnp.max(jnp.where(sub8 == i1, -jnp.inf, xg), axis=0, keepdims=True)
        group_rows.append(m1 + m2)
    cur = jnp.concatenate(group_rows, axis=0)
    gself = jnp.zeros(cur.shape, F32)
    for _ in range(TOPK_GROUPS):
        _, gi = _first_argmax(cur, sub8, N_GROUPS)
        hit = sub8 == gi
        gself = jnp.where(hit, 1.0, gself)
        cur = jnp.where(hit, -jnp.inf, cur)
    masked = jnp.concatenate(
        [jnp.where(gself[g:g + 1, :] > 0.5, biased[g * per_group:(g + 1) * per_group, :], -jnp.inf)
         for g in range(N_GROUPS)], axis=0)
    sub64 = lax.broadcasted_iota(jnp.int32, (N_EXPERTS, tm), 0).astype(F32)
    e_rows, s_rows = [], []
    sel = jnp.zeros(masked.shape, F32)
    for _ in range(TOP_K):
        _, ei = _first_argmax(masked, sub64, N_EXPERTS)
        hit = sub64 == ei
        e_rows.append(ei)
        s_rows.append(jnp.sum(jnp.where(hit, scores, 0.0), axis=0, keepdims=True))
        sel = jnp.where(hit, 1.0, sel)
        masked = jnp.where(hit, -jnp.inf, masked)
    total = s_rows[0]
    for s in s_rows[1:]:
        total = total + s
    w_rows = [s / total * ROUTED_SCALE for s in s_rows]
    before = jnp.dot(sel.astype(BF16), upper_ref[...], preferred_element_type=F32) + carry_ref[:, 0:1]
    r_rows = [jnp.sum(jnp.where(sub64 == ei, before, 0.0), axis=0, keepdims=True) for ei in e_rows]
    zrow = jnp.zeros((8 - TOP_K, tm), F32)
    ek_ref[...] = jnp.concatenate(e_rows + [zrow], axis=0).astype(jnp.int32)
    rk_ref[...] = jnp.concatenate(r_rows + [zrow], axis=0).astype(jnp.int32)
    wc_ref[...] = jnp.concatenate(w_rows + [jnp.zeros((LANES - TOP_K, tm), F32)], axis=0).T
    carry_ref[...] = carry_ref[...] + jnp.sum(sel, axis=1, keepdims=True)

    @pl.when(i == pl.num_programs(0) - 1)
    def _():
        counts = carry_ref[...]
        padded = jnp.floor((counts + (EXPERT_BLOCK - 1.0)) * (1.0 / EXPERT_BLOCK)) * EXPERT_BLOCK
        r = lax.broadcasted_iota(jnp.int32, (N_EXPERTS, N_EXPERTS), 0)
        cc = lax.broadcasted_iota(jnp.int32, (N_EXPERTS, N_EXPERTS), 1)
        pad_end = jnp.dot((cc <= r).astype(F32), padded, precision=lax.Precision.HIGHEST,
                          preferred_element_type=F32)
        pad_start = pad_end - padded
        ps_ref[...] = pad_start
        blk0 = lax.broadcasted_iota(jnp.int32, (N_EXPERTS, MAX_BLOCK_LANES), 1).astype(F32) * EXPERT_BLOCK
        e_of_blk = jnp.minimum(jnp.sum((pad_end[:, 0:1] <= blk0).astype(F32), axis=0, keepdims=True),
                               N_EXPERTS - 1.0)
        sub = lax.broadcasted_iota(jnp.int32, (N_EXPERTS, MAX_BLOCK_LANES), 0).astype(F32)
        seg_end = jnp.sum(jnp.where(sub == e_of_blk, (pad_start + counts)[:, 0:1], 0.0), axis=0, keepdims=True)
        n_valid = jnp.clip(seg_end - blk0[0:1, :], 0.0, float(EXPERT_BLOCK))
        n_active = jnp.broadcast_to(jnp.sum((n_valid > 0.0).astype(F32), axis=1, keepdims=True), n_valid.shape)
        be_ref[...] = jnp.concatenate([e_of_blk, n_valid, n_active, jnp.zeros((5, MAX_BLOCK_LANES), F32)],
                                      axis=0).astype(jnp.int32)


def _route_pallas(logits_t, bias_col):
    t = logits_t.shape[1]
    tm = min(ROUTE_TILE, t)
    row8 = pl.BlockSpec((8, tm), lambda i: (0, i))
    return pl.pallas_call(
        _route_kernel,
        grid=(t // tm,),
        in_specs=[pl.BlockSpec((LANES, tm), lambda i: (0, i)),
                  pl.BlockSpec((N_EXPERTS, 1), lambda i: (0, 0))],
        out_specs=[row8, row8,
                   pl.BlockSpec((tm, LANES), lambda i: (i, 0)),
                   pl.BlockSpec((N_EXPERTS, LANES), lambda i: (0, 0)),
                   pl.BlockSpec((8, MAX_BLOCK_LANES), lambda i: (0, 0))],
        out_shape=[jax.ShapeDtypeStruct((8, t), jnp.int32),
                   jax.ShapeDtypeStruct((8, t), jnp.int32),
                   jax.ShapeDtypeStruct((t, LANES), F32),
                   jax.ShapeDtypeStruct((N_EXPERTS, LANES), F32),
                   jax.ShapeDtypeStruct((8, MAX_BLOCK_LANES), jnp.int32)],
        scratch_shapes=[pltpu.VMEM((N_EXPERTS, LANES), F32), pltpu.VMEM((tm, tm), BF16)],
        compiler_params=_params(0, 1),
        name="route",
    )(logits_t, bias_col)


def _slot_kernel(ek_ref, rk_ref, ps_ref, slot_ref):
    tm = ek_ref.shape[1]
    sub64 = lax.broadcasted_iota(jnp.int32, (N_EXPERTS, tm), 0)
    start = ps_ref[:, 0:1]
    rows = []
    for kk in range(TOP_K):
        seg = jnp.sum(jnp.where(sub64 == ek_ref[kk:kk + 1, :], start, 0.0), axis=0, keepdims=True)
        rows.append(seg.astype(jnp.int32) + rk_ref[kk:kk + 1, :])
    slot_ref[...] = jnp.concatenate(rows + [jnp.zeros((8 - TOP_K, tm), jnp.int32)], axis=0)


def _slots(ek, rk, ps):
    t = ek.shape[1]
    tm = min(ROUTE_TILE, t)
    row8 = pl.BlockSpec((8, tm), lambda i: (0, i))
    return pl.pallas_call(
        _slot_kernel,
        grid=(t // tm,),
        in_specs=[row8, row8, pl.BlockSpec((N_EXPERTS, LANES), lambda i: (0, 0))],
        out_specs=row8,
        out_shape=jax.ShapeDtypeStruct((8, t), jnp.int32),
        compiler_params=_params(1),
        name="slots",
    )(ek, rk, ps)


SC_ROWS = 32
SC_INDEX_WIDTH = LANES
EXPERT_RANGES = 4


def _sc_mesh():
    info = plsc.get_sparse_core_info()
    mesh = plsc.VectorSubcoreMesh(core_axis_name="c", subcore_axis_name="s")
    return mesh, info.num_cores, info.num_cores * info.num_subcores


def _sc_scatter_rows(x, idx, out):
    n, w = x.shape
    in_place = not isinstance(out, jax.ShapeDtypeStruct)
    mesh, n_cores, n_workers = _sc_mesh()
    per_w = n // n_workers
    n_chunks = per_w // SC_ROWS
    assert per_w * n_workers == n and n_chunks * SC_ROWS == per_w and n_chunks % 2 == 0

    def scatter_rows(x_hbm, i_hbm, o_hbm, idx_v, rows_v, read_sem, scatter_sem):
        base = (lax.axis_index("s") * n_cores + lax.axis_index("c")) * per_w

        def read(j, b):
            return pltpu.make_async_copy(x_hbm.at[pl.ds(pl.multiple_of(base + j * SC_ROWS, 8), SC_ROWS)],
                                         rows_v.at[b], read_sem.at[b])

        def scatter(b):
            return pltpu.make_async_copy(rows_v.at[b], o_hbm.at[idx_v.at[b]], scatter_sem.at[b])

        read(0, 0).start()

        @pl.loop(0, n_chunks, step=2)
        def _(j):
            for b in range(2):
                jj = j + b

                @pl.when(jj >= 1)
                def _():
                    scatter(1 - b).wait()

                @pl.when(jj + 1 < n_chunks)
                def _():
                    read(jj + 1, 1 - b).start()

                pltpu.sync_copy(i_hbm.at[pl.ds(pl.multiple_of(base + jj * SC_ROWS, 8), SC_ROWS)], idx_v.at[b])
                read(jj, b).wait()
                scatter(b).start()

        scatter((n_chunks - 1) % 2).wait()

    kern = pl.kernel(
        scatter_rows, out_type=() if in_place else out, mesh=mesh,
        scratch_types=[pltpu.VMEM((2, SC_ROWS), jnp.int32), pltpu.VMEM((2, SC_ROWS, w), x.dtype),
                       pltpu.SemaphoreType.DMA((2,)), pltpu.SemaphoreType.DMA((2,))])
    if in_place:
        kern(x, idx, out)
        return out
    return kern(x, idx)


def _sc_gather_rows(y, slot_flat):
    n = slot_flat.shape[0]
    dp = y.shape[1]
    mesh, n_cores, n_workers = _sc_mesh()
    per_w = n // n_workers
    n_chunks = per_w // SC_ROWS
    assert per_w * n_workers == n and n_chunks * SC_ROWS == per_w and n_chunks % 2 == 0

    @functools.partial(
        pl.kernel, out_type=jax.ShapeDtypeStruct((n, dp), y.dtype), mesh=mesh,
        scratch_types=[pltpu.VMEM((per_w,), jnp.int32), pltpu.VMEM((2, SC_ROWS, dp), y.dtype),
                       pltpu.SemaphoreType.DMA((2,)), pltpu.SemaphoreType.DMA((2,))])
    def combine(y_hbm, i_hbm, o_hbm, idx_v, rows_v, gather_sem, write_sem):
        base = (lax.axis_index("s") * n_cores + lax.axis_index("c")) * per_w
        pltpu.sync_copy(i_hbm.at[pl.ds(pl.multiple_of(base, 8), per_w)], idx_v)

        def gather(j, b):
            return pltpu.make_async_copy(y_hbm.at[idx_v.at[pl.ds(pl.multiple_of(j * SC_ROWS, 8), SC_ROWS)]],
                                         rows_v.at[b], gather_sem.at[b])

        def write(j, b):
            return pltpu.make_async_copy(rows_v.at[b],
                                         o_hbm.at[pl.ds(pl.multiple_of(base + j * SC_ROWS, 8), SC_ROWS)],
                                         write_sem.at[b])

        gather(0, 0).start()

        @pl.loop(0, n_chunks, step=2)
        def _(j):
            for b in range(2):
                jj = j + b

                @pl.when(jj >= 1)
                def _():
                    write(jj - 1, 1 - b).wait()

                @pl.when(jj + 1 < n_chunks)
                def _():
                    gather(jj + 1, 1 - b).start()

                gather(jj, b).wait()
                write(jj, b).start()

        write(n_chunks - 1, (n_chunks - 1) % 2).wait()

    return combine(y, slot_flat)


def _layer(x, cond_ada, positions, w_in, dn_conv_w, dn_a_log, dn_dt_bias, dn_norm_gain, mla_q_norm_gain,
           w_q_up, mla_kv_norm_gain, w_kv_up, w_out, norm1_gain, norm2_gain, w_router, router_bias,
           w_exp_gate_up, w_exp_down, w_sh_gate_up, w_sh_down, out_gain):
    bsz, seq, d = x.shape
    t = bsz * seq
    x2 = x.reshape(t, d)
    ada3 = cond_ada.reshape(bsz, 6, d)

    s_z = 4 * DN_WIDTH
    s_a = s_z + 2 * DN_HEADS
    s_kpe = s_a + MLA_Q_RANK + MLA_KV_RANK
    w_main = jnp.concatenate([w_in[:, :s_z], w_in[:, s_a:s_kpe]], axis=1).astype(BF16)
    w_aux = jnp.concatenate([w_in[:, s_kpe:], w_in[:, s_z:s_a],
                             jnp.zeros((d, LANES - MLA_ROPE - 2 * DN_HEADS), F32)], axis=1).astype(BF16)
    auxp = jnp.zeros((8, LANES), F32)
    auxp = auxp.at[0, AUX_G:AUX_G + DN_HEADS].set(dn_a_log).at[1, AUX_G:AUX_G + DN_HEADS].set(dn_dt_bias)
    main, auxc, auxr = _in_proj(x2, ada3, norm1_gain.reshape(1, d), w_main, w_aux, auxp, seq)

    dn = _delta(main.reshape(bsz, seq, MAIN_WIDTH), auxc.reshape(bsz, seq, LANES), auxr,
                dn_conv_w.T, dn_norm_gain.reshape(1, DN_DIM))

    qk = MLA_NOPE + MLA_ROPE
    wq3 = w_q_up.reshape(MLA_Q_RANK, MLA_HEADS, qk)
    wq = jnp.concatenate([wq3, jnp.zeros((MLA_Q_RANK, MLA_HEADS, MLA_QK_PAD - qk), F32)], axis=2)
    wq = wq.reshape(MLA_Q_RANK, MLA_HEADS * MLA_QK_PAD).astype(BF16)
    wkv3 = w_kv_up.reshape(MLA_KV_RANK, MLA_HEADS, MLA_NOPE + MLA_V)
    wkv = jnp.concatenate([wkv3[:, :, :MLA_NOPE].reshape(MLA_KV_RANK, -1),
                           wkv3[:, :, MLA_NOPE:].reshape(MLA_KV_RANK, -1)], axis=1).astype(BF16)
    half = MLA_ROPE // 2
    inv_freq = ROPE_THETA ** (-jnp.arange(half, dtype=F32) / half)
    zeros = jnp.zeros((LANES - MLA_ROPE,), F32)
    rope_tab = jnp.zeros((8, LANES), F32)
    rope_tab = rope_tab.at[0].set(jnp.concatenate([inv_freq, inv_freq, zeros]))
    rope_tab = rope_tab.at[1].set(jnp.concatenate([jnp.ones((MLA_ROPE,), F32), zeros]))
    rope_tab = rope_tab.at[2].set(jnp.concatenate([-jnp.ones((half,), F32), jnp.ones((half,), F32), zeros]))
    q, k, v = _mla_proj(main, auxc, positions.reshape(t, 1), mla_q_norm_gain.reshape(1, -1),
                        mla_kv_norm_gain.reshape(1, -1), wq, wkv, rope_tab, seq)
    mla = _attention(q.reshape(bsz, seq, -1), k.reshape(bsz, seq, -1), v)

    w_r = jnp.concatenate([w_router, jnp.zeros((d, LANES - N_EXPERTS), F32)], axis=1)
    w_r_hi = w_r.astype(BF16)
    w_r = jnp.concatenate([w_r_hi, (w_r - w_r_hi.astype(F32)).astype(BF16)], axis=1)
    x1, h2, logits_t = _out_proj(x2, dn.reshape(t, DN_WIDTH), mla.reshape(t, -1), ada3,
                                 norm2_gain.reshape(1, d), w_out.astype(BF16), w_r, seq)

    n_rows = -(-(t * TOP_K + N_EXPERTS * (EXPERT_BLOCK - 1)) // EXPERT_BLOCK) * EXPERT_BLOCK
    n_blocks = n_rows // EXPERT_BLOCK
    assert n_blocks <= MAX_BLOCK_LANES
    ek, rk, wc, seg_start, blocks = _route_pallas(logits_t, router_bias.reshape(N_EXPERTS, 1))
    slot_flat = _slots(ek, rk, seg_start)[:TOP_K].reshape(-1)
    n_assign = t * TOP_K

    ids = jnp.broadcast_to(jnp.arange(n_assign, dtype=jnp.int32)[:, None], (n_assign, SC_INDEX_WIDTH))
    row_assign = jax.new_ref(jnp.full((n_rows, SC_INDEX_WIDTH), n_assign, jnp.int32))
    _sc_scatter_rows(ids, slot_flat, row_assign)
    row_assign = row_assign[...][:, 0]
    row_tok = row_assign % t

    shared = _shared_ffn(h2, w_sh_gate_up.astype(BF16), w_sh_down.astype(BF16))

    n_ranges = EXPERT_RANGES if n_blocks % EXPERT_RANGES == 0 else 1
    nb = n_blocks // n_ranges
    nr = nb * EXPERT_BLOCK
    y_tok = jax.ShapeDtypeStruct((n_assign + SC_ROWS, d // 2), jnp.uint32)
    for r in range(n_ranges):
        x_sorted = _sc_gather_rows(h2, row_tok[r * nr:(r + 1) * nr])
        active = jnp.clip(blocks[2, :1] - r * nb, 0, nb)
        y_sorted = _experts(blocks[0, r * nb:(r + 1) * nb], blocks[1, r * nb:(r + 1) * nb], active, x_sorted,
                            w_exp_gate_up, w_exp_down)
        y_tok = _sc_scatter_rows(y_sorted, row_assign[r * nr:(r + 1) * nr], y_tok)
        if r == 0:
            y_tok = jax.new_ref(y_tok)
    out = _final(x1, shared, y_tok[...], wc, ada3, out_gain.reshape(1, d), seq)
    return out.reshape(bsz, seq, d)


def kernel(x, c, positions, w_ada, b_ada, norm1_gain, w_in, dn_conv_w, dn_a_log, dn_dt_bias, dn_norm_gain,
           mla_q_norm_gain, w_q_up, mla_kv_norm_gain, w_kv_up, w_out, norm2_gain, w_router, router_bias,
           w_exp_gate_up, w_exp_down, w_sh_gate_up, w_sh_down, final_norm_gain):
    depth = w_ada.shape[0]
    assert depth == 1, "the final RMSNorm is fused into the single layer's last kernel"
    ada = _ada(c, w_ada[0], b_ada[0])
    return _layer(x, ada, positions, w_in[0], dn_conv_w[0], dn_a_log[0], dn_dt_bias[0], dn_norm_gain[0],
                  mla_q_norm_gain[0], w_q_up[0], mla_kv_norm_gain[0], w_kv_up[0], w_out[0], norm1_gain[0],
                  norm2_gain[0], w_router[0], router_bias[0], w_exp_gate_up[0], w_exp_down[0],
                  w_sh_gate_up[0], w_sh_down[0], final_norm_gain)
```

```python
import functools

import jax
import jax.numpy as jnp
from jax import lax
from jax.experimental import pallas as pl
from jax.experimental.pallas import tpu as pltpu
from jax.experimental.pallas import tpu_sc as plsc

F32 = jnp.float32
BF16 = jnp.bfloat16

DN_HEADS = 8
DN_DIM = 128
DN_WIDTH = DN_HEADS * DN_DIM
DN_CONV = 4
DN_CHUNK = 64
MLA_HEADS = 8
MLA_Q_RANK = 512
MLA_KV_RANK = 512
MLA_NOPE = 128
MLA_ROPE = 64
MLA_V = 128
MLA_QK_PAD = 256
ROPE_THETA = 10000.0
N_EXPERTS = 64
TOP_K = 6
N_GROUPS = 8
TOPK_GROUPS = 4
EXPERT_FF = 512
ROUTED_SCALE = 2.5
EXPERT_BLOCK = 512
RMS_EPS = 1e-6
L2_EPS = 1e-6

LANES = 128
MXU_WIDTH = 256
MAIN_WIDTH = 4 * DN_WIDTH + MLA_Q_RANK + MLA_KV_RANK
AUX_BETA = MLA_ROPE
AUX_G = MLA_ROPE + DN_HEADS
VMEM_LIMIT = 56 * 1024 * 1024


def _params(n_parallel, n_arbitrary=0):
    sem = ("parallel",) * n_parallel + ("arbitrary",) * n_arbitrary
    return pltpu.CompilerParams(dimension_semantics=sem, vmem_limit_bytes=VMEM_LIMIT)


def _silu(x):
    return x * jax.nn.sigmoid(x)


def _bdot(a, b):
    return jnp.dot(a.astype(BF16), b.astype(BF16), preferred_element_type=F32)


def _bdot_nt(a, b):
    return lax.dot_general(a.astype(BF16), b.astype(BF16), (((1,), (1,)), ((), ())),
                           preferred_element_type=F32)


def _rms(x, gain):
    return x * lax.rsqrt(jnp.mean(x * x, axis=-1, keepdims=True) + RMS_EPS) * gain


HIGH_HALF = 0xFFFF0000


def _pack_bf16_pairs(x):
    n = x.shape[1] // 2
    bits = lax.bitcast_convert_type(x.astype(BF16).astype(F32), jnp.uint32)
    return (bits[:, :n] >> 16) | (bits[:, n:] & jnp.uint32(HIGH_HALF))


def _unpack_bf16_pairs(p):
    lo = lax.bitcast_convert_type(p << 16, F32)
    hi = lax.bitcast_convert_type(p & jnp.uint32(HIGH_HALF), F32)
    return jnp.concatenate([lo, hi], axis=1)


def _ada_kernel(c_ref, w_ref, b_ref, o_ref):
    o_ref[...] = _bdot(_silu(c_ref[...]), w_ref[...]) + b_ref[...]


def _ada(c, w_ada, b_ada):
    bsz, d = c.shape
    n = w_ada.shape[1]
    tn = 1024
    return pl.pallas_call(
        _ada_kernel,
        grid=(n // tn,),
        in_specs=[pl.BlockSpec((bsz, d), lambda j: (0, 0)),
                  pl.BlockSpec((d, tn), lambda j: (0, j)),
                  pl.BlockSpec((1, tn), lambda j: (0, j))],
        out_specs=pl.BlockSpec((bsz, tn), lambda j: (0, j)),
        out_shape=jax.ShapeDtypeStruct((bsz, n), F32),
        compiler_params=_params(1),
        name="ada",
    )(c, w_ada, b_ada.reshape(1, n))


def _in_proj_kernel(x_ref, ada_ref, gain_ref, wm_ref, wa_ref, auxp_ref,
                    main_ref, auxc_ref, auxr_ref, h_ref):
    j = pl.program_id(1)

    @pl.when(j == 0)
    def _():
        ada = ada_ref[0]
        h = _rms(x_ref[...], gain_ref[...]) * (1.0 + ada[1:2, :]) + ada[0:1, :]
        hb = h.astype(BF16)
        h_ref[...] = hb
        aux = jnp.dot(hb, wa_ref[...], preferred_element_type=F32)
        tm = aux.shape[0]
        lane = lax.broadcasted_iota(jnp.int32, aux.shape, 1)
        is_beta = (lane >= AUX_BETA) & (lane < AUX_G)
        is_g = (lane >= AUX_G) & (lane < AUX_G + DN_HEADS)
        a_log = auxp_ref[0:1, :]
        dt_bias = auxp_ref[1:2, :]
        sp_in = aux + dt_bias
        softplus = jnp.maximum(sp_in, 0.0) + jnp.log(1.0 + jnp.exp(-jnp.abs(sp_in)))
        g = -jnp.exp(a_log) * softplus
        aux = jnp.where(is_beta, jax.nn.sigmoid(aux), jnp.where(is_g, g, aux))
        r = lax.broadcasted_iota(jnp.int32, (LANES, LANES), 0)
        cidx = lax.broadcasted_iota(jnp.int32, (LANES, LANES), 1)
        tri = ((cidx <= r) & (cidx // DN_CHUNK == r // DN_CHUNK)).astype(F32)
        g_cols = (cidx >= AUX_G) & (cidx < AUX_G + DN_HEADS)
        parts = []
        for t in range(tm // LANES):
            blk = aux[t * LANES:(t + 1) * LANES, :]
            cs = jnp.dot(tri, blk, precision=lax.Precision.HIGHEST, preferred_element_type=F32)
            parts.append(jnp.where(g_cols, cs, blk))
        aux = jnp.concatenate(parts, axis=0)
        auxc_ref[...] = aux
        auxr_ref[...] = aux.T

    main_ref[...] = jnp.dot(h_ref[...], wm_ref[...], preferred_element_type=F32).astype(main_ref.dtype)


def _in_proj(x2, ada3, gain, w_main, w_aux, auxp, seq):
    t, d = x2.shape
    n = w_main.shape[1]
    tm = min(1024, seq)
    tn = 1024
    per_b = seq // tm
    return pl.pallas_call(
        _in_proj_kernel,
        grid=(t // tm, n // tn),
        in_specs=[pl.BlockSpec((tm, d), lambda i, j: (i, 0)),
                  pl.BlockSpec((1, 6, d), lambda i, j: (i // per_b, 0, 0)),
                  pl.BlockSpec((1, d), lambda i, j: (0, 0)),
                  pl.BlockSpec((d, tn), lambda i, j: (0, j)),
                  pl.BlockSpec((d, LANES), lambda i, j: (0, 0)),
                  pl.BlockSpec((8, LANES), lambda i, j: (0, 0))],
        out_specs=[pl.BlockSpec((tm, tn), lambda i, j: (i, j)),
                   pl.BlockSpec((tm, LANES), lambda i, j: (i, 0)),
                   pl.BlockSpec((LANES, tm), lambda i, j: (0, i))],
        out_shape=[jax.ShapeDtypeStruct((t, n), BF16),
                   jax.ShapeDtypeStruct((t, LANES), F32),
                   jax.ShapeDtypeStruct((LANES, t), F32)],
        scratch_shapes=[pltpu.VMEM((tm, d), BF16)],
        compiler_params=_params(1, 1),
        name="in_proj",
    )(x2, ada3, gain, w_main, w_aux, auxp)


def _unit_lower_inverses(ms):
    c = ms[0].shape[0]
    r = lax.broadcasted_iota(jnp.int32, (c, c), 0)
    cc = lax.broadcasted_iota(jnp.int32, (c, c), 1)
    eye = (r == cc).astype(F32)
    same = (r // 16) == (cc // 16)
    md = [jnp.where(same, m, 0.0) for m in ms]
    mo = [jnp.where(same, 0.0, m) for m in ms]
    p1 = [_bdot(a, a) for a in md]
    p2 = [_bdot(a, a) for a in p1]
    p3 = [_bdot(a, a) for a in p2]
    td = [eye - a for a in md]
    td = [t + _bdot(t, p) for t, p in zip(td, p1)]
    td = [t + _bdot(t, p) for t, p in zip(td, p2)]
    td = [t + _bdot(t, p) for t, p in zip(td, p3)]
    n1 = [_bdot(t, o) for t, o in zip(td, mo)]
    n2 = [_bdot(n, n) for n in n1]
    left = [(eye - a) + _bdot(eye - a, b) for a, b in zip(n1, n2)]
    return [_bdot(l, t) for l, t in zip(left, td)]


DELTA_TILE = 2048
DELTA_SCAN_TILE = 512


def _delta_prep_kernel(q_ref, k_ref, v_ref, qh_ref, kh_ref, vh_ref, auxc_ref, auxr_ref, wq_ref, wk_ref, wv_ref,
                       u_ref, w_ref, qe_ref, kd_ref, a_ref, pad_ref):
    head = pl.program_id(1)
    first_tile = pl.program_id(2) == 0
    ts = q_ref.shape[0]
    c = DN_CHUNK
    halo = qh_ref.shape[0]

    def conv_silu(x_ref, h_ref, w_ref):
        top = pad_ref.shape[0] - ts
        pad_ref[0:top, :] = jnp.where(first_tile, 0.0, h_ref[halo - top:, :].astype(F32))
        pad_ref[top:, :] = x_ref[...].astype(F32)
        w = w_ref[...]
        acc = pad_ref[top:, :] * w[DN_CONV - 1:DN_CONV, :]
        for s in range(1, DN_CONV):
            acc = acc + pad_ref[top - s:top - s + ts, :] * w[DN_CONV - 1 - s:DN_CONV - s, :]
        return _silu(acc)

    q = conv_silu(q_ref, qh_ref, wq_ref)
    k = conv_silu(k_ref, kh_ref, wk_ref)
    v = conv_silu(v_ref, vh_ref, wv_ref)
    q = q * lax.rsqrt(jnp.sum(q * q, axis=-1, keepdims=True) + L2_EPS) * (DN_DIM ** -0.5)
    k = k * lax.rsqrt(jnp.sum(k * k, axis=-1, keepdims=True) + L2_EPS)

    auxc = auxc_ref[...]
    lane = lax.broadcasted_iota(jnp.int32, auxc.shape, 1)
    beta_all = jnp.sum(jnp.where(lane == AUX_BETA + head, auxc, 0.0), axis=1, keepdims=True)
    gcol_all = jnp.sum(jnp.where(lane == AUX_G + head, auxc, 0.0), axis=1, keepdims=True)
    grows = auxr_ref[AUX_G:AUX_G + DN_HEADS, :]
    head_row = lax.broadcasted_iota(jnp.int32, grows.shape, 0)
    grow_all = jnp.sum(jnp.where(head_row == head, grows, 0.0), axis=0, keepdims=True)

    ri = lax.broadcasted_iota(jnp.int32, (c, c), 0)
    ci = lax.broadcasted_iota(jnp.int32, (c, c), 1)
    causal = ri >= ci
    strict = ri > ci
    chunks = [slice(j * c, (j + 1) * c) for j in range(ts // c)]
    decay = [jnp.exp(jnp.where(causal, gcol_all[rows] - grow_all[:, rows], -jnp.inf)) for rows in chunks]
    kb = [k[rows] * beta_all[rows] for rows in chunks]
    both = [_bdot_nt(jnp.concatenate([kb_j, q[rows]], axis=0), k[rows])
            for kb_j, rows in zip(kb, chunks)]
    tinv = _unit_lower_inverses([jnp.where(strict, b[:c] * d, 0.0) for b, d in zip(both, decay)])
    eg = [jnp.exp(gcol_all[rows]) for rows in chunks]
    sol = [_bdot(t, jnp.concatenate([v[rows] * beta_all[rows], kb_j * e], axis=1))
           for t, rows, kb_j, e in zip(tinv, chunks, kb, eg)]
    for j, rows in enumerate(chunks):
        gcol = gcol_all[rows]
        u_ref[rows, :] = sol[j][:, :DN_DIM].astype(u_ref.dtype)
        w_ref[rows, :] = sol[j][:, DN_DIM:].astype(w_ref.dtype)
        qe_ref[rows, :] = (q[rows] * eg[j]).astype(qe_ref.dtype)
        kd_ref[rows, :] = (k[rows] * jnp.exp(gcol[c - 1:c, :] - gcol)).astype(kd_ref.dtype)
        a_ref[rows, :] = jnp.where(causal, both[j][c:] * decay[j], 0.0).astype(a_ref.dtype)


def _delta_scan_kernel(u_ref, w_ref, qe_ref, kd_ref, a_ref, z_ref, auxc_ref, gain_ref, o_ref, st_ref):
    seq = u_ref.shape[0]
    n_heads = a_ref.shape[0]
    c = DN_CHUNK

    @pl.when(pl.program_id(1) == 0)
    def _():
        st_ref[...] = jnp.zeros_like(st_ref)

    gain = gain_ref[...]
    lane = lax.broadcasted_iota(jnp.int32, (1, LANES), 1)

    def scan_chunk(i, carry):
        start = pl.multiple_of(i * c, c)
        rows = pl.ds(start, c)
        last = auxc_ref[pl.ds(start + c - 1, 1), :]
        heads = range(n_heads)
        cols = [slice(hh * DN_DIM, (hh + 1) * DN_DIM) for hh in heads]
        state = [st_ref[hh] for hh in heads]
        sb = [s.astype(BF16) for s in state]
        ws = [jnp.dot(w_ref[rows, cl], s, preferred_element_type=F32) for cl, s in zip(cols, sb)]
        qs = [jnp.dot(qe_ref[rows, cl], s, preferred_element_type=F32) for cl, s in zip(cols, sb)]
        vb = [(u_ref[rows, cl].astype(F32) - x).astype(BF16) for cl, x in zip(cols, ws)]
        kv = [lax.dot_general(kd_ref[rows, cl], x, (((0,), (0,)), ((), ())), preferred_element_type=F32)
              for cl, x in zip(cols, vb)]
        av = [jnp.dot(a_ref[hh, rows, :], x, preferred_element_type=F32) for hh, x in zip(heads, vb)]
        for hh in heads:
            glast = jnp.sum(jnp.where(lane == AUX_G + hh, last, 0.0), axis=1, keepdims=True)
            st_ref[hh] = state[hh] * jnp.exp(glast) + kv[hh]
            z = z_ref[rows, cols[hh]].astype(F32)
            o_ref[rows, cols[hh]] = (_rms(qs[hh] + av[hh], gain) * _silu(z)).astype(o_ref.dtype)
        return carry

    lax.fori_loop(0, seq // c, scan_chunk, 0)


def _delta(main3, auxc3, auxr, conv_w_t, gain):
    bsz, seq, _ = main3.shape
    hb = DN_WIDTH // DN_DIM
    ts = min(DELTA_TILE, seq)
    halo = 16
    tiles = seq // ts

    def col(offset):
        return pl.BlockSpec((None, ts, DN_DIM), lambda b, h, i: (b, i, offset * hb + h))

    def halo_col(offset):
        return pl.BlockSpec((None, halo, DN_DIM),
                            lambda b, h, i: (b, jnp.maximum(i * (ts // halo) - 1, 0), offset * hb + h))

    def wcol(offset):
        return pl.BlockSpec((DN_CONV, DN_DIM), lambda b, h, i: (0, offset * hb + h))

    tok = pl.BlockSpec((None, ts, DN_DIM), lambda b, h, i: (b, i, h))
    tok_shape = jax.ShapeDtypeStruct((bsz, seq, DN_WIDTH), BF16)
    u, w, qe, kd, a = pl.pallas_call(
        _delta_prep_kernel,
        grid=(bsz, DN_HEADS, tiles),
        in_specs=[col(0), col(1), col(2), halo_col(0), halo_col(1), halo_col(2),
                  pl.BlockSpec((None, ts, LANES), lambda b, h, i: (b, i, 0)),
                  pl.BlockSpec((LANES, ts), lambda b, h, i: (0, b * tiles + i)),
                  wcol(0), wcol(1), wcol(2)],
        out_specs=[tok, tok, tok, tok,
                   pl.BlockSpec((None, None, ts, DN_CHUNK), lambda b, h, i: (b, h, i, 0))],
        out_shape=[tok_shape, tok_shape, tok_shape, tok_shape,
                   jax.ShapeDtypeStruct((bsz, DN_HEADS, seq, DN_CHUNK), BF16)],
        scratch_shapes=[pltpu.VMEM((ts + 8, DN_DIM), F32)],
        compiler_params=_params(3),
        name="delta_prep",
    )(main3, main3, main3, main3, main3, main3, auxc3, auxr, conv_w_t, conv_w_t, conv_w_t)

    tsc = min(DELTA_SCAN_TILE, seq)
    wide = pl.BlockSpec((None, tsc, DN_WIDTH), lambda b, i: (b, i, 0))
    return pl.pallas_call(
        _delta_scan_kernel,
        grid=(bsz, seq // tsc),
        in_specs=[wide, wide, wide, wide,
                  pl.BlockSpec((None, DN_HEADS, tsc, DN_CHUNK), lambda b, i: (b, 0, i, 0)),
                  pl.BlockSpec((None, tsc, DN_WIDTH), lambda b, i: (b, i, 3)),
                  pl.BlockSpec((None, tsc, LANES), lambda b, i: (b, i, 0)),
                  pl.BlockSpec((1, DN_DIM), lambda b, i: (0, 0))],
        out_specs=wide,
        out_shape=tok_shape,
        scratch_shapes=[pltpu.VMEM((DN_HEADS, DN_DIM, DN_DIM), F32)],
        compiler_params=_params(1, 1),
        name="delta_scan",
    )(u, w, qe, kd, a, main3, auxc3, gain)


def _mla_proj_kernel(cq_ref, ckv_ref, auxc_ref, pos_ref, qg_ref, kvg_ref, wq_ref, wkv_ref, rope_ref,
                     q_ref, k_ref, v_ref):
    scale = (MLA_NOPE + MLA_ROPE) ** -0.5
    ang = pos_ref[...].astype(F32) * rope_ref[0:1, :]
    cos_t = jnp.cos(ang) * rope_ref[1:2, :]
    sin_t = jnp.sin(ang) * rope_ref[2:3, :]
    lane = lax.broadcasted_iota(jnp.int32, ang.shape, 1)
    first = lane < MLA_ROPE // 2

    def rope(a):
        swapped = jnp.where(first, pltpu.roll(a, LANES - MLA_ROPE // 2, 1), pltpu.roll(a, MLA_ROPE // 2, 1))
        return a * cos_t + swapped * sin_t

    ql = _bdot(_rms(cq_ref[...].astype(F32), qg_ref[...]), wq_ref[...]) * scale
    kv = _bdot(_rms(ckv_ref[...].astype(F32), kvg_ref[...]), wkv_ref[...])
    kpe = rope(auxc_ref[...]).astype(k_ref.dtype)
    for h in range(MLA_HEADS):
        o = h * MLA_QK_PAD
        q_ref[:, o:o + MLA_NOPE] = ql[:, o:o + MLA_NOPE].astype(q_ref.dtype)
        q_ref[:, o + MLA_NOPE:o + MLA_QK_PAD] = rope(ql[:, o + MLA_NOPE:o + MLA_QK_PAD]).astype(q_ref.dtype)
        k_ref[:, o:o + MLA_NOPE] = kv[:, h * MLA_NOPE:(h + 1) * MLA_NOPE].astype(k_ref.dtype)
        k_ref[:, o + MLA_NOPE:o + MLA_QK_PAD] = kpe
    v_ref[...] = kv[:, MLA_HEADS * MLA_NOPE:].T.astype(v_ref.dtype)


def _mla_proj(main, auxc, pos, q_gain, kv_gain, wq, wkv, rope_tab, seq):
    t = main.shape[0]
    tm = min(512, seq)
    per_b = seq // tm
    cq_blk = (4 * DN_WIDTH) // MLA_Q_RANK
    hq = MLA_HEADS * MLA_QK_PAD
    hv = MLA_HEADS * MLA_V
    return pl.pallas_call(
        _mla_proj_kernel,
        grid=(t // tm,),
        in_specs=[pl.BlockSpec((tm, MLA_Q_RANK), lambda i: (i, cq_blk)),
                  pl.BlockSpec((tm, MLA_KV_RANK), lambda i: (i, cq_blk + 1)),
                  pl.BlockSpec((tm, LANES), lambda i: (i, 0)),
                  pl.BlockSpec((tm, 1), lambda i: (i, 0)),
                  pl.BlockSpec((1, MLA_Q_RANK), lambda i: (0, 0)),
                  pl.BlockSpec((1, MLA_KV_RANK), lambda i: (0, 0)),
                  pl.BlockSpec((MLA_Q_RANK, hq), lambda i: (0, 0)),
                  pl.BlockSpec((MLA_KV_RANK, 2 * hv), lambda i: (0, 0)),
                  pl.BlockSpec((8, LANES), lambda i: (0, 0))],
        out_specs=[pl.BlockSpec((tm, hq), lambda i: (i, 0)),
                   pl.BlockSpec((tm, hq), lambda i: (i, 0)),
                   pl.BlockSpec((None, hv, tm), lambda i: (i // per_b, 0, i % per_b))],
        out_shape=[jax.ShapeDtypeStruct((t, hq), BF16),
                   jax.ShapeDtypeStruct((t, hq), BF16),
                   jax.ShapeDtypeStruct((t // seq, hv, seq), BF16)],
        compiler_params=_params(1),
        name="mla_proj",
    )(main, main, auxc, pos, q_gain, kv_gain, wq, wkv, rope_tab)


ATTN_PARTS = 4


def _attn_kernel(q_ref, k_ref, vt_ref, o_ref, m_ref, l_ref, acc_ref):
    qi = pl.program_id(2)
    n_parts = m_ref.shape[0]
    tk = m_ref.shape[2]
    q = [q_ref[p * tk:(p + 1) * tk, :] for p in range(n_parts)]
    m_ref[...] = jnp.full_like(m_ref, -jnp.inf)
    l_ref[...] = jnp.zeros_like(l_ref)
    acc_ref[...] = jnp.zeros_like(acc_ref)
    key = lax.broadcasted_iota(jnp.int32, (tk, tk), 0)
    query = lax.broadcasted_iota(jnp.int32, (tk, tk), 1)
    visible = key <= query

    def step(j, parts, masked_part):
        start = pl.multiple_of(j * tk, tk)
        kb = k_ref[pl.ds(start, tk), :]
        vtb = vt_ref[:, pl.ds(start, tk)]
        s = [lax.dot_general(kb, q[p], (((1,), (1,)), ((), ())), preferred_element_type=F32) for p in parts]
        s = [jnp.where(visible, x, -jnp.inf) if p == masked_part else x for p, x in zip(parts, s)]
        m_old = [m_ref[p] for p in parts]
        m_new = [jnp.maximum(mo, jnp.max(x, axis=0, keepdims=True)) for mo, x in zip(m_old, s)]
        e = [jnp.exp(x - mn) for x, mn in zip(s, m_new)]
        pv = [jnp.dot(vtb, x.astype(BF16), preferred_element_type=F32) for x in e]
        for i, p in enumerate(parts):
            alpha = jnp.exp(m_old[i] - m_new[i])
            l_ref[p] = alpha * l_ref[p] + jnp.sum(e[i], axis=0, keepdims=True)
            acc_ref[p] = alpha * acc_ref[p] + pv[i]
            m_ref[p] = m_new[i]

    every = tuple(range(n_parts))

    def body(j, carry):
        step(j, every, None)
        return carry

    lax.fori_loop(0, qi * n_parts, body, 0)
    for d in range(n_parts):
        step(qi * n_parts + d, every[d:], d)
    for p in every:
        o_ref[p * tk:(p + 1) * tk, :] = (acc_ref[p] / l_ref[p]).T.astype(o_ref.dtype)


def _attention(q3, k3, vt3):
    bsz, seq, _ = q3.shape
    tk = min(512, seq)
    n_parts = min(ATTN_PARTS, seq // tk)
    tq = tk * n_parts
    return pl.pallas_call(
        _attn_kernel,
        grid=(bsz, MLA_HEADS, seq // tq),
        in_specs=[pl.BlockSpec((None, tq, MLA_QK_PAD), lambda b, h, i: (b, i, h)),
                  pl.BlockSpec((None, seq, MLA_QK_PAD), lambda b, h, i: (b, 0, h)),
                  pl.BlockSpec((None, MLA_V, seq), lambda b, h, i: (b, h, 0))],
        out_specs=pl.BlockSpec((None, tq, MLA_V), lambda b, h, i: (b, i, h)),
        out_shape=jax.ShapeDtypeStruct((bsz, seq, MLA_HEADS * MLA_V), BF16),
        scratch_shapes=[pltpu.VMEM((n_parts, 1, tk), F32), pltpu.VMEM((n_parts, 1, tk), F32),
                        pltpu.VMEM((n_parts, MLA_V, tk), F32)],
        compiler_params=_params(3),
        name="attention",
    )(q3, k3, vt3)


def _out_proj_kernel(x_ref, dn_ref, mla_ref, ada_ref, gain_ref, w_ref, wr_ref, x1_ref, h2_ref, lg_ref):
    ada = ada_ref[0]
    half = dn_ref.shape[1]
    mix = (jnp.dot(dn_ref[...], w_ref[:half, :], preferred_element_type=F32)
           + jnp.dot(mla_ref[...], w_ref[half:, :], preferred_element_type=F32))
    x1 = x_ref[...] + ada[2:3, :] * mix
    x1_ref[...] = x1
    h2 = _rms(x1, gain_ref[...]) * (1.0 + ada[4:5, :]) + ada[3:4, :]
    h2_ref[...] = _pack_bf16_pairs(h2)
    h_hi = h2.astype(BF16)
    h_lo = (h2 - h_hi.astype(F32)).astype(BF16)
    both = jnp.dot(h_hi, wr_ref[...], preferred_element_type=F32)
    logits = both[:, :LANES] + both[:, LANES:] + jnp.dot(h_lo, wr_ref[:, :LANES], preferred_element_type=F32)
    lg_ref[...] = logits.T


def _out_proj(x2, dn, mla, ada3, gain, w_out, w_router, seq):
    t, d = x2.shape
    tm = min(256, seq)
    per_b = seq // tm
    half = dn.shape[1]
    return pl.pallas_call(
        _out_proj_kernel,
        grid=(t // tm,),
        in_specs=[pl.BlockSpec((tm, d), lambda i: (i, 0)),
                  pl.BlockSpec((tm, half), lambda i: (i, 0)),
                  pl.BlockSpec((tm, half), lambda i: (i, 0)),
                  pl.BlockSpec((1, 6, d), lambda i: (i // per_b, 0, 0)),
                  pl.BlockSpec((1, d), lambda i: (0, 0)),
                  pl.BlockSpec((2 * half, d), lambda i: (0, 0)),
                  pl.BlockSpec((d, 2 * LANES), lambda i: (0, 0))],
        out_specs=[pl.BlockSpec((tm, d), lambda i: (i, 0)),
                   pl.BlockSpec((tm, d // 2), lambda i: (i, 0)),
                   pl.BlockSpec((LANES, tm), lambda i: (0, i))],
        out_shape=[jax.ShapeDtypeStruct((t, d), F32),
                   jax.ShapeDtypeStruct((t, d // 2), jnp.uint32),
                   jax.ShapeDtypeStruct((LANES, t), F32)],
        compiler_params=_params(1),
        name="out_proj",
    )(x2, dn, mla, ada3, gain, w_out, w_router)


def _expert_kernel(be_ref, nv_ref, na_ref, x_ref, wgu_hbm, wd_hbm, y_ref, gu_f, d_f, gu_s, d_s, slot_ref, sem):
    i = pl.program_id(0)
    n_blocks = pl.num_programs(0)
    active = nv_ref[i] > 0
    first = active & ((i == 0) | (be_ref[i] != be_ref[jnp.maximum(i - 1, 0)]))

    def fetch(e, slot):
        return (pltpu.make_async_copy(wgu_hbm.at[e], gu_f.at[slot], sem.at[slot, 0]),
                pltpu.make_async_copy(wd_hbm.at[e], d_f.at[slot], sem.at[slot, 1]))

    @pl.when((i == 0) & active)
    def _():
        slot_ref[0] = 0
        for cp in fetch(be_ref[0], 0):
            cp.start()

    @pl.when(first)
    def _():
        cur = slot_ref[0]
        for cp in fetch(be_ref[i], cur):
            cp.wait()
        nxt = lax.while_loop(lambda j: (j < n_blocks) & (be_ref[jnp.minimum(j, n_blocks - 1)] == be_ref[i]),
                             lambda j: j + 1, i + 1)
        nxt_c = jnp.minimum(nxt, n_blocks - 1)

        @pl.when((nxt < n_blocks) & (nv_ref[nxt_c] > 0))
        def _():
            for cp in fetch(be_ref[nxt_c], 1 - cur):
                cp.start()

        gu_s[...] = gu_f[cur].astype(BF16)
        d_s[...] = d_f[cur].astype(BF16)
        slot_ref[0] = 1 - cur

    @pl.when(active)
    def _():
        x = _unpack_bf16_pairs(x_ref[...]).astype(BF16)
        ff = d_s.shape[0]
        chunks = [slice(c0, c0 + MXU_WIDTH) for c0 in range(0, ff, MXU_WIDTH)]
        gate_up = [(jnp.dot(x, gu_s[:, cs], preferred_element_type=F32),
                    jnp.dot(x, gu_s[:, slice(ff + cs.start, ff + cs.stop)], preferred_element_type=F32))
                   for cs in chunks]
        act = [(_silu(g) * u).astype(BF16) for g, u in gate_up]
        y = jnp.dot(act[0], d_s[chunks[0], :], preferred_element_type=F32)
        for a, cs in zip(act[1:], chunks[1:]):
            y = y + jnp.dot(a, d_s[cs, :], preferred_element_type=F32)
        row = lax.broadcasted_iota(jnp.int32, y.shape, 0)
        y_ref[...] = _pack_bf16_pairs(jnp.where(row < nv_ref[i], y, 0.0))


def _experts(block_e, n_valid, n_active, x_sorted, w_gu, w_d):
    n_rows, dp = x_sorted.shape
    ff2 = w_gu.shape[2]
    n_blocks = n_rows // EXPERT_BLOCK

    def block(i, be, nv, na):
        return (jnp.maximum(jnp.minimum(i, na[0] - 1), 0), 0)

    grid_spec = pltpu.PrefetchScalarGridSpec(
        num_scalar_prefetch=3,
        grid=(n_blocks,),
        in_specs=[pl.BlockSpec((EXPERT_BLOCK, dp), block),
                  pl.BlockSpec(memory_space=pl.ANY),
                  pl.BlockSpec(memory_space=pl.ANY)],
        out_specs=pl.BlockSpec((EXPERT_BLOCK, dp), block),
        scratch_shapes=[pltpu.VMEM((2, 2 * dp, ff2), F32), pltpu.VMEM((2, ff2 // 2, 2 * dp), F32),
                        pltpu.VMEM((2 * dp, ff2), BF16), pltpu.VMEM((ff2 // 2, 2 * dp), BF16),
                        pltpu.SMEM((1,), jnp.int32), pltpu.SemaphoreType.DMA((2, 2))],
    )
    return pl.pallas_call(
        _expert_kernel,
        grid_spec=grid_spec,
        out_shape=jax.ShapeDtypeStruct((n_rows, dp), jnp.uint32),
        compiler_params=_params(0, 1),
        name="experts",
    )(block_e, n_valid, n_active, x_sorted, w_gu, w_d)


def _shared_ffn_kernel(h2_ref, wgu_ref, wd_ref, o_ref):
    h2 = _unpack_bf16_pairs(h2_ref[...]).astype(BF16)
    gu = jnp.dot(h2, wgu_ref[...], preferred_element_type=F32)
    ff = gu.shape[1] // 2
    act = _silu(gu[:, :ff]) * gu[:, ff:]
    o_ref[...] = jnp.dot(act.astype(BF16), wd_ref[...], preferred_element_type=F32).astype(o_ref.dtype)


def _shared_ffn(h2, w_gu, w_d):
    t, dp = h2.shape
    d = 2 * dp
    tm = min(512, t)
    ff2 = w_gu.shape[1]
    return pl.pallas_call(
        _shared_ffn_kernel,
        grid=(t // tm,),
        in_specs=[pl.BlockSpec((tm, dp), lambda i: (i, 0)),
                  pl.BlockSpec((d, ff2), lambda i: (0, 0)),
                  pl.BlockSpec((ff2 // 2, d), lambda i: (0, 0))],
        out_specs=pl.BlockSpec((tm, d), lambda i: (i, 0)),
        out_shape=jax.ShapeDtypeStruct((t, d), BF16),
        compiler_params=_params(1),
        name="shared_ffn",
    )(h2, w_gu, w_d)


def _final_kernel(x1_ref, sh_ref, wc_ref, ada_ref, gain_ref, *rest):
    y_refs, o_ref = rest[:-1], rest[-1]
    ada = ada_ref[0]
    ffn = sh_ref[...].astype(F32)
    wc = wc_ref[...]
    for kk, y_ref in enumerate(y_refs):
        ffn = ffn + wc[:, kk:kk + 1] * _unpack_bf16_pairs(y_ref[...])
    x2 = x1_ref[...] + ada[5:6, :] * ffn
    o_ref[...] = _rms(x2, gain_ref[...])


def _final(x1, shared, y_tok, wc, ada3, gain, seq):
    t, d = x1.shape
    tm = min(256, seq)
    per_b = seq // tm
    steps = t // tm
    slot_specs = [pl.BlockSpec((tm, d // 2), lambda i, kk=kk: (kk * steps + i, 0)) for kk in range(TOP_K)]
    return pl.pallas_call(
        _final_kernel,
        grid=(steps,),
        in_specs=[pl.BlockSpec((tm, d), lambda i: (i, 0)),
                  pl.BlockSpec((tm, d), lambda i: (i, 0)),
                  pl.BlockSpec((tm, LANES), lambda i: (i, 0)),
                  pl.BlockSpec((1, 6, d), lambda i: (i // per_b, 0, 0)),
                  pl.BlockSpec((1, d), lambda i: (0, 0))] + slot_specs,
        out_specs=pl.BlockSpec((tm, d), lambda i: (i, 0)),
        out_shape=jax.ShapeDtypeStruct((t, d), F32),
        compiler_params=_params(1),
        name="final",
    )(x1, shared, wc, ada3, gain, *([y_tok] * TOP_K))


ROUTE_TILE = 512
MAX_BLOCK_LANES = 512


def _first_argmax(x, idx_f, n):
    m = jnp.max(x, axis=0, keepdims=True)
    first = jnp.min(jnp.where(x == m, idx_f, float(n)), axis=0, keepdims=True)
    return m, first


def _route_kernel(lg_ref, bias_ref, ek_ref, rk_ref, wc_ref, ps_ref, be_ref, carry_ref, upper_ref):
    i = pl.program_id(0)
    tm = lg_ref.shape[1]
    per_group = N_EXPERTS // N_GROUPS

    @pl.when(i == 0)
    def _():
        carry_ref[...] = jnp.zeros_like(carry_ref)
        r = lax.broadcasted_iota(jnp.int32, (tm, tm), 0)
        cc = lax.broadcasted_iota(jnp.int32, (tm, tm), 1)
        upper_ref[...] = (r < cc).astype(BF16)

    scores = jax.nn.sigmoid(lg_ref[0:N_EXPERTS, :])
    biased = scores + bias_ref[:, 0:1]
    sub8 = lax.broadcasted_iota(jnp.int32, (per_group, tm), 0).astype(F32)
    group_rows = []
    for g in range(N_GROUPS):
        xg = biased[g * per_group:(g + 1) * per_group, :]
        m1, i1 = _first_argmax(xg, sub8, per_group)
        m2 = jnp.max(jnp.where(sub8 == i1, -jnp.inf, xg), axis=0, keepdims=True)
        group_rows.append(m1 + m2)
    cur = jnp.concatenate(group_rows, axis=0)
    gself = jnp.zeros(cur.shape, F32)
    for _ in range(TOPK_GROUPS):
        _, gi = _first_argmax(cur, sub8, N_GROUPS)
        hit = sub8 == gi
        gself = jnp.where(hit, 1.0, gself)
        cur = jnp.where(hit, -jnp.inf, cur)
    masked = jnp.concatenate(
        [jnp.where(gself[g:g + 1, :] > 0.5, biased[g * per_group:(g + 1) * per_group, :], -jnp.inf)
         for g in range(N_GROUPS)], axis=0)
    sub64 = lax.broadcasted_iota(jnp.int32, (N_EXPERTS, tm), 0).astype(F32)
    e_rows, s_rows = [], []
    sel = jnp.zeros(masked.shape, F32)
    for _ in range(TOP_K):
        _, ei = _first_argmax(masked, sub64, N_EXPERTS)
        hit = sub64 == ei
        e_rows.append(ei)
        s_rows.append(jnp.sum(jnp.where(hit, scores, 0.0), axis=0, keepdims=True))
        sel = jnp.where(hit, 1.0, sel)
        masked = jnp.where(hit, -jnp.inf, masked)
    total = s_rows[0]
    for s in s_rows[1:]:
        total = total + s
    w_rows = [s / total * ROUTED_SCALE for s in s_rows]
    before = jnp.dot(sel.astype(BF16), upper_ref[...], preferred_element_type=F32) + carry_ref[:, 0:1]
    r_rows = [jnp.sum(jnp.where(sub64 == ei, before, 0.0), axis=0, keepdims=True) for ei in e_rows]
    zrow = jnp.zeros((8 - TOP_K, tm), F32)
    ek_ref[...] = jnp.concatenate(e_rows + [zrow], axis=0).astype(jnp.int32)
    rk_ref[...] = jnp.concatenate(r_rows + [zrow], axis=0).astype(jnp.int32)
    wc_ref[...] = jnp.concatenate(w_rows + [jnp.zeros((LANES - TOP_K, tm), F32)], axis=0).T
    carry_ref[...] = carry_ref[...] + jnp.sum(sel, axis=1, keepdims=True)

    @pl.when(i == pl.num_programs(0) - 1)
    def _():
        counts = carry_ref[...]
        padded = jnp.floor((counts + (EXPERT_BLOCK - 1.0)) * (1.0 / EXPERT_BLOCK)) * EXPERT_BLOCK
        r = lax.broadcasted_iota(jnp.int32, (N_EXPERTS, N_EXPERTS), 0)
        cc = lax.broadcasted_iota(jnp.int32, (N_EXPERTS, N_EXPERTS), 1)
        pad_end = jnp.dot((cc <= r).astype(F32), padded, precision=lax.Precision.HIGHEST,
                          preferred_element_type=F32)
        pad_start = pad_end - padded
        ps_ref[...] = pad_start
        blk0 = lax.broadcasted_iota(jnp.int32, (N_EXPERTS, MAX_BLOCK_LANES), 1).astype(F32) * EXPERT_BLOCK
        e_of_blk = jnp.minimum(jnp.sum((pad_end[:, 0:1] <= blk0).astype(F32), axis=0, keepdims=True),
                               N_EXPERTS - 1.0)
        sub = lax.broadcasted_iota(jnp.int32, (N_EXPERTS, MAX_BLOCK_LANES), 0).astype(F32)
        seg_end = jnp.sum(jnp.where(sub == e_of_blk, (pad_start + counts)[:, 0:1], 0.0), axis=0, keepdims=True)
        n_valid = jnp.clip(seg_end - blk0[0:1, :], 0.0, float(EXPERT_BLOCK))
        n_active = jnp.broadcast_to(jnp.sum((n_valid > 0.0).astype(F32), axis=1, keepdims=True), n_valid.shape)
        be_ref[...] = jnp.concatenate([e_of_blk, n_valid, n_active, jnp.zeros((5, MAX_BLOCK_LANES), F32)],
                                      axis=0).astype(jnp.int32)


def _route_pallas(logits_t, bias_col):
    t = logits_t.shape[1]
    tm = min(ROUTE_TILE, t)
    row8 = pl.BlockSpec((8, tm), lambda i: (0, i))
    return pl.pallas_call(
        _route_kernel,
        grid=(t // tm,),
        in_specs=[pl.BlockSpec((LANES, tm), lambda i: (0, i)),
                  pl.BlockSpec((N_EXPERTS, 1), lambda i: (0, 0))],
        out_specs=[row8, row8,
                   pl.BlockSpec((tm, LANES), lambda i: (i, 0)),
                   pl.BlockSpec((N_EXPERTS, LANES), lambda i: (0, 0)),
                   pl.BlockSpec((8, MAX_BLOCK_LANES), lambda i: (0, 0))],
        out_shape=[jax.ShapeDtypeStruct((8, t), jnp.int32),
                   jax.ShapeDtypeStruct((8, t), jnp.int32),
                   jax.ShapeDtypeStruct((t, LANES), F32),
                   jax.ShapeDtypeStruct((N_EXPERTS, LANES), F32),
                   jax.ShapeDtypeStruct((8, MAX_BLOCK_LANES), jnp.int32)],
        scratch_shapes=[pltpu.VMEM((N_EXPERTS, LANES), F32), pltpu.VMEM((tm, tm), BF16)],
        compiler_params=_params(0, 1),
        name="route",
    )(logits_t, bias_col)


def _slot_kernel(ek_ref, rk_ref, ps_ref, slot_ref):
    tm = ek_ref.shape[1]
    sub64 = lax.broadcasted_iota(jnp.int32, (N_EXPERTS, tm), 0)
    start = ps_ref[:, 0:1]
    rows = []
    for kk in range(TOP_K):
        seg = jnp.sum(jnp.where(sub64 == ek_ref[kk:kk + 1, :], start, 0.0), axis=0, keepdims=True)
        rows.append(seg.astype(jnp.int32) + rk_ref[kk:kk + 1, :])
    slot_ref[...] = jnp.concatenate(rows + [jnp.zeros((8 - TOP_K, tm), jnp.int32)], axis=0)


def _slots(ek, rk, ps):
    t = ek.shape[1]
    tm = min(ROUTE_TILE, t)
    row8 = pl.BlockSpec((8, tm), lambda i: (0, i))
    return pl.pallas_call(
        _slot_kernel,
        grid=(t // tm,),
        in_specs=[row8, row8, pl.BlockSpec((N_EXPERTS, LANES), lambda i: (0, 0))],
        out_specs=row8,
        out_shape=jax.ShapeDtypeStruct((8, t), jnp.int32),
        compiler_params=_params(1),
        name="slots",
    )(ek, rk, ps)


SC_ROWS = 32
SC_INDEX_WIDTH = LANES
SPARE_ROWS = 4096
EXPERT_RANGES = 4


def _sc_mesh():
    info = plsc.get_sparse_core_info()
    mesh = plsc.VectorSubcoreMesh(core_axis_name="c", subcore_axis_name="s")
    return mesh, info.num_cores, info.num_cores * info.num_subcores


def _sc_scatter_rows(x, idx, out):
    n, w = x.shape
    in_place = not isinstance(out, jax.ShapeDtypeStruct)
    mesh, n_cores, n_workers = _sc_mesh()
    per_w = n // n_workers
    n_chunks = per_w // SC_ROWS
    assert per_w * n_workers == n and n_chunks * SC_ROWS == per_w and n_chunks % 2 == 0

    def scatter_rows(x_hbm, i_hbm, o_hbm, idx_v, rows_v, read_sem, scatter_sem):
        base = (lax.axis_index("s") * n_cores + lax.axis_index("c")) * per_w

        def read(j, b):
            return pltpu.make_async_copy(x_hbm.at[pl.ds(pl.multiple_of(base + j * SC_ROWS, 8), SC_ROWS)],
                                         rows_v.at[b], read_sem.at[b])

        def scatter(b):
            return pltpu.make_async_copy(rows_v.at[b], o_hbm.at[idx_v.at[b]], scatter_sem.at[b])

        read(0, 0).start()

        @pl.loop(0, n_chunks, step=2)
        def _(j):
            for b in range(2):
                jj = j + b

                @pl.when(jj >= 1)
                def _():
                    scatter(1 - b).wait()

                @pl.when(jj + 1 < n_chunks)
                def _():
                    read(jj + 1, 1 - b).start()

                pltpu.sync_copy(i_hbm.at[pl.ds(pl.multiple_of(base + jj * SC_ROWS, 8), SC_ROWS)], idx_v.at[b])
                read(jj, b).wait()
                scatter(b).start()

        scatter((n_chunks - 1) % 2).wait()

    kern = pl.kernel(
        scatter_rows, out_type=() if in_place else out, mesh=mesh,
        scratch_types=[pltpu.VMEM((2, SC_ROWS), jnp.int32), pltpu.VMEM((2, SC_ROWS, w), x.dtype),
                       pltpu.SemaphoreType.DMA((2,)), pltpu.SemaphoreType.DMA((2,))])
    if in_place:
        kern(x, idx, out)
        return out
    return kern(x, idx)


def _sc_gather_rows(y, slot_flat):
    n = slot_flat.shape[0]
    dp = y.shape[1]
    mesh, n_cores, n_workers = _sc_mesh()
    per_w = n // n_workers
    n_chunks = per_w // SC_ROWS
    assert per_w * n_workers == n and n_chunks * SC_ROWS == per_w and n_chunks % 2 == 0

    @functools.partial(
        pl.kernel, out_type=jax.ShapeDtypeStruct((n, dp), y.dtype), mesh=mesh,
        scratch_types=[pltpu.VMEM((per_w,), jnp.int32), pltpu.VMEM((2, SC_ROWS, dp), y.dtype),
                       pltpu.SemaphoreType.DMA((2,)), pltpu.SemaphoreType.DMA((2,))])
    def combine(y_hbm, i_hbm, o_hbm, idx_v, rows_v, gather_sem, write_sem):
        base = (lax.axis_index("s") * n_cores + lax.axis_index("c")) * per_w
        pltpu.sync_copy(i_hbm.at[pl.ds(pl.multiple_of(base, 8), per_w)], idx_v)

        def gather(j, b):
            return pltpu.make_async_copy(y_hbm.at[idx_v.at[pl.ds(pl.multiple_of(j * SC_ROWS, 8), SC_ROWS)]],
                                         rows_v.at[b], gather_sem.at[b])

        def write(j, b):
            return pltpu.make_async_copy(rows_v.at[b],
                                         o_hbm.at[pl.ds(pl.multiple_of(base + j * SC_ROWS, 8), SC_ROWS)],
                                         write_sem.at[b])

        gather(0, 0).start()

        @pl.loop(0, n_chunks, step=2)
        def _(j):
            for b in range(2):
                jj = j + b

                @pl.when(jj >= 1)
                def _():
                    write(jj - 1, 1 - b).wait()

                @pl.when(jj + 1 < n_chunks)
                def _():
                    gather(jj + 1, 1 - b).start()

                gather(jj, b).wait()
                write(jj, b).start()

        write(n_chunks - 1, (n_chunks - 1) % 2).wait()

    return combine(y, slot_flat)


def _layer(x, cond_ada, positions, w_in, dn_conv_w, dn_a_log, dn_dt_bias, dn_norm_gain, mla_q_norm_gain,
           w_q_up, mla_kv_norm_gain, w_kv_up, w_out, norm1_gain, norm2_gain, w_router, router_bias,
           w_exp_gate_up, w_exp_down, w_sh_gate_up, w_sh_down, out_gain):
    bsz, seq, d = x.shape
    t = bsz * seq
    x2 = x.reshape(t, d)
    ada3 = cond_ada.reshape(bsz, 6, d)

    s_z = 4 * DN_WIDTH
    s_a = s_z + 2 * DN_HEADS
    s_kpe = s_a + MLA_Q_RANK + MLA_KV_RANK
    w_main = jnp.concatenate([w_in[:, :s_z], w_in[:, s_a:s_kpe]], axis=1).astype(BF16)
    w_aux = jnp.concatenate([w_in[:, s_kpe:], w_in[:, s_z:s_a],
                             jnp.zeros((d, LANES - MLA_ROPE - 2 * DN_HEADS), F32)], axis=1).astype(BF16)
    auxp = jnp.zeros((8, LANES), F32)
    auxp = auxp.at[0, AUX_G:AUX_G + DN_HEADS].set(dn_a_log).at[1, AUX_G:AUX_G + DN_HEADS].set(dn_dt_bias)
    main, auxc, auxr = _in_proj(x2, ada3, norm1_gain.reshape(1, d), w_main, w_aux, auxp, seq)

    dn = _delta(main.reshape(bsz, seq, MAIN_WIDTH), auxc.reshape(bsz, seq, LANES), auxr,
                dn_conv_w.T, dn_norm_gain.reshape(1, DN_DIM))

    qk = MLA_NOPE + MLA_ROPE
    wq3 = w_q_up.reshape(MLA_Q_RANK, MLA_HEADS, qk)
    wq = jnp.concatenate([wq3, jnp.zeros((MLA_Q_RANK, MLA_HEADS, MLA_QK_PAD - qk), F32)], axis=2)
    wq = wq.reshape(MLA_Q_RANK, MLA_HEADS * MLA_QK_PAD).astype(BF16)
    wkv3 = w_kv_up.reshape(MLA_KV_RANK, MLA_HEADS, MLA_NOPE + MLA_V)
    wkv = jnp.concatenate([wkv3[:, :, :MLA_NOPE].reshape(MLA_KV_RANK, -1),
                           wkv3[:, :, MLA_NOPE:].reshape(MLA_KV_RANK, -1)], axis=1).astype(BF16)
    half = MLA_ROPE // 2
    inv_freq = ROPE_THETA ** (-jnp.arange(half, dtype=F32) / half)
    zeros = jnp.zeros((LANES - MLA_ROPE,), F32)
    rope_tab = jnp.zeros((8, LANES), F32)
    rope_tab = rope_tab.at[0].set(jnp.concatenate([inv_freq, inv_freq, zeros]))
    rope_tab = rope_tab.at[1].set(jnp.concatenate([jnp.ones((MLA_ROPE,), F32), zeros]))
    rope_tab = rope_tab.at[2].set(jnp.concatenate([-jnp.ones((half,), F32), jnp.ones((half,), F32), zeros]))
    q, k, v = _mla_proj(main, auxc, positions.reshape(t, 1), mla_q_norm_gain.reshape(1, -1),
                        mla_kv_norm_gain.reshape(1, -1), wq, wkv, rope_tab, seq)
    mla = _attention(q.reshape(bsz, seq, -1), k.reshape(bsz, seq, -1), v)

    w_r = jnp.concatenate([w_router, jnp.zeros((d, LANES - N_EXPERTS), F32)], axis=1)
    w_r_hi = w_r.astype(BF16)
    w_r = jnp.concatenate([w_r_hi, (w_r - w_r_hi.astype(F32)).astype(BF16)], axis=1)
    x1, h2, logits_t = _out_proj(x2, dn.reshape(t, DN_WIDTH), mla.reshape(t, -1), ada3,
                                 norm2_gain.reshape(1, d), w_out.astype(BF16), w_r, seq)

    n_rows = -(-(t * TOP_K + N_EXPERTS * (EXPERT_BLOCK - 1)) // EXPERT_BLOCK) * EXPERT_BLOCK
    n_blocks = n_rows // EXPERT_BLOCK
    assert n_blocks <= MAX_BLOCK_LANES
    ek, rk, wc, seg_start, blocks = _route_pallas(logits_t, router_bias.reshape(N_EXPERTS, 1))
    slot_flat = _slots(ek, rk, seg_start)[:TOP_K].reshape(-1)
    n_assign = t * TOP_K

    ids = jnp.broadcast_to(jnp.arange(n_assign, dtype=jnp.int32)[:, None], (n_assign, SC_INDEX_WIDTH))
    spare = n_assign + jnp.arange(n_rows, dtype=jnp.int32) % SPARE_ROWS
    row_assign = jax.new_ref(jnp.broadcast_to(spare[:, None], (n_rows, SC_INDEX_WIDTH)))
    _sc_scatter_rows(ids, slot_flat, row_assign)
    row_assign = row_assign[...][:, 0]
    row_tok = row_assign % t

    shared = _shared_ffn(h2, w_sh_gate_up.astype(BF16), w_sh_down.astype(BF16))

    n_ranges = EXPERT_RANGES if n_blocks % EXPERT_RANGES == 0 else 1
    nb = n_blocks // n_ranges
    nr = nb * EXPERT_BLOCK
    y_tok = jax.ShapeDtypeStruct((n_assign + SPARE_ROWS, d // 2), jnp.uint32)
    for r in range(n_ranges):
        x_sorted = _sc_gather_rows(h2, row_tok[r * nr:(r + 1) * nr])
        active = jnp.clip(blocks[2, :1] - r * nb, 0, nb)
        y_sorted = _experts(blocks[0, r * nb:(r + 1) * nb], blocks[1, r * nb:(r + 1) * nb], active, x_sorted,
                            w_exp_gate_up, w_exp_down)
        y_tok = _sc_scatter_rows(y_sorted, row_assign[r * nr:(r + 1) * nr], y_tok)
        if r == 0:
            y_tok = jax.new_ref(y_tok)
    out = _final(x1, shared, y_tok[...], wc, ada3, out_gain.reshape(1, d), seq)
    return out.reshape(bsz, seq, d)


def kernel(x, c, positions, w_ada, b_ada, norm1_gain, w_in, dn_conv_w, dn_a_log, dn_dt_bias, dn_norm_gain,
           mla_q_norm_gain, w_q_up, mla_kv_norm_gain, w_kv_up, w_out, norm2_gain, w_router, router_bias,
           w_exp_gate_up, w_exp_down, w_sh_gate_up, w_sh_down, final_norm_gain):
    depth = w_ada.shape[0]
    assert depth == 1, "the final RMSNorm is fused into the single layer's last kernel"
    ada = _ada(c, w_ada[0], b_ada[0])
    return _layer(x, ada, positions, w_in[0], dn_conv_w[0], dn_a_log[0], dn_dt_bias[0], dn_norm_gain[0],
                  mla_q_norm_gain[0], w_q_up[0], mla_kv_norm_gain[0], w_kv_up[0], w_out[0], norm1_gain[0],
                  norm2_gain[0], w_router[0], router_bias[0], w_exp_gate_up[0], w_exp_down[0],
                  w_sh_gate_up[0], w_sh_down[0], final_norm_gain)
```

```python
import functools

import jax
import jax.numpy as jnp
from jax import lax
from jax.experimental import pallas as pl
from jax.experimental.pallas import tpu as pltpu
from jax.experimental.pallas import tpu_sc as plsc

F32 = jnp.float32
BF16 = jnp.bfloat16

DN_HEADS = 8
DN_DIM = 128
DN_WIDTH = DN_HEADS * DN_DIM
DN_CONV = 4
DN_CHUNK = 64
MLA_HEADS = 8
MLA_Q_RANK = 512
MLA_KV_RANK = 512
MLA_NOPE = 128
MLA_ROPE = 64
MLA_V = 128
MLA_QK_PAD = 256
ROPE_THETA = 10000.0
N_EXPERTS = 64
TOP_K = 6
N_GROUPS = 8
TOPK_GROUPS = 4
EXPERT_FF = 512
ROUTED_SCALE = 2.5
EXPERT_BLOCK = 512
RMS_EPS = 1e-6
L2_EPS = 1e-6

LANES = 128
MXU_WIDTH = 256
MAIN_WIDTH = 4 * DN_WIDTH + MLA_Q_RANK + MLA_KV_RANK
AUX_BETA = MLA_ROPE
AUX_G = MLA_ROPE + DN_HEADS
VMEM_LIMIT = 56 * 1024 * 1024


def _params(n_parallel, n_arbitrary=0):
    sem = ("parallel",) * n_parallel + ("arbitrary",) * n_arbitrary
    return pltpu.CompilerParams(dimension_semantics=sem, vmem_limit_bytes=VMEM_LIMIT)


def _silu(x):
    return x * jax.nn.sigmoid(x)


def _bdot(a, b):
    return jnp.dot(a.astype(BF16), b.astype(BF16), preferred_element_type=F32)


def _bdot_nt(a, b):
    return lax.dot_general(a.astype(BF16), b.astype(BF16), (((1,), (1,)), ((), ())),
                           preferred_element_type=F32)


def _rms(x, gain):
    return x * lax.rsqrt(jnp.mean(x * x, axis=-1, keepdims=True) + RMS_EPS) * gain


HIGH_HALF = 0xFFFF0000


def _pack_bf16_pairs(x):
    n = x.shape[1] // 2
    bits = lax.bitcast_convert_type(x.astype(BF16).astype(F32), jnp.uint32)
    return (bits[:, :n] >> 16) | (bits[:, n:] & jnp.uint32(HIGH_HALF))


def _unpack_bf16_pairs(p):
    lo = lax.bitcast_convert_type(p << 16, F32)
    hi = lax.bitcast_convert_type(p & jnp.uint32(HIGH_HALF), F32)
    return jnp.concatenate([lo, hi], axis=1)


def _ada_kernel(c_ref, w_ref, b_ref, o_ref):
    o_ref[...] = _bdot(_silu(c_ref[...]), w_ref[...]) + b_ref[...]


def _ada(c, w_ada, b_ada):
    bsz, d = c.shape
    n = w_ada.shape[1]
    tn = 1024
    return pl.pallas_call(
        _ada_kernel,
        grid=(n // tn,),
        in_specs=[pl.BlockSpec((bsz, d), lambda j: (0, 0)),
                  pl.BlockSpec((d, tn), lambda j: (0, j)),
                  pl.BlockSpec((1, tn), lambda j: (0, j))],
        out_specs=pl.BlockSpec((bsz, tn), lambda j: (0, j)),
        out_shape=jax.ShapeDtypeStruct((bsz, n), F32),
        compiler_params=_params(1),
        name="ada",
    )(c, w_ada, b_ada.reshape(1, n))


def _in_proj_kernel(x_ref, ada_ref, gain_ref, wm_ref, wa_ref, auxp_ref,
                    main_ref, auxc_ref, auxr_ref, h_ref):
    j = pl.program_id(1)

    @pl.when(j == 0)
    def _():
        ada = ada_ref[0]
        h = _rms(x_ref[...], gain_ref[...]) * (1.0 + ada[1:2, :]) + ada[0:1, :]
        hb = h.astype(BF16)
        h_ref[...] = hb
        aux = jnp.dot(hb, wa_ref[...], preferred_element_type=F32)
        tm = aux.shape[0]
        lane = lax.broadcasted_iota(jnp.int32, aux.shape, 1)
        is_beta = (lane >= AUX_BETA) & (lane < AUX_G)
        is_g = (lane >= AUX_G) & (lane < AUX_G + DN_HEADS)
        a_log = auxp_ref[0:1, :]
        dt_bias = auxp_ref[1:2, :]
        sp_in = aux + dt_bias
        softplus = jnp.maximum(sp_in, 0.0) + jnp.log(1.0 + jnp.exp(-jnp.abs(sp_in)))
        g = -jnp.exp(a_log) * softplus
        aux = jnp.where(is_beta, jax.nn.sigmoid(aux), jnp.where(is_g, g, aux))
        r = lax.broadcasted_iota(jnp.int32, (LANES, LANES), 0)
        cidx = lax.broadcasted_iota(jnp.int32, (LANES, LANES), 1)
        tri = ((cidx <= r) & (cidx // DN_CHUNK == r // DN_CHUNK)).astype(F32)
        g_cols = (cidx >= AUX_G) & (cidx < AUX_G + DN_HEADS)
        parts = []
        for t in range(tm // LANES):
            blk = aux[t * LANES:(t + 1) * LANES, :]
            cs = jnp.dot(tri, blk, precision=lax.Precision.HIGHEST, preferred_element_type=F32)
            parts.append(jnp.where(g_cols, cs, blk))
        aux = jnp.concatenate(parts, axis=0)
        auxc_ref[...] = aux
        auxr_ref[...] = aux.T

    main_ref[...] = jnp.dot(h_ref[...], wm_ref[...], preferred_element_type=F32).astype(main_ref.dtype)


def _in_proj(x2, ada3, gain, w_main, w_aux, auxp, seq):
    t, d = x2.shape
    n = w_main.shape[1]
    tm = min(1024, seq)
    tn = 1024
    per_b = seq // tm
    return pl.pallas_call(
        _in_proj_kernel,
        grid=(t // tm, n // tn),
        in_specs=[pl.BlockSpec((tm, d), lambda i, j: (i, 0)),
                  pl.BlockSpec((1, 6, d), lambda i, j: (i // per_b, 0, 0)),
                  pl.BlockSpec((1, d), lambda i, j: (0, 0)),
                  pl.BlockSpec((d, tn), lambda i, j: (0, j)),
                  pl.BlockSpec((d, LANES), lambda i, j: (0, 0)),
                  pl.BlockSpec((8, LANES), lambda i, j: (0, 0))],
        out_specs=[pl.BlockSpec((tm, tn), lambda i, j: (i, j)),
                   pl.BlockSpec((tm, LANES), lambda i, j: (i, 0)),
                   pl.BlockSpec((LANES, tm), lambda i, j: (0, i))],
        out_shape=[jax.ShapeDtypeStruct((t, n), BF16),
                   jax.ShapeDtypeStruct((t, LANES), F32),
                   jax.ShapeDtypeStruct((LANES, t), F32)],
        scratch_shapes=[pltpu.VMEM((tm, d), BF16)],
        compiler_params=_params(1, 1),
        name="in_proj",
    )(x2, ada3, gain, w_main, w_aux, auxp)


def _unit_lower_inverses(ms):
    c = ms[0].shape[0]
    r = lax.broadcasted_iota(jnp.int32, (c, c), 0)
    cc = lax.broadcasted_iota(jnp.int32, (c, c), 1)
    eye = (r == cc).astype(F32)
    same = (r // 16) == (cc // 16)
    md = [jnp.where(same, m, 0.0) for m in ms]
    mo = [jnp.where(same, 0.0, m) for m in ms]
    p1 = [_bdot(a, a) for a in md]
    p2 = [_bdot(a, a) for a in p1]
    p3 = [_bdot(a, a) for a in p2]
    td = [eye - a for a in md]
    td = [t + _bdot(t, p) for t, p in zip(td, p1)]
    td = [t + _bdot(t, p) for t, p in zip(td, p2)]
    td = [t + _bdot(t, p) for t, p in zip(td, p3)]
    n1 = [_bdot(t, o) for t, o in zip(td, mo)]
    n2 = [_bdot(n, n) for n in n1]
    left = [(eye - a) + _bdot(eye - a, b) for a, b in zip(n1, n2)]
    return [_bdot(l, t) for l, t in zip(left, td)]


DELTA_TILE = 2048
DELTA_SCAN_TILE = 512


def _delta_prep_kernel(q_ref, k_ref, v_ref, qh_ref, kh_ref, vh_ref, auxc_ref, auxr_ref, wq_ref, wk_ref, wv_ref,
                       u_ref, w_ref, qe_ref, kd_ref, a_ref, pad_ref):
    head = pl.program_id(1)
    first_tile = pl.program_id(2) == 0
    ts = q_ref.shape[0]
    c = DN_CHUNK
    halo = qh_ref.shape[0]

    def conv_silu(x_ref, h_ref, w_ref):
        top = pad_ref.shape[0] - ts
        pad_ref[0:top, :] = jnp.where(first_tile, 0.0, h_ref[halo - top:, :].astype(F32))
        pad_ref[top:, :] = x_ref[...].astype(F32)
        w = w_ref[...]
        acc = pad_ref[top:, :] * w[DN_CONV - 1:DN_CONV, :]
        for s in range(1, DN_CONV):
            acc = acc + pad_ref[top - s:top - s + ts, :] * w[DN_CONV - 1 - s:DN_CONV - s, :]
        return _silu(acc)

    q = conv_silu(q_ref, qh_ref, wq_ref)
    k = conv_silu(k_ref, kh_ref, wk_ref)
    v = conv_silu(v_ref, vh_ref, wv_ref)
    q = q * lax.rsqrt(jnp.sum(q * q, axis=-1, keepdims=True) + L2_EPS) * (DN_DIM ** -0.5)
    k = k * lax.rsqrt(jnp.sum(k * k, axis=-1, keepdims=True) + L2_EPS)

    auxc = auxc_ref[...]
    lane = lax.broadcasted_iota(jnp.int32, auxc.shape, 1)
    beta_all = jnp.sum(jnp.where(lane == AUX_BETA + head, auxc, 0.0), axis=1, keepdims=True)
    gcol_all = jnp.sum(jnp.where(lane == AUX_G + head, auxc, 0.0), axis=1, keepdims=True)
    grows = auxr_ref[AUX_G:AUX_G + DN_HEADS, :]
    head_row = lax.broadcasted_iota(jnp.int32, grows.shape, 0)
    grow_all = jnp.sum(jnp.where(head_row == head, grows, 0.0), axis=0, keepdims=True)

    ri = lax.broadcasted_iota(jnp.int32, (c, c), 0)
    ci = lax.broadcasted_iota(jnp.int32, (c, c), 1)
    causal = ri >= ci
    strict = ri > ci
    chunks = [slice(j * c, (j + 1) * c) for j in range(ts // c)]
    decay = [jnp.exp(jnp.where(causal, gcol_all[rows] - grow_all[:, rows], -jnp.inf)) for rows in chunks]
    kb = [k[rows] * beta_all[rows] for rows in chunks]
    both = [_bdot_nt(jnp.concatenate([kb_j, q[rows]], axis=0), k[rows])
            for kb_j, rows in zip(kb, chunks)]
    tinv = _unit_lower_inverses([jnp.where(strict, b[:c] * d, 0.0) for b, d in zip(both, decay)])
    eg = [jnp.exp(gcol_all[rows]) for rows in chunks]
    sol = [_bdot(t, jnp.concatenate([v[rows] * beta_all[rows], kb_j * e], axis=1))
           for t, rows, kb_j, e in zip(tinv, chunks, kb, eg)]
    for j, rows in enumerate(chunks):
        gcol = gcol_all[rows]
        u_ref[rows, :] = sol[j][:, :DN_DIM].astype(u_ref.dtype)
        w_ref[rows, :] = sol[j][:, DN_DIM:].astype(w_ref.dtype)
        qe_ref[rows, :] = (q[rows] * eg[j]).astype(qe_ref.dtype)
        kd_ref[rows, :] = (k[rows] * jnp.exp(gcol[c - 1:c, :] - gcol)).astype(kd_ref.dtype)
        a_ref[rows, :] = jnp.where(causal, both[j][c:] * decay[j], 0.0).astype(a_ref.dtype)


def _delta_scan_kernel(u_ref, w_ref, qe_ref, kd_ref, a_ref, z_ref, auxc_ref, gain_ref, o_ref, st_ref):
    seq = u_ref.shape[0]
    n_heads = a_ref.shape[0]
    c = DN_CHUNK

    @pl.when(pl.program_id(1) == 0)
    def _():
        st_ref[...] = jnp.zeros_like(st_ref)

    gain = gain_ref[...]
    lane = lax.broadcasted_iota(jnp.int32, (1, LANES), 1)

    def scan_chunk(i, carry):
        start = pl.multiple_of(i * c, c)
        rows = pl.ds(start, c)
        last = auxc_ref[pl.ds(start + c - 1, 1), :]
        heads = range(n_heads)
        cols = [slice(hh * DN_DIM, (hh + 1) * DN_DIM) for hh in heads]
        state = [st_ref[hh] for hh in heads]
        sb = [s.astype(BF16) for s in state]
        ws = [jnp.dot(w_ref[rows, cl], s, preferred_element_type=F32) for cl, s in zip(cols, sb)]
        qs = [jnp.dot(qe_ref[rows, cl], s, preferred_element_type=F32) for cl, s in zip(cols, sb)]
        vb = [(u_ref[rows, cl].astype(F32) - x).astype(BF16) for cl, x in zip(cols, ws)]
        kv = [lax.dot_general(kd_ref[rows, cl], x, (((0,), (0,)), ((), ())), preferred_element_type=F32)
              for cl, x in zip(cols, vb)]
        av = [jnp.dot(a_ref[hh, rows, :], x, preferred_element_type=F32) for hh, x in zip(heads, vb)]
        for hh in heads:
            glast = jnp.sum(jnp.where(lane == AUX_G + hh, last, 0.0), axis=1, keepdims=True)
            st_ref[hh] = state[hh] * jnp.exp(glast) + kv[hh]
            z = z_ref[rows, cols[hh]].astype(F32)
            o_ref[rows, cols[hh]] = (_rms(qs[hh] + av[hh], gain) * _silu(z)).astype(o_ref.dtype)
        return carry

    lax.fori_loop(0, seq // c, scan_chunk, 0)


def _delta(main3, auxc3, auxr, conv_w_t, gain):
    bsz, seq, _ = main3.shape
    hb = DN_WIDTH // DN_DIM
    ts = min(DELTA_TILE, seq)
    halo = 16
    tiles = seq // ts

    def col(offset):
        return pl.BlockSpec((None, ts, DN_DIM), lambda b, h, i: (b, i, offset * hb + h))

    def halo_col(offset):
        return pl.BlockSpec((None, halo, DN_DIM),
                            lambda b, h, i: (b, jnp.maximum(i * (ts // halo) - 1, 0), offset * hb + h))

    def wcol(offset):
        return pl.BlockSpec((DN_CONV, DN_DIM), lambda b, h, i: (0, offset * hb + h))

    tok = pl.BlockSpec((None, ts, DN_DIM), lambda b, h, i: (b, i, h))
    tok_shape = jax.ShapeDtypeStruct((bsz, seq, DN_WIDTH), BF16)
    u, w, qe, kd, a = pl.pallas_call(
        _delta_prep_kernel,
        grid=(bsz, DN_HEADS, tiles),
        in_specs=[col(0), col(1), col(2), halo_col(0), halo_col(1), halo_col(2),
                  pl.BlockSpec((None, ts, LANES), lambda b, h, i: (b, i, 0)),
                  pl.BlockSpec((LANES, ts), lambda b, h, i: (0, b * tiles + i)),
                  wcol(0), wcol(1), wcol(2)],
        out_specs=[tok, tok, tok, tok,
                   pl.BlockSpec((None, None, ts, DN_CHUNK), lambda b, h, i: (b, h, i, 0))],
        out_shape=[tok_shape, tok_shape, tok_shape, tok_shape,
                   jax.ShapeDtypeStruct((bsz, DN_HEADS, seq, DN_CHUNK), BF16)],
        scratch_shapes=[pltpu.VMEM((ts + 8, DN_DIM), F32)],
        compiler_params=_params(3),
        name="delta_prep",
    )(main3, main3, main3, main3, main3, main3, auxc3, auxr, conv_w_t, conv_w_t, conv_w_t)

    tsc = min(DELTA_SCAN_TILE, seq)
    wide = pl.BlockSpec((None, tsc, DN_WIDTH), lambda b, i: (b, i, 0))
    return pl.pallas_call(
        _delta_scan_kernel,
        grid=(bsz, seq // tsc),
        in_specs=[wide, wide, wide, wide,
                  pl.BlockSpec((None, DN_HEADS, tsc, DN_CHUNK), lambda b, i: (b, 0, i, 0)),
                  pl.BlockSpec((None, tsc, DN_WIDTH), lambda b, i: (b, i, 3)),
                  pl.BlockSpec((None, tsc, LANES), lambda b, i: (b, i, 0)),
                  pl.BlockSpec((1, DN_DIM), lambda b, i: (0, 0))],
        out_specs=wide,
        out_shape=tok_shape,
        scratch_shapes=[pltpu.VMEM((DN_HEADS, DN_DIM, DN_DIM), F32)],
        compiler_params=_params(1, 1),
        name="delta_scan",
    )(u, w, qe, kd, a, main3, auxc3, gain)


def _mla_proj_kernel(cq_ref, ckv_ref, auxc_ref, pos_ref, qg_ref, kvg_ref, wq_ref, wkv_ref, rope_ref,
                     q_ref, k_ref, v_ref):
    scale = (MLA_NOPE + MLA_ROPE) ** -0.5
    ang = pos_ref[...].astype(F32) * rope_ref[0:1, :]
    cos_t = jnp.cos(ang) * rope_ref[1:2, :]
    sin_t = jnp.sin(ang) * rope_ref[2:3, :]
    lane = lax.broadcasted_iota(jnp.int32, ang.shape, 1)
    first = lane < MLA_ROPE // 2

    def rope(a):
        swapped = jnp.where(first, pltpu.roll(a, LANES - MLA_ROPE // 2, 1), pltpu.roll(a, MLA_ROPE // 2, 1))
        return a * cos_t + swapped * sin_t

    ql = _bdot(_rms(cq_ref[...].astype(F32), qg_ref[...]), wq_ref[...]) * scale
    kv = _bdot(_rms(ckv_ref[...].astype(F32), kvg_ref[...]), wkv_ref[...])
    kpe = rope(auxc_ref[...]).astype(k_ref.dtype)
    for h in range(MLA_HEADS):
        o = h * MLA_QK_PAD
        q_ref[:, o:o + MLA_NOPE] = ql[:, o:o + MLA_NOPE].astype(q_ref.dtype)
        q_ref[:, o + MLA_NOPE:o + MLA_QK_PAD] = rope(ql[:, o + MLA_NOPE:o + MLA_QK_PAD]).astype(q_ref.dtype)
        k_ref[:, o:o + MLA_NOPE] = kv[:, h * MLA_NOPE:(h + 1) * MLA_NOPE].astype(k_ref.dtype)
        k_ref[:, o + MLA_NOPE:o + MLA_QK_PAD] = kpe
    v_ref[...] = kv[:, MLA_HEADS * MLA_NOPE:].T.astype(v_ref.dtype)


def _mla_proj(main, auxc, pos, q_gain, kv_gain, wq, wkv, rope_tab, seq):
    t = main.shape[0]
    tm = min(512, seq)
    per_b = seq // tm
    cq_blk = (4 * DN_WIDTH) // MLA_Q_RANK
    hq = MLA_HEADS * MLA_QK_PAD
    hv = MLA_HEADS * MLA_V
    return pl.pallas_call(
        _mla_proj_kernel,
        grid=(t // tm,),
        in_specs=[pl.BlockSpec((tm, MLA_Q_RANK), lambda i: (i, cq_blk)),
                  pl.BlockSpec((tm, MLA_KV_RANK), lambda i: (i, cq_blk + 1)),
                  pl.BlockSpec((tm, LANES), lambda i: (i, 0)),
                  pl.BlockSpec((tm, 1), lambda i: (i, 0)),
                  pl.BlockSpec((1, MLA_Q_RANK), lambda i: (0, 0)),
                  pl.BlockSpec((1, MLA_KV_RANK), lambda i: (0, 0)),
                  pl.BlockSpec((MLA_Q_RANK, hq), lambda i: (0, 0)),
                  pl.BlockSpec((MLA_KV_RANK, 2 * hv), lambda i: (0, 0)),
                  pl.BlockSpec((8, LANES), lambda i: (0, 0))],
        out_specs=[pl.BlockSpec((tm, hq), lambda i: (i, 0)),
                   pl.BlockSpec((tm, hq), lambda i: (i, 0)),
                   pl.BlockSpec((None, hv, tm), lambda i: (i // per_b, 0, i % per_b))],
        out_shape=[jax.ShapeDtypeStruct((t, hq), BF16),
                   jax.ShapeDtypeStruct((t, hq), BF16),
                   jax.ShapeDtypeStruct((t // seq, hv, seq), BF16)],
        compiler_params=_params(1),
        name="mla_proj",
    )(main, main, auxc, pos, q_gain, kv_gain, wq, wkv, rope_tab)


ATTN_PARTS = 4


def _attn_kernel(q_ref, k_ref, vt_ref, o_ref, m_ref, l_ref, acc_ref):
    qi = pl.program_id(2)
    n_parts = m_ref.shape[0]
    tk = m_ref.shape[2]
    q = [q_ref[p * tk:(p + 1) * tk, :] for p in range(n_parts)]
    m_ref[...] = jnp.full_like(m_ref, -jnp.inf)
    l_ref[...] = jnp.zeros_like(l_ref)
    acc_ref[...] = jnp.zeros_like(acc_ref)
    key = lax.broadcasted_iota(jnp.int32, (tk, tk), 0)
    query = lax.broadcasted_iota(jnp.int32, (tk, tk), 1)
    visible = key <= query

    def step(j, parts, masked_part):
        start = pl.multiple_of(j * tk, tk)
        kb = k_ref[pl.ds(start, tk), :]
        vtb = vt_ref[:, pl.ds(start, tk)]
        s = [lax.dot_general(kb, q[p], (((1,), (1,)), ((), ())), preferred_element_type=F32) for p in parts]
        s = [jnp.where(visible, x, -jnp.inf) if p == masked_part else x for p, x in zip(parts, s)]
        m_old = [m_ref[p] for p in parts]
        m_new = [jnp.maximum(mo, jnp.max(x, axis=0, keepdims=True)) for mo, x in zip(m_old, s)]
        e = [jnp.exp(x - mn) for x, mn in zip(s, m_new)]
        pv = [jnp.dot(vtb, x.astype(BF16), preferred_element_type=F32) for x in e]
        for i, p in enumerate(parts):
            alpha = jnp.exp(m_old[i] - m_new[i])
            l_ref[p] = alpha * l_ref[p] + jnp.sum(e[i], axis=0, keepdims=True)
            acc_ref[p] = alpha * acc_ref[p] + pv[i]
            m_ref[p] = m_new[i]

    every = tuple(range(n_parts))

    def body(j, carry):
        step(j, every, None)
        return carry

    lax.fori_loop(0, qi * n_parts, body, 0)
    for d in range(n_parts):
        step(qi * n_parts + d, every[d:], d)
    for p in every:
        o_ref[p * tk:(p + 1) * tk, :] = (acc_ref[p] / l_ref[p]).T.astype(o_ref.dtype)


def _attention(q3, k3, vt3):
    bsz, seq, _ = q3.shape
    tk = min(512, seq)
    n_parts = min(ATTN_PARTS, seq // tk)
    tq = tk * n_parts
    return pl.pallas_call(
        _attn_kernel,
        grid=(bsz, MLA_HEADS, seq // tq),
        in_specs=[pl.BlockSpec((None, tq, MLA_QK_PAD), lambda b, h, i: (b, i, h)),
                  pl.BlockSpec((None, seq, MLA_QK_PAD), lambda b, h, i: (b, 0, h)),
                  pl.BlockSpec((None, MLA_V, seq), lambda b, h, i: (b, h, 0))],
        out_specs=pl.BlockSpec((None, tq, MLA_V), lambda b, h, i: (b, i, h)),
        out_shape=jax.ShapeDtypeStruct((bsz, seq, MLA_HEADS * MLA_V), BF16),
        scratch_shapes=[pltpu.VMEM((n_parts, 1, tk), F32), pltpu.VMEM((n_parts, 1, tk), F32),
                        pltpu.VMEM((n_parts, MLA_V, tk), F32)],
        compiler_params=_params(3),
        name="attention",
    )(q3, k3, vt3)


def _out_proj_kernel(x_ref, dn_ref, mla_ref, ada_ref, gain_ref, w_ref, wr_ref, x1_ref, h2_ref, lg_ref):
    ada = ada_ref[0]
    half = dn_ref.shape[1]
    mix = (jnp.dot(dn_ref[...], w_ref[:half, :], preferred_element_type=F32)
           + jnp.dot(mla_ref[...], w_ref[half:, :], preferred_element_type=F32))
    x1 = x_ref[...] + ada[2:3, :] * mix
    x1_ref[...] = x1
    h2 = _rms(x1, gain_ref[...]) * (1.0 + ada[4:5, :]) + ada[3:4, :]
    h2_ref[...] = _pack_bf16_pairs(h2)
    h_hi = h2.astype(BF16)
    h_lo = (h2 - h_hi.astype(F32)).astype(BF16)
    both = jnp.dot(h_hi, wr_ref[...], preferred_element_type=F32)
    logits = both[:, :LANES] + both[:, LANES:] + jnp.dot(h_lo, wr_ref[:, :LANES], preferred_element_type=F32)
    lg_ref[...] = logits.T


def _out_proj(x2, dn, mla, ada3, gain, w_out, w_router, seq):
    t, d = x2.shape
    tm = min(256, seq)
    per_b = seq // tm
    half = dn.shape[1]
    return pl.pallas_call(
        _out_proj_kernel,
        grid=(t // tm,),
        in_specs=[pl.BlockSpec((tm, d), lambda i: (i, 0)),
                  pl.BlockSpec((tm, half), lambda i: (i, 0)),
                  pl.BlockSpec((tm, half), lambda i: (i, 0)),
                  pl.BlockSpec((1, 6, d), lambda i: (i // per_b, 0, 0)),
                  pl.BlockSpec((1, d), lambda i: (0, 0)),
                  pl.BlockSpec((2 * half, d), lambda i: (0, 0)),
                  pl.BlockSpec((d, 2 * LANES), lambda i: (0, 0))],
        out_specs=[pl.BlockSpec((tm, d), lambda i: (i, 0)),
                   pl.BlockSpec((tm, d // 2), lambda i: (i, 0)),
                   pl.BlockSpec((LANES, tm), lambda i: (0, i))],
        out_shape=[jax.ShapeDtypeStruct((t, d), F32),
                   jax.ShapeDtypeStruct((t, d // 2), jnp.uint32),
                   jax.ShapeDtypeStruct((LANES, t), F32)],
        compiler_params=_params(1),
        name="out_proj",
    )(x2, dn, mla, ada3, gain, w_out, w_router)


def _expert_kernel(be_ref, nv_ref, na_ref, x_ref, wgu_hbm, wd_hbm, after_hbm, y_ref,
                   gu_f, d_f, gu_s, d_s, slot_ref, sem):
    i = pl.program_id(0)
    n_blocks = pl.num_programs(0)
    active = nv_ref[i] > 0
    first = active & ((i == 0) | (be_ref[i] != be_ref[jnp.maximum(i - 1, 0)]))

    def fetch(e, slot):
        return (pltpu.make_async_copy(wgu_hbm.at[e], gu_f.at[slot], sem.at[slot, 0]),
                pltpu.make_async_copy(wd_hbm.at[e], d_f.at[slot], sem.at[slot, 1]))

    @pl.when((i == 0) & active)
    def _():
        slot_ref[0] = 0
        for cp in fetch(be_ref[0], 0):
            cp.start()

    @pl.when(first)
    def _():
        cur = slot_ref[0]
        for cp in fetch(be_ref[i], cur):
            cp.wait()
        nxt = lax.while_loop(lambda j: (j < n_blocks) & (be_ref[jnp.minimum(j, n_blocks - 1)] == be_ref[i]),
                             lambda j: j + 1, i + 1)
        nxt_c = jnp.minimum(nxt, n_blocks - 1)

        @pl.when((nxt < n_blocks) & (nv_ref[nxt_c] > 0))
        def _():
            for cp in fetch(be_ref[nxt_c], 1 - cur):
                cp.start()

        gu_s[...] = gu_f[cur].astype(BF16)
        d_s[...] = d_f[cur].astype(BF16)
        slot_ref[0] = 1 - cur

    @pl.when(active)
    def _():
        x = _unpack_bf16_pairs(x_ref[...]).astype(BF16)
        ff = d_s.shape[0]
        chunks = [slice(c0, c0 + MXU_WIDTH) for c0 in range(0, ff, MXU_WIDTH)]
        gate_up = [(jnp.dot(x, gu_s[:, cs], preferred_element_type=F32),
                    jnp.dot(x, gu_s[:, slice(ff + cs.start, ff + cs.stop)], preferred_element_type=F32))
                   for cs in chunks]
        act = [(_silu(g) * u).astype(BF16) for g, u in gate_up]
        y = jnp.dot(act[0], d_s[chunks[0], :], preferred_element_type=F32)
        for a, cs in zip(act[1:], chunks[1:]):
            y = y + jnp.dot(a, d_s[cs, :], preferred_element_type=F32)
        row = lax.broadcasted_iota(jnp.int32, y.shape, 0)
        y_ref[...] = _pack_bf16_pairs(jnp.where(row < nv_ref[i], y, 0.0))


def _experts(block_e, n_valid, n_active, x_sorted, w_gu, w_d, after):
    n_rows, dp = x_sorted.shape
    ff2 = w_gu.shape[2]
    n_blocks = n_rows // EXPERT_BLOCK

    def block(i, be, nv, na):
        return (jnp.maximum(jnp.minimum(i, na[0] - 1), 0), 0)

    grid_spec = pltpu.PrefetchScalarGridSpec(
        num_scalar_prefetch=3,
        grid=(n_blocks,),
        in_specs=[pl.BlockSpec((EXPERT_BLOCK, dp), block),
                  pl.BlockSpec(memory_space=pl.ANY),
                  pl.BlockSpec(memory_space=pl.ANY),
                  pl.BlockSpec(memory_space=pl.ANY)],
        out_specs=pl.BlockSpec((EXPERT_BLOCK, dp), block),
        scratch_shapes=[pltpu.VMEM((2, 2 * dp, ff2), F32), pltpu.VMEM((2, ff2 // 2, 2 * dp), F32),
                        pltpu.VMEM((2 * dp, ff2), BF16), pltpu.VMEM((ff2 // 2, 2 * dp), BF16),
                        pltpu.SMEM((1,), jnp.int32), pltpu.SemaphoreType.DMA((2, 2))],
    )
    return pl.pallas_call(
        _expert_kernel,
        grid_spec=grid_spec,
        out_shape=jax.ShapeDtypeStruct((n_rows, dp), jnp.uint32),
        compiler_params=_params(0, 1),
        name="experts",
    )(block_e, n_valid, n_active, x_sorted, w_gu, w_d, after)


def _shared_ffn_kernel(h2_ref, wgu_ref, wd_ref, o_ref):
    h2 = _unpack_bf16_pairs(h2_ref[...]).astype(BF16)
    gu = jnp.dot(h2, wgu_ref[...], preferred_element_type=F32)
    ff = gu.shape[1] // 2
    act = _silu(gu[:, :ff]) * gu[:, ff:]
    o_ref[...] = jnp.dot(act.astype(BF16), wd_ref[...], preferred_element_type=F32).astype(o_ref.dtype)


def _shared_ffn(h2, w_gu, w_d):
    t, dp = h2.shape
    d = 2 * dp
    tm = min(512, t)
    ff2 = w_gu.shape[1]
    return pl.pallas_call(
        _shared_ffn_kernel,
        grid=(t // tm,),
        in_specs=[pl.BlockSpec((tm, dp), lambda i: (i, 0)),
                  pl.BlockSpec((d, ff2), lambda i: (0, 0)),
                  pl.BlockSpec((ff2 // 2, d), lambda i: (0, 0))],
        out_specs=pl.BlockSpec((tm, d), lambda i: (i, 0)),
        out_shape=jax.ShapeDtypeStruct((t, d), BF16),
        compiler_params=_params(1),
        name="shared_ffn",
    )(h2, w_gu, w_d)


def _final_kernel(x1_ref, sh_ref, y_ref, wc_ref, ada_ref, gain_ref, *rest):
    o_ref = rest[-1]
    ada = ada_ref[0]
    ffn = sh_ref[...].astype(F32)
    wc = wc_ref[...]
    for kk in range(TOP_K):
        ffn = ffn + wc[:, kk:kk + 1] * _unpack_bf16_pairs(y_ref[kk])
    x2 = x1_ref[...] + ada[5:6, :] * ffn
    o_ref[...] = _rms(x2, gain_ref[...])


def _final_part(x1, shared, y_part, wc, ada3, gain, seq, part, n_parts, prev_out):
    t, d = x1.shape
    tm = min(256, seq)
    per_b = seq // tm
    steps = t // tm // n_parts
    off = part * steps
    in_specs = [pl.BlockSpec((tm, d), lambda i: (i + off, 0)),
                pl.BlockSpec((tm, d), lambda i: (i + off, 0)),
                pl.BlockSpec((TOP_K, tm, d // 2), lambda i: (0, i, 0)),
                pl.BlockSpec((tm, LANES), lambda i: (i + off, 0)),
                pl.BlockSpec((1, 6, d), lambda i: ((i + off) // per_b, 0, 0)),
                pl.BlockSpec((1, d), lambda i: (0, 0))]
    args = [x1, shared, y_part, wc, ada3, gain]
    aliases = {}
    if prev_out is not None:
        in_specs.append(pl.BlockSpec(memory_space=pl.ANY))
        args.append(prev_out)
        aliases = {len(args) - 1: 0}
    return pl.pallas_call(
        _final_kernel,
        grid=(steps,),
        in_specs=in_specs,
        out_specs=pl.BlockSpec((tm, d), lambda i: (i + off, 0)),
        out_shape=jax.ShapeDtypeStruct((t, d), F32),
        input_output_aliases=aliases,
        compiler_params=_params(1),
        name="final",
    )(*args)


ROUTE_TILE = 512
MAX_BLOCK_LANES = 512


def _first_argmax(x, idx_f, n):
    m = jnp.max(x, axis=0, keepdims=True)
    first = jnp.min(jnp.where(x == m, idx_f, float(n)), axis=0, keepdims=True)
    return m, first


def _route_kernel(lg_ref, bias_ref, ek_ref, rk_ref, wc_ref, ps_ref, be_ref, carry_ref, upper_ref):
    i = pl.program_id(0)
    tm = lg_ref.shape[1]
    per_group = N_EXPERTS // N_GROUPS

    @pl.when(i == 0)
    def _():
        carry_ref[...] = jnp.zeros_like(carry_ref)
        r = lax.broadcasted_iota(jnp.int32, (tm, tm), 0)
        cc = lax.broadcasted_iota(jnp.int32, (tm, tm), 1)
        upper_ref[...] = (r < cc).astype(BF16)

    scores = jax.nn.sigmoid(lg_ref[0:N_EXPERTS, :])
    biased = scores + bias_ref[:, 0:1]
    sub8 = lax.broadcasted_iota(jnp.int32, (per_group, tm), 0).astype(F32)
    group_rows = []
    for g in range(N_GROUPS):
        xg = biased[g * per_group:(g + 1) * per_group, :]
        m1, i1 = _first_argmax(xg, sub8, per_group)
        m2 = jnp.max(jnp.where(sub8 == i1, -jnp.inf, xg), axis=0, keepdims=True)
        group_rows.append(m1 + m2)
    cur = jnp.concatenate(group_rows, axis=0)
    gself = jnp.zeros(cur.shape, F32)
    for _ in range(TOPK_GROUPS):
        _, gi = _first_argmax(cur, sub8, N_GROUPS)
        hit = sub8 == gi
        gself = jnp.where(hit, 1.0, gself)
        cur = jnp.where(hit, -jnp.inf, cur)
    masked = jnp.concatenate(
        [jnp.where(gself[g:g + 1, :] > 0.5, biased[g * per_group:(g + 1) * per_group, :], -jnp.inf)
         for g in range(N_GROUPS)], axis=0)
    sub64 = lax.broadcasted_iota(jnp.int32, (N_EXPERTS, tm), 0).astype(F32)
    e_rows, s_rows = [], []
    sel = jnp.zeros(masked.shape, F32)
    for _ in range(TOP_K):
        _, ei = _first_argmax(masked, sub64, N_EXPERTS)
        hit = sub64 == ei
        e_rows.append(ei)
        s_rows.append(jnp.sum(jnp.where(hit, scores, 0.0), axis=0, keepdims=True))
        sel = jnp.where(hit, 1.0, sel)
        masked = jnp.where(hit, -jnp.inf, masked)
    total = s_rows[0]
    for s in s_rows[1:]:
        total = total + s
    w_rows = [s / total * ROUTED_SCALE for s in s_rows]
    before = jnp.dot(sel.astype(BF16), upper_ref[...], preferred_element_type=F32) + carry_ref[:, 0:1]
    r_rows = [jnp.sum(jnp.where(sub64 == ei, before, 0.0), axis=0, keepdims=True) for ei in e_rows]
    zrow = jnp.zeros((8 - TOP_K, tm), F32)
    ek_ref[...] = jnp.concatenate(e_rows + [zrow], axis=0).astype(jnp.int32)
    rk_ref[...] = jnp.concatenate(r_rows + [zrow], axis=0).astype(jnp.int32)
    wc_ref[...] = jnp.concatenate(w_rows + [jnp.zeros((LANES - TOP_K, tm), F32)], axis=0).T
    carry_ref[...] = carry_ref[...] + jnp.sum(sel, axis=1, keepdims=True)

    @pl.when(i == pl.num_programs(0) - 1)
    def _():
        counts = carry_ref[...]
        padded = jnp.floor((counts + (EXPERT_BLOCK - 1.0)) * (1.0 / EXPERT_BLOCK)) * EXPERT_BLOCK
        r = lax.broadcasted_iota(jnp.int32, (N_EXPERTS, N_EXPERTS), 0)
        cc = lax.broadcasted_iota(jnp.int32, (N_EXPERTS, N_EXPERTS), 1)
        pad_end = jnp.dot((cc <= r).astype(F32), padded, precision=lax.Precision.HIGHEST,
                          preferred_element_type=F32)
        pad_start = pad_end - padded
        ps_ref[...] = pad_start
        blk0 = lax.broadcasted_iota(jnp.int32, (N_EXPERTS, MAX_BLOCK_LANES), 1).astype(F32) * EXPERT_BLOCK
        e_of_blk = jnp.minimum(jnp.sum((pad_end[:, 0:1] <= blk0).astype(F32), axis=0, keepdims=True),
                               N_EXPERTS - 1.0)
        sub = lax.broadcasted_iota(jnp.int32, (N_EXPERTS, MAX_BLOCK_LANES), 0).astype(F32)
        seg_end = jnp.sum(jnp.where(sub == e_of_blk, (pad_start + counts)[:, 0:1], 0.0), axis=0, keepdims=True)
        n_valid = jnp.clip(seg_end - blk0[0:1, :], 0.0, float(EXPERT_BLOCK))
        n_active = jnp.broadcast_to(jnp.sum((n_valid > 0.0).astype(F32), axis=1, keepdims=True), n_valid.shape)
        be_ref[...] = jnp.concatenate([e_of_blk, n_valid, n_active, jnp.zeros((5, MAX_BLOCK_LANES), F32)],
                                      axis=0).astype(jnp.int32)


def _route_pallas(logits_t, bias_col):
    t = logits_t.shape[1]
    tm = min(ROUTE_TILE, t)
    row8 = pl.BlockSpec((8, tm), lambda i: (0, i))
    return pl.pallas_call(
        _route_kernel,
        grid=(t // tm,),
        in_specs=[pl.BlockSpec((LANES, tm), lambda i: (0, i)),
                  pl.BlockSpec((N_EXPERTS, 1), lambda i: (0, 0))],
        out_specs=[row8, row8,
                   pl.BlockSpec((tm, LANES), lambda i: (i, 0)),
                   pl.BlockSpec((N_EXPERTS, LANES), lambda i: (0, 0)),
                   pl.BlockSpec((8, MAX_BLOCK_LANES), lambda i: (0, 0))],
        out_shape=[jax.ShapeDtypeStruct((8, t), jnp.int32),
                   jax.ShapeDtypeStruct((8, t), jnp.int32),
                   jax.ShapeDtypeStruct((t, LANES), F32),
                   jax.ShapeDtypeStruct((N_EXPERTS, LANES), F32),
                   jax.ShapeDtypeStruct((8, MAX_BLOCK_LANES), jnp.int32)],
        scratch_shapes=[pltpu.VMEM((N_EXPERTS, LANES), F32), pltpu.VMEM((tm, tm), BF16)],
        compiler_params=_params(0, 1),
        name="route",
    )(logits_t, bias_col)


def _slot_kernel(ek_ref, rk_ref, ps_ref, slot_ref):
    tm = ek_ref.shape[1]
    sub64 = lax.broadcasted_iota(jnp.int32, (N_EXPERTS, tm), 0)
    start = ps_ref[:, 0:1]
    rows = []
    for kk in range(TOP_K):
        seg = jnp.sum(jnp.where(sub64 == ek_ref[kk:kk + 1, :], start, 0.0), axis=0, keepdims=True)
        rows.append(seg.astype(jnp.int32) + rk_ref[kk:kk + 1, :])
    slot_ref[...] = jnp.concatenate(rows + [jnp.zeros((8 - TOP_K, tm), jnp.int32)], axis=0)


def _slots(ek, rk, ps):
    t = ek.shape[1]
    tm = min(ROUTE_TILE, t)
    row8 = pl.BlockSpec((8, tm), lambda i: (0, i))
    return pl.pallas_call(
        _slot_kernel,
        grid=(t // tm,),
        in_specs=[row8, row8, pl.BlockSpec((N_EXPERTS, LANES), lambda i: (0, 0))],
        out_specs=row8,
        out_shape=jax.ShapeDtypeStruct((8, t), jnp.int32),
        compiler_params=_params(1),
        name="slots",
    )(ek, rk, ps)


SC_ROWS = 32
FINAL_PARTS = 2


def _sc_mesh():
    info = plsc.get_sparse_core_info()
    mesh = plsc.VectorSubcoreMesh(core_axis_name="c", subcore_axis_name="s")
    return mesh, info.num_cores, info.num_cores * info.num_subcores


def _sc_dispatch(x, slot_flat, n_rows):
    t, dp = x.shape
    n_slots = slot_flat.shape[0] // t
    mesh, n_cores, n_workers = _sc_mesh()
    per_w = t // n_workers
    n_chunks = per_w // SC_ROWS
    assert per_w * n_workers == t and n_chunks * SC_ROWS == per_w and n_chunks % 2 == 0

    @functools.partial(
        pl.kernel, out_type=jax.ShapeDtypeStruct((n_rows, dp), x.dtype), mesh=mesh,
        scratch_types=[pltpu.VMEM((2, n_slots, SC_ROWS), jnp.int32), pltpu.VMEM((2, SC_ROWS, dp), x.dtype),
                       pltpu.SemaphoreType.DMA((2,)), pltpu.SemaphoreType.DMA((2,))])
    def dispatch(x_hbm, i_hbm, o_hbm, idx_v, rows_v, read_sem, scatter_sem):
        base = (lax.axis_index("s") * n_cores + lax.axis_index("c")) * per_w

        def read(j, b):
            return pltpu.make_async_copy(x_hbm.at[pl.ds(pl.multiple_of(base + j * SC_ROWS, 8), SC_ROWS)],
                                         rows_v.at[b], read_sem.at[b])

        def scatter(k, b):
            return pltpu.make_async_copy(rows_v.at[b], o_hbm.at[idx_v.at[b, k]], scatter_sem.at[b])

        read(0, 0).start()

        @pl.loop(0, n_chunks, step=2)
        def _(j):
            for b in range(2):
                jj = j + b

                @pl.when(jj >= 1)
                def _():
                    for k in range(n_slots):
                        scatter(k, 1 - b).wait()

                @pl.when(jj + 1 < n_chunks)
                def _():
                    read(jj + 1, 1 - b).start()

                for k in range(n_slots):
                    pltpu.sync_copy(i_hbm.at[pl.ds(pl.multiple_of(k * t + base + jj * SC_ROWS, 8), SC_ROWS)],
                                    idx_v.at[b, k])
                read(jj, b).wait()
                for k in range(n_slots):
                    scatter(k, b).start()

        for k in range(n_slots):
            scatter(k, (n_chunks - 1) % 2).wait()

    return dispatch(x, slot_flat)


def _sc_gather_rows(y, slot_flat):
    n = slot_flat.shape[0]
    dp = y.shape[1]
    mesh, n_cores, n_workers = _sc_mesh()
    per_w = n // n_workers
    n_chunks = per_w // SC_ROWS
    assert per_w * n_workers == n and n_chunks * SC_ROWS == per_w and n_chunks % 2 == 0

    @functools.partial(
        pl.kernel, out_type=jax.ShapeDtypeStruct((n, dp), y.dtype), mesh=mesh,
        scratch_types=[pltpu.VMEM((per_w,), jnp.int32), pltpu.VMEM((2, SC_ROWS, dp), y.dtype),
                       pltpu.SemaphoreType.DMA((2,)), pltpu.SemaphoreType.DMA((2,))])
    def combine(y_hbm, i_hbm, o_hbm, idx_v, rows_v, gather_sem, write_sem):
        base = (lax.axis_index("s") * n_cores + lax.axis_index("c")) * per_w
        pltpu.sync_copy(i_hbm.at[pl.ds(pl.multiple_of(base, 8), per_w)], idx_v)

        def gather(j, b):
            return pltpu.make_async_copy(y_hbm.at[idx_v.at[pl.ds(pl.multiple_of(j * SC_ROWS, 8), SC_ROWS)]],
                                         rows_v.at[b], gather_sem.at[b])

        def write(j, b):
            return pltpu.make_async_copy(rows_v.at[b],
                                         o_hbm.at[pl.ds(pl.multiple_of(base + j * SC_ROWS, 8), SC_ROWS)],
                                         write_sem.at[b])

        gather(0, 0).start()

        @pl.loop(0, n_chunks, step=2)
        def _(j):
            for b in range(2):
                jj = j + b

                @pl.when(jj >= 1)
                def _():
                    write(jj - 1, 1 - b).wait()

                @pl.when(jj + 1 < n_chunks)
                def _():
                    gather(jj + 1, 1 - b).start()

                gather(jj, b).wait()
                write(jj, b).start()

        write(n_chunks - 1, (n_chunks - 1) % 2).wait()

    return combine(y, slot_flat)


def _layer(x, cond_ada, positions, w_in, dn_conv_w, dn_a_log, dn_dt_bias, dn_norm_gain, mla_q_norm_gain,
           w_q_up, mla_kv_norm_gain, w_kv_up, w_out, norm1_gain, norm2_gain, w_router, router_bias,
           w_exp_gate_up, w_exp_down, w_sh_gate_up, w_sh_down, out_gain):
    bsz, seq, d = x.shape
    t = bsz * seq
    x2 = x.reshape(t, d)
    ada3 = cond_ada.reshape(bsz, 6, d)

    s_z = 4 * DN_WIDTH
    s_a = s_z + 2 * DN_HEADS
    s_kpe = s_a + MLA_Q_RANK + MLA_KV_RANK
    w_main = jnp.concatenate([w_in[:, :s_z], w_in[:, s_a:s_kpe]], axis=1).astype(BF16)
    w_aux = jnp.concatenate([w_in[:, s_kpe:], w_in[:, s_z:s_a],
                             jnp.zeros((d, LANES - MLA_ROPE - 2 * DN_HEADS), F32)], axis=1).astype(BF16)
    auxp = jnp.zeros((8, LANES), F32)
    auxp = auxp.at[0, AUX_G:AUX_G + DN_HEADS].set(dn_a_log).at[1, AUX_G:AUX_G + DN_HEADS].set(dn_dt_bias)
    main, auxc, auxr = _in_proj(x2, ada3, norm1_gain.reshape(1, d), w_main, w_aux, auxp, seq)

    dn = _delta(main.reshape(bsz, seq, MAIN_WIDTH), auxc.reshape(bsz, seq, LANES), auxr,
                dn_conv_w.T, dn_norm_gain.reshape(1, DN_DIM))

    qk = MLA_NOPE + MLA_ROPE
    wq3 = w_q_up.reshape(MLA_Q_RANK, MLA_HEADS, qk)
    wq = jnp.concatenate([wq3, jnp.zeros((MLA_Q_RANK, MLA_HEADS, MLA_QK_PAD - qk), F32)], axis=2)
    wq = wq.reshape(MLA_Q_RANK, MLA_HEADS * MLA_QK_PAD).astype(BF16)
    wkv3 = w_kv_up.reshape(MLA_KV_RANK, MLA_HEADS, MLA_NOPE + MLA_V)
    wkv = jnp.concatenate([wkv3[:, :, :MLA_NOPE].reshape(MLA_KV_RANK, -1),
                           wkv3[:, :, MLA_NOPE:].reshape(MLA_KV_RANK, -1)], axis=1).astype(BF16)
    half = MLA_ROPE // 2
    inv_freq = ROPE_THETA ** (-jnp.arange(half, dtype=F32) / half)
    zeros = jnp.zeros((LANES - MLA_ROPE,), F32)
    rope_tab = jnp.zeros((8, LANES), F32)
    rope_tab = rope_tab.at[0].set(jnp.concatenate([inv_freq, inv_freq, zeros]))
    rope_tab = rope_tab.at[1].set(jnp.concatenate([jnp.ones((MLA_ROPE,), F32), zeros]))
    rope_tab = rope_tab.at[2].set(jnp.concatenate([-jnp.ones((half,), F32), jnp.ones((half,), F32), zeros]))
    q, k, v = _mla_proj(main, auxc, positions.reshape(t, 1), mla_q_norm_gain.reshape(1, -1),
                        mla_kv_norm_gain.reshape(1, -1), wq, wkv, rope_tab, seq)
    mla = _attention(q.reshape(bsz, seq, -1), k.reshape(bsz, seq, -1), v)

    w_r = jnp.concatenate([w_router, jnp.zeros((d, LANES - N_EXPERTS), F32)], axis=1)
    w_r_hi = w_r.astype(BF16)
    w_r = jnp.concatenate([w_r_hi, (w_r - w_r_hi.astype(F32)).astype(BF16)], axis=1)
    x1, h2, logits_t = _out_proj(x2, dn.reshape(t, DN_WIDTH), mla.reshape(t, -1), ada3,
                                 norm2_gain.reshape(1, d), w_out.astype(BF16), w_r, seq)

    n_rows = -(-(t * TOP_K + N_EXPERTS * (EXPERT_BLOCK - 1)) // EXPERT_BLOCK) * EXPERT_BLOCK
    n_blocks = n_rows // EXPERT_BLOCK
    assert n_blocks <= MAX_BLOCK_LANES
    ek, rk, wc, seg_start, blocks = _route_pallas(logits_t, router_bias.reshape(N_EXPERTS, 1))
    slot = _slots(ek, rk, seg_start)[:TOP_K]
    x_sorted = _sc_dispatch(h2, slot.reshape(-1), n_rows)
    shared = _shared_ffn(h2, w_sh_gate_up.astype(BF16), w_sh_down.astype(BF16))
    y_sorted = _experts(blocks[0, :n_blocks], blocks[1, :n_blocks], blocks[2, :1], x_sorted,
                        w_exp_gate_up, w_exp_down, shared)

    n_parts = FINAL_PARTS if (t // min(256, seq)) % FINAL_PARTS == 0 else 1
    out = None
    for part in range(n_parts):
        part_slots = slot[:, part * (t // n_parts):(part + 1) * (t // n_parts)].reshape(-1)
        y_part = _sc_gather_rows(y_sorted, part_slots).reshape(TOP_K, t // n_parts, d // 2)
        out = _final_part(x1, shared, y_part, wc, ada3, out_gain.reshape(1, d), seq, part, n_parts, out)
    return out.reshape(bsz, seq, d)


def kernel(x, c, positions, w_ada, b_ada, norm1_gain, w_in, dn_conv_w, dn_a_log, dn_dt_bias, dn_norm_gain,
           mla_q_norm_gain, w_q_up, mla_kv_norm_gain, w_kv_up, w_out, norm2_gain, w_router, router_bias,
           w_exp_gate_up, w_exp_down, w_sh_gate_up, w_sh_down, final_norm_gain):
    depth = w_ada.shape[0]
    assert depth == 1, "the final RMSNorm is fused into the single layer's last kernel"
    ada = _ada(c, w_ada[0], b_ada[0])
    return _layer(x, ada, positions, w_in[0], dn_conv_w[0], dn_a_log[0], dn_dt_bias[0], dn_norm_gain[0],
                  mla_q_norm_gain[0], w_q_up[0], mla_kv_norm_gain[0], w_kv_up[0], w_out[0], norm1_gain[0],
                  norm2_gain[0], w_router[0], router_bias[0], w_exp_gate_up[0], w_exp_down[0],
                  w_sh_gate_up[0], w_sh_down[0], final_norm_gain)
```

```python
import functools

import jax
import jax.numpy as jnp
from jax import lax
from jax.experimental import pallas as pl
from jax.experimental.pallas import tpu as pltpu
from jax.experimental.pallas import tpu_sc as plsc

F32 = jnp.float32
BF16 = jnp.bfloat16

DN_HEADS = 8
DN_DIM = 128
DN_WIDTH = DN_HEADS * DN_DIM
DN_CONV = 4
DN_CHUNK = 128
MLA_HEADS = 8
MLA_Q_RANK = 512
MLA_KV_RANK = 512
MLA_NOPE = 128
MLA_ROPE = 64
MLA_V = 128
MLA_QK_PAD = 256
ROPE_THETA = 10000.0
N_EXPERTS = 64
TOP_K = 6
N_GROUPS = 8
TOPK_GROUPS = 4
EXPERT_FF = 512
ROUTED_SCALE = 2.5
EXPERT_BLOCK = 512
RMS_EPS = 1e-6
L2_EPS = 1e-6

LANES = 128
MXU_WIDTH = 256
MAIN_WIDTH = 4 * DN_WIDTH + MLA_Q_RANK + MLA_KV_RANK
AUX_BETA = MLA_ROPE
AUX_G = MLA_ROPE + DN_HEADS
VMEM_LIMIT = 56 * 1024 * 1024


def _params(n_parallel, n_arbitrary=0):
    sem = ("parallel",) * n_parallel + ("arbitrary",) * n_arbitrary
    return pltpu.CompilerParams(dimension_semantics=sem, vmem_limit_bytes=VMEM_LIMIT)


def _silu(x):
    return x * jax.nn.sigmoid(x)


def _bdot(a, b):
    return jnp.dot(a.astype(BF16), b.astype(BF16), preferred_element_type=F32)


def _bdot_nt(a, b):
    return lax.dot_general(a.astype(BF16), b.astype(BF16), (((1,), (1,)), ((), ())),
                           preferred_element_type=F32)


def _rms(x, gain):
    return x * lax.rsqrt(jnp.mean(x * x, axis=-1, keepdims=True) + RMS_EPS) * gain


HIGH_HALF = 0xFFFF0000


def _pack_bf16_pairs(x):
    n = x.shape[1] // 2
    bits = lax.bitcast_convert_type(x.astype(BF16).astype(F32), jnp.uint32)
    return (bits[:, :n] >> 16) | (bits[:, n:] & jnp.uint32(HIGH_HALF))


def _unpack_bf16_pairs(p):
    lo = lax.bitcast_convert_type(p << 16, F32)
    hi = lax.bitcast_convert_type(p & jnp.uint32(HIGH_HALF), F32)
    return jnp.concatenate([lo, hi], axis=1)


def _ada_kernel(c_ref, w_ref, b_ref, o_ref):
    o_ref[...] = _bdot(_silu(c_ref[...]), w_ref[...]) + b_ref[...]


def _ada(c, w_ada, b_ada):
    bsz, d = c.shape
    n = w_ada.shape[1]
    tn = 1024
    return pl.pallas_call(
        _ada_kernel,
        grid=(n // tn,),
        in_specs=[pl.BlockSpec((bsz, d), lambda j: (0, 0)),
                  pl.BlockSpec((d, tn), lambda j: (0, j)),
                  pl.BlockSpec((1, tn), lambda j: (0, j))],
        out_specs=pl.BlockSpec((bsz, tn), lambda j: (0, j)),
        out_shape=jax.ShapeDtypeStruct((bsz, n), F32),
        compiler_params=_params(1),
        name="ada",
    )(c, w_ada, b_ada.reshape(1, n))


def _in_proj_kernel(x_ref, ada_ref, gain_ref, wm_ref, wa_ref, auxp_ref,
                    main_ref, auxc_ref, auxr_ref, h_ref):
    j = pl.program_id(1)

    @pl.when(j == 0)
    def _():
        ada = ada_ref[0]
        h = _rms(x_ref[...], gain_ref[...]) * (1.0 + ada[1:2, :]) + ada[0:1, :]
        hb = h.astype(BF16)
        h_ref[...] = hb
        aux = jnp.dot(hb, wa_ref[...], preferred_element_type=F32)
        tm = aux.shape[0]
        lane = lax.broadcasted_iota(jnp.int32, aux.shape, 1)
        is_beta = (lane >= AUX_BETA) & (lane < AUX_G)
        is_g = (lane >= AUX_G) & (lane < AUX_G + DN_HEADS)
        a_log = auxp_ref[0:1, :]
        dt_bias = auxp_ref[1:2, :]
        sp_in = aux + dt_bias
        softplus = jnp.maximum(sp_in, 0.0) + jnp.log(1.0 + jnp.exp(-jnp.abs(sp_in)))
        g = -jnp.exp(a_log) * softplus
        aux = jnp.where(is_beta, jax.nn.sigmoid(aux), jnp.where(is_g, g, aux))
        r = lax.broadcasted_iota(jnp.int32, (LANES, LANES), 0)
        cidx = lax.broadcasted_iota(jnp.int32, (LANES, LANES), 1)
        tri = ((cidx <= r) & (cidx // DN_CHUNK == r // DN_CHUNK)).astype(F32)
        g_cols = (cidx >= AUX_G) & (cidx < AUX_G + DN_HEADS)
        parts = []
        for t in range(tm // LANES):
            blk = aux[t * LANES:(t + 1) * LANES, :]
            cs = jnp.dot(tri, blk, precision=lax.Precision.HIGHEST, preferred_element_type=F32)
            parts.append(jnp.where(g_cols, cs, blk))
        aux = jnp.concatenate(parts, axis=0)
        auxc_ref[...] = aux
        auxr_ref[...] = aux.T

    main_ref[...] = jnp.dot(h_ref[...], wm_ref[...], preferred_element_type=F32).astype(main_ref.dtype)


def _in_proj(x2, ada3, gain, w_main, w_aux, auxp, seq):
    t, d = x2.shape
    n = w_main.shape[1]
    tm = min(1024, seq)
    tn = 1024
    per_b = seq // tm
    return pl.pallas_call(
        _in_proj_kernel,
        grid=(t // tm, n // tn),
        in_specs=[pl.BlockSpec((tm, d), lambda i, j: (i, 0)),
                  pl.BlockSpec((1, 6, d), lambda i, j: (i // per_b, 0, 0)),
                  pl.BlockSpec((1, d), lambda i, j: (0, 0)),
                  pl.BlockSpec((d, tn), lambda i, j: (0, j)),
                  pl.BlockSpec((d, LANES), lambda i, j: (0, 0)),
                  pl.BlockSpec((8, LANES), lambda i, j: (0, 0))],
        out_specs=[pl.BlockSpec((tm, tn), lambda i, j: (i, j)),
                   pl.BlockSpec((tm, LANES), lambda i, j: (i, 0)),
                   pl.BlockSpec((LANES, tm), lambda i, j: (0, i))],
        out_shape=[jax.ShapeDtypeStruct((t, n), BF16),
                   jax.ShapeDtypeStruct((t, LANES), F32),
                   jax.ShapeDtypeStruct((LANES, t), F32)],
        scratch_shapes=[pltpu.VMEM((tm, d), BF16)],
        compiler_params=_params(1, 1),
        name="in_proj",
    )(x2, ada3, gain, w_main, w_aux, auxp)


def _unit_lower_inverses(ms):
    c = ms[0].shape[0]
    base = 16
    r = lax.broadcasted_iota(jnp.int32, (c, c), 0)
    cc = lax.broadcasted_iota(jnp.int32, (c, c), 1)
    eye = (r == cc).astype(F32)
    same = (r // base) == (cc // base)

    def neumann(a, order):
        acc = [eye - x for x in a]
        pw = a
        done = 2
        while done < order:
            pw = [_bdot(x, x) for x in pw]
            acc = [t + _bdot(t, p) for t, p in zip(acc, pw)]
            done *= 2
        return acc

    md = [jnp.where(same, m, 0.0) for m in ms]
    mo = [jnp.where(same, 0.0, m) for m in ms]
    td = neumann(md, base)
    n1 = [_bdot(t, o) for t, o in zip(td, mo)]
    return [_bdot(l, t) for l, t in zip(neumann(n1, c // base), td)]


DELTA_TILE = 2048
DELTA_SCAN_TILE = 512


def _delta_prep_kernel(q_ref, k_ref, v_ref, qh_ref, kh_ref, vh_ref, auxc_ref, auxr_ref, wq_ref, wk_ref, wv_ref,
                       u_ref, w_ref, qe_ref, kd_ref, a_ref, pad_ref):
    head = pl.program_id(1)
    first_tile = pl.program_id(2) == 0
    ts = q_ref.shape[0]
    c = DN_CHUNK
    halo = qh_ref.shape[0]

    def conv_silu(x_ref, h_ref, w_ref):
        top = pad_ref.shape[0] - ts
        pad_ref[0:top, :] = jnp.where(first_tile, 0.0, h_ref[halo - top:, :].astype(F32))
        pad_ref[top:, :] = x_ref[...].astype(F32)
        w = w_ref[...]
        acc = pad_ref[top:, :] * w[DN_CONV - 1:DN_CONV, :]
        for s in range(1, DN_CONV):
            acc = acc + pad_ref[top - s:top - s + ts, :] * w[DN_CONV - 1 - s:DN_CONV - s, :]
        return _silu(acc)

    q = conv_silu(q_ref, qh_ref, wq_ref)
    k = conv_silu(k_ref, kh_ref, wk_ref)
    v = conv_silu(v_ref, vh_ref, wv_ref)
    q = q * lax.rsqrt(jnp.sum(q * q, axis=-1, keepdims=True) + L2_EPS) * (DN_DIM ** -0.5)
    k = k * lax.rsqrt(jnp.sum(k * k, axis=-1, keepdims=True) + L2_EPS)

    auxc = auxc_ref[...]
    lane = lax.broadcasted_iota(jnp.int32, auxc.shape, 1)
    beta_all = jnp.sum(jnp.where(lane == AUX_BETA + head, auxc, 0.0), axis=1, keepdims=True)
    gcol_all = jnp.sum(jnp.where(lane == AUX_G + head, auxc, 0.0), axis=1, keepdims=True)
    grows = auxr_ref[AUX_G:AUX_G + DN_HEADS, :]
    head_row = lax.broadcasted_iota(jnp.int32, grows.shape, 0)
    grow_all = jnp.sum(jnp.where(head_row == head, grows, 0.0), axis=0, keepdims=True)

    ri = lax.broadcasted_iota(jnp.int32, (c, c), 0)
    ci = lax.broadcasted_iota(jnp.int32, (c, c), 1)
    causal = ri >= ci
    strict = ri > ci
    chunks = [slice(j * c, (j + 1) * c) for j in range(ts // c)]
    decay = [jnp.exp(jnp.where(causal, gcol_all[rows] - grow_all[:, rows], -jnp.inf)) for rows in chunks]
    kb = [k[rows] * beta_all[rows] for rows in chunks]
    both = [_bdot_nt(jnp.concatenate([kb_j, q[rows]], axis=0), k[rows])
            for kb_j, rows in zip(kb, chunks)]
    tinv = _unit_lower_inverses([jnp.where(strict, b[:c] * d, 0.0) for b, d in zip(both, decay)])
    eg = [jnp.exp(gcol_all[rows]) for rows in chunks]
    sol = [_bdot(t, jnp.concatenate([v[rows] * beta_all[rows], kb_j * e], axis=1))
           for t, rows, kb_j, e in zip(tinv, chunks, kb, eg)]
    for j, rows in enumerate(chunks):
        gcol = gcol_all[rows]
        u_ref[rows, :] = sol[j][:, :DN_DIM].astype(u_ref.dtype)
        w_ref[rows, :] = sol[j][:, DN_DIM:].astype(w_ref.dtype)
        qe_ref[rows, :] = (q[rows] * eg[j]).astype(qe_ref.dtype)
        kd_ref[rows, :] = (k[rows] * jnp.exp(gcol[c - 1:c, :] - gcol)).astype(kd_ref.dtype)
        a_ref[rows, :] = jnp.where(causal, both[j][c:] * decay[j], 0.0).astype(a_ref.dtype)


def _delta_scan_kernel(u_ref, w_ref, qe_ref, kd_ref, a_ref, z_ref, auxc_ref, gain_ref, o_ref, st_ref):
    seq = u_ref.shape[0]
    n_heads = a_ref.shape[0]
    c = DN_CHUNK

    @pl.when(pl.program_id(1) == 0)
    def _():
        st_ref[...] = jnp.zeros_like(st_ref)

    gain = gain_ref[...]
    lane = lax.broadcasted_iota(jnp.int32, (1, LANES), 1)

    def scan_chunk(i, carry):
        start = pl.multiple_of(i * c, c)
        rows = pl.ds(start, c)
        last = auxc_ref[pl.ds(start + c - 1, 1), :]
        heads = range(n_heads)
        cols = [slice(hh * DN_DIM, (hh + 1) * DN_DIM) for hh in heads]
        state = [st_ref[hh] for hh in heads]
        sb = [s.astype(BF16) for s in state]
        ws = [jnp.dot(w_ref[rows, cl], s, preferred_element_type=F32) for cl, s in zip(cols, sb)]
        qs = [jnp.dot(qe_ref[rows, cl], s, preferred_element_type=F32) for cl, s in zip(cols, sb)]
        vb = [(u_ref[rows, cl].astype(F32) - x).astype(BF16) for cl, x in zip(cols, ws)]
        kv = [lax.dot_general(kd_ref[rows, cl], x, (((0,), (0,)), ((), ())), preferred_element_type=F32)
              for cl, x in zip(cols, vb)]
        av = [jnp.dot(a_ref[hh, rows, :], x, preferred_element_type=F32) for hh, x in zip(heads, vb)]
        for hh in heads:
            glast = jnp.sum(jnp.where(lane == AUX_G + hh, last, 0.0), axis=1, keepdims=True)
            st_ref[hh] = state[hh] * jnp.exp(glast) + kv[hh]
            z = z_ref[rows, cols[hh]].astype(F32)
            o_ref[rows, cols[hh]] = (_rms(qs[hh] + av[hh], gain) * _silu(z)).astype(o_ref.dtype)
        return carry

    lax.fori_loop(0, seq // c, scan_chunk, 0)


def _delta(main3, auxc3, auxr, conv_w_t, gain):
    bsz, seq, _ = main3.shape
    hb = DN_WIDTH // DN_DIM
    ts = min(DELTA_TILE, seq)
    halo = 16
    tiles = seq // ts

    def col(offset):
        return pl.BlockSpec((None, ts, DN_DIM), lambda b, h, i: (b, i, offset * hb + h))

    def halo_col(offset):
        return pl.BlockSpec((None, halo, DN_DIM),
                            lambda b, h, i: (b, jnp.maximum(i * (ts // halo) - 1, 0), offset * hb + h))

    def wcol(offset):
        return pl.BlockSpec((DN_CONV, DN_DIM), lambda b, h, i: (0, offset * hb + h))

    tok = pl.BlockSpec((None, ts, DN_DIM), lambda b, h, i: (b, i, h))
    tok_shape = jax.ShapeDtypeStruct((bsz, seq, DN_WIDTH), BF16)
    u, w, qe, kd, a = pl.pallas_call(
        _delta_prep_kernel,
        grid=(bsz, DN_HEADS, tiles),
        in_specs=[col(0), col(1), col(2), halo_col(0), halo_col(1), halo_col(2),
                  pl.BlockSpec((None, ts, LANES), lambda b, h, i: (b, i, 0)),
                  pl.BlockSpec((LANES, ts), lambda b, h, i: (0, b * tiles + i)),
                  wcol(0), wcol(1), wcol(2)],
        out_specs=[tok, tok, tok, tok,
                   pl.BlockSpec((None, None, ts, DN_CHUNK), lambda b, h, i: (b, h, i, 0))],
        out_shape=[tok_shape, tok_shape, tok_shape, tok_shape,
                   jax.ShapeDtypeStruct((bsz, DN_HEADS, seq, DN_CHUNK), BF16)],
        scratch_shapes=[pltpu.VMEM((ts + 8, DN_DIM), F32)],
        compiler_params=_params(3),
        name="delta_prep",
    )(main3, main3, main3, main3, main3, main3, auxc3, auxr, conv_w_t, conv_w_t, conv_w_t)

    tsc = min(DELTA_SCAN_TILE, seq)
    wide = pl.BlockSpec((None, tsc, DN_WIDTH), lambda b, i: (b, i, 0))
    return pl.pallas_call(
        _delta_scan_kernel,
        grid=(bsz, seq // tsc),
        in_specs=[wide, wide, wide, wide,
                  pl.BlockSpec((None, DN_HEADS, tsc, DN_CHUNK), lambda b, i: (b, 0, i, 0)),
                  pl.BlockSpec((None, tsc, DN_WIDTH), lambda b, i: (b, i, 3)),
                  pl.BlockSpec((None, tsc, LANES), lambda b, i: (b, i, 0)),
                  pl.BlockSpec((1, DN_DIM), lambda b, i: (0, 0))],
        out_specs=wide,
        out_shape=tok_shape,
        scratch_shapes=[pltpu.VMEM((DN_HEADS, DN_DIM, DN_DIM), F32)],
        compiler_params=_params(1, 1),
        name="delta_scan",
    )(u, w, qe, kd, a, main3, auxc3, gain)


def _mla_proj_kernel(cq_ref, ckv_ref, auxc_ref, pos_ref, qg_ref, kvg_ref, wq_ref, wkv_ref, rope_ref,
                     q_ref, k_ref, v_ref):
    scale = (MLA_NOPE + MLA_ROPE) ** -0.5
    ang = pos_ref[...].astype(F32) * rope_ref[0:1, :]
    cos_t = jnp.cos(ang) * rope_ref[1:2, :]
    sin_t = jnp.sin(ang) * rope_ref[2:3, :]
    lane = lax.broadcasted_iota(jnp.int32, ang.shape, 1)
    first = lane < MLA_ROPE // 2

    def rope(a):
        swapped = jnp.where(first, pltpu.roll(a, LANES - MLA_ROPE // 2, 1), pltpu.roll(a, MLA_ROPE // 2, 1))
        return a * cos_t + swapped * sin_t

    ql = _bdot(_rms(cq_ref[...].astype(F32), qg_ref[...]), wq_ref[...]) * scale
    kv = _bdot(_rms(ckv_ref[...].astype(F32), kvg_ref[...]), wkv_ref[...])
    kpe = rope(auxc_ref[...]).astype(k_ref.dtype)
    for h in range(MLA_HEADS):
        o = h * MLA_QK_PAD
        q_ref[:, o:o + MLA_NOPE] = ql[:, o:o + MLA_NOPE].astype(q_ref.dtype)
        q_ref[:, o + MLA_NOPE:o + MLA_QK_PAD] = rope(ql[:, o + MLA_NOPE:o + MLA_QK_PAD]).astype(q_ref.dtype)
        k_ref[:, o:o + MLA_NOPE] = kv[:, h * MLA_NOPE:(h + 1) * MLA_NOPE].astype(k_ref.dtype)
        k_ref[:, o + MLA_NOPE:o + MLA_QK_PAD] = kpe
    v_ref[...] = kv[:, MLA_HEADS * MLA_NOPE:].T.astype(v_ref.dtype)


def _mla_proj(main, auxc, pos, q_gain, kv_gain, wq, wkv, rope_tab, seq):
    t = main.shape[0]
    tm = min(512, seq)
    per_b = seq // tm
    cq_blk = (4 * DN_WIDTH) // MLA_Q_RANK
    hq = MLA_HEADS * MLA_QK_PAD
    hv = MLA_HEADS * MLA_V
    return pl.pallas_call(
        _mla_proj_kernel,
        grid=(t // tm,),
        in_specs=[pl.BlockSpec((tm, MLA_Q_RANK), lambda i: (i, cq_blk)),
                  pl.BlockSpec((tm, MLA_KV_RANK), lambda i: (i, cq_blk + 1)),
                  pl.BlockSpec((tm, LANES), lambda i: (i, 0)),
                  pl.BlockSpec((tm, 1), lambda i: (i, 0)),
                  pl.BlockSpec((1, MLA_Q_RANK), lambda i: (0, 0)),
                  pl.BlockSpec((1, MLA_KV_RANK), lambda i: (0, 0)),
                  pl.BlockSpec((MLA_Q_RANK, hq), lambda i: (0, 0)),
                  pl.BlockSpec((MLA_KV_RANK, 2 * hv), lambda i: (0, 0)),
                  pl.BlockSpec((8, LANES), lambda i: (0, 0))],
        out_specs=[pl.BlockSpec((tm, hq), lambda i: (i, 0)),
                   pl.BlockSpec((tm, hq), lambda i: (i, 0)),
                   pl.BlockSpec((None, hv, tm), lambda i: (i // per_b, 0, i % per_b))],
        out_shape=[jax.ShapeDtypeStruct((t, hq), BF16),
                   jax.ShapeDtypeStruct((t, hq), BF16),
                   jax.ShapeDtypeStruct((t // seq, hv, seq), BF16)],
        compiler_params=_params(1),
        name="mla_proj",
    )(main, main, auxc, pos, q_gain, kv_gain, wq, wkv, rope_tab)


ATTN_PARTS = 4


def _attn_kernel(q_ref, k_ref, vt_ref, o_ref, m_ref, l_ref, acc_ref):
    qi = pl.program_id(2)
    n_parts = m_ref.shape[0]
    tk = m_ref.shape[2]
    q = [q_ref[p * tk:(p + 1) * tk, :] for p in range(n_parts)]
    m_ref[...] = jnp.full_like(m_ref, -jnp.inf)
    l_ref[...] = jnp.zeros_like(l_ref)
    acc_ref[...] = jnp.zeros_like(acc_ref)
    key = lax.broadcasted_iota(jnp.int32, (tk, tk), 0)
    query = lax.broadcasted_iota(jnp.int32, (tk, tk), 1)
    visible = key <= query

    def step(j, parts, masked_part):
        start = pl.multiple_of(j * tk, tk)
        kb = k_ref[pl.ds(start, tk), :]
        vtb = vt_ref[:, pl.ds(start, tk)]
        s = [lax.dot_general(kb, q[p], (((1,), (1,)), ((), ())), preferred_element_type=F32) for p in parts]
        s = [jnp.where(visible, x, -jnp.inf) if p == masked_part else x for p, x in zip(parts, s)]
        m_old = [m_ref[p] for p in parts]
        m_new = [jnp.maximum(mo, jnp.max(x, axis=0, keepdims=True)) for mo, x in zip(m_old, s)]
        e = [jnp.exp(x - mn) for x, mn in zip(s, m_new)]
        pv = [jnp.dot(vtb, x.astype(BF16), preferred_element_type=F32) for x in e]
        for i, p in enumerate(parts):
            alpha = jnp.exp(m_old[i] - m_new[i])
            l_ref[p] = alpha * l_ref[p] + jnp.sum(e[i], axis=0, keepdims=True)
            acc_ref[p] = alpha * acc_ref[p] + pv[i]
            m_ref[p] = m_new[i]

    every = tuple(range(n_parts))

    def body(j, carry):
        step(j, every, None)
        return carry

    lax.fori_loop(0, qi * n_parts, body, 0)
    for d in range(n_parts):
        step(qi * n_parts + d, every[d:], d)
    for p in every:
        o_ref[p * tk:(p + 1) * tk, :] = (acc_ref[p] / l_ref[p]).T.astype(o_ref.dtype)


def _attention(q3, k3, vt3):
    bsz, seq, _ = q3.shape
    tk = min(512, seq)
    n_parts = min(ATTN_PARTS, seq // tk)
    tq = tk * n_parts
    return pl.pallas_call(
        _attn_kernel,
        grid=(bsz, MLA_HEADS, seq // tq),
        in_specs=[pl.BlockSpec((None, tq, MLA_QK_PAD), lambda b, h, i: (b, i, h)),
                  pl.BlockSpec((None, seq, MLA_QK_PAD), lambda b, h, i: (b, 0, h)),
                  pl.BlockSpec((None, MLA_V, seq), lambda b, h, i: (b, h, 0))],
        out_specs=pl.BlockSpec((None, tq, MLA_V), lambda b, h, i: (b, i, h)),
        out_shape=jax.ShapeDtypeStruct((bsz, seq, MLA_HEADS * MLA_V), BF16),
        scratch_shapes=[pltpu.VMEM((n_parts, 1, tk), F32), pltpu.VMEM((n_parts, 1, tk), F32),
                        pltpu.VMEM((n_parts, MLA_V, tk), F32)],
        compiler_params=_params(3),
        name="attention",
    )(q3, k3, vt3)


def _out_proj_kernel(x_ref, dn_ref, mla_ref, ada_ref, gain_ref, w_ref, wr_ref, x1_ref, h2_ref, lg_ref):
    ada = ada_ref[0]
    half = dn_ref.shape[1]
    mix = (jnp.dot(dn_ref[...], w_ref[:half, :], preferred_element_type=F32)
           + jnp.dot(mla_ref[...], w_ref[half:, :], preferred_element_type=F32))
    x1 = x_ref[...] + ada[2:3, :] * mix
    x1_ref[...] = x1
    h2 = _rms(x1, gain_ref[...]) * (1.0 + ada[4:5, :]) + ada[3:4, :]
    h2_ref[...] = _pack_bf16_pairs(h2)
    h_hi = h2.astype(BF16)
    h_lo = (h2 - h_hi.astype(F32)).astype(BF16)
    both = jnp.dot(h_hi, wr_ref[...], preferred_element_type=F32)
    logits = both[:, :LANES] + both[:, LANES:] + jnp.dot(h_lo, wr_ref[:, :LANES], preferred_element_type=F32)
    lg_ref[...] = logits.T


def _out_proj(x2, dn, mla, ada3, gain, w_out, w_router, seq):
    t, d = x2.shape
    tm = min(256, seq)
    per_b = seq // tm
    half = dn.shape[1]
    return pl.pallas_call(
        _out_proj_kernel,
        grid=(t // tm,),
        in_specs=[pl.BlockSpec((tm, d), lambda i: (i, 0)),
                  pl.BlockSpec((tm, half), lambda i: (i, 0)),
                  pl.BlockSpec((tm, half), lambda i: (i, 0)),
                  pl.BlockSpec((1, 6, d), lambda i: (i // per_b, 0, 0)),
                  pl.BlockSpec((1, d), lambda i: (0, 0)),
                  pl.BlockSpec((2 * half, d), lambda i: (0, 0)),
                  pl.BlockSpec((d, 2 * LANES), lambda i: (0, 0))],
        out_specs=[pl.BlockSpec((tm, d), lambda i: (i, 0)),
                   pl.BlockSpec((tm, d // 2), lambda i: (i, 0)),
                   pl.BlockSpec((LANES, tm), lambda i: (0, i))],
        out_shape=[jax.ShapeDtypeStruct((t, d), F32),
                   jax.ShapeDtypeStruct((t, d // 2), jnp.uint32),
                   jax.ShapeDtypeStruct((LANES, t), F32)],
        compiler_params=_params(1),
        name="out_proj",
    )(x2, dn, mla, ada3, gain, w_out, w_router)


def _expert_kernel(be_ref, nv_ref, na_ref, x_ref, wgu_hbm, wd_hbm, after_hbm, y_ref,
                   gu_f, d_f, gu_s, d_s, slot_ref, sem):
    i = pl.program_id(0)
    n_blocks = pl.num_programs(0)
    active = nv_ref[i] > 0
    first = active & ((i == 0) | (be_ref[i] != be_ref[jnp.maximum(i - 1, 0)]))

    def fetch(e, slot):
        return (pltpu.make_async_copy(wgu_hbm.at[e], gu_f.at[slot], sem.at[slot, 0]),
                pltpu.make_async_copy(wd_hbm.at[e], d_f.at[slot], sem.at[slot, 1]))

    @pl.when((i == 0) & active)
    def _():
        slot_ref[0] = 0
        for cp in fetch(be_ref[0], 0):
            cp.start()

    @pl.when(first)
    def _():
        cur = slot_ref[0]
        for cp in fetch(be_ref[i], cur):
            cp.wait()
        nxt = lax.while_loop(lambda j: (j < n_blocks) & (be_ref[jnp.minimum(j, n_blocks - 1)] == be_ref[i]),
                             lambda j: j + 1, i + 1)
        nxt_c = jnp.minimum(nxt, n_blocks - 1)

        @pl.when((nxt < n_blocks) & (nv_ref[nxt_c] > 0))
        def _():
            for cp in fetch(be_ref[nxt_c], 1 - cur):
                cp.start()

        gu_s[...] = gu_f[cur].astype(BF16)
        d_s[...] = d_f[cur].astype(BF16)
        slot_ref[0] = 1 - cur

    @pl.when(active)
    def _():
        x = _unpack_bf16_pairs(x_ref[...]).astype(BF16)
        ff = d_s.shape[0]
        chunks = [slice(c0, c0 + MXU_WIDTH) for c0 in range(0, ff, MXU_WIDTH)]
        gate_up = [(jnp.dot(x, gu_s[:, cs], preferred_element_type=F32),
                    jnp.dot(x, gu_s[:, slice(ff + cs.start, ff + cs.stop)], preferred_element_type=F32))
                   for cs in chunks]
        act = [(_silu(g) * u).astype(BF16) for g, u in gate_up]
        y = jnp.dot(act[0], d_s[chunks[0], :], preferred_element_type=F32)
        for a, cs in zip(act[1:], chunks[1:]):
            y = y + jnp.dot(a, d_s[cs, :], preferred_element_type=F32)
        row = lax.broadcasted_iota(jnp.int32, y.shape, 0)
        y_ref[...] = _pack_bf16_pairs(jnp.where(row < nv_ref[i], y, 0.0))


def _experts(block_e, n_valid, n_active, x_sorted, w_gu, w_d, after):
    n_rows, dp = x_sorted.shape
    ff2 = w_gu.shape[2]
    n_blocks = n_rows // EXPERT_BLOCK

    def block(i, be, nv, na):
        return (jnp.maximum(jnp.minimum(i, na[0] - 1), 0), 0)

    grid_spec = pltpu.PrefetchScalarGridSpec(
        num_scalar_prefetch=3,
        grid=(n_blocks,),
        in_specs=[pl.BlockSpec((EXPERT_BLOCK, dp), block),
                  pl.BlockSpec(memory_space=pl.ANY),
                  pl.BlockSpec(memory_space=pl.ANY),
                  pl.BlockSpec(memory_space=pl.ANY)],
        out_specs=pl.BlockSpec((EXPERT_BLOCK, dp), block),
        scratch_shapes=[pltpu.VMEM((2, 2 * dp, ff2), F32), pltpu.VMEM((2, ff2 // 2, 2 * dp), F32),
                        pltpu.VMEM((2 * dp, ff2), BF16), pltpu.VMEM((ff2 // 2, 2 * dp), BF16),
                        pltpu.SMEM((1,), jnp.int32), pltpu.SemaphoreType.DMA((2, 2))],
    )
    return pl.pallas_call(
        _expert_kernel,
        grid_spec=grid_spec,
        out_shape=jax.ShapeDtypeStruct((n_rows, dp), jnp.uint32),
        compiler_params=_params(0, 1),
        name="experts",
    )(block_e, n_valid, n_active, x_sorted, w_gu, w_d, after)


def _shared_ffn_kernel(h2_ref, wgu_ref, wd_ref, o_ref):
    h2 = _unpack_bf16_pairs(h2_ref[...]).astype(BF16)
    gu = jnp.dot(h2, wgu_ref[...], preferred_element_type=F32)
    ff = gu.shape[1] // 2
    act = _silu(gu[:, :ff]) * gu[:, ff:]
    o_ref[...] = jnp.dot(act.astype(BF16), wd_ref[...], preferred_element_type=F32).astype(o_ref.dtype)


def _shared_ffn(h2, w_gu, w_d):
    t, dp = h2.shape
    d = 2 * dp
    tm = min(512, t)
    ff2 = w_gu.shape[1]
    return pl.pallas_call(
        _shared_ffn_kernel,
        grid=(t // tm,),
        in_specs=[pl.BlockSpec((tm, dp), lambda i: (i, 0)),
                  pl.BlockSpec((d, ff2), lambda i: (0, 0)),
                  pl.BlockSpec((ff2 // 2, d), lambda i: (0, 0))],
        out_specs=pl.BlockSpec((tm, d), lambda i: (i, 0)),
        out_shape=jax.ShapeDtypeStruct((t, d), BF16),
        compiler_params=_params(1),
        name="shared_ffn",
    )(h2, w_gu, w_d)


def _final_kernel(x1_ref, sh_ref, y_ref, wc_ref, ada_ref, gain_ref, *rest):
    o_ref = rest[-1]
    ada = ada_ref[0]
    ffn = sh_ref[...].astype(F32)
    wc = wc_ref[...]
    for kk in range(TOP_K):
        ffn = ffn + wc[:, kk:kk + 1] * _unpack_bf16_pairs(y_ref[kk])
    x2 = x1_ref[...] + ada[5:6, :] * ffn
    o_ref[...] = _rms(x2, gain_ref[...])


def _final_part(x1, shared, y_part, wc, ada3, gain, seq, part, n_parts, prev_out):
    t, d = x1.shape
    tm = min(256, seq)
    per_b = seq // tm
    steps = t // tm // n_parts
    off = part * steps
    in_specs = [pl.BlockSpec((tm, d), lambda i: (i + off, 0)),
                pl.BlockSpec((tm, d), lambda i: (i + off, 0)),
                pl.BlockSpec((TOP_K, tm, d // 2), lambda i: (0, i, 0)),
                pl.BlockSpec((tm, LANES), lambda i: (i + off, 0)),
                pl.BlockSpec((1, 6, d), lambda i: ((i + off) // per_b, 0, 0)),
                pl.BlockSpec((1, d), lambda i: (0, 0))]
    args = [x1, shared, y_part, wc, ada3, gain]
    aliases = {}
    if prev_out is not None:
        in_specs.append(pl.BlockSpec(memory_space=pl.ANY))
        args.append(prev_out)
        aliases = {len(args) - 1: 0}
    return pl.pallas_call(
        _final_kernel,
        grid=(steps,),
        in_specs=in_specs,
        out_specs=pl.BlockSpec((tm, d), lambda i: (i + off, 0)),
        out_shape=jax.ShapeDtypeStruct((t, d), F32),
        input_output_aliases=aliases,
        compiler_params=_params(1),
        name="final",
    )(*args)


ROUTE_TILE = 512
MAX_BLOCK_LANES = 512


def _first_argmax(x, idx_f, n):
    m = jnp.max(x, axis=0, keepdims=True)
    first = jnp.min(jnp.where(x == m, idx_f, float(n)), axis=0, keepdims=True)
    return m, first


def _route_kernel(lg_ref, bias_ref, ek_ref, rk_ref, wc_ref, ps_ref, be_ref, carry_ref, upper_ref):
    i = pl.program_id(0)
    tm = lg_ref.shape[1]
    per_group = N_EXPERTS // N_GROUPS

    @pl.when(i == 0)
    def _():
        carry_ref[...] = jnp.zeros_like(carry_ref)
        r = lax.broadcasted_iota(jnp.int32, (tm, tm), 0)
        cc = lax.broadcasted_iota(jnp.int32, (tm, tm), 1)
        upper_ref[...] = (r < cc).astype(BF16)

    scores = jax.nn.sigmoid(lg_ref[0:N_EXPERTS, :])
    biased = scores + bias_ref[:, 0:1]
    sub8 = lax.broadcasted_iota(jnp.int32, (per_group, tm), 0).astype(F32)
    group_rows = []
    for g in range(N_GROUPS):
        xg = biased[g * per_group:(g + 1) * per_group, :]
        m1, i1 = _first_argmax(xg, sub8, per_group)
        m2 = jnp.max(jnp.where(sub8 == i1, -jnp.inf, xg), axis=0, keepdims=True)
        group_rows.append(m1 + m2)
    cur = jnp.concatenate(group_rows, axis=0)
    gself = jnp.zeros(cur.shape, F32)
    for _ in range(TOPK_GROUPS):
        _, gi = _first_argmax(cur, sub8, N_GROUPS)
        hit = sub8 == gi
        gself = jnp.where(hit, 1.0, gself)
        cur = jnp.where(hit, -jnp.inf, cur)
    masked = jnp.concatenate(
        [jnp.where(gself[g:g + 1, :] > 0.5, biased[g * per_group:(g + 1) * per_group, :], -jnp.inf)
         for g in range(N_GROUPS)], axis=0)
    sub64 = lax.broadcasted_iota(jnp.int32, (N_EXPERTS, tm), 0).astype(F32)
    e_rows, s_rows = [], []
    sel = jnp.zeros(masked.shape, F32)
    for _ in range(TOP_K):
        _, ei = _first_argmax(masked, sub64, N_EXPERTS)
        hit = sub64 == ei
        e_rows.append(ei)
        s_rows.append(jnp.sum(jnp.where(hit, scores, 0.0), axis=0, keepdims=True))
        sel = jnp.where(hit, 1.0, sel)
        masked = jnp.where(hit, -jnp.inf, masked)
    total = s_rows[0]
    for s in s_rows[1:]:
        total = total + s
    w_rows = [s / total * ROUTED_SCALE for s in s_rows]
    before = jnp.dot(sel.astype(BF16), upper_ref[...], preferred_element_type=F32) + carry_ref[:, 0:1]
    r_rows = [jnp.sum(jnp.where(sub64 == ei, before, 0.0), axis=0, keepdims=True) for ei in e_rows]
    zrow = jnp.zeros((8 - TOP_K, tm), F32)
    ek_ref[...] = jnp.concatenate(e_rows + [zrow], axis=0).astype(jnp.int32)
    rk_ref[...] = jnp.concatenate(r_rows + [zrow], axis=0).astype(jnp.int32)
    wc_ref[...] = jnp.concatenate(w_rows + [jnp.zeros((LANES - TOP_K, tm), F32)], axis=0).T
    carry_ref[...] = carry_ref[...] + jnp.sum(sel, axis=1, keepdims=True)

    @pl.when(i == pl.num_programs(0) - 1)
    def _():
        counts = carry_ref[...]
        padded = jnp.floor((counts + (EXPERT_BLOCK - 1.0)) * (1.0 / EXPERT_BLOCK)) * EXPERT_BLOCK
        r = lax.broadcasted_iota(jnp.int32, (N_EXPERTS, N_EXPERTS), 0)
        cc = lax.broadcasted_iota(jnp.int32, (N_EXPERTS, N_EXPERTS), 1)
        pad_end = jnp.dot((cc <= r).astype(F32), padded, precision=lax.Precision.HIGHEST,
                          preferred_element_type=F32)
        pad_start = pad_end - padded
        ps_ref[...] = pad_start
        blk0 = lax.broadcasted_iota(jnp.int32, (N_EXPERTS, MAX_BLOCK_LANES), 1).astype(F32) * EXPERT_BLOCK
        e_of_blk = jnp.minimum(jnp.sum((pad_end[:, 0:1] <= blk0).astype(F32), axis=0, keepdims=True),
                               N_EXPERTS - 1.0)
        sub = lax.broadcasted_iota(jnp.int32, (N_EXPERTS, MAX_BLOCK_LANES), 0).astype(F32)
        seg_end = jnp.sum(jnp.where(sub == e_of_blk, (pad_start + counts)[:, 0:1], 0.0), axis=0, keepdims=True)
        n_valid = jnp.clip(seg_end - blk0[0:1, :], 0.0, float(EXPERT_BLOCK))
        n_active = jnp.broadcast_to(jnp.sum((n_valid > 0.0).astype(F32), axis=1, keepdims=True), n_valid.shape)
        be_ref[...] = jnp.concatenate([e_of_blk, n_valid, n_active, jnp.zeros((5, MAX_BLOCK_LANES), F32)],
                                      axis=0).astype(jnp.int32)


def _route_pallas(logits_t, bias_col):
    t = logits_t.shape[1]
    tm = min(ROUTE_TILE, t)
    row8 = pl.BlockSpec((8, tm), lambda i: (0, i))
    return pl.pallas_call(
        _route_kernel,
        grid=(t // tm,),
        in_specs=[pl.BlockSpec((LANES, tm), lambda i: (0, i)),
                  pl.BlockSpec((N_EXPERTS, 1), lambda i: (0, 0))],
        out_specs=[row8, row8,
                   pl.BlockSpec((tm, LANES), lambda i: (i, 0)),
                   pl.BlockSpec((N_EXPERTS, LANES), lambda i: (0, 0)),
                   pl.BlockSpec((8, MAX_BLOCK_LANES), lambda i: (0, 0))],
        out_shape=[jax.ShapeDtypeStruct((8, t), jnp.int32),
                   jax.ShapeDtypeStruct((8, t), jnp.int32),
                   jax.ShapeDtypeStruct((t, LANES), F32),
                   jax.ShapeDtypeStruct((N_EXPERTS, LANES), F32),
                   jax.ShapeDtypeStruct((8, MAX_BLOCK_LANES), jnp.int32)],
        scratch_shapes=[pltpu.VMEM((N_EXPERTS, LANES), F32), pltpu.VMEM((tm, tm), BF16)],
        compiler_params=_params(0, 1),
        name="route",
    )(logits_t, bias_col)


def _slot_kernel(ek_ref, rk_ref, ps_ref, slot_ref):
    tm = ek_ref.shape[1]
    sub64 = lax.broadcasted_iota(jnp.int32, (N_EXPERTS, tm), 0)
    start = ps_ref[:, 0:1]
    rows = []
    for kk in range(TOP_K):
        seg = jnp.sum(jnp.where(sub64 == ek_ref[kk:kk + 1, :], start, 0.0), axis=0, keepdims=True)
        rows.append(seg.astype(jnp.int32) + rk_ref[kk:kk + 1, :])
    slot_ref[...] = jnp.concatenate(rows + [jnp.zeros((8 - TOP_K, tm), jnp.int32)], axis=0)


def _slots(ek, rk, ps):
    t = ek.shape[1]
    tm = min(ROUTE_TILE, t)
    row8 = pl.BlockSpec((8, tm), lambda i: (0, i))
    return pl.pallas_call(
        _slot_kernel,
        grid=(t // tm,),
        in_specs=[row8, row8, pl.BlockSpec((N_EXPERTS, LANES), lambda i: (0, 0))],
        out_specs=row8,
        out_shape=jax.ShapeDtypeStruct((8, t), jnp.int32),
        compiler_params=_params(1),
        name="slots",
    )(ek, rk, ps)


SC_ROWS = 32
FINAL_PARTS = 2


def _sc_mesh():
    info = plsc.get_sparse_core_info()
    mesh = plsc.VectorSubcoreMesh(core_axis_name="c", subcore_axis_name="s")
    return mesh, info.num_cores, info.num_cores * info.num_subcores


def _sc_dispatch(x, slot_flat, n_rows):
    t, dp = x.shape
    n_slots = slot_flat.shape[0] // t
    mesh, n_cores, n_workers = _sc_mesh()
    per_w = t // n_workers
    n_chunks = per_w // SC_ROWS
    assert per_w * n_workers == t and n_chunks * SC_ROWS == per_w and n_chunks % 2 == 0

    @functools.partial(
        pl.kernel, out_type=jax.ShapeDtypeStruct((n_rows, dp), x.dtype), mesh=mesh,
        scratch_types=[pltpu.VMEM((2, n_slots, SC_ROWS), jnp.int32), pltpu.VMEM((2, SC_ROWS, dp), x.dtype),
                       pltpu.SemaphoreType.DMA((2,)), pltpu.SemaphoreType.DMA((2,))])
    def dispatch(x_hbm, i_hbm, o_hbm, idx_v, rows_v, read_sem, scatter_sem):
        base = (lax.axis_index("s") * n_cores + lax.axis_index("c")) * per_w

        def read(j, b):
            return pltpu.make_async_copy(x_hbm.at[pl.ds(pl.multiple_of(base + j * SC_ROWS, 8), SC_ROWS)],
                                         rows_v.at[b], read_sem.at[b])

        def scatter(k, b):
            return pltpu.make_async_copy(rows_v.at[b], o_hbm.at[idx_v.at[b, k]], scatter_sem.at[b])

        read(0, 0).start()

        @pl.loop(0, n_chunks, step=2)
        def _(j):
            for b in range(2):
                jj = j + b

                @pl.when(jj >= 1)
                def _():
                    for k in range(n_slots):
                        scatter(k, 1 - b).wait()

                @pl.when(jj + 1 < n_chunks)
                def _():
                    read(jj + 1, 1 - b).start()

                for k in range(n_slots):
                    pltpu.sync_copy(i_hbm.at[pl.ds(pl.multiple_of(k * t + base + jj * SC_ROWS, 8), SC_ROWS)],
                                    idx_v.at[b, k])
                read(jj, b).wait()
                for k in range(n_slots):
                    scatter(k, b).start()

        for k in range(n_slots):
            scatter(k, (n_chunks - 1) % 2).wait()

    return dispatch(x, slot_flat)


def _sc_gather_rows(y, slot_flat):
    n = slot_flat.shape[0]
    dp = y.shape[1]
    mesh, n_cores, n_workers = _sc_mesh()
    per_w = n // n_workers
    n_chunks = per_w // SC_ROWS
    assert per_w * n_workers == n and n_chunks * SC_ROWS == per_w and n_chunks % 2 == 0

    @functools.partial(
        pl.kernel, out_type=jax.ShapeDtypeStruct((n, dp), y.dtype), mesh=mesh,
        scratch_types=[pltpu.VMEM((per_w,), jnp.int32), pltpu.VMEM((2, SC_ROWS, dp), y.dtype),
                       pltpu.SemaphoreType.DMA((2,)), pltpu.SemaphoreType.DMA((2,))])
    def combine(y_hbm, i_hbm, o_hbm, idx_v, rows_v, gather_sem, write_sem):
        base = (lax.axis_index("s") * n_cores + lax.axis_index("c")) * per_w
        pltpu.sync_copy(i_hbm.at[pl.ds(pl.multiple_of(base, 8), per_w)], idx_v)

        def gather(j, b):
            return pltpu.make_async_copy(y_hbm.at[idx_v.at[pl.ds(pl.multiple_of(j * SC_ROWS, 8), SC_ROWS)]],
                                         rows_v.at[b], gather_sem.at[b])

        def write(j, b):
            return pltpu.make_async_copy(rows_v.at[b],
                                         o_hbm.at[pl.ds(pl.multiple_of(base + j * SC_ROWS, 8), SC_ROWS)],
                                         write_sem.at[b])

        gather(0, 0).start()

        @pl.loop(0, n_chunks, step=2)
        def _(j):
            for b in range(2):
                jj = j + b

                @pl.when(jj >= 1)
                def _():
                    write(jj - 1, 1 - b).wait()

                @pl.when(jj + 1 < n_chunks)
                def _():
                    gather(jj + 1, 1 - b).start()

                gather(jj, b).wait()
                write(jj, b).start()

        write(n_chunks - 1, (n_chunks - 1) % 2).wait()

    return combine(y, slot_flat)


def _layer(x, cond_ada, positions, w_in, dn_conv_w, dn_a_log, dn_dt_bias, dn_norm_gain, mla_q_norm_gain,
           w_q_up, mla_kv_norm_gain, w_kv_up, w_out, norm1_gain, norm2_gain, w_router, router_bias,
           w_exp_gate_up, w_exp_down, w_sh_gate_up, w_sh_down, out_gain):
    bsz, seq, d = x.shape
    t = bsz * seq
    x2 = x.reshape(t, d)
    ada3 = cond_ada.reshape(bsz, 6, d)

    s_z = 4 * DN_WIDTH
    s_a = s_z + 2 * DN_HEADS
    s_kpe = s_a + MLA_Q_RANK + MLA_KV_RANK
    w_main = jnp.concatenate([w_in[:, :s_z], w_in[:, s_a:s_kpe]], axis=1).astype(BF16)
    w_aux = jnp.concatenate([w_in[:, s_kpe:], w_in[:, s_z:s_a],
                             jnp.zeros((d, LANES - MLA_ROPE - 2 * DN_HEADS), F32)], axis=1).astype(BF16)
    auxp = jnp.zeros((8, LANES), F32)
    auxp = auxp.at[0, AUX_G:AUX_G + DN_HEADS].set(dn_a_log).at[1, AUX_G:AUX_G + DN_HEADS].set(dn_dt_bias)
    main, auxc, auxr = _in_proj(x2, ada3, norm1_gain.reshape(1, d), w_main, w_aux, auxp, seq)

    dn = _delta(main.reshape(bsz, seq, MAIN_WIDTH), auxc.reshape(bsz, seq, LANES), auxr,
                dn_conv_w.T, dn_norm_gain.reshape(1, DN_DIM))

    qk = MLA_NOPE + MLA_ROPE
    wq3 = w_q_up.reshape(MLA_Q_RANK, MLA_HEADS, qk)
    wq = jnp.concatenate([wq3, jnp.zeros((MLA_Q_RANK, MLA_HEADS, MLA_QK_PAD - qk), F32)], axis=2)
    wq = wq.reshape(MLA_Q_RANK, MLA_HEADS * MLA_QK_PAD).astype(BF16)
    wkv3 = w_kv_up.reshape(MLA_KV_RANK, MLA_HEADS, MLA_NOPE + MLA_V)
    wkv = jnp.concatenate([wkv3[:, :, :MLA_NOPE].reshape(MLA_KV_RANK, -1),
                           wkv3[:, :, MLA_NOPE:].reshape(MLA_KV_RANK, -1)], axis=1).astype(BF16)
    half = MLA_ROPE // 2
    inv_freq = ROPE_THETA ** (-jnp.arange(half, dtype=F32) / half)
    zeros = jnp.zeros((LANES - MLA_ROPE,), F32)
    rope_tab = jnp.zeros((8, LANES), F32)
    rope_tab = rope_tab.at[0].set(jnp.concatenate([inv_freq, inv_freq, zeros]))
    rope_tab = rope_tab.at[1].set(jnp.concatenate([jnp.ones((MLA_ROPE,), F32), zeros]))
    rope_tab = rope_tab.at[2].set(jnp.concatenate([-jnp.ones((half,), F32), jnp.ones((half,), F32), zeros]))
    q, k, v = _mla_proj(main, auxc, positions.reshape(t, 1), mla_q_norm_gain.reshape(1, -1),
                        mla_kv_norm_gain.reshape(1, -1), wq, wkv, rope_tab, seq)
    mla = _attention(q.reshape(bsz, seq, -1), k.reshape(bsz, seq, -1), v)

    w_r = jnp.concatenate([w_router, jnp.zeros((d, LANES - N_EXPERTS), F32)], axis=1)
    w_r_hi = w_r.astype(BF16)
    w_r = jnp.concatenate([w_r_hi, (w_r - w_r_hi.astype(F32)).astype(BF16)], axis=1)
    x1, h2, logits_t = _out_proj(x2, dn.reshape(t, DN_WIDTH), mla.reshape(t, -1), ada3,
                                 norm2_gain.reshape(1, d), w_out.astype(BF16), w_r, seq)

    n_rows = -(-(t * TOP_K + N_EXPERTS * (EXPERT_BLOCK - 1)) // EXPERT_BLOCK) * EXPERT_BLOCK
    n_blocks = n_rows // EXPERT_BLOCK
    assert n_blocks <= MAX_BLOCK_LANES
    ek, rk, wc, seg_start, blocks = _route_pallas(logits_t, router_bias.reshape(N_EXPERTS, 1))
    slot = _slots(ek, rk, seg_start)[:TOP_K]
    x_sorted = _sc_dispatch(h2, slot.reshape(-1), n_rows)
    shared = _shared_ffn(h2, w_sh_gate_up.astype(BF16), w_sh_down.astype(BF16))
    y_sorted = _experts(blocks[0, :n_blocks], blocks[1, :n_blocks], blocks[2, :1], x_sorted,
                        w_exp_gate_up, w_exp_down, shared)

    n_parts = FINAL_PARTS if (t // min(256, seq)) % FINAL_PARTS == 0 else 1
    out = None
    for part in range(n_parts):
        part_slots = slot[:, part * (t // n_parts):(part + 1) * (t // n_parts)].reshape(-1)
        y_part = _sc_gather_rows(y_sorted, part_slots).reshape(TOP_K, t // n_parts, d // 2)
        out = _final_part(x1, shared, y_part, wc, ada3, out_gain.reshape(1, d), seq, part, n_parts, out)
    return out.reshape(bsz, seq, d)


def kernel(x, c, positions, w_ada, b_ada, norm1_gain, w_in, dn_conv_w, dn_a_log, dn_dt_bias, dn_norm_gain,
           mla_q_norm_gain, w_q_up, mla_kv_norm_gain, w_kv_up, w_out, norm2_gain, w_router, router_bias,
           w_exp_gate_up, w_exp_down, w_sh_gate_up, w_sh_down, final_norm_gain):
    depth = w_ada.shape[0]
    assert depth == 1, "the final RMSNorm is fused into the single layer's last kernel"
    ada = _ada(c, w_ada[0], b_ada[0])
    return _layer(x, ada, positions, w_in[0], dn_conv_w[0], dn_a_log[0], dn_dt_bias[0], dn_norm_gain[0],
                  mla_q_norm_gain[0], w_q_up[0], mla_kv_norm_gain[0], w_kv_up[0], w_out[0], norm1_gain[0],
                  norm2_gain[0], w_router[0], router_bias[0], w_exp_gate_up[0], w_exp_down[0],
                  w_sh_gate_up[0], w_sh_down[0], final_norm_gain)
```
